```python
import math
import jax, jax.numpy as jnp
from jax import lax
import numpy as np

D_MODEL = 2048
BATCH = 1
SEQ = 8192
DEPTH = 1

ATT_HEADS = 8
ATT_HEAD_DIM = 128
ATT_WIDTH = ATT_HEADS * ATT_HEAD_DIM
MOBA_BLOCK = 256
MOBA_TOPK = 3
Q_CHUNK = 128
ROPE_THETA = 500000.0
ROPE_DIM = ATT_HEAD_DIM // 4
RWKV_HEAD_DIM = 64
RWKV_WIDTH = 1024
RWKV_HEADS = RWKV_WIDTH // RWKV_HEAD_DIM
DECAY_LORA = 96
AAA_LORA = 96
GATE_LORA = 256
GN_EPS = 64e-5
IN_WIDTH = 3 * ATT_WIDTH + 3 * RWKV_WIDTH + 2 * D_MODEL
N_GROUPS = 8
EXPERTS_PER_GROUP = 8
N_EXPERTS = N_GROUPS * EXPERTS_PER_GROUP
TOP_K = 2
EXPERT_FF = 512
EXPERT_BLOCK = 128
RMS_EPS = 1e-6
NEG = -1e30

kernel_name = "hybrid_moba_rwkv7_hmoe_block"


def rms_norm(z, g):
    zf = z.astype(jnp.float32)
    zf = zf * lax.rsqrt(jnp.mean(zf * zf, axis=-1, keepdims=True) + RMS_EPS)
    return zf.astype(z.dtype) * g


def shift_right(z):
    return jnp.pad(z, ((0, 0), (1, 0), (0, 0)))[:, :-1]


def partial_rope(z, pos):
    half = ROPE_DIM // 2
    inv = ROPE_THETA ** (-jnp.arange(half, dtype=jnp.float32) / half)
    ang = pos.astype(jnp.float32)[:, None] * inv[None, :]
    cos = jnp.cos(ang).astype(z.dtype)
    sin = jnp.sin(ang).astype(z.dtype)
    z1 = z[..., :half]
    z2 = z[..., half:ROPE_DIM]
    rot = jnp.concatenate([z1 * cos - z2 * sin, z2 * cos + z1 * sin], axis=-1)
    return jnp.concatenate([rot, z[..., ROPE_DIM:]], axis=-1)


def moba_attention(q, k, v):
    B, H, S, Dh = q.shape
    nb = -(-S // MOBA_BLOCK)
    sp = nb * MOBA_BLOCK
    pad = ((0, 0), (0, 0), (0, sp - S), (0, 0))
    q, k, v = jnp.pad(q, pad), jnp.pad(k, pad), jnp.pad(v, pad)
    kb = k.reshape(B, H, nb, MOBA_BLOCK, Dh)
    vb = v.reshape(B, H, nb, MOBA_BLOCK, Dh)
    kmean = jnp.mean(kb.astype(jnp.float32), axis=3)
    nc = sp // Q_CHUNK
    qc = q.reshape(B, H, nc, Q_CHUNK, Dh).transpose(2, 0, 1, 3, 4)
    ntop = min(MOBA_TOPK, nb)
    scale = ATT_HEAD_DIM ** -0.5
    bi = jnp.arange(B)[:, None, None, None]
    hi = jnp.arange(H)[None, :, None, None]

    def one_chunk(args):
        ci, qx = args
        qblk = (ci * Q_CHUNK) // MOBA_BLOCK
        qpos = ci * Q_CHUNK + jnp.arange(Q_CHUNK)
        gate = jnp.einsum('bhqd,bhnd->bhqn', qx.astype(jnp.float32), kmean)
        gate = jnp.where(jnp.arange(nb) < qblk, gate, NEG)
        _, sel = lax.top_k(gate, ntop)
        valid = sel < qblk
        k_own = lax.dynamic_index_in_dim(kb, qblk, axis=2, keepdims=False)
        v_own = lax.dynamic_index_in_dim(vb, qblk, axis=2, keepdims=False)
        kpos = qblk * MOBA_BLOCK + jnp.arange(MOBA_BLOCK)
        s_own = jnp.einsum('bhqd,bhkd->bhqk', qx, k_own).astype(jnp.float32) * scale
        s_own = jnp.where(kpos[None, :] <= qpos[:, None], s_own, NEG)
        k_sel = kb[bi, hi, sel]
        v_sel = vb[bi, hi, sel]
        s_sel = jnp.einsum('bhqd,bhqnkd->bhqnk', qx, k_sel).astype(jnp.float32) * scale
        s_sel = jnp.where(valid[..., None], s_sel, NEG)
        s_all = jnp.concatenate([s_own, s_sel.reshape(B, H, Q_CHUNK, ntop * MOBA_BLOCK)], axis=-1)
        p = jax.nn.softmax(s_all, axis=-1).astype(v.dtype)
        p_own = p[..., :MOBA_BLOCK]
        p_sel = p[..., MOBA_BLOCK:].reshape(B, H, Q_CHUNK, ntop, MOBA_BLOCK)
        return (jnp.einsum('bhqk,bhkd->bhqd', p_own, v_own)
                + jnp.einsum('bhqnk,bhqnkd->bhqd', p_sel, v_sel))

    out = lax.map(one_chunk, (jnp.arange(nc), qc))
    out = out.transpose(1, 2, 0, 3, 4).reshape(B, H, sp, Dh)
    return out[:, :, :S]


def rwkv7_time_mix(h, r, k, v, mu_w, mu_a, mu_g, w0, w_w1, w_w2, a0, w_a1, w_a2,
                   w_g1, w_g2, k_k, k_a, r_k, gn_w, gn_b):
    B, S, _ = h.shape
    H, N = RWKV_HEADS, RWKV_HEAD_DIM
    f32 = jnp.float32
    dh = shift_right(h) - h
    xw = h + dh * mu_w
    xa = h + dh * mu_a
    xg = h + dh * mu_g
    d = (w0 + jnp.tanh(xw @ w_w1) @ w_w2).astype(f32)
    w = jnp.exp(-math.exp(-0.5) * jax.nn.sigmoid(d))
    a = jax.nn.sigmoid((a0 + (xa @ w_a1) @ w_a2).astype(f32))
    g = jax.nn.sigmoid(xg @ w_g1) @ w_g2
    rf, kf, vf = r.astype(f32), k.astype(f32), v.astype(f32)
    kk = (kf * k_k).reshape(B, S, H, N)
    kk = kk / jnp.maximum(jnp.sqrt(jnp.sum(kk * kk, axis=-1, keepdims=True)), 1e-12)
    kt = kf * (1.0 + (a - 1.0) * k_a)
    rh, wh, kth, vh, ah = (z.reshape(B, S, H, N) for z in (rf, w, kt, vf, a))
    xs = tuple(jnp.moveaxis(z, 1, 0) for z in (rh, wh, kth, vh, kk, ah))

    def step(state, inp):
        r_t, w_t, k_t, v_t, kk_t, a_t = inp
        sa = jnp.einsum('bhvk,bhk->bhv', state, -kk_t)
        state = (state * w_t[:, :, None, :]
                 + sa[..., None] * (kk_t * a_t)[:, :, None, :]
                 + v_t[..., None] * k_t[:, :, None, :])
        return state, jnp.einsum('bhvk,bhk->bhv', state, r_t)

    _, ys = lax.scan(step, jnp.zeros((B, H, N, N), f32), xs)
    y = jnp.moveaxis(ys, 0, 1)
    mean = jnp.mean(y, axis=-1, keepdims=True)
    var = jnp.mean(jnp.square(y - mean), axis=-1, keepdims=True)
    yn = ((y - mean) * lax.rsqrt(var + GN_EPS)).reshape(B, S, RWKV_WIDTH) * gn_w + gn_b
    bonus = (jnp.sum(rh * kth * r_k, axis=-1, keepdims=True) * vh).reshape(B, S, RWKV_WIDTH)
    return ((yn + bonus) * g).astype(h.dtype)


def token_mixer(h, w_in, mu_r, mu_k, mu_v, mu_w, mu_a, mu_g, w0, w_w1, w_w2, a0, w_a1, w_a2,
                w_g1, w_g2, k_k, k_a, r_k, gn_w, gn_b, w_up_att, w_up_rwkv, w_o):
    B, S, _ = h.shape
    proj = h @ w_in
    q, k, v, rkv, gates = jnp.split(
        proj, [ATT_WIDTH, 2 * ATT_WIDTH, 3 * ATT_WIDTH, 3 * ATT_WIDTH + 3 * RWKV_WIDTH], axis=-1)
    gate_att, gate_rwkv = jnp.split(gates, 2, axis=-1)
    pos = jnp.arange(S)
    def heads(z):
        return z.reshape(B, S, ATT_HEADS, ATT_HEAD_DIM).transpose(0, 2, 1, 3)
    qh = partial_rope(heads(q), pos)
    kh = partial_rope(heads(k), pos)
    o_att = moba_attention(qh, kh, heads(v)).transpose(0, 2, 1, 3).reshape(B, S, ATT_WIDTH)
    mu_rkv = jnp.concatenate([mu_r, mu_k, mu_v])
    rkv = rkv + (shift_right(rkv) - rkv) * mu_rkv
    r, kr, vr = jnp.split(rkv, 3, axis=-1)
    o_rwkv = rwkv7_time_mix(h, r, kr, vr, mu_w, mu_a, mu_g, w0, w_w1, w_w2, a0, w_a1, w_a2,
                            w_g1, w_g2, k_k, k_a, r_k, gn_w, gn_b)
    mix = (jax.nn.sigmoid(gate_att) * (o_att @ w_up_att)
           + jax.nn.sigmoid(gate_rwkv) * (o_rwkv @ w_up_rwkv))
    return mix @ w_o


def hierarchical_moe(h, w_rg, b_rg, w_re, b_re, w_gate_e, w_up_e, w_down_e):
    B, S, D = h.shape
    N = B * S
    t = h.reshape(N, D)
    pg = jax.nn.softmax((t @ w_rg).astype(jnp.float32) + b_rg, axis=-1)
    pg_top, g_idx = lax.top_k(pg, 1)
    le = ((t @ w_re).astype(jnp.float32) + b_re).reshape(N, N_GROUPS, EXPERTS_PER_GROUP)
    le = jnp.take_along_axis(le, g_idx[:, :, None], axis=1)[:, 0]
    pe = jax.nn.softmax(le, axis=-1)
    pe_top, e_local = lax.top_k(pe, TOP_K)
    wts = pg_top * pe_top / jnp.sum(pe_top, axis=-1, keepdims=True)
    e_idx = g_idx * EXPERTS_PER_GROUP + e_local
    P = N * TOP_K
    n_blk = (P + N_EXPERTS * (EXPERT_BLOCK - 1) + EXPERT_BLOCK - 1) // EXPERT_BLOCK
    e_flat = e_idx.reshape(P)
    tok_flat = jnp.repeat(jnp.arange(N, dtype=jnp.int32), TOP_K)
    w_flat = wts.reshape(P)
    order = jnp.argsort(e_flat)
    e_s, tok_s, w_s = e_flat[order], tok_flat[order], w_flat[order]
    counts = jnp.zeros((N_EXPERTS,), jnp.int32).at[e_flat].add(1)
    start = jnp.cumsum(counts) - counts
    pcounts = (counts + EXPERT_BLOCK - 1) // EXPERT_BLOCK * EXPERT_BLOCK
    pend = jnp.cumsum(pcounts)
    pstart = pend - pcounts
    dest = pstart[e_s] + (jnp.arange(P, dtype=jnp.int32) - start[e_s])
    buf_tok = jnp.zeros((n_blk * EXPERT_BLOCK,), jnp.int32).at[dest].set(tok_s)
    buf_w = jnp.zeros((n_blk * EXPERT_BLOCK,), jnp.float32).at[dest].set(w_s)
    blk_e = jnp.clip(jnp.searchsorted(pend, jnp.arange(n_blk, dtype=jnp.int32) * EXPERT_BLOCK,
                                      side='right'), 0, N_EXPERTS - 1)
    x_buf = t[buf_tok].reshape(n_blk, EXPERT_BLOCK, D)

    def run_block(args):
        xb, e = args
        hid = jax.nn.silu(xb @ w_gate_e[e]) * (xb @ w_up_e[e])
        return hid @ w_down_e[e]

    y_buf = lax.map(run_block, (x_buf, blk_e)).reshape(n_blk * EXPERT_BLOCK, D)
    out = jax.ops.segment_sum(y_buf * buf_w[:, None], buf_tok, num_segments=N)
    return out.astype(h.dtype).reshape(B, S, D)


def setup_inputs(seed: int = 0) -> dict:
    key = jax.random.key(seed)
    ks = iter(jax.random.split(key, 64))
    L, D = DEPTH, D_MODEL
    def nrm(shape, scale):
        return jax.random.normal(next(ks), shape, jnp.float32) * scale
    def uni(shape, lo, hi):
        return jax.random.uniform(next(ks), shape, jnp.float32, lo, hi)
    return {
        "x": nrm((BATCH, SEQ, D), 1.0),
        "c": nrm((BATCH, D), 1.0),
        "w_ada": nrm((L, D, 6 * D), 0.1 * D ** -0.5),
        "b_ada": nrm((L, 6 * D), 0.02),
        "g_pre_mix": 1.0 + nrm((L, D), 0.02),
        "g_post_mix": 1.0 + nrm((L, D), 0.02),
        "g_pre_ffn": 1.0 + nrm((L, D), 0.02),
        "g_post_ffn": 1.0 + nrm((L, D), 0.02),
        "w_in": nrm((L, D, IN_WIDTH), D ** -0.5),
        "mu_r": uni((L, RWKV_WIDTH), 0.0, 1.0),
        "mu_k": uni((L, RWKV_WIDTH), 0.0, 1.0),
        "mu_v": uni((L, RWKV_WIDTH), 0.0, 1.0),
        "mu_w": uni((L, D), 0.0, 1.0),
        "mu_a": uni((L, D), 0.0, 1.0),
        "mu_g": uni((L, D), 0.0, 1.0),
        "w0": uni((L, RWKV_WIDTH), -3.0, 1.0),
        "w_w1": nrm((L, D, DECAY_LORA), D ** -0.5),
        "w_w2": nrm((L, DECAY_LORA, RWKV_WIDTH), DECAY_LORA ** -0.5),
        "a0": nrm((L, RWKV_WIDTH), 0.5),
        "w_a1": nrm((L, D, AAA_LORA), D ** -0.5),
        "w_a2": nrm((L, AAA_LORA, RWKV_WIDTH), AAA_LORA ** -0.5),
        "w_g1": nrm((L, D, GATE_LORA), D ** -0.5),
        "w_g2": nrm((L, GATE_LORA, RWKV_WIDTH), GATE_LORA ** -0.5),
        "k_k": 0.85 + nrm((L, RWKV_WIDTH), 0.05),
        "k_a": 1.0 + nrm((L, RWKV_WIDTH), 0.05),
        "r_k": nrm((L, RWKV_HEADS, RWKV_HEAD_DIM), 0.1),
        "gn_w": 1.0 + nrm((L, RWKV_WIDTH), 0.02),
        "gn_b": nrm((L, RWKV_WIDTH), 0.02),
        "w_up_att": nrm((L, ATT_WIDTH, D), ATT_WIDTH ** -0.5),
        "w_up_rwkv": nrm((L, RWKV_WIDTH, D), RWKV_WIDTH ** -0.5),
        "w_o": nrm((L, D, D), D ** -0.5),
        "w_rg": nrm((L, D, N_GROUPS), D ** -0.5),
        "b_rg": nrm((L, N_GROUPS), 0.01),
        "w_re": nrm((L, D, N_EXPERTS), D ** -0.5),
        "b_re": nrm((L, N_EXPERTS), 0.01),
        "w_gate_e": nrm((L, N_EXPERTS, D, EXPERT_FF), D ** -0.5),
        "w_up_e": nrm((L, N_EXPERTS, D, EXPERT_FF), D ** -0.5),
        "w_down_e": nrm((L, N_EXPERTS, EXPERT_FF, D), EXPERT_FF ** -0.5),
    }


def reference(x, c, w_ada, b_ada, g_pre_mix, g_post_mix, g_pre_ffn, g_post_ffn, w_in,
              mu_r, mu_k, mu_v, mu_w, mu_a, mu_g, w0, w_w1, w_w2, a0, w_a1, w_a2,
              w_g1, w_g2, k_k, k_a, r_k, gn_w, gn_b, w_up_att, w_up_rwkv, w_o,
              w_rg, b_rg, w_re, b_re, w_gate_e, w_up_e, w_down_e):
    for l in range(DEPTH):
        ada = (c @ w_ada[l] + b_ada[l])[:, None, :]
        sh1, sc1, gt1, sh2, sc2, gt2 = jnp.split(ada, 6, axis=-1)
        h = rms_norm(x, g_pre_mix[l]) * (1.0 + sc1) + sh1
        y = token_mixer(h, w_in[l], mu_r[l], mu_k[l], mu_v[l], mu_w[l], mu_a[l], mu_g[l],
                        w0[l], w_w1[l], w_w2[l], a0[l], w_a1[l], w_a2[l], w_g1[l], w_g2[l],
                        k_k[l], k_a[l], r_k[l], gn_w[l], gn_b[l],
                        w_up_att[l], w_up_rwkv[l], w_o[l])
        x = x + gt1 * rms_norm(y, g_post_mix[l])
        h = rms_norm(x, g_pre_ffn[l]) * (1.0 + sc2) + sh2
        y = hierarchical_moe(h, w_rg[l], b_rg[l], w_re[l], b_re[l],
                             w_gate_e[l], w_up_e[l], w_down_e[l])
        x = x + gt2 * rms_norm(y, g_post_ffn[l])
    return x
```

```python
import functools
import math

import jax
import jax.numpy as jnp
from jax import lax
from jax.experimental import pallas as pl
from jax.experimental.pallas import tpu as pltpu

F32 = jnp.float32
BF16 = jnp.bfloat16
I32 = jnp.int32
HI = lax.Precision.HIGHEST

LANES = 128
SUBLANES = 8
VMEM_LIMIT = 56 * 1024 * 1024

ATT_HEADS = 8
ATT_HEAD_DIM = 128
MOBA_BLOCK = 256
MOBA_TOPK = 3
ROPE_THETA = 500000.0
ROPE_DIM = ATT_HEAD_DIM // 4
RWKV_HEAD_DIM = 64
GN_EPS = 64e-5
N_GROUPS = 8
EXPERTS_PER_GROUP = 8
N_EXPERTS = N_GROUPS * EXPERTS_PER_GROUP
TOP_K = 2
EXPERT_BLOCK = 128
RMS_EPS = 1e-6
NEG = -1e30
SCAN_CHUNK = 64


def _params(*sem):
    return pltpu.CompilerParams(dimension_semantics=sem, vmem_limit_bytes=VMEM_LIMIT)


def _rms(z, g):
    return z * lax.rsqrt(jnp.mean(z * z, axis=-1, keepdims=True) + RMS_EPS) * g


def _sigmoid(z):
    return 1.0 / (1.0 + jnp.exp(-z))


def _dot(a, b):
    return jnp.dot(a.astype(BF16), b.astype(BF16), preferred_element_type=F32)


def _dot_nt(a, b):
    return lax.dot_general(a.astype(BF16), b.astype(BF16), (((1,), (1,)), ((), ())),
                           preferred_element_type=F32)


def _dot_tn(a, b):
    return lax.dot_general(a.astype(BF16), b.astype(BF16), (((0,), (0,)), ((), ())),
                           preferred_element_type=F32)


def _dot_hi(a, b):
    return jnp.dot(a, b, precision=HI, preferred_element_type=F32)


def _shift_rows(z, prev_row):
    rolled = pltpu.roll(z, 1, 0)
    row = lax.broadcasted_iota(I32, z.shape, 0)
    return jnp.where(row == 0, prev_row, rolled)


def _ada_kernel(c_ref, w_ref, b_ref, o_ref):
    o_ref[...] = jnp.sum(c_ref[...] * w_ref[...], axis=0, keepdims=True) + b_ref[...]


def _ada(c_col, w_ada, b_ada):
    d, n = w_ada.shape
    tn = 1024
    return pl.pallas_call(
        _ada_kernel,
        grid=(n // tn,),
        in_specs=[pl.BlockSpec((d, 1), lambda j: (0, 0)),
                  pl.BlockSpec((d, tn), lambda j: (0, j)),
                  pl.BlockSpec((1, tn), lambda j: (0, j))],
        out_specs=pl.BlockSpec((1, tn), lambda j: (0, j)),
        out_shape=jax.ShapeDtypeStruct((1, n), F32),
        name="ada",
        compiler_params=_params("parallel"),
    )(c_col, w_ada, b_ada)


def _prenorm_kernel(x_ref, xp_ref, g_ref, sc_ref, sh_ref, muw_ref, mua_ref, mug_ref,
                    ww1_ref, wa1_ref, wg1_ref, h_ref, lw_ref, la_ref, lg_ref):
    i = pl.program_id(0)
    g, sc, sh = g_ref[...], sc_ref[...], sh_ref[...]
    h = _rms(x_ref[...], g) * (1.0 + sc) + sh
    hp = _rms(xp_ref[SUBLANES - 1:SUBLANES, :], g) * (1.0 + sc) + sh
    hp = jnp.where(i == 0, 0.0, hp)
    dh = _shift_rows(h, hp) - h
    h_ref[...] = h.astype(BF16)
    lw_ref[...] = jnp.tanh(_dot(h + dh * muw_ref[...], ww1_ref[...]))
    la_ref[...] = _dot(h + dh * mua_ref[...], wa1_ref[...])
    lg_ref[...] = _sigmoid(_dot(h + dh * mug_ref[...], wg1_ref[...]))


def _prenorm(x, g, sc, sh, mu_w, mu_a, mu_g, w_w1, w_a1, w_g1):
    s, d = x.shape
    tm = 256
    rpb = tm // SUBLANES
    vec = pl.BlockSpec((1, d), lambda i: (0, 0))
    full = lambda a: pl.BlockSpec(a.shape, lambda i: (0, 0))
    lw, la, lg = w_w1.shape[1], w_a1.shape[1], w_g1.shape[1]
    return pl.pallas_call(
        _prenorm_kernel,
        grid=(s // tm,),
        in_specs=[pl.BlockSpec((tm, d), lambda i: (i, 0)),
                  pl.BlockSpec((SUBLANES, d), lambda i: (jnp.maximum(i * rpb - 1, 0), 0)),
                  vec, vec, vec, vec, vec, vec, full(w_w1), full(w_a1), full(w_g1)],
        out_specs=[pl.BlockSpec((tm, d), lambda i: (i, 0)),
                   pl.BlockSpec((tm, lw), lambda i: (i, 0)),
                   pl.BlockSpec((tm, la), lambda i: (i, 0)),
                   pl.BlockSpec((tm, lg), lambda i: (i, 0))],
        out_shape=[jax.ShapeDtypeStruct((s, d), BF16),
                   jax.ShapeDtypeStruct((s, lw), F32),
                   jax.ShapeDtypeStruct((s, la), F32),
                   jax.ShapeDtypeStruct((s, lg), F32)],
        name="prenorm_lora",
        compiler_params=_params("parallel"),
    )(x, x, g, sc, sh, mu_w, mu_a, mu_g, w_w1, w_a1, w_g1)


def _mm_kernel(a_ref, w_ref, o_ref, wb_ref):
    @pl.when(pl.program_id(1) == 0)
    def _():
        wb_ref[...] = w_ref[...].astype(BF16)

    o_ref[...] = jnp.dot(a_ref[...], wb_ref[...], preferred_element_type=F32).astype(o_ref.dtype)


def _matmul(a, w, out_dtype, tm=512, tn=1024):
    m, k = a.shape
    n = w.shape[1]
    tn = min(tn, n)
    return pl.pallas_call(
        _mm_kernel,
        grid=(n // tn, m // tm),
        in_specs=[pl.BlockSpec((tm, k), lambda j, i: (i, 0)),
                  pl.BlockSpec((k, tn), lambda j, i: (0, j))],
        out_specs=pl.BlockSpec((tm, tn), lambda j, i: (i, j)),
        out_shape=jax.ShapeDtypeStruct((m, n), out_dtype),
        scratch_shapes=[pltpu.VMEM((k, tn), BF16)],
        name="matmul",
        compiler_params=_params("arbitrary", "arbitrary"),
    )(a, w)


def _rope_kernel(p_ref, c_ref, s1_ref, s2_ref, o_ref, km_ref):
    c, s1, s2 = c_ref[...], s1_ref[...], s2_ref[...]
    for hh in range(2 * ATT_HEADS):
        z = p_ref[:, hh * LANES:(hh + 1) * LANES].astype(F32)
        zr = z * c + pltpu.roll(z, LANES - ROPE_DIM // 2, 1) * s1 + pltpu.roll(z, ROPE_DIM // 2, 1) * s2
        o_ref[:, hh * LANES:(hh + 1) * LANES] = zr.astype(BF16)
        if hh >= ATT_HEADS:
            km_ref[0, hh - ATT_HEADS:hh - ATT_HEADS + 1, :] = jnp.mean(zr, axis=0, keepdims=True)


def _rope(proj, cos_t, sin1_t, sin2_t):
    s = proj.shape[0]
    nb = s // MOBA_BLOCK
    w = 2 * ATT_HEADS * ATT_HEAD_DIM
    tab = pl.BlockSpec((MOBA_BLOCK, LANES), lambda i: (i, 0))
    return pl.pallas_call(
        _rope_kernel,
        grid=(nb,),
        in_specs=[pl.BlockSpec((MOBA_BLOCK, w), lambda i: (i, 0)), tab, tab, tab],
        out_specs=[pl.BlockSpec((MOBA_BLOCK, w), lambda i: (i, 0)),
                   pl.BlockSpec((1, ATT_HEADS, LANES), lambda i: (i, 0, 0))],
        out_shape=[jax.ShapeDtypeStruct((s, w), BF16),
                   jax.ShapeDtypeStruct((nb, ATT_HEADS, LANES), F32)],
        name="rope_kmean",
        compiler_params=_params("parallel"),
    )(proj, cos_t, sin1_t, sin2_t)


def _attn_kernel(q_ref, k_ref, v_ref, km_ref, o_ref):
    qi = pl.program_id(1)
    bs = MOBA_BLOCK
    scale = ATT_HEAD_DIM ** -0.5
    q = q_ref[...]
    km = km_ref[0]
    nbp = km.shape[0]
    g = lax.dot_general(km, q.astype(F32), (((1,), (1,)), ((), ())), precision=HI,
                        preferred_element_type=F32)
    row = lax.broadcasted_iota(I32, g.shape, 0)
    g = jnp.where(row < qi, g, NEG)
    sel_t = jnp.zeros(g.shape, F32)
    for _ in range(MOBA_TOPK):
        mx = jnp.max(g, axis=0, keepdims=True)
        idx = jnp.min(jnp.where(g == mx, row, nbp), axis=0, keepdims=True)
        hit = row == idx
        sel_t = jnp.where(hit & (row < qi), 1.0, sel_t)
        g = jnp.where(hit, -jnp.inf, g)
    if nbp < LANES:
        sel_t = jnp.concatenate([sel_t, jnp.zeros((LANES - nbp, bs), F32)], axis=0)
    sel = sel_t.T
    bias = jnp.where(sel > 0.5, 0.0, NEG).astype(BF16)
    q_aug = jnp.concatenate([q, bias], axis=1)

    kd = k_ref[pl.ds(pl.multiple_of(qi * bs, bs), bs), :]
    vd = v_ref[pl.ds(pl.multiple_of(qi * bs, bs), bs), :]
    s = _dot_nt(q, kd) * scale
    r_i = lax.broadcasted_iota(I32, s.shape, 0)
    c_i = lax.broadcasted_iota(I32, s.shape, 1)
    s = jnp.where(c_i <= r_i, s, NEG)
    m = jnp.max(s, axis=1, keepdims=True)
    p = jnp.exp(s - m)
    l = jnp.sum(p, axis=1, keepdims=True)
    acc = _dot(p, vd)

    lane = lax.broadcasted_iota(I32, (bs, LANES), 1)

    def body(n, carry):
        m, l, acc = carry
        kn = k_ref[pl.ds(pl.multiple_of(n * bs, bs), bs), :]
        vn = v_ref[pl.ds(pl.multiple_of(n * bs, bs), bs), :]
        onehot = jnp.where(lane == n, 1.0, 0.0).astype(BF16)
        k_aug = jnp.concatenate([kn, onehot], axis=1)
        s = _dot_nt(q_aug, k_aug) * scale
        mn = jnp.maximum(m, jnp.max(s, axis=1, keepdims=True))
        alpha = jnp.exp(m - mn)
        p = jnp.exp(s - mn)
        l = alpha * l + jnp.sum(p, axis=1, keepdims=True)
        acc = alpha * acc + _dot(p, vn)
        return mn, l, acc

    m, l, acc = lax.fori_loop(0, qi, body, (m, l, acc))
    o_ref[...] = (acc / l).astype(BF16)


def _attention(qk_r, proj, kmean_t):
    s = qk_r.shape[0]
    nb = s // MOBA_BLOCK
    nbp = kmean_t.shape[1]
    v_col0 = 2 * ATT_HEADS
    return pl.pallas_call(
        _attn_kernel,
        grid=(ATT_HEADS, nb),
        in_specs=[pl.BlockSpec((MOBA_BLOCK, LANES), lambda h, i: (i, h)),
                  pl.BlockSpec((s, LANES), lambda h, i: (0, ATT_HEADS + h)),
                  pl.BlockSpec((s, LANES), lambda h, i: (0, v_col0 + h)),
                  pl.BlockSpec((1, nbp, LANES), lambda h, i: (h, 0, 0))],
        out_specs=pl.BlockSpec((MOBA_BLOCK, LANES), lambda h, i: (i, h)),
        out_shape=jax.ShapeDtypeStruct((s, ATT_HEADS * ATT_HEAD_DIM), BF16),
        name="moba_attention",
        compiler_params=_params("parallel", "arbitrary"),
    )(qk_r, qk_r, proj, kmean_t)


def _rwkv_kernel(r_ref, k_ref, v_ref, lw_ref, la_ref, lg_ref, ww2_ref, wa2_ref, wg2_ref,
                 mur_ref, muk_ref, muv_ref, w0_ref, a0_ref, kk_ref, ka_ref, rk_ref, gnw_ref, gnb_ref,
                 o_ref, st_ref, prev_ref):
    j = pl.program_id(1)
    t = SCAN_CHUNK
    n = RWKV_HEAD_DIM
    ts = r_ref.shape[0]

    @pl.when(j == 0)
    def _():
        st_ref[...] = jnp.zeros_like(st_ref)
        prev_ref[...] = jnp.zeros_like(prev_ref)

    rp, kp, vp = r_ref[...].astype(F32), k_ref[...].astype(F32), v_ref[...].astype(F32)
    r = rp + (_shift_rows(rp, prev_ref[0:1, :]) - rp) * mur_ref[...]
    k = kp + (_shift_rows(kp, prev_ref[1:2, :]) - kp) * muk_ref[...]
    v = vp + (_shift_rows(vp, prev_ref[2:3, :]) - vp) * muv_ref[...]
    prev_ref[0:1, :] = rp[ts - 1:ts, :]
    prev_ref[1:2, :] = kp[ts - 1:ts, :]
    prev_ref[2:3, :] = vp[ts - 1:ts, :]

    d = w0_ref[...] + _dot(lw_ref[...], ww2_ref[...])
    logw = -math.exp(-0.5) * _sigmoid(d)
    a = _sigmoid(a0_ref[...] + _dot(la_ref[...], wa2_ref[...]))
    g = _dot(lg_ref[...], wg2_ref[...])

    li = lax.broadcasted_iota(I32, (LANES, LANES), 0)
    lj = lax.broadcasted_iota(I32, (LANES, LANES), 1)
    same_head = (li // n) == (lj // n)
    head_sum = jnp.where(same_head, 1.0, 0.0)

    kk = k * kk_ref[...]
    kk = kk / jnp.maximum(jnp.sqrt(_dot_hi(kk * kk, head_sum)), 1e-12)
    kt = k * (1.0 + (a - 1.0) * ka_ref[...])
    bonus = _dot_hi(r * kt * rk_ref[...], head_sum) * v

    same_blk = (li // t) == (lj // t)
    m_strict = same_blk & (lj < li)
    m_incl = same_blk & (lj <= li)
    ti = lax.broadcasted_iota(I32, (t, t), 0)
    tj = lax.broadcasted_iota(I32, (t, t), 1)
    tril_incl = jnp.where(tj <= ti, 1.0, 0.0)
    lane_a = lax.broadcasted_iota(I32, (t, LANES), 1) < n
    eye = jnp.where(li == lj, 1.0, 0.0)

    def stack_masked(z):
        return jnp.concatenate([jnp.where(lane_a, z, 0.0), jnp.where(lane_a, 0.0, z)], axis=0)

    def stack_plain(z):
        return jnp.concatenate([z, z], axis=0)

    st = st_ref[...]
    ys = []
    for c in range(ts // t):
        sl = slice(c * t, (c + 1) * t)
        lw_c = logw[sl]
        cum = _dot_hi(tril_incl, lw_c)
        g_t = jnp.exp(cum)
        g_inv = jnp.exp(-cum)
        g_prev = jnp.exp(cum - lw_c)
        xa = stack_masked(-kk[sl] * g_prev)
        xr = stack_masked(r[sl] * g_t)
        yb = stack_plain(kk[sl] * a[sl] * g_inv)
        yk = stack_plain(kt[sl] * g_inv)
        vs = stack_masked(v[sl])
        sc = _dot_nt(jnp.concatenate([xa, xr], axis=0), jnp.concatenate([yb, yk], axis=0))
        a_ab = jnp.where(m_strict, sc[:2 * t, :2 * t], 0.0)
        a_ak = jnp.where(m_strict, sc[:2 * t, 2 * t:], 0.0)
        a_rb = jnp.where(m_incl, sc[2 * t:, :2 * t], 0.0)
        a_rk = jnp.where(m_incl, sc[2 * t:, 2 * t:], 0.0)
        inv = eye + a_ab
        pw = a_ab
        for _ in range(int(math.log2(t)) - 1):
            pw = _dot(pw, pw)
            inv = inv + _dot(inv, pw)
        u = _dot(inv, _dot_nt(xa, st) + _dot(a_ak, vs))
        y2 = _dot_nt(xr, st) + _dot(a_rb, u) + _dot(a_rk, vs)
        ys.append(y2[:t] + y2[t:])
        upd = _dot_tn(jnp.concatenate([u, vs], axis=0), jnp.concatenate([yb, yk], axis=0))
        st = jnp.where(same_head, st + upd, 0.0) * g_t[t - 1:t, :]
    st_ref[...] = st
    y = jnp.concatenate(ys, axis=0)

    mean = _dot_hi(y, head_sum) * (1.0 / n)
    yc = y - mean
    var = _dot_hi(yc * yc, head_sum) * (1.0 / n)
    yn = yc * lax.rsqrt(var + GN_EPS) * gnw_ref[...] + gnb_ref[...]
    o_ref[...] = ((yn + bonus) * g).astype(BF16)


def _rwkv(proj, lw, la, lg, w_w2, w_a2, w_g2, mu_r, mu_k, mu_v, w0, a0, k_k, k_a, r_k, gn_w, gn_b,
          col0, width):
    s = proj.shape[0]
    ts = 256
    npair = width // LANES
    cb = col0 // LANES
    row = lambda a: pl.BlockSpec((ts, a.shape[1]), lambda p, j: (j, 0))
    wcol = lambda a: pl.BlockSpec((a.shape[0], LANES), lambda p, j: (0, p))
    vec = pl.BlockSpec((1, LANES), lambda p, j: (0, p))
    return pl.pallas_call(
        _rwkv_kernel,
        grid=(npair, s // ts),
        in_specs=[pl.BlockSpec((ts, LANES), lambda p, j: (j, cb + p)),
                  pl.BlockSpec((ts, LANES), lambda p, j: (j, cb + npair + p)),
                  pl.BlockSpec((ts, LANES), lambda p, j: (j, cb + 2 * npair + p)),
                  row(lw), row(la), row(lg), wcol(w_w2), wcol(w_a2), wcol(w_g2)] + [vec] * 10,
        out_specs=pl.BlockSpec((ts, LANES), lambda p, j: (j, p)),
        out_shape=jax.ShapeDtypeStruct((s, width), BF16),
        scratch_shapes=[pltpu.VMEM((LANES, LANES), F32), pltpu.VMEM((SUBLANES, LANES), F32)],
        name="rwkv7_scan",
        compiler_params=_params("parallel", "arbitrary"),
    )(proj, proj, proj, lw, la, lg, w_w2, w_a2, w_g2, mu_r, mu_k, mu_v, w0, a0, k_k, k_a, r_k, gn_w, gn_b)


def _merge_kernel(oa_ref, or_ref, ga_ref, gr_ref, wa_ref, wr_ref, o_ref, wab_ref, wrb_ref):
    @pl.when(pl.program_id(1) == 0)
    def _():
        wab_ref[...] = wa_ref[...].astype(BF16)
        wrb_ref[...] = wr_ref[...].astype(BF16)

    ua = jnp.dot(oa_ref[...], wab_ref[...], preferred_element_type=F32)
    ur = jnp.dot(or_ref[...], wrb_ref[...], preferred_element_type=F32)
    mix = _sigmoid(ga_ref[...].astype(F32)) * ua + _sigmoid(gr_ref[...].astype(F32)) * ur
    o_ref[...] = mix.astype(BF16)


def _merge(o_att, o_rwkv, proj, w_up_att, w_up_rwkv, gate_col0):
    s, ka = o_att.shape
    kr = o_rwkv.shape[1]
    d = w_up_att.shape[1]
    tm, tn = 512, 1024
    gb = gate_col0 // tn
    return pl.pallas_call(
        _merge_kernel,
        grid=(d // tn, s // tm),
        in_specs=[pl.BlockSpec((tm, ka), lambda j, i: (i, 0)),
                  pl.BlockSpec((tm, kr), lambda j, i: (i, 0)),
                  pl.BlockSpec((tm, tn), lambda j, i: (i, gb + j)),
                  pl.BlockSpec((tm, tn), lambda j, i: (i, gb + d // tn + j)),
                  pl.BlockSpec((ka, tn), lambda j, i: (0, j)),
                  pl.BlockSpec((kr, tn), lambda j, i: (0, j))],
        out_specs=pl.BlockSpec((tm, tn), lambda j, i: (i, j)),
        out_shape=jax.ShapeDtypeStruct((s, d), BF16),
        scratch_shapes=[pltpu.VMEM((ka, tn), BF16), pltpu.VMEM((kr, tn), BF16)],
        name="gated_merge",
        compiler_params=_params("arbitrary", "arbitrary"),
    )(o_att, o_rwkv, proj, proj, w_up_att, w_up_rwkv)


def _route_kernel(x_ref, y_ref, gt_ref, gpost_ref, gpre_ref, sc_ref, sh_ref, wr_ref, br_ref,
                  x1_ref, h2_ref, ei_ref, wt_ref):
    x1 = x_ref[...] + gt_ref[...] * _rms(y_ref[...], gpost_ref[...])
    x1_ref[...] = x1
    h2 = _rms(x1, gpre_ref[...]) * (1.0 + sc_ref[...]) + sh_ref[...]
    h2_ref[...] = h2
    logits = _dot_hi(h2, wr_ref[...]) + br_ref[...]
    lane = lax.broadcasted_iota(I32, logits.shape, 1)
    big = jnp.int32(4 * LANES)
    gmask = (lane >= N_EXPERTS) & (lane < N_EXPERTS + N_GROUPS)
    mg = jnp.max(jnp.where(gmask, logits, -jnp.inf), axis=1, keepdims=True)
    eg = jnp.where(gmask, jnp.exp(logits - mg), 0.0)
    pg = eg / jnp.sum(eg, axis=1, keepdims=True)
    pg_top = jnp.max(pg, axis=1, keepdims=True)
    g_idx = jnp.min(jnp.where(gmask & (pg == pg_top), lane, big), axis=1, keepdims=True) - N_EXPERTS
    emask = (lane >= g_idx * EXPERTS_PER_GROUP) & (lane < (g_idx + 1) * EXPERTS_PER_GROUP)
    me = jnp.max(jnp.where(emask, logits, -jnp.inf), axis=1, keepdims=True)
    ee = jnp.where(emask, jnp.exp(logits - me), 0.0)
    pe = ee / jnp.sum(ee, axis=1, keepdims=True)
    p1 = jnp.max(pe, axis=1, keepdims=True)
    i1 = jnp.min(jnp.where(emask & (pe == p1), lane, big), axis=1, keepdims=True)
    rest = emask & (lane != i1)
    p2 = jnp.max(jnp.where(rest, pe, -jnp.inf), axis=1, keepdims=True)
    i2 = jnp.min(jnp.where(rest & (pe == p2), lane, big), axis=1, keepdims=True)
    den = p1 + p2
    ei_ref[...] = jnp.where(lane == 0, i1, jnp.where(lane == 1, i2, 0))
    wt_ref[...] = jnp.where(lane == 0, pg_top * p1 / den, jnp.where(lane == 1, pg_top * p2 / den, 0.0))


def _route(x, y, gt1, g_post, g_pre, sc2, sh2, w_router, b_router):
    s, d = x.shape
    tm = 256
    vec = pl.BlockSpec((1, d), lambda i: (0, 0))
    rowblk = pl.BlockSpec((tm, d), lambda i: (i, 0))
    small = pl.BlockSpec((tm, LANES), lambda i: (i, 0))
    return pl.pallas_call(
        _route_kernel,
        grid=(s // tm,),
        in_specs=[rowblk, rowblk, vec, vec, vec, vec, vec,
                  pl.BlockSpec((d, LANES), lambda i: (0, 0)),
                  pl.BlockSpec((1, LANES), lambda i: (0, 0))],
        out_specs=[rowblk, rowblk, small, small],
        out_shape=[jax.ShapeDtypeStruct((s, d), F32), jax.ShapeDtypeStruct((s, d), F32),
                   jax.ShapeDtypeStruct((s, LANES), I32), jax.ShapeDtypeStruct((s, LANES), F32)],
        name="norm_route",
        compiler_params=_params("parallel"),
    )(x, y, gt1, g_post, g_pre, sc2, sh2, w_router, b_router)


def _row_copy(src_hbm, row, dst, dst_row, sem):
    return pltpu.make_async_copy(src_hbm.at[pl.ds(row, 1), :], dst.at[pl.ds(dst_row, 1), :], sem)


def _expert_kernel(blk_e_ref, tok_ref, nused_ref, h_hbm, wg_ref, wu_ref, wd_ref, o_ref,
                   xg_ref, wgb_ref, wub_ref, wdb_ref, sem):
    b = pl.program_id(0)
    nused = nused_ref[0]
    rows = EXPERT_BLOCK

    def gather(blk, slot, start):
        def body(i, _):
            cp = _row_copy(h_hbm, tok_ref[blk * rows + i], xg_ref.at[slot], i, sem.at[slot])
            if start:
                cp.start()
            else:
                cp.wait()
            return 0
        lax.fori_loop(0, rows, body, 0)

    @pl.when(b == 0)
    def _():
        gather(0, 0, True)

    @pl.when(b < nused)
    def _():
        slot = b % 2
        gather(b, slot, False)

        @pl.when(b + 1 < nused)
        def _():
            gather(b + 1, 1 - slot, True)

        changed = jnp.logical_or(b == 0, blk_e_ref[b] != blk_e_ref[jnp.maximum(b - 1, 0)])

        @pl.when(changed)
        def _():
            wgb_ref[...] = wg_ref[0].astype(BF16)
            wub_ref[...] = wu_ref[0].astype(BF16)
            wdb_ref[...] = wd_ref[0].astype(BF16)

        xb = xg_ref[slot].astype(BF16)
        hg = jnp.dot(xb, wgb_ref[...], preferred_element_type=F32)
        hu = jnp.dot(xb, wub_ref[...], preferred_element_type=F32)
        hid = hg * _sigmoid(hg) * hu
        o_ref[...] = jnp.dot(hid.astype(BF16), wdb_ref[...], preferred_element_type=F32)

    @pl.when(b >= nused)
    def _():
        o_ref[...] = jnp.zeros_like(o_ref)


def _experts(h2, blk_e, buf_tok, nused, w_gate_e, w_up_e, w_down_e):
    n, d = h2.shape
    n_blk = blk_e.shape[0]
    f = w_gate_e.shape[2]
    grid_spec = pltpu.PrefetchScalarGridSpec(
        num_scalar_prefetch=3,
        grid=(n_blk,),
        in_specs=[pl.BlockSpec(memory_space=pl.ANY),
                  pl.BlockSpec((1, d, f), lambda b, be, tok, nu: (be[b], 0, 0)),
                  pl.BlockSpec((1, d, f), lambda b, be, tok, nu: (be[b], 0, 0)),
                  pl.BlockSpec((1, f, d), lambda b, be, tok, nu: (be[b], 0, 0))],
        out_specs=pl.BlockSpec((EXPERT_BLOCK, d), lambda b, be, tok, nu: (b, 0)),
        scratch_shapes=[pltpu.VMEM((2, EXPERT_BLOCK, d), F32),
                        pltpu.VMEM((d, f), BF16), pltpu.VMEM((d, f), BF16), pltpu.VMEM((f, d), BF16),
                        pltpu.SemaphoreType.DMA((2,))],
    )
    return pl.pallas_call(
        _expert_kernel,
        grid_spec=grid_spec,
        out_shape=jax.ShapeDtypeStruct((n_blk * EXPERT_BLOCK, d), F32),
        name="experts",
        compiler_params=_params("arbitrary"),
    )(blk_e, buf_tok, nused, h2, w_gate_e, w_up_e, w_down_e)


def _combine_kernel(pos_ref, y_hbm, wt_ref, x1_ref, gt_ref, gpost_ref, o_ref, rows_ref, sem):
    i = pl.program_id(0)
    nsteps = pl.num_programs(0)
    tm = x1_ref.shape[0]

    def gather(step, slot, start):
        def body(r, _):
            for kk in range(TOP_K):
                cp = _row_copy(y_hbm, pos_ref[(step * tm + r) * TOP_K + kk], rows_ref.at[slot, kk], r,
                               sem.at[slot])
                if start:
                    cp.start()
                else:
                    cp.wait()
            return 0
        lax.fori_loop(0, tm, body, 0)

    @pl.when(i == 0)
    def _():
        gather(0, 0, True)

    slot = i % 2
    gather(i, slot, False)

    @pl.when(i + 1 < nsteps)
    def _():
        gather(i + 1, 1 - slot, True)

    wt = wt_ref[...]
    y = rows_ref[slot, 0] * wt[:, 0:1] + rows_ref[slot, 1] * wt[:, 1:2]
    o_ref[...] = x1_ref[...] + gt_ref[...] * _rms(y, gpost_ref[...])


def _combine(pos, y_buf, wts, x1, gt2, g_post):
    n, d = x1.shape
    tm = 128
    vec = pl.BlockSpec((1, d), lambda i, p: (0, 0))
    grid_spec = pltpu.PrefetchScalarGridSpec(
        num_scalar_prefetch=1,
        grid=(n // tm,),
        in_specs=[pl.BlockSpec(memory_space=pl.ANY),
                  pl.BlockSpec((tm, LANES), lambda i, p: (i, 0)),
                  pl.BlockSpec((tm, d), lambda i, p: (i, 0)), vec, vec],
        out_specs=pl.BlockSpec((tm, d), lambda i, p: (i, 0)),
        scratch_shapes=[pltpu.VMEM((2, TOP_K, tm, d), F32), pltpu.SemaphoreType.DMA((2,))],
    )
    return pl.pallas_call(
        _combine_kernel,
        grid_spec=grid_spec,
        out_shape=jax.ShapeDtypeStruct((n, d), F32),
        name="combine",
        compiler_params=_params("arbitrary"),
    )(pos, y_buf, wts, x1, gt2, g_post)


def _dispatch_tables(e_idx):
    n = e_idx.shape[0]
    p = n * TOP_K
    n_blk = (p + N_EXPERTS * (EXPERT_BLOCK - 1) + EXPERT_BLOCK - 1) // EXPERT_BLOCK
    e_flat = e_idx.reshape(p)
    tok_flat = jnp.repeat(jnp.arange(n, dtype=I32), TOP_K)
    order = jnp.argsort(e_flat)
    e_s, tok_s = e_flat[order], tok_flat[order]
    counts = jnp.zeros((N_EXPERTS,), I32).at[e_flat].add(1)
    start = jnp.cumsum(counts) - counts
    pcounts = (counts + EXPERT_BLOCK - 1) // EXPERT_BLOCK * EXPERT_BLOCK
    pend = jnp.cumsum(pcounts)
    pstart = pend - pcounts
    dest = pstart[e_s] + (jnp.arange(p, dtype=I32) - start[e_s])
    buf_tok = jnp.zeros((n_blk * EXPERT_BLOCK,), I32).at[dest].set(tok_s)
    pos = jnp.zeros((p,), I32).at[order].set(dest)
    nused = (pend[-1] // EXPERT_BLOCK).astype(I32)
    blk = jnp.minimum(jnp.arange(n_blk, dtype=I32), nused - 1)
    blk_e = jnp.clip(jnp.searchsorted(pend, blk * EXPERT_BLOCK, side='right'), 0, N_EXPERTS - 1).astype(I32)
    return blk_e, buf_tok, nused.reshape(1), pos


def _rope_tables(s):
    half = ROPE_DIM // 2
    inv = ROPE_THETA ** (-jnp.arange(half, dtype=F32) / half)
    ang = jnp.arange(s, dtype=F32)[:, None] * inv[None, :]
    cos, sin = jnp.cos(ang), jnp.sin(ang)
    pad = jnp.zeros((s, LANES - ROPE_DIM), F32)
    zero = jnp.zeros((s, half), F32)
    cos_t = jnp.concatenate([cos, cos, pad + 1.0], axis=1)
    sin1_t = jnp.concatenate([-sin, zero, pad], axis=1)
    sin2_t = jnp.concatenate([zero, sin, pad], axis=1)
    return cos_t, sin1_t, sin2_t


def _layer(x, c_col, w_ada, b_ada, g_pre_mix, g_post_mix, g_pre_ffn, g_post_ffn, w_in, mu_r, mu_k, mu_v,
           mu_w, mu_a, mu_g, w0, w_w1, w_w2, a0, w_a1, w_a2, w_g1, w_g2, k_k, k_a, r_k, gn_w, gn_b,
           w_up_att, w_up_rwkv, w_o, w_rg, b_rg, w_re, b_re, w_gate_e, w_up_e, w_down_e):
    s, d = x.shape
    att_w = ATT_HEADS * ATT_HEAD_DIM
    rwkv_w = w_up_rwkv.shape[0]
    row = lambda a: a.reshape(1, -1)

    ada = _ada(c_col, w_ada, row(b_ada))
    sh1, sc1, gt1, sh2, sc2, gt2 = (ada[:, i * d:(i + 1) * d] for i in range(6))

    h, lw, la, lg = _prenorm(x, row(g_pre_mix), sc1, sh1, row(mu_w), row(mu_a), row(mu_g), w_w1, w_a1, w_g1)
    proj = _matmul(h, w_in, BF16)

    qk_r, kmean = _rope(proj, *_rope_tables(s))
    nb = s // MOBA_BLOCK
    nbp = -(-nb // SUBLANES) * SUBLANES
    kmean_t = jnp.pad(kmean.transpose(1, 0, 2), ((0, 0), (0, nbp - nb), (0, 0)))
    o_att = _attention(qk_r, proj, kmean_t)

    o_rwkv = _rwkv(proj, lw, la, lg, w_w2, w_a2, w_g2, row(mu_r), row(mu_k), row(mu_v), row(w0), row(a0),
                   row(k_k), row(k_a), row(r_k), row(gn_w), row(gn_b), col0=3 * att_w, width=rwkv_w)

    mix = _merge(o_att, o_rwkv, proj, w_up_att, w_up_rwkv, gate_col0=3 * att_w + 3 * rwkv_w)
    y = _matmul(mix, w_o, F32)

    w_router = jnp.pad(jnp.concatenate([w_re, w_rg], axis=1), ((0, 0), (0, LANES - N_EXPERTS - N_GROUPS)))
    b_router = jnp.pad(jnp.concatenate([b_re, b_rg]), (0, LANES - N_EXPERTS - N_GROUPS)).reshape(1, LANES)
    x1, h2, e_idx, wts = _route(x, y, gt1, row(g_post_mix), row(g_pre_ffn), sc2, sh2, w_router, b_router)

    blk_e, buf_tok, nused, pos = _dispatch_tables(e_idx[:, :TOP_K])
    y_buf = _experts(h2, blk_e, buf_tok, nused, w_gate_e, w_up_e, w_down_e)
    return _combine(pos, y_buf, wts, x1, gt2, row(g_post_ffn))


def kernel(x, c, w_ada, b_ada, g_pre_mix, g_post_mix, g_pre_ffn, g_post_ffn, w_in, mu_r, mu_k, mu_v, mu_w, mu_a, mu_g, w0, w_w1, w_w2, a0, w_a1, w_a2, w_g1, w_g2, k_k, k_a, r_k, gn_w, gn_b, w_up_att, w_up_rwkv, w_o, w_rg, b_rg, w_re, b_re, w_gate_e, w_up_e, w_down_e):
    b, s, d = x.shape
    assert b == 1, "one sequence per call"
    params = (w_ada, b_ada, g_pre_mix, g_post_mix, g_pre_ffn, g_post_ffn, w_in, mu_r, mu_k, mu_v, mu_w, mu_a,
              mu_g, w0, w_w1, w_w2, a0, w_a1, w_a2, w_g1, w_g2, k_k, k_a, r_k, gn_w, gn_b, w_up_att,
              w_up_rwkv, w_o, w_rg, b_rg, w_re, b_re, w_gate_e, w_up_e, w_down_e)
    xs = x.reshape(s, d)
    c_col = c.reshape(d, 1)
    for l in range(w_ada.shape[0]):
        xs = _layer(xs, c_col, *(p[l] for p in params))
    return xs.reshape(b, s, d)
```

```python
import math

import jax
import jax.numpy as jnp
from jax import lax
from jax.experimental import pallas as pl
from jax.experimental.pallas import tpu as pltpu

F32 = jnp.float32
BF16 = jnp.bfloat16
I32 = jnp.int32
HI = lax.Precision.HIGHEST

LANES = 128
SUBLANES = 8
VMEM_LIMIT = 56 * 1024 * 1024

ATT_HEADS = 8
ATT_HEAD_DIM = 128
MOBA_BLOCK = 256
MOBA_TOPK = 3
ATT_GROUP = 4
ATT_HEADS_PER_STEP = 2
ROPE_THETA = 500000.0
ROPE_DIM = ATT_HEAD_DIM // 4
RWKV_HEAD_DIM = 64
GN_EPS = 64e-5
N_GROUPS = 8
EXPERTS_PER_GROUP = 8
N_EXPERTS = N_GROUPS * EXPERTS_PER_GROUP
TOP_K = 2
EXPERT_BLOCK = 128
RMS_EPS = 1e-6
NEG = -1e30
SCAN_CHUNK = 64
Q_SCALE = ATT_HEAD_DIM ** -0.5 * math.log2(math.e)


def _params(*sem):
    return pltpu.CompilerParams(dimension_semantics=sem, vmem_limit_bytes=VMEM_LIMIT)


def _rms(z, g):
    return z * lax.rsqrt(jnp.mean(z * z, axis=-1, keepdims=True) + RMS_EPS) * g


def _sigmoid(z):
    return 1.0 / (1.0 + jnp.exp(-z))


def _dot(a, b):
    return jnp.dot(a.astype(BF16), b.astype(BF16), preferred_element_type=F32)


def _dot_nt(a, b):
    return lax.dot_general(a.astype(BF16), b.astype(BF16), (((1,), (1,)), ((), ())),
                           preferred_element_type=F32)


def _dot_tn(a, b):
    return lax.dot_general(a.astype(BF16), b.astype(BF16), (((0,), (0,)), ((), ())),
                           preferred_element_type=F32)


def _dot_hi(a, b):
    return jnp.dot(a, b, precision=HI, preferred_element_type=F32)


def _split3(a):
    hi = a.astype(BF16)
    r1 = a - hi.astype(F32)
    mid = r1.astype(BF16)
    lo = (r1 - mid.astype(F32)).astype(BF16)
    return hi, mid, lo


def _dot_split(a, b01):
    b = b01.astype(BF16)
    hi, mid, _ = _split3(a)
    return jnp.dot(jnp.concatenate([hi, mid], axis=1), jnp.concatenate([b, b], axis=0),
                   preferred_element_type=F32)


def _dot_split_t(b01, a):
    b = b01.astype(BF16)
    return jnp.dot(jnp.concatenate([b, b, b], axis=1), jnp.concatenate(_split3(a), axis=0),
                   preferred_element_type=F32)


def _shift_rows(z, prev_row):
    rolled = pltpu.roll(z, 1, 0)
    row = lax.broadcasted_iota(I32, z.shape, 0)
    return jnp.where(row == 0, prev_row, rolled)


def _ada_kernel(c_ref, w_ref, b_ref, o_ref):
    o_ref[...] = jnp.sum(c_ref[...] * w_ref[...], axis=0, keepdims=True) + b_ref[...]


def _ada(c_col, w_ada, b_ada):
    d, n = w_ada.shape
    tn = 1024
    return pl.pallas_call(
        _ada_kernel,
        grid=(n // tn,),
        in_specs=[pl.BlockSpec((d, 1), lambda j: (0, 0)),
                  pl.BlockSpec((d, tn), lambda j: (0, j)),
                  pl.BlockSpec((1, tn), lambda j: (0, j))],
        out_specs=pl.BlockSpec((1, tn), lambda j: (0, j)),
        out_shape=jax.ShapeDtypeStruct((1, n), F32),
        name="ada",
        compiler_params=_params("parallel"),
    )(c_col, w_ada, b_ada)


def _prenorm_kernel(x_ref, xp_ref, g_ref, sc_ref, sh_ref, muw_ref, mua_ref, mug_ref,
                    ww1_ref, wa1_ref, wg1_ref, h_ref, lw_ref, la_ref, lg_ref):
    i = pl.program_id(0)
    g, sc, sh = g_ref[...], sc_ref[...], sh_ref[...]
    h = _rms(x_ref[...], g) * (1.0 + sc) + sh
    hp = _rms(xp_ref[SUBLANES - 1:SUBLANES, :], g) * (1.0 + sc) + sh
    hp = jnp.where(i == 0, 0.0, hp)
    dh = _shift_rows(h, hp) - h
    h_ref[...] = h.astype(BF16)
    lw_ref[...] = jnp.tanh(_dot(h + dh * muw_ref[...], ww1_ref[...]))
    la_ref[...] = _dot(h + dh * mua_ref[...], wa1_ref[...])
    lg_ref[...] = _sigmoid(_dot(h + dh * mug_ref[...], wg1_ref[...]))


def _prenorm(x, g, sc, sh, mu_w, mu_a, mu_g, w_w1, w_a1, w_g1):
    s, d = x.shape
    tm = 256
    rpb = tm // SUBLANES
    vec = pl.BlockSpec((1, d), lambda i: (0, 0))
    full = lambda a: pl.BlockSpec(a.shape, lambda i: (0, 0))
    lw, la, lg = w_w1.shape[1], w_a1.shape[1], w_g1.shape[1]
    return pl.pallas_call(
        _prenorm_kernel,
        grid=(s // tm,),
        in_specs=[pl.BlockSpec((tm, d), lambda i: (i, 0)),
                  pl.BlockSpec((SUBLANES, d), lambda i: (jnp.maximum(i * rpb - 1, 0), 0)),
                  vec, vec, vec, vec, vec, vec, full(w_w1), full(w_a1), full(w_g1)],
        out_specs=[pl.BlockSpec((tm, d), lambda i: (i, 0)),
                   pl.BlockSpec((tm, lw), lambda i: (i, 0)),
                   pl.BlockSpec((tm, la), lambda i: (i, 0)),
                   pl.BlockSpec((tm, lg), lambda i: (i, 0))],
        out_shape=[jax.ShapeDtypeStruct((s, d), BF16),
                   jax.ShapeDtypeStruct((s, lw), F32),
                   jax.ShapeDtypeStruct((s, la), F32),
                   jax.ShapeDtypeStruct((s, lg), F32)],
        name="prenorm_lora",
        compiler_params=_params("parallel"),
    )(x, x, g, sc, sh, mu_w, mu_a, mu_g, w_w1, w_a1, w_g1)


def _mm_kernel(a_ref, w_ref, o_ref, wb_ref):
    @pl.when(pl.program_id(1) == 0)
    def _():
        wb_ref[...] = w_ref[...].astype(BF16)

    o_ref[...] = jnp.dot(a_ref[...], wb_ref[...], preferred_element_type=F32).astype(o_ref.dtype)


def _matmul(a, w, out_dtype, tm=512, tn=1024):
    m, k = a.shape
    n = w.shape[1]
    tn = min(tn, n)
    return pl.pallas_call(
        _mm_kernel,
        grid=(n // tn, m // tm),
        in_specs=[pl.BlockSpec((tm, k), lambda j, i: (i, 0)),
                  pl.BlockSpec((k, tn), lambda j, i: (0, j))],
        out_specs=pl.BlockSpec((tm, tn), lambda j, i: (i, j)),
        out_shape=jax.ShapeDtypeStruct((m, n), out_dtype),
        scratch_shapes=[pltpu.VMEM((k, tn), BF16)],
        name="matmul",
        compiler_params=_params("arbitrary", "arbitrary"),
    )(a, w)


def _rope_gate_kernel(p_ref, c_ref, s1_ref, s2_ref, qa_ref, ka_ref, vt_ref, km_ref):
    i = pl.program_id(0)
    bs = MOBA_BLOCK
    nbp = km_ref.shape[1]

    @pl.when(i == 0)
    def _():
        km_ref[...] = jnp.zeros_like(km_ref)

    c, s1, s2 = c_ref[...], s1_ref[...], s2_ref[...]

    def rope(z):
        return z * c + pltpu.roll(z, LANES - ROPE_DIM // 2, 1) * s1 + pltpu.roll(z, ROPE_DIM // 2, 1) * s2

    row = lax.broadcasted_iota(I32, (nbp, bs), 0)
    lane = lax.broadcasted_iota(I32, (bs, LANES), 1)
    onehot = jnp.where(lane == i, 1.0, 0.0).astype(BF16)
    for h in range(ATT_HEADS):
        q = rope(p_ref[:, h * LANES:(h + 1) * LANES].astype(F32))
        k = rope(p_ref[:, (ATT_HEADS + h) * LANES:(ATT_HEADS + h + 1) * LANES].astype(F32))
        g = lax.dot_general(km_ref[h], q, (((1,), (1,)), ((), ())), precision=HI, preferred_element_type=F32)
        g = jnp.where(row < i, g, NEG)
        sel_t = jnp.zeros(g.shape, F32)
        for _ in range(MOBA_TOPK):
            mx = jnp.max(g, axis=0, keepdims=True)
            idx = jnp.min(jnp.where(g == mx, row, nbp), axis=0, keepdims=True)
            hit = row == idx
            sel_t = jnp.where(hit & (row < i), 1.0, sel_t)
            g = jnp.where(hit, -jnp.inf, g)
        if nbp < LANES:
            sel_t = jnp.concatenate([sel_t, jnp.zeros((LANES - nbp, bs), F32)], axis=0)
        w = 2 * LANES
        qa_ref[h, :LANES, :] = (q * Q_SCALE).T.astype(BF16)
        qa_ref[h, LANES:, :] = jnp.where(sel_t > 0.5, 0.0, NEG).astype(BF16)
        ka_ref[:, h * w:h * w + LANES] = k.astype(BF16)
        ka_ref[:, h * w + LANES:(h + 1) * w] = onehot
        v = p_ref[:, (2 * ATT_HEADS + h) * LANES:(2 * ATT_HEADS + h + 1) * LANES].astype(F32)
        vt_ref[h, 0] = v.T.astype(BF16)
        km_ref[h, pl.ds(i, 1), :] = jnp.mean(k, axis=0, keepdims=True)


def _rope_gate(proj, cos_t, sin1_t, sin2_t):
    s = proj.shape[0]
    nb = s // MOBA_BLOCK
    assert nb <= LANES
    nbp = -(-nb // SUBLANES) * SUBLANES
    w_in = 3 * ATT_HEADS * ATT_HEAD_DIM
    w_out = 2 * ATT_HEADS * LANES
    tab = pl.BlockSpec((MOBA_BLOCK, LANES), lambda i: (i, 0))
    return pl.pallas_call(
        _rope_gate_kernel,
        grid=(nb,),
        in_specs=[pl.BlockSpec((MOBA_BLOCK, w_in), lambda i: (i, 0)), tab, tab, tab],
        out_specs=[pl.BlockSpec((ATT_HEADS, 2 * LANES, MOBA_BLOCK), lambda i: (0, 0, i)),
                   pl.BlockSpec((MOBA_BLOCK, w_out), lambda i: (i, 0)),
                   pl.BlockSpec((ATT_HEADS, 1, LANES, MOBA_BLOCK), lambda i: (0, i, 0, 0))],
        out_shape=[jax.ShapeDtypeStruct((ATT_HEADS, 2 * LANES, s), BF16),
                   jax.ShapeDtypeStruct((s, w_out), BF16),
                   jax.ShapeDtypeStruct((ATT_HEADS, nb, LANES, MOBA_BLOCK), BF16)],
        scratch_shapes=[pltpu.VMEM((ATT_HEADS, nbp, LANES), F32)],
        name="rope_gate",
        compiler_params=_params("arbitrary"),
    )(proj, cos_t, sin1_t, sin2_t)


def _attn_kernel(qa_ref, ka_ref, vt_ref, o_ref, s_ref):
    qi = pl.program_id(1)
    bs = MOBA_BLOCK
    grp = ATT_GROUP * bs
    w = 2 * LANES
    heads = range(ATT_HEADS_PER_STEP)

    n_groups = ka_ref.shape[0] // grp
    assert n_groups % 2 == 0

    def issue_scores(g, buf):
        base = pl.multiple_of(jnp.minimum(g, n_groups - 1) * grp, grp)
        for h in heads:
            s_ref[buf, h] = jnp.dot(ka_ref[pl.ds(base, grp), h * w:(h + 1) * w], qa_ref[h],
                                    preferred_element_type=F32)

    issue_scores(0, 0)

    own = pl.multiple_of(qi * bs, bs)
    k_i = lax.broadcasted_iota(I32, (bs, bs), 0)
    q_i = lax.broadcasted_iota(I32, (bs, bs), 1)
    carry = []
    for h in heads:
        s = jnp.dot(ka_ref[pl.ds(own, bs), h * w:h * w + LANES], qa_ref[h, :LANES, :],
                    preferred_element_type=F32)
        s = jnp.where(k_i <= q_i, s, NEG)
        m = jnp.max(s, axis=0, keepdims=True)
        p = jnp.exp2(s - m)
        l = jnp.sum(p, axis=0, keepdims=True)
        acc = jnp.dot(vt_ref[h, qi], p.astype(BF16), preferred_element_type=F32)
        carry += [m, l, acc]

    def absorb(g, buf, carry):
        out = []
        for h in heads:
            m, l, acc = carry[3 * h:3 * h + 3]
            s = s_ref[buf, h]
            mn = jnp.maximum(m, jnp.max(s, axis=0, keepdims=True))
            alpha = jnp.exp2(m - mn)
            p = jnp.exp2(s - mn)
            l = alpha * l + jnp.sum(p, axis=0, keepdims=True)
            p = p.astype(BF16)
            acc = alpha * acc
            for jb in range(ATT_GROUP):
                acc = acc + jnp.dot(vt_ref[h, g * ATT_GROUP + jb], p[jb * bs:(jb + 1) * bs],
                                    preferred_element_type=F32)
            out += [mn, l, acc]
        return out

    def body(t, carry):
        issue_scores(2 * t + 1, 1)
        carry = absorb(2 * t, 0, carry)
        issue_scores(2 * t + 2, 0)
        return tuple(absorb(2 * t + 1, 1, carry))

    n_used = (qi + ATT_GROUP - 1) // ATT_GROUP
    carry = lax.fori_loop(0, (n_used + 1) // 2, body, tuple(carry))
    for h in heads:
        m, l, acc = carry[3 * h:3 * h + 3]
        o_ref[:, h * LANES:(h + 1) * LANES] = (acc / l).T.astype(BF16)


def _attention(q_aug_t, k_aug, v_t):
    s = k_aug.shape[0]
    assert s % (ATT_GROUP * MOBA_BLOCK) == 0
    nb = s // MOBA_BLOCK
    hps = ATT_HEADS_PER_STEP
    once = pl.Buffered(1)
    return pl.pallas_call(
        _attn_kernel,
        grid=(ATT_HEADS // hps, nb),
        in_specs=[pl.BlockSpec((hps, 2 * LANES, MOBA_BLOCK), lambda h, i: (h, 0, i)),
                  pl.BlockSpec((s, hps * 2 * LANES), lambda h, i: (0, h), pipeline_mode=once),
                  pl.BlockSpec((hps, nb, LANES, MOBA_BLOCK), lambda h, i: (h, 0, 0, 0), pipeline_mode=once)],
        out_specs=pl.BlockSpec((MOBA_BLOCK, hps * LANES), lambda h, i: (i, h)),
        out_shape=jax.ShapeDtypeStruct((s, ATT_HEADS * ATT_HEAD_DIM), BF16),
        scratch_shapes=[pltpu.VMEM((2, hps, ATT_GROUP * MOBA_BLOCK, MOBA_BLOCK), F32)],
        name="moba_attention",
        compiler_params=_params("parallel", "arbitrary"),
    )(q_aug_t, k_aug, v_t)


def _rwkv_kernel(r_ref, k_ref, v_ref, lw_ref, la_ref, lg_ref, ww2_ref, wa2_ref, wg2_ref,
                 mur_ref, muk_ref, muv_ref, w0_ref, a0_ref, kk_ref, ka_ref, rk_ref, gnw_ref, gnb_ref,
                 o_ref, st_ref, prev_ref):
    j = pl.program_id(1)
    t = SCAN_CHUNK
    n = RWKV_HEAD_DIM
    ts = r_ref.shape[0]
    chunks = range(ts // t)

    @pl.when(j == 0)
    def _():
        st_ref[...] = jnp.zeros_like(st_ref)
        prev_ref[...] = jnp.zeros_like(prev_ref)

    rp, kp, vp = r_ref[...].astype(F32), k_ref[...].astype(F32), v_ref[...].astype(F32)
    r = rp + (_shift_rows(rp, prev_ref[0:1, :]) - rp) * mur_ref[...]
    k = kp + (_shift_rows(kp, prev_ref[1:2, :]) - kp) * muk_ref[...]
    v = vp + (_shift_rows(vp, prev_ref[2:3, :]) - vp) * muv_ref[...]
    prev_ref[0:1, :] = rp[ts - 1:ts, :]
    prev_ref[1:2, :] = kp[ts - 1:ts, :]
    prev_ref[2:3, :] = vp[ts - 1:ts, :]

    d = w0_ref[...] + _dot(lw_ref[...], ww2_ref[...])
    logw = -math.exp(-0.5) * _sigmoid(d)
    a = _sigmoid(a0_ref[...] + _dot(la_ref[...], wa2_ref[...]))
    g = _dot(lg_ref[...], wg2_ref[...])

    li = lax.broadcasted_iota(I32, (LANES, LANES), 0)
    lj = lax.broadcasted_iota(I32, (LANES, LANES), 1)
    same_head = (li // n) == (lj // n)
    head_sum = jnp.where(same_head, 1.0, 0.0)

    kk = k * kk_ref[...]
    kk = kk / jnp.maximum(jnp.sqrt(_dot_split(kk * kk, head_sum)), 1e-12)
    kt = k * (1.0 + (a - 1.0) * ka_ref[...])
    bonus = _dot_split(r * kt * rk_ref[...], head_sum) * v

    same_blk = (li // t) == (lj // t)
    m_strict = same_blk & (lj < li)
    m_incl = same_blk & (lj <= li)
    ti = lax.broadcasted_iota(I32, (t, t), 0)
    tj = lax.broadcasted_iota(I32, (t, t), 1)
    tril_incl = jnp.where(tj <= ti, 1.0, 0.0)
    lane_a = lax.broadcasted_iota(I32, (t, LANES), 1) < n
    eye = jnp.where(li == lj, 1.0, 0.0)

    def stack_masked(z):
        return jnp.concatenate([jnp.where(lane_a, z, 0.0), jnp.where(lane_a, 0.0, z)], axis=0)

    def stack_plain(z):
        return jnp.concatenate([z, z], axis=0)

    def rows(z, c):
        return z[c * t:(c + 1) * t]

    cum = jnp.concatenate([_dot_split_t(tril_incl, rows(logw, c)) for c in chunks], axis=0)
    g_t = jnp.exp(cum)
    g_inv = jnp.exp(-cum)
    xa_f = -kk * jnp.exp(cum - logw)
    xr_f = r * g_t
    yb_f = kk * a * g_inv
    yk_f = kt * g_inv

    xa = [stack_masked(rows(xa_f, c)) for c in chunks]
    xr = [stack_masked(rows(xr_f, c)) for c in chunks]
    yb = [stack_plain(rows(yb_f, c)) for c in chunks]
    vs = [stack_masked(rows(v, c)) for c in chunks]
    ybk = [jnp.concatenate([yb[c], stack_plain(rows(yk_f, c))], axis=0) for c in chunks]
    sc = [_dot_nt(jnp.concatenate([xa[c], xr[c]], axis=0), ybk[c]) for c in chunks]
    a_ab = [jnp.where(m_strict, sc[c][:2 * t, :2 * t], 0.0) for c in chunks]
    a_ak = [jnp.where(m_strict, sc[c][:2 * t, 2 * t:], 0.0) for c in chunks]
    a_rb = [jnp.where(m_incl, sc[c][2 * t:, :2 * t], 0.0) for c in chunks]
    a_rk = [jnp.where(m_incl, sc[c][2 * t:, 2 * t:], 0.0) for c in chunks]
    inv = [eye + a_ab[c] for c in chunks]
    pw = [_dot(a_ab[c], a_ab[c]) for c in chunks]
    for _ in range(int(math.log2(t)) - 2):
        both = [_dot(pw[c], jnp.concatenate([pw[c], inv[c]], axis=1)) for c in chunks]
        pw = [both[c][:, :LANES] for c in chunks]
        inv = [inv[c] + both[c][:, LANES:] for c in chunks]
    inv = [inv[c] + _dot(pw[c], inv[c]) for c in chunks]
    av = [_dot(a_ak[c], vs[c]) for c in chunks]
    mw = [_dot(inv[c], jnp.concatenate([xa[c], av[c]], axis=1)) for c in chunks]
    zeros = jnp.zeros((2 * t, LANES), F32)
    rw = [_dot(jnp.concatenate([a_rb[c], a_rk[c]], axis=1),
               jnp.concatenate([mw[c], jnp.concatenate([zeros, vs[c]], axis=1)], axis=0)) for c in chunks]
    r_hat = [xr[c] + rw[c][:, :LANES] for c in chunks]
    y_hat = [rw[c][:, LANES:] for c in chunks]
    pm = [jnp.where(same_head, _dot_tn(mw[c][:, :LANES], yb[c]), 0.0) for c in chunks]
    qm = [jnp.where(same_head, _dot_tn(jnp.concatenate([mw[c][:, LANES:], vs[c]], axis=0), ybk[c]), 0.0)
          for c in chunks]

    st = st_ref[...]
    ys = []
    for c in chunks:
        y2 = _dot_nt(r_hat[c], st) + y_hat[c]
        ys.append(y2[:t] + y2[t:])
        st = (st + _dot(st, pm[c]) + qm[c]) * g_t[(c + 1) * t - 1:(c + 1) * t, :]
    st_ref[...] = st
    y = jnp.concatenate(ys, axis=0)

    mean = _dot_split(y, head_sum) * (1.0 / n)
    yc = y - mean
    var = _dot_split(yc * yc, head_sum) * (1.0 / n)
    yn = yc * lax.rsqrt(var + GN_EPS) * gnw_ref[...] + gnb_ref[...]
    o_ref[...] = ((yn + bonus) * g).astype(BF16)


def _rwkv(proj, lw, la, lg, w_w2, w_a2, w_g2, mu_r, mu_k, mu_v, w0, a0, k_k, k_a, r_k, gn_w, gn_b,
          col0, width):
    s = proj.shape[0]
    ts = 512
    npair = width // LANES
    cb = col0 // LANES
    row = lambda a: pl.BlockSpec((ts, a.shape[1]), lambda p, j: (j, 0))
    wcol = lambda a: pl.BlockSpec((a.shape[0], LANES), lambda p, j: (0, p))
    vec = pl.BlockSpec((1, LANES), lambda p, j: (0, p))
    return pl.pallas_call(
        _rwkv_kernel,
        grid=(npair, s // ts),
        in_specs=[pl.BlockSpec((ts, LANES), lambda p, j: (j, cb + p)),
                  pl.BlockSpec((ts, LANES), lambda p, j: (j, cb + npair + p)),
                  pl.BlockSpec((ts, LANES), lambda p, j: (j, cb + 2 * npair + p)),
                  row(lw), row(la), row(lg), wcol(w_w2), wcol(w_a2), wcol(w_g2)] + [vec] * 10,
        out_specs=pl.BlockSpec((ts, LANES), lambda p, j: (j, p)),
        out_shape=jax.ShapeDtypeStruct((s, width), BF16),
        scratch_shapes=[pltpu.VMEM((LANES, LANES), F32), pltpu.VMEM((SUBLANES, LANES), F32)],
        name="rwkv7_scan",
        compiler_params=_params("parallel", "arbitrary"),
    )(proj, proj, proj, lw, la, lg, w_w2, w_a2, w_g2, mu_r, mu_k, mu_v, w0, a0, k_k, k_a, r_k, gn_w, gn_b)


def _merge_kernel(oa_ref, or_ref, ga_ref, gr_ref, wa_ref, wr_ref, o_ref, wab_ref, wrb_ref):
    @pl.when(pl.program_id(1) == 0)
    def _():
        wab_ref[...] = wa_ref[...].astype(BF16)
        wrb_ref[...] = wr_ref[...].astype(BF16)

    ua = jnp.dot(oa_ref[...], wab_ref[...], preferred_element_type=F32)
    ur = jnp.dot(or_ref[...], wrb_ref[...], preferred_element_type=F32)
    mix = _sigmoid(ga_ref[...].astype(F32)) * ua + _sigmoid(gr_ref[...].astype(F32)) * ur
    o_ref[...] = mix.astype(BF16)


def _merge(o_att, o_rwkv, proj, w_up_att, w_up_rwkv, gate_col0):
    s, ka = o_att.shape
    kr = o_rwkv.shape[1]
    d = w_up_att.shape[1]
    tm, tn = 512, 1024
    gb = gate_col0 // tn
    return pl.pallas_call(
        _merge_kernel,
        grid=(d // tn, s // tm),
        in_specs=[pl.BlockSpec((tm, ka), lambda j, i: (i, 0)),
                  pl.BlockSpec((tm, kr), lambda j, i: (i, 0)),
                  pl.BlockSpec((tm, tn), lambda j, i: (i, gb + j)),
                  pl.BlockSpec((tm, tn), lambda j, i: (i, gb + d // tn + j)),
                  pl.BlockSpec((ka, tn), lambda j, i: (0, j)),
                  pl.BlockSpec((kr, tn), lambda j, i: (0, j))],
        out_specs=pl.BlockSpec((tm, tn), lambda j, i: (i, j)),
        out_shape=jax.ShapeDtypeStruct((s, d), BF16),
        scratch_shapes=[pltpu.VMEM((ka, tn), BF16), pltpu.VMEM((kr, tn), BF16)],
        name="gated_merge",
        compiler_params=_params("arbitrary", "arbitrary"),
    )(o_att, o_rwkv, proj, proj, w_up_att, w_up_rwkv)


def _route_kernel(x_ref, y_ref, gt_ref, gpost_ref, gpre_ref, sc_ref, sh_ref, wr_ref, br_ref,
                  x1_ref, h2_ref, ei_ref, wt_ref):
    x1 = x_ref[...] + gt_ref[...] * _rms(y_ref[...], gpost_ref[...])
    x1_ref[...] = x1
    h2 = _rms(x1, gpre_ref[...]) * (1.0 + sc_ref[...]) + sh_ref[...]
    h2_ref[...] = h2
    logits = _dot_hi(h2, wr_ref[...]) + br_ref[...]
    lane = lax.broadcasted_iota(I32, logits.shape, 1)
    big = jnp.int32(4 * LANES)
    gmask = (lane >= N_EXPERTS) & (lane < N_EXPERTS + N_GROUPS)
    mg = jnp.max(jnp.where(gmask, logits, -jnp.inf), axis=1, keepdims=True)
    eg = jnp.where(gmask, jnp.exp(logits - mg), 0.0)
    pg = eg / jnp.sum(eg, axis=1, keepdims=True)
    pg_top = jnp.max(pg, axis=1, keepdims=True)
    g_idx = jnp.min(jnp.where(gmask & (pg == pg_top), lane, big), axis=1, keepdims=True) - N_EXPERTS
    emask = (lane >= g_idx * EXPERTS_PER_GROUP) & (lane < (g_idx + 1) * EXPERTS_PER_GROUP)
    me = jnp.max(jnp.where(emask, logits, -jnp.inf), axis=1, keepdims=True)
    ee = jnp.where(emask, jnp.exp(logits - me), 0.0)
    pe = ee / jnp.sum(ee, axis=1, keepdims=True)
    p1 = jnp.max(pe, axis=1, keepdims=True)
    i1 = jnp.min(jnp.where(emask & (pe == p1), lane, big), axis=1, keepdims=True)
    rest = emask & (lane != i1)
    p2 = jnp.max(jnp.where(rest, pe, -jnp.inf), axis=1, keepdims=True)
    i2 = jnp.min(jnp.where(rest & (pe == p2), lane, big), axis=1, keepdims=True)
    den = p1 + p2
    ei_ref[...] = jnp.where(lane == 0, i1, jnp.where(lane == 1, i2, 0))
    wt_ref[...] = jnp.where(lane == 0, pg_top * p1 / den, jnp.where(lane == 1, pg_top * p2 / den, 0.0))


def _route(x, y, gt1, g_post, g_pre, sc2, sh2, w_router, b_router):
    s, d = x.shape
    tm = 256
    vec = pl.BlockSpec((1, d), lambda i: (0, 0))
    rowblk = pl.BlockSpec((tm, d), lambda i: (i, 0))
    small = pl.BlockSpec((tm, LANES), lambda i: (i, 0))
    return pl.pallas_call(
        _route_kernel,
        grid=(s // tm,),
        in_specs=[rowblk, rowblk, vec, vec, vec, vec, vec,
                  pl.BlockSpec((d, LANES), lambda i: (0, 0)),
                  pl.BlockSpec((1, LANES), lambda i: (0, 0))],
        out_specs=[rowblk, rowblk, small, small],
        out_shape=[jax.ShapeDtypeStruct((s, d), F32), jax.ShapeDtypeStruct((s, d), F32),
                   jax.ShapeDtypeStruct((s, LANES), I32), jax.ShapeDtypeStruct((s, LANES), F32)],
        name="norm_route",
        compiler_params=_params("parallel"),
    )(x, y, gt1, g_post, g_pre, sc2, sh2, w_router, b_router)


def _row_copy(src_hbm, row, dst, dst_row, sem):
    return pltpu.make_async_copy(src_hbm.at[pl.ds(row, 1), :], dst.at[pl.ds(dst_row, 1), :], sem)


def _expert_kernel(blk_e_ref, tok_ref, nused_ref, h_hbm, wg_ref, wu_ref, wd_ref, o_ref,
                   xg_ref, wgb_ref, wub_ref, wdb_ref, sem):
    b = pl.program_id(0)
    nused = nused_ref[0]
    rows = EXPERT_BLOCK

    def gather(blk, slot, start):
        def body(i, _):
            cp = _row_copy(h_hbm, tok_ref[blk * rows + i], xg_ref.at[slot], i, sem.at[slot])
            if start:
                cp.start()
            else:
                cp.wait()
            return 0
        lax.fori_loop(0, rows, body, 0)

    @pl.when(b == 0)
    def _():
        gather(0, 0, True)

    @pl.when(b < nused)
    def _():
        slot = b % 2
        gather(b, slot, False)

        @pl.when(b + 1 < nused)
        def _():
            gather(b + 1, 1 - slot, True)

        changed = jnp.logical_or(b == 0, blk_e_ref[b] != blk_e_ref[jnp.maximum(b - 1, 0)])

        @pl.when(changed)
        def _():
            wgb_ref[...] = wg_ref[0].astype(BF16)
            wub_ref[...] = wu_ref[0].astype(BF16)
            wdb_ref[...] = wd_ref[0].astype(BF16)

        xb = xg_ref[slot].astype(BF16)
        hg = jnp.dot(xb, wgb_ref[...], preferred_element_type=F32)
        hu = jnp.dot(xb, wub_ref[...], preferred_element_type=F32)
        hid = hg * _sigmoid(hg) * hu
        o_ref[...] = jnp.dot(hid.astype(BF16), wdb_ref[...], preferred_element_type=F32)

    @pl.when(b >= nused)
    def _():
        o_ref[...] = jnp.zeros_like(o_ref)


def _experts(h2, blk_e, buf_tok, nused, w_gate_e, w_up_e, w_down_e):
    n, d = h2.shape
    n_blk = blk_e.shape[0]
    f = w_gate_e.shape[2]
    grid_spec = pltpu.PrefetchScalarGridSpec(
        num_scalar_prefetch=3,
        grid=(n_blk,),
        in_specs=[pl.BlockSpec(memory_space=pl.ANY),
                  pl.BlockSpec((1, d, f), lambda b, be, tok, nu: (be[b], 0, 0)),
                  pl.BlockSpec((1, d, f), lambda b, be, tok, nu: (be[b], 0, 0)),
                  pl.BlockSpec((1, f, d), lambda b, be, tok, nu: (be[b], 0, 0))],
        out_specs=pl.BlockSpec((EXPERT_BLOCK, d), lambda b, be, tok, nu: (b, 0)),
        scratch_shapes=[pltpu.VMEM((2, EXPERT_BLOCK, d), F32),
                        pltpu.VMEM((d, f), BF16), pltpu.VMEM((d, f), BF16), pltpu.VMEM((f, d), BF16),
                        pltpu.SemaphoreType.DMA((2,))],
    )
    return pl.pallas_call(
        _expert_kernel,
        grid_spec=grid_spec,
        out_shape=jax.ShapeDtypeStruct((n_blk * EXPERT_BLOCK, d), F32),
        name="experts",
        compiler_params=_params("arbitrary"),
    )(blk_e, buf_tok, nused, h2, w_gate_e, w_up_e, w_down_e)


def _combine_kernel(pos_ref, y_hbm, wt_ref, x1_ref, gt_ref, gpost_ref, o_ref, rows_ref, sem):
    i = pl.program_id(0)
    nsteps = pl.num_programs(0)
    tm = x1_ref.shape[0]

    def gather(step, slot, start):
        def body(r, _):
            for kk in range(TOP_K):
                cp = _row_copy(y_hbm, pos_ref[(step * tm + r) * TOP_K + kk], rows_ref.at[slot, kk], r,
                               sem.at[slot])
                if start:
                    cp.start()
                else:
                    cp.wait()
            return 0
        lax.fori_loop(0, tm, body, 0)

    @pl.when(i == 0)
    def _():
        gather(0, 0, True)

    slot = i % 2
    gather(i, slot, False)

    @pl.when(i + 1 < nsteps)
    def _():
        gather(i + 1, 1 - slot, True)

    wt = wt_ref[...]
    y = rows_ref[slot, 0] * wt[:, 0:1] + rows_ref[slot, 1] * wt[:, 1:2]
    o_ref[...] = x1_ref[...] + gt_ref[...] * _rms(y, gpost_ref[...])


def _combine(pos, y_buf, wts, x1, gt2, g_post):
    n, d = x1.shape
    tm = 128
    vec = pl.BlockSpec((1, d), lambda i, p: (0, 0))
    grid_spec = pltpu.PrefetchScalarGridSpec(
        num_scalar_prefetch=1,
        grid=(n // tm,),
        in_specs=[pl.BlockSpec(memory_space=pl.ANY),
                  pl.BlockSpec((tm, LANES), lambda i, p: (i, 0)),
                  pl.BlockSpec((tm, d), lambda i, p: (i, 0)), vec, vec],
        out_specs=pl.BlockSpec((tm, d), lambda i, p: (i, 0)),
        scratch_shapes=[pltpu.VMEM((2, TOP_K, tm, d), F32), pltpu.SemaphoreType.DMA((2,))],
    )
    return pl.pallas_call(
        _combine_kernel,
        grid_spec=grid_spec,
        out_shape=jax.ShapeDtypeStruct((n, d), F32),
        name="combine",
        compiler_params=_params("arbitrary"),
    )(pos, y_buf, wts, x1, gt2, g_post)


def _dispatch_tables(e_idx):
    n = e_idx.shape[0]
    p = n * TOP_K
    n_blk = (p + N_EXPERTS * (EXPERT_BLOCK - 1) + EXPERT_BLOCK - 1) // EXPERT_BLOCK
    e_flat = e_idx.reshape(p)
    tok_flat = jnp.repeat(jnp.arange(n, dtype=I32), TOP_K)
    order = jnp.argsort(e_flat)
    e_s, tok_s = e_flat[order], tok_flat[order]
    counts = jnp.zeros((N_EXPERTS,), I32).at[e_flat].add(1)
    start = jnp.cumsum(counts) - counts
    pcounts = (counts + EXPERT_BLOCK - 1) // EXPERT_BLOCK * EXPERT_BLOCK
    pend = jnp.cumsum(pcounts)
    pstart = pend - pcounts
    dest = pstart[e_s] + (jnp.arange(p, dtype=I32) - start[e_s])
    buf_tok = jnp.zeros((n_blk * EXPERT_BLOCK,), I32).at[dest].set(tok_s)
    pos = jnp.zeros((p,), I32).at[order].set(dest)
    nused = (pend[-1] // EXPERT_BLOCK).astype(I32)
    blk = jnp.minimum(jnp.arange(n_blk, dtype=I32), nused - 1)
    blk_e = jnp.clip(jnp.searchsorted(pend, blk * EXPERT_BLOCK, side='right'), 0, N_EXPERTS - 1).astype(I32)
    return blk_e, buf_tok, nused.reshape(1), pos


def _rope_tables(s):
    half = ROPE_DIM // 2
    inv = ROPE_THETA ** (-jnp.arange(half, dtype=F32) / half)
    ang = jnp.arange(s, dtype=F32)[:, None] * inv[None, :]
    cos, sin = jnp.cos(ang), jnp.sin(ang)
    pad = jnp.zeros((s, LANES - ROPE_DIM), F32)
    zero = jnp.zeros((s, half), F32)
    cos_t = jnp.concatenate([cos, cos, pad + 1.0], axis=1)
    sin1_t = jnp.concatenate([-sin, zero, pad], axis=1)
    sin2_t = jnp.concatenate([zero, sin, pad], axis=1)
    return cos_t, sin1_t, sin2_t


def _layer(x, c_col, w_ada, b_ada, g_pre_mix, g_post_mix, g_pre_ffn, g_post_ffn, w_in, mu_r, mu_k, mu_v,
           mu_w, mu_a, mu_g, w0, w_w1, w_w2, a0, w_a1, w_a2, w_g1, w_g2, k_k, k_a, r_k, gn_w, gn_b,
           w_up_att, w_up_rwkv, w_o, w_rg, b_rg, w_re, b_re, w_gate_e, w_up_e, w_down_e):
    s, d = x.shape
    att_w = ATT_HEADS * ATT_HEAD_DIM
    rwkv_w = w_up_rwkv.shape[0]
    row = lambda a: a.reshape(1, -1)

    ada = _ada(c_col, w_ada, row(b_ada))
    sh1, sc1, gt1, sh2, sc2, gt2 = (ada[:, i * d:(i + 1) * d] for i in range(6))

    h, lw, la, lg = _prenorm(x, row(g_pre_mix), sc1, sh1, row(mu_w), row(mu_a), row(mu_g), w_w1, w_a1, w_g1)
    proj = _matmul(h, w_in, BF16)

    q_aug_t, k_aug, v_t = _rope_gate(proj, *_rope_tables(s))
    o_att = _attention(q_aug_t, k_aug, v_t)

    o_rwkv = _rwkv(proj, lw, la, lg, w_w2, w_a2, w_g2, row(mu_r), row(mu_k), row(mu_v), row(w0), row(a0),
                   row(k_k), row(k_a), row(r_k), row(gn_w), row(gn_b), col0=3 * att_w, width=rwkv_w)

    mix = _merge(o_att, o_rwkv, proj, w_up_att, w_up_rwkv, gate_col0=3 * att_w + 3 * rwkv_w)
    y = _matmul(mix, w_o, F32)

    w_router = jnp.pad(jnp.concatenate([w_re, w_rg], axis=1), ((0, 0), (0, LANES - N_EXPERTS - N_GROUPS)))
    b_router = jnp.pad(jnp.concatenate([b_re, b_rg]), (0, LANES - N_EXPERTS - N_GROUPS)).reshape(1, LANES)
    x1, h2, e_idx, wts = _route(x, y, gt1, row(g_post_mix), row(g_pre_ffn), sc2, sh2, w_router, b_router)

    blk_e, buf_tok, nused, pos = _dispatch_tables(e_idx[:, :TOP_K])
    y_buf = _experts(h2, blk_e, buf_tok, nused, w_gate_e, w_up_e, w_down_e)
    return _combine(pos, y_buf, wts, x1, gt2, row(g_post_ffn))


def kernel(x, c, w_ada, b_ada, g_pre_mix, g_post_mix, g_pre_ffn, g_post_ffn, w_in, mu_r, mu_k, mu_v, mu_w, mu_a, mu_g, w0, w_w1, w_w2, a0, w_a1, w_a2, w_g1, w_g2, k_k, k_a, r_k, gn_w, gn_b, w_up_att, w_up_rwkv, w_o, w_rg, b_rg, w_re, b_re, w_gate_e, w_up_e, w_down_e):
    b, s, d = x.shape
    assert b == 1, "one sequence per call"
    params = (w_ada, b_ada, g_pre_mix, g_post_mix, g_pre_ffn, g_post_ffn, w_in, mu_r, mu_k, mu_v, mu_w, mu_a,
              mu_g, w0, w_w1, w_w2, a0, w_a1, w_a2, w_g1, w_g2, k_k, k_a, r_k, gn_w, gn_b, w_up_att,
              w_up_rwkv, w_o, w_rg, b_rg, w_re, b_re, w_gate_e, w_up_e, w_down_e)
    xs = x.reshape(s, d)
    c_col = c.reshape(d, 1)
    for l in range(w_ada.shape[0]):
        xs = _layer(xs, c_col, *(p[l] for p in params))
    return xs.reshape(b, s, d)
```

```python
import math

import jax
import jax.numpy as jnp
from jax import lax
from jax.experimental import pallas as pl
from jax.experimental.pallas import tpu as pltpu

F32 = jnp.float32
BF16 = jnp.bfloat16
I32 = jnp.int32
HI = lax.Precision.HIGHEST

LANES = 128
SUBLANES = 8
VMEM_LIMIT = 56 * 1024 * 1024

ATT_HEADS = 8
ATT_HEAD_DIM = 128
MOBA_BLOCK = 256
MOBA_TOPK = 3
ATT_GROUP = 4
ATT_HEADS_PER_STEP = 2
ROPE_THETA = 500000.0
ROPE_DIM = ATT_HEAD_DIM // 4
RWKV_HEAD_DIM = 64
GN_EPS = 64e-5
N_GROUPS = 8
EXPERTS_PER_GROUP = 8
N_EXPERTS = N_GROUPS * EXPERTS_PER_GROUP
TOP_K = 2
EXPERT_BLOCK = 128
RMS_EPS = 1e-6
NEG = -1e30
SCAN_CHUNK = 64
Q_SCALE = ATT_HEAD_DIM ** -0.5 * math.log2(math.e)


def _params(*sem):
    return pltpu.CompilerParams(dimension_semantics=sem, vmem_limit_bytes=VMEM_LIMIT)


def _rms(z, g):
    return z * lax.rsqrt(jnp.mean(z * z, axis=-1, keepdims=True) + RMS_EPS) * g


def _sigmoid(z):
    return 1.0 / (1.0 + jnp.exp(-z))


def _dot(a, b):
    return jnp.dot(a.astype(BF16), b.astype(BF16), preferred_element_type=F32)


def _dot_nt(a, b):
    return lax.dot_general(a.astype(BF16), b.astype(BF16), (((1,), (1,)), ((), ())),
                           preferred_element_type=F32)


def _dot_tn(a, b):
    return lax.dot_general(a.astype(BF16), b.astype(BF16), (((0,), (0,)), ((), ())),
                           preferred_element_type=F32)


def _dot_hi(a, b):
    return jnp.dot(a, b, precision=HI, preferred_element_type=F32)


def _split3(a):
    hi = a.astype(BF16)
    r1 = a - hi.astype(F32)
    mid = r1.astype(BF16)
    lo = (r1 - mid.astype(F32)).astype(BF16)
    return hi, mid, lo


def _dot_split(a, b01):
    b = b01.astype(BF16)
    hi, mid, _ = _split3(a)
    return jnp.dot(jnp.concatenate([hi, mid], axis=1), jnp.concatenate([b, b], axis=0),
                   preferred_element_type=F32)


def _dot_split_t(b01, a):
    b = b01.astype(BF16)
    return jnp.dot(jnp.concatenate([b, b, b], axis=1), jnp.concatenate(_split3(a), axis=0),
                   preferred_element_type=F32)


def _shift_rows(z, prev_row):
    rolled = pltpu.roll(z, 1, 0)
    row = lax.broadcasted_iota(I32, z.shape, 0)
    return jnp.where(row == 0, prev_row, rolled)


def _ada_kernel(c_ref, w_ref, b_ref, o_ref):
    o_ref[...] = jnp.sum(c_ref[...] * w_ref[...], axis=0, keepdims=True) + b_ref[...]


def _ada(c_col, w_ada, b_ada):
    d, n = w_ada.shape
    tn = 1024
    return pl.pallas_call(
        _ada_kernel,
        grid=(n // tn,),
        in_specs=[pl.BlockSpec((d, 1), lambda j: (0, 0)),
                  pl.BlockSpec((d, tn), lambda j: (0, j)),
                  pl.BlockSpec((1, tn), lambda j: (0, j))],
        out_specs=pl.BlockSpec((1, tn), lambda j: (0, j)),
        out_shape=jax.ShapeDtypeStruct((1, n), F32),
        name="ada",
        compiler_params=_params("parallel"),
    )(c_col, w_ada, b_ada)


def _prenorm_kernel(x_ref, xp_ref, g_ref, sc_ref, sh_ref, muw_ref, mua_ref, mug_ref,
                    ww1_ref, wa1_ref, wg1_ref, h_ref, lw_ref, la_ref, lg_ref):
    i = pl.program_id(0)
    g, sc, sh = g_ref[...], sc_ref[...], sh_ref[...]
    h = _rms(x_ref[...], g) * (1.0 + sc) + sh
    hp = _rms(xp_ref[SUBLANES - 1:SUBLANES, :], g) * (1.0 + sc) + sh
    hp = jnp.where(i == 0, 0.0, hp)
    dh = _shift_rows(h, hp) - h
    h_ref[...] = h.astype(BF16)
    lw_ref[...] = jnp.tanh(_dot(h + dh * muw_ref[...], ww1_ref[...]))
    la_ref[...] = _dot(h + dh * mua_ref[...], wa1_ref[...])
    lg_ref[...] = _sigmoid(_dot(h + dh * mug_ref[...], wg1_ref[...]))


def _prenorm(x, g, sc, sh, mu_w, mu_a, mu_g, w_w1, w_a1, w_g1):
    s, d = x.shape
    tm = 256
    rpb = tm // SUBLANES
    vec = pl.BlockSpec((1, d), lambda i: (0, 0))
    full = lambda a: pl.BlockSpec(a.shape, lambda i: (0, 0))
    lw, la, lg = w_w1.shape[1], w_a1.shape[1], w_g1.shape[1]
    return pl.pallas_call(
        _prenorm_kernel,
        grid=(s // tm,),
        in_specs=[pl.BlockSpec((tm, d), lambda i: (i, 0)),
                  pl.BlockSpec((SUBLANES, d), lambda i: (jnp.maximum(i * rpb - 1, 0), 0)),
                  vec, vec, vec, vec, vec, vec, full(w_w1), full(w_a1), full(w_g1)],
        out_specs=[pl.BlockSpec((tm, d), lambda i: (i, 0)),
                   pl.BlockSpec((tm, lw), lambda i: (i, 0)),
                   pl.BlockSpec((tm, la), lambda i: (i, 0)),
                   pl.BlockSpec((tm, lg), lambda i: (i, 0))],
        out_shape=[jax.ShapeDtypeStruct((s, d), BF16),
                   jax.ShapeDtypeStruct((s, lw), F32),
                   jax.ShapeDtypeStruct((s, la), F32),
                   jax.ShapeDtypeStruct((s, lg), F32)],
        name="prenorm_lora",
        compiler_params=_params("parallel"),
    )(x, x, g, sc, sh, mu_w, mu_a, mu_g, w_w1, w_a1, w_g1)


def _mm_kernel(a_ref, w_ref, o_ref, wb_ref):
    @pl.when(pl.program_id(1) == 0)
    def _():
        wb_ref[...] = w_ref[...].astype(BF16)

    o_ref[...] = jnp.dot(a_ref[...], wb_ref[...], preferred_element_type=F32).astype(o_ref.dtype)


def _matmul(a, w, out_dtype, tm=512, tn=1024):
    m, k = a.shape
    n = w.shape[1]
    tn = min(tn, n)
    return pl.pallas_call(
        _mm_kernel,
        grid=(n // tn, m // tm),
        in_specs=[pl.BlockSpec((tm, k), lambda j, i: (i, 0)),
                  pl.BlockSpec((k, tn), lambda j, i: (0, j))],
        out_specs=pl.BlockSpec((tm, tn), lambda j, i: (i, j)),
        out_shape=jax.ShapeDtypeStruct((m, n), out_dtype),
        scratch_shapes=[pltpu.VMEM((k, tn), BF16)],
        name="matmul",
        compiler_params=_params("arbitrary", "arbitrary"),
    )(a, w)


def _rope_gate_kernel(p_ref, c_ref, s1_ref, s2_ref, qa_ref, ka_ref, vt_ref, km_ref):
    i = pl.program_id(0)
    bs = MOBA_BLOCK
    nbp = km_ref.shape[1]

    @pl.when(i == 0)
    def _():
        km_ref[...] = jnp.zeros_like(km_ref)

    c, s1, s2 = c_ref[...], s1_ref[...], s2_ref[...]

    def rope(z):
        return z * c + pltpu.roll(z, LANES - ROPE_DIM // 2, 1) * s1 + pltpu.roll(z, ROPE_DIM // 2, 1) * s2

    row = lax.broadcasted_iota(I32, (nbp, bs), 0)
    lane = lax.broadcasted_iota(I32, (bs, LANES), 1)
    onehot = jnp.where(lane == i, 1.0, 0.0).astype(BF16)
    for h in range(ATT_HEADS):
        q = rope(p_ref[:, h * LANES:(h + 1) * LANES].astype(F32))
        k = rope(p_ref[:, (ATT_HEADS + h) * LANES:(ATT_HEADS + h + 1) * LANES].astype(F32))
        g = lax.dot_general(km_ref[h], q, (((1,), (1,)), ((), ())), precision=HI, preferred_element_type=F32)
        g = jnp.where(row < i, g, NEG)
        sel_t = jnp.zeros(g.shape, F32)
        for _ in range(MOBA_TOPK):
            mx = jnp.max(g, axis=0, keepdims=True)
            idx = jnp.min(jnp.where(g == mx, row, nbp), axis=0, keepdims=True)
            hit = row == idx
            sel_t = jnp.where(hit & (row < i), 1.0, sel_t)
            g = jnp.where(hit, -jnp.inf, g)
        if nbp < LANES:
            sel_t = jnp.concatenate([sel_t, jnp.zeros((LANES - nbp, bs), F32)], axis=0)
        w = 2 * LANES
        qa_ref[h, :LANES, :] = (q * Q_SCALE).T.astype(BF16)
        qa_ref[h, LANES:, :] = jnp.where(sel_t > 0.5, 0.0, NEG).astype(BF16)
        ka_ref[:, h * w:h * w + LANES] = k.astype(BF16)
        ka_ref[:, h * w + LANES:(h + 1) * w] = onehot
        v = p_ref[:, (2 * ATT_HEADS + h) * LANES:(2 * ATT_HEADS + h + 1) * LANES].astype(F32)
        vt_ref[h, 0] = v.T.astype(BF16)
        km_ref[h, pl.ds(i, 1), :] = jnp.mean(k, axis=0, keepdims=True)


def _rope_gate(proj, cos_t, sin1_t, sin2_t):
    s = proj.shape[0]
    nb = s // MOBA_BLOCK
    assert nb <= LANES
    nbp = -(-nb // SUBLANES) * SUBLANES
    w_in = 3 * ATT_HEADS * ATT_HEAD_DIM
    w_out = 2 * ATT_HEADS * LANES
    tab = pl.BlockSpec((MOBA_BLOCK, LANES), lambda i: (i, 0))
    return pl.pallas_call(
        _rope_gate_kernel,
        grid=(nb,),
        in_specs=[pl.BlockSpec((MOBA_BLOCK, w_in), lambda i: (i, 0)), tab, tab, tab],
        out_specs=[pl.BlockSpec((ATT_HEADS, 2 * LANES, MOBA_BLOCK), lambda i: (0, 0, i)),
                   pl.BlockSpec((MOBA_BLOCK, w_out), lambda i: (i, 0)),
                   pl.BlockSpec((ATT_HEADS, 1, LANES, MOBA_BLOCK), lambda i: (0, i, 0, 0))],
        out_shape=[jax.ShapeDtypeStruct((ATT_HEADS, 2 * LANES, s), BF16),
                   jax.ShapeDtypeStruct((s, w_out), BF16),
                   jax.ShapeDtypeStruct((ATT_HEADS, nb, LANES, MOBA_BLOCK), BF16)],
        scratch_shapes=[pltpu.VMEM((ATT_HEADS, nbp, LANES), F32)],
        name="rope_gate",
        compiler_params=_params("arbitrary"),
    )(proj, cos_t, sin1_t, sin2_t)


def _attn_kernel(qa_ref, ka_ref, vt_ref, o_ref, s_ref):
    qi = pl.program_id(1)
    bs = MOBA_BLOCK
    grp = ATT_GROUP * bs
    w = 2 * LANES
    heads = range(ATT_HEADS_PER_STEP)

    n_groups = ka_ref.shape[0] // grp
    assert n_groups % 2 == 0

    def issue_scores(g, buf):
        base = pl.multiple_of(jnp.minimum(g, n_groups - 1) * grp, grp)
        for h in heads:
            s_ref[buf, h] = jnp.dot(ka_ref[pl.ds(base, grp), h * w:(h + 1) * w], qa_ref[h],
                                    preferred_element_type=F32)

    issue_scores(0, 0)

    own = pl.multiple_of(qi * bs, bs)
    k_i = lax.broadcasted_iota(I32, (bs, bs), 0)
    q_i = lax.broadcasted_iota(I32, (bs, bs), 1)
    carry = []
    for h in heads:
        s = jnp.dot(ka_ref[pl.ds(own, bs), h * w:h * w + LANES], qa_ref[h, :LANES, :],
                    preferred_element_type=F32)
        s = jnp.where(k_i <= q_i, s, NEG)
        m = jnp.max(s, axis=0, keepdims=True)
        p = jnp.exp2(s - m)
        l = jnp.sum(p, axis=0, keepdims=True)
        acc = jnp.dot(vt_ref[h, qi], p.astype(BF16), preferred_element_type=F32)
        carry += [m, l, acc]

    def absorb(g, buf, carry):
        out = []
        for h in heads:
            m, l, acc = carry[3 * h:3 * h + 3]
            s = s_ref[buf, h]
            mn = jnp.maximum(m, jnp.max(s, axis=0, keepdims=True))
            alpha = jnp.exp2(m - mn)
            p = jnp.exp2(s - mn)
            l = alpha * l + jnp.sum(p, axis=0, keepdims=True)
            p = p.astype(BF16)
            acc = alpha * acc
            for jb in range(ATT_GROUP):
                acc = acc + jnp.dot(vt_ref[h, g * ATT_GROUP + jb], p[jb * bs:(jb + 1) * bs],
                                    preferred_element_type=F32)
            out += [mn, l, acc]
        return out

    def body(t, carry):
        issue_scores(2 * t + 1, 1)
        carry = absorb(2 * t, 0, carry)
        issue_scores(2 * t + 2, 0)
        return tuple(absorb(2 * t + 1, 1, carry))

    n_used = (qi + ATT_GROUP - 1) // ATT_GROUP
    carry = lax.fori_loop(0, (n_used + 1) // 2, body, tuple(carry))
    for h in heads:
        m, l, acc = carry[3 * h:3 * h + 3]
        o_ref[:, h * LANES:(h + 1) * LANES] = (acc / l).T.astype(BF16)


def _attention(q_aug_t, k_aug, v_t):
    s = k_aug.shape[0]
    assert s % (ATT_GROUP * MOBA_BLOCK) == 0
    nb = s // MOBA_BLOCK
    hps = ATT_HEADS_PER_STEP
    once = pl.Buffered(1)
    return pl.pallas_call(
        _attn_kernel,
        grid=(ATT_HEADS // hps, nb),
        in_specs=[pl.BlockSpec((hps, 2 * LANES, MOBA_BLOCK), lambda h, i: (h, 0, i)),
                  pl.BlockSpec((s, hps * 2 * LANES), lambda h, i: (0, h), pipeline_mode=once),
                  pl.BlockSpec((hps, nb, LANES, MOBA_BLOCK), lambda h, i: (h, 0, 0, 0), pipeline_mode=once)],
        out_specs=pl.BlockSpec((MOBA_BLOCK, hps * LANES), lambda h, i: (i, h)),
        out_shape=jax.ShapeDtypeStruct((s, ATT_HEADS * ATT_HEAD_DIM), BF16),
        scratch_shapes=[pltpu.VMEM((2, hps, ATT_GROUP * MOBA_BLOCK, MOBA_BLOCK), F32)],
        name="moba_attention",
        compiler_params=_params("parallel", "arbitrary"),
    )(q_aug_t, k_aug, v_t)


def _rwkv_kernel(r_ref, k_ref, v_ref, lw_ref, la_ref, lg_ref, ww2_ref, wa2_ref, wg2_ref,
                 mur_ref, muk_ref, muv_ref, w0_ref, a0_ref, kk_ref, ka_ref, rk_ref, gnw_ref, gnb_ref,
                 o_ref, st_ref, prev_ref):
    j = pl.program_id(1)
    t = SCAN_CHUNK
    n = RWKV_HEAD_DIM
    ts = r_ref.shape[0]
    chunks = range(ts // t)

    @pl.when(j == 0)
    def _():
        st_ref[...] = jnp.zeros_like(st_ref)
        prev_ref[...] = jnp.zeros_like(prev_ref)

    rp, kp, vp = r_ref[...].astype(F32), k_ref[...].astype(F32), v_ref[...].astype(F32)
    r = rp + (_shift_rows(rp, prev_ref[0:1, :]) - rp) * mur_ref[...]
    k = kp + (_shift_rows(kp, prev_ref[1:2, :]) - kp) * muk_ref[...]
    v = vp + (_shift_rows(vp, prev_ref[2:3, :]) - vp) * muv_ref[...]
    prev_ref[0:1, :] = rp[ts - 1:ts, :]
    prev_ref[1:2, :] = kp[ts - 1:ts, :]
    prev_ref[2:3, :] = vp[ts - 1:ts, :]

    d = w0_ref[...] + _dot(lw_ref[...], ww2_ref[...])
    logw = -math.exp(-0.5) * _sigmoid(d)
    a = _sigmoid(a0_ref[...] + _dot(la_ref[...], wa2_ref[...]))
    g = _dot(lg_ref[...], wg2_ref[...])

    li = lax.broadcasted_iota(I32, (LANES, LANES), 0)
    lj = lax.broadcasted_iota(I32, (LANES, LANES), 1)
    same_head = (li // n) == (lj // n)
    head_sum = jnp.where(same_head, 1.0, 0.0)

    kk = k * kk_ref[...]
    kk = kk / jnp.maximum(jnp.sqrt(_dot_split(kk * kk, head_sum)), 1e-12)
    kt = k * (1.0 + (a - 1.0) * ka_ref[...])
    bonus = _dot_split(r * kt * rk_ref[...], head_sum) * v

    same_blk = (li // t) == (lj // t)
    m_strict = same_blk & (lj < li)
    m_incl = same_blk & (lj <= li)
    ti = lax.broadcasted_iota(I32, (t, t), 0)
    tj = lax.broadcasted_iota(I32, (t, t), 1)
    tril_incl = jnp.where(tj <= ti, 1.0, 0.0)
    lane_a = lax.broadcasted_iota(I32, (t, LANES), 1) < n
    eye = jnp.where(li == lj, 1.0, 0.0)

    def stack_masked(z):
        return jnp.concatenate([jnp.where(lane_a, z, 0.0), jnp.where(lane_a, 0.0, z)], axis=0)

    def stack_plain(z):
        return jnp.concatenate([z, z], axis=0)

    def rows(z, c):
        return z[c * t:(c + 1) * t]

    cum = jnp.concatenate([_dot_split_t(tril_incl, rows(logw, c)) for c in chunks], axis=0)
    g_t = jnp.exp(cum)
    g_inv = jnp.exp(-cum)
    xa_f = -kk * jnp.exp(cum - logw)
    xr_f = r * g_t
    yb_f = kk * a * g_inv
    yk_f = kt * g_inv

    xa = [stack_masked(rows(xa_f, c)) for c in chunks]
    xr = [stack_masked(rows(xr_f, c)) for c in chunks]
    yb = [stack_plain(rows(yb_f, c)) for c in chunks]
    vs = [stack_masked(rows(v, c)) for c in chunks]
    ybk = [jnp.concatenate([yb[c], stack_plain(rows(yk_f, c))], axis=0) for c in chunks]
    sc = [_dot_nt(jnp.concatenate([xa[c], xr[c]], axis=0), ybk[c]) for c in chunks]
    a_ab = [jnp.where(m_strict, sc[c][:2 * t, :2 * t], 0.0) for c in chunks]
    a_ak = [jnp.where(m_strict, sc[c][:2 * t, 2 * t:], 0.0) for c in chunks]
    a_rb = [jnp.where(m_incl, sc[c][2 * t:, :2 * t], 0.0) for c in chunks]
    a_rk = [jnp.where(m_incl, sc[c][2 * t:, 2 * t:], 0.0) for c in chunks]
    inv = [eye + a_ab[c] for c in chunks]
    pw = [_dot(a_ab[c], a_ab[c]) for c in chunks]
    for _ in range(int(math.log2(t)) - 2):
        both = [_dot(pw[c], jnp.concatenate([pw[c], inv[c]], axis=1)) for c in chunks]
        pw = [both[c][:, :LANES] for c in chunks]
        inv = [inv[c] + both[c][:, LANES:] for c in chunks]
    inv = [inv[c] + _dot(pw[c], inv[c]) for c in chunks]
    av = [_dot(a_ak[c], vs[c]) for c in chunks]
    mw = [_dot(inv[c], jnp.concatenate([xa[c], av[c]], axis=1)) for c in chunks]
    zeros = jnp.zeros((2 * t, LANES), F32)
    rw = [_dot(jnp.concatenate([a_rb[c], a_rk[c]], axis=1),
               jnp.concatenate([mw[c], jnp.concatenate([zeros, vs[c]], axis=1)], axis=0)) for c in chunks]
    r_hat = [xr[c] + rw[c][:, :LANES] for c in chunks]
    y_hat = [rw[c][:, LANES:] for c in chunks]
    pm = [jnp.where(same_head, _dot_tn(mw[c][:, :LANES], yb[c]), 0.0) for c in chunks]
    qm = [jnp.where(same_head, _dot_tn(jnp.concatenate([mw[c][:, LANES:], vs[c]], axis=0), ybk[c]), 0.0)
          for c in chunks]

    st = st_ref[...]
    ys = []
    for c in chunks:
        y2 = _dot_nt(r_hat[c], st) + y_hat[c]
        ys.append(y2[:t] + y2[t:])
        st = (st + _dot(st, pm[c]) + qm[c]) * g_t[(c + 1) * t - 1:(c + 1) * t, :]
    st_ref[...] = st
    y = jnp.concatenate(ys, axis=0)

    mean = _dot_split(y, head_sum) * (1.0 / n)
    yc = y - mean
    var = _dot_split(yc * yc, head_sum) * (1.0 / n)
    yn = yc * lax.rsqrt(var + GN_EPS) * gnw_ref[...] + gnb_ref[...]
    o_ref[...] = ((yn + bonus) * g).astype(BF16)


def _rwkv(proj, lw, la, lg, w_w2, w_a2, w_g2, mu_r, mu_k, mu_v, w0, a0, k_k, k_a, r_k, gn_w, gn_b,
          col0, width):
    s = proj.shape[0]
    ts = 512
    npair = width // LANES
    cb = col0 // LANES
    row = lambda a: pl.BlockSpec((ts, a.shape[1]), lambda p, j: (j, 0))
    wcol = lambda a: pl.BlockSpec((a.shape[0], LANES), lambda p, j: (0, p))
    vec = pl.BlockSpec((1, LANES), lambda p, j: (0, p))
    return pl.pallas_call(
        _rwkv_kernel,
        grid=(npair, s // ts),
        in_specs=[pl.BlockSpec((ts, LANES), lambda p, j: (j, cb + p)),
                  pl.BlockSpec((ts, LANES), lambda p, j: (j, cb + npair + p)),
                  pl.BlockSpec((ts, LANES), lambda p, j: (j, cb + 2 * npair + p)),
                  row(lw), row(la), row(lg), wcol(w_w2), wcol(w_a2), wcol(w_g2)] + [vec] * 10,
        out_specs=pl.BlockSpec((ts, LANES), lambda p, j: (j, p)),
        out_shape=jax.ShapeDtypeStruct((s, width), BF16),
        scratch_shapes=[pltpu.VMEM((LANES, LANES), F32), pltpu.VMEM((SUBLANES, LANES), F32)],
        name="rwkv7_scan",
        compiler_params=_params("parallel", "arbitrary"),
    )(proj, proj, proj, lw, la, lg, w_w2, w_a2, w_g2, mu_r, mu_k, mu_v, w0, a0, k_k, k_a, r_k, gn_w, gn_b)


def _merge_kernel(oa_ref, or_ref, ga_ref, gr_ref, wa_ref, wr_ref, o_ref, wab_ref, wrb_ref):
    @pl.when(pl.program_id(1) == 0)
    def _():
        wab_ref[...] = wa_ref[...].astype(BF16)
        wrb_ref[...] = wr_ref[...].astype(BF16)

    ua = jnp.dot(oa_ref[...], wab_ref[...], preferred_element_type=F32)
    ur = jnp.dot(or_ref[...], wrb_ref[...], preferred_element_type=F32)
    mix = _sigmoid(ga_ref[...].astype(F32)) * ua + _sigmoid(gr_ref[...].astype(F32)) * ur
    o_ref[...] = mix.astype(BF16)


def _merge(o_att, o_rwkv, proj, w_up_att, w_up_rwkv, gate_col0):
    s, ka = o_att.shape
    kr = o_rwkv.shape[1]
    d = w_up_att.shape[1]
    tm, tn = 512, 1024
    gb = gate_col0 // tn
    return pl.pallas_call(
        _merge_kernel,
        grid=(d // tn, s // tm),
        in_specs=[pl.BlockSpec((tm, ka), lambda j, i: (i, 0)),
                  pl.BlockSpec((tm, kr), lambda j, i: (i, 0)),
                  pl.BlockSpec((tm, tn), lambda j, i: (i, gb + j)),
                  pl.BlockSpec((tm, tn), lambda j, i: (i, gb + d // tn + j)),
                  pl.BlockSpec((ka, tn), lambda j, i: (0, j)),
                  pl.BlockSpec((kr, tn), lambda j, i: (0, j))],
        out_specs=pl.BlockSpec((tm, tn), lambda j, i: (i, j)),
        out_shape=jax.ShapeDtypeStruct((s, d), BF16),
        scratch_shapes=[pltpu.VMEM((ka, tn), BF16), pltpu.VMEM((kr, tn), BF16)],
        name="gated_merge",
        compiler_params=_params("arbitrary", "arbitrary"),
    )(o_att, o_rwkv, proj, proj, w_up_att, w_up_rwkv)


def _route_kernel(x_ref, y_ref, gt_ref, gpost_ref, gpre_ref, sc_ref, sh_ref, wr_ref, br_ref,
                  x1_ref, h2_ref, ei_ref, wt_ref):
    x1 = x_ref[...] + gt_ref[...] * _rms(y_ref[...], gpost_ref[...])
    x1_ref[...] = x1
    h2 = _rms(x1, gpre_ref[...]) * (1.0 + sc_ref[...]) + sh_ref[...]
    h2_ref[...] = h2
    logits = _dot_hi(h2, wr_ref[...]) + br_ref[...]
    lane = lax.broadcasted_iota(I32, logits.shape, 1)
    big = jnp.int32(4 * LANES)
    gmask = (lane >= N_EXPERTS) & (lane < N_EXPERTS + N_GROUPS)
    mg = jnp.max(jnp.where(gmask, logits, -jnp.inf), axis=1, keepdims=True)
    eg = jnp.where(gmask, jnp.exp(logits - mg), 0.0)
    pg = eg / jnp.sum(eg, axis=1, keepdims=True)
    pg_top = jnp.max(pg, axis=1, keepdims=True)
    g_idx = jnp.min(jnp.where(gmask & (pg == pg_top), lane, big), axis=1, keepdims=True) - N_EXPERTS
    emask = (lane >= g_idx * EXPERTS_PER_GROUP) & (lane < (g_idx + 1) * EXPERTS_PER_GROUP)
    me = jnp.max(jnp.where(emask, logits, -jnp.inf), axis=1, keepdims=True)
    ee = jnp.where(emask, jnp.exp(logits - me), 0.0)
    pe = ee / jnp.sum(ee, axis=1, keepdims=True)
    p1 = jnp.max(pe, axis=1, keepdims=True)
    i1 = jnp.min(jnp.where(emask & (pe == p1), lane, big), axis=1, keepdims=True)
    rest = emask & (lane != i1)
    p2 = jnp.max(jnp.where(rest, pe, -jnp.inf), axis=1, keepdims=True)
    i2 = jnp.min(jnp.where(rest & (pe == p2), lane, big), axis=1, keepdims=True)
    den = p1 + p2
    ei_ref[...] = jnp.where(lane == 0, i1, jnp.where(lane == 1, i2, 0))
    wt_ref[...] = jnp.where(lane == 0, pg_top * p1 / den, jnp.where(lane == 1, pg_top * p2 / den, 0.0))


def _route(x, y, gt1, g_post, g_pre, sc2, sh2, w_router, b_router):
    s, d = x.shape
    tm = 256
    vec = pl.BlockSpec((1, d), lambda i: (0, 0))
    rowblk = pl.BlockSpec((tm, d), lambda i: (i, 0))
    small = pl.BlockSpec((tm, LANES), lambda i: (i, 0))
    return pl.pallas_call(
        _route_kernel,
        grid=(s // tm,),
        in_specs=[rowblk, rowblk, vec, vec, vec, vec, vec,
                  pl.BlockSpec((d, LANES), lambda i: (0, 0)),
                  pl.BlockSpec((1, LANES), lambda i: (0, 0))],
        out_specs=[rowblk, rowblk, small, small],
        out_shape=[jax.ShapeDtypeStruct((s, d), F32), jax.ShapeDtypeStruct((s, d), F32),
                   jax.ShapeDtypeStruct((s, LANES), I32), jax.ShapeDtypeStruct((s, LANES), F32)],
        name="norm_route",
        compiler_params=_params("parallel"),
    )(x, y, gt1, g_post, g_pre, sc2, sh2, w_router, b_router)


def _row_copy(src_hbm, row, dst, dst_row, sem):
    return pltpu.make_async_copy(src_hbm.at[pl.ds(row, 1), :], dst.at[pl.ds(dst_row, 1), :], sem)


def _expert_kernel(blk_e_ref, tok_ref, nused_ref, h_hbm, wg_ref, wu_ref, wd_ref, o_ref,
                   xg_ref, wgb_ref, wub_ref, wdb_ref, sem):
    b = pl.program_id(0)
    nused = nused_ref[0]
    rows = EXPERT_BLOCK

    def start_rows(blk, slot):
        for i in range(rows):
            _row_copy(h_hbm, tok_ref[blk * rows + i], xg_ref.at[slot], i, sem.at[slot]).start()

    def wait_rows(slot):
        pltpu.make_async_copy(h_hbm.at[pl.ds(0, rows), :], xg_ref.at[slot], sem.at[slot]).wait()

    @pl.when(b == 0)
    def _():
        start_rows(0, 0)

    @pl.when(b < nused)
    def _():
        slot = b % 2
        wait_rows(slot)
        changed = jnp.logical_or(b == 0, blk_e_ref[b] != blk_e_ref[jnp.maximum(b - 1, 0)])

        @pl.when(changed)
        def _():
            wgb_ref[...] = wg_ref[0].astype(BF16)
            wub_ref[...] = wu_ref[0].astype(BF16)
            wdb_ref[...] = wd_ref[0].astype(BF16)

        start_rows(jnp.minimum(b + 1, nused - 1), 1 - slot)
        xb = xg_ref[slot].astype(BF16)
        hg = jnp.dot(xb, wgb_ref[...], preferred_element_type=F32)
        hu = jnp.dot(xb, wub_ref[...], preferred_element_type=F32)
        hid = hg * _sigmoid(hg) * hu
        o_ref[...] = jnp.dot(hid.astype(BF16), wdb_ref[...], preferred_element_type=F32)

        @pl.when(b == nused - 1)
        def _():
            wait_rows(1 - slot)

    @pl.when(b >= nused)
    def _():
        o_ref[...] = jnp.zeros_like(o_ref)


def _experts(h2, blk_e, buf_tok, nused, w_gate_e, w_up_e, w_down_e):
    n, d = h2.shape
    n_blk = blk_e.shape[0]
    f = w_gate_e.shape[2]
    grid_spec = pltpu.PrefetchScalarGridSpec(
        num_scalar_prefetch=3,
        grid=(n_blk,),
        in_specs=[pl.BlockSpec(memory_space=pl.ANY),
                  pl.BlockSpec((1, d, f), lambda b, be, tok, nu: (be[b], 0, 0)),
                  pl.BlockSpec((1, d, f), lambda b, be, tok, nu: (be[b], 0, 0)),
                  pl.BlockSpec((1, f, d), lambda b, be, tok, nu: (be[b], 0, 0))],
        out_specs=pl.BlockSpec((EXPERT_BLOCK, d), lambda b, be, tok, nu: (b, 0)),
        scratch_shapes=[pltpu.VMEM((2, EXPERT_BLOCK, d), F32),
                        pltpu.VMEM((d, f), BF16), pltpu.VMEM((d, f), BF16), pltpu.VMEM((f, d), BF16),
                        pltpu.SemaphoreType.DMA((2,))],
    )
    return pl.pallas_call(
        _expert_kernel,
        grid_spec=grid_spec,
        out_shape=jax.ShapeDtypeStruct((n_blk * EXPERT_BLOCK, d), F32),
        name="experts",
        compiler_params=_params("arbitrary"),
    )(blk_e, buf_tok, nused, h2, w_gate_e, w_up_e, w_down_e)


def _combine_kernel(pos_ref, y_hbm, wt_ref, x1_ref, gt_ref, gpost_ref, o_ref, rows_ref, sem):
    i = pl.program_id(0)
    nsteps = pl.num_programs(0)
    tm = x1_ref.shape[0]

    def start_rows(step, slot):
        for r in range(tm):
            for kk in range(TOP_K):
                _row_copy(y_hbm, pos_ref[(step * tm + r) * TOP_K + kk], rows_ref.at[slot, kk], r,
                          sem.at[slot]).start()

    def wait_rows(slot):
        for kk in range(TOP_K):
            pltpu.make_async_copy(y_hbm.at[pl.ds(0, tm), :], rows_ref.at[slot, kk], sem.at[slot]).wait()

    @pl.when(i == 0)
    def _():
        start_rows(0, 0)

    slot = i % 2
    wait_rows(slot)
    start_rows(jnp.minimum(i + 1, nsteps - 1), 1 - slot)
    wt = wt_ref[...]
    y = rows_ref[slot, 0] * wt[:, 0:1] + rows_ref[slot, 1] * wt[:, 1:2]
    o_ref[...] = x1_ref[...] + gt_ref[...] * _rms(y, gpost_ref[...])

    @pl.when(i == nsteps - 1)
    def _():
        wait_rows(1 - slot)


def _combine(pos, y_buf, wts, x1, gt2, g_post):
    n, d = x1.shape
    tm = 128
    vec = pl.BlockSpec((1, d), lambda i, p: (0, 0))
    grid_spec = pltpu.PrefetchScalarGridSpec(
        num_scalar_prefetch=1,
        grid=(n // tm,),
        in_specs=[pl.BlockSpec(memory_space=pl.ANY),
                  pl.BlockSpec((tm, LANES), lambda i, p: (i, 0)),
                  pl.BlockSpec((tm, d), lambda i, p: (i, 0)), vec, vec],
        out_specs=pl.BlockSpec((tm, d), lambda i, p: (i, 0)),
        scratch_shapes=[pltpu.VMEM((2, TOP_K, tm, d), F32), pltpu.SemaphoreType.DMA((2,))],
    )
    return pl.pallas_call(
        _combine_kernel,
        grid_spec=grid_spec,
        out_shape=jax.ShapeDtypeStruct((n, d), F32),
        name="combine",
        compiler_params=_params("arbitrary"),
    )(pos, y_buf, wts, x1, gt2, g_post)


def _dispatch_tables(e_idx):
    n = e_idx.shape[0]
    p = n * TOP_K
    n_blk = (p + N_EXPERTS * (EXPERT_BLOCK - 1) + EXPERT_BLOCK - 1) // EXPERT_BLOCK
    e_flat = e_idx.reshape(p)
    tok_flat = jnp.repeat(jnp.arange(n, dtype=I32), TOP_K)
    order = jnp.argsort(e_flat)
    e_s, tok_s = e_flat[order], tok_flat[order]
    counts = jnp.zeros((N_EXPERTS,), I32).at[e_flat].add(1)
    start = jnp.cumsum(counts) - counts
    pcounts = (counts + EXPERT_BLOCK - 1) // EXPERT_BLOCK * EXPERT_BLOCK
    pend = jnp.cumsum(pcounts)
    pstart = pend - pcounts
    dest = pstart[e_s] + (jnp.arange(p, dtype=I32) - start[e_s])
    buf_tok = jnp.zeros((n_blk * EXPERT_BLOCK,), I32).at[dest].set(tok_s)
    pos = jnp.zeros((p,), I32).at[order].set(dest)
    nused = (pend[-1] // EXPERT_BLOCK).astype(I32)
    blk = jnp.minimum(jnp.arange(n_blk, dtype=I32), nused - 1)
    blk_e = jnp.clip(jnp.searchsorted(pend, blk * EXPERT_BLOCK, side='right'), 0, N_EXPERTS - 1).astype(I32)
    return blk_e, buf_tok, nused.reshape(1), pos


def _rope_tables(s):
    half = ROPE_DIM // 2
    inv = ROPE_THETA ** (-jnp.arange(half, dtype=F32) / half)
    ang = jnp.arange(s, dtype=F32)[:, None] * inv[None, :]
    cos, sin = jnp.cos(ang), jnp.sin(ang)
    pad = jnp.zeros((s, LANES - ROPE_DIM), F32)
    zero = jnp.zeros((s, half), F32)
    cos_t = jnp.concatenate([cos, cos, pad + 1.0], axis=1)
    sin1_t = jnp.concatenate([-sin, zero, pad], axis=1)
    sin2_t = jnp.concatenate([zero, sin, pad], axis=1)
    return cos_t, sin1_t, sin2_t


def _layer(x, c_col, w_ada, b_ada, g_pre_mix, g_post_mix, g_pre_ffn, g_post_ffn, w_in, mu_r, mu_k, mu_v,
           mu_w, mu_a, mu_g, w0, w_w1, w_w2, a0, w_a1, w_a2, w_g1, w_g2, k_k, k_a, r_k, gn_w, gn_b,
           w_up_att, w_up_rwkv, w_o, w_rg, b_rg, w_re, b_re, w_gate_e, w_up_e, w_down_e):
    s, d = x.shape
    att_w = ATT_HEADS * ATT_HEAD_DIM
    rwkv_w = w_up_rwkv.shape[0]
    row = lambda a: a.reshape(1, -1)

    ada = _ada(c_col, w_ada, row(b_ada))
    sh1, sc1, gt1, sh2, sc2, gt2 = (ada[:, i * d:(i + 1) * d] for i in range(6))

    h, lw, la, lg = _prenorm(x, row(g_pre_mix), sc1, sh1, row(mu_w), row(mu_a), row(mu_g), w_w1, w_a1, w_g1)
    proj = _matmul(h, w_in, BF16)

    q_aug_t, k_aug, v_t = _rope_gate(proj, *_rope_tables(s))
    o_att = _attention(q_aug_t, k_aug, v_t)

    o_rwkv = _rwkv(proj, lw, la, lg, w_w2, w_a2, w_g2, row(mu_r), row(mu_k), row(mu_v), row(w0), row(a0),
                   row(k_k), row(k_a), row(r_k), row(gn_w), row(gn_b), col0=3 * att_w, width=rwkv_w)

    mix = _merge(o_att, o_rwkv, proj, w_up_att, w_up_rwkv, gate_col0=3 * att_w + 3 * rwkv_w)
    y = _matmul(mix, w_o, F32)

    w_router = jnp.pad(jnp.concatenate([w_re, w_rg], axis=1), ((0, 0), (0, LANES - N_EXPERTS - N_GROUPS)))
    b_router = jnp.pad(jnp.concatenate([b_re, b_rg]), (0, LANES - N_EXPERTS - N_GROUPS)).reshape(1, LANES)
    x1, h2, e_idx, wts = _route(x, y, gt1, row(g_post_mix), row(g_pre_ffn), sc2, sh2, w_router, b_router)

    blk_e, buf_tok, nused, pos = _dispatch_tables(e_idx[:, :TOP_K])
    y_buf = _experts(h2, blk_e, buf_tok, nused, w_gate_e, w_up_e, w_down_e)
    return _combine(pos, y_buf, wts, x1, gt2, row(g_post_ffn))


def kernel(x, c, w_ada, b_ada, g_pre_mix, g_post_mix, g_pre_ffn, g_post_ffn, w_in, mu_r, mu_k, mu_v, mu_w, mu_a, mu_g, w0, w_w1, w_w2, a0, w_a1, w_a2, w_g1, w_g2, k_k, k_a, r_k, gn_w, gn_b, w_up_att, w_up_rwkv, w_o, w_rg, b_rg, w_re, b_re, w_gate_e, w_up_e, w_down_e):
    b, s, d = x.shape
    assert b == 1, "one sequence per call"
    params = (w_ada, b_ada, g_pre_mix, g_post_mix, g_pre_ffn, g_post_ffn, w_in, mu_r, mu_k, mu_v, mu_w, mu_a,
              mu_g, w0, w_w1, w_w2, a0, w_a1, w_a2, w_g1, w_g2, k_k, k_a, r_k, gn_w, gn_b, w_up_att,
              w_up_rwkv, w_o, w_rg, b_rg, w_re, b_re, w_gate_e, w_up_e, w_down_e)
    xs = x.reshape(s, d)
    c_col = c.reshape(d, 1)
    for l in range(w_ada.shape[0]):
        xs = _layer(xs, c_col, *(p[l] for p in params))
    return xs.reshape(b, s, d)
```

```python
import math

import jax
import jax.numpy as jnp
from jax import lax
from jax.experimental import pallas as pl
from jax.experimental.pallas import tpu as pltpu

F32 = jnp.float32
BF16 = jnp.bfloat16
I32 = jnp.int32
HI = lax.Precision.HIGHEST

LANES = 128
SUBLANES = 8
VMEM_LIMIT = 56 * 1024 * 1024

ATT_HEADS = 8
ATT_HEAD_DIM = 128
MOBA_BLOCK = 256
MOBA_TOPK = 3
ATT_GROUP = 4
ATT_HEADS_PER_STEP = 2
ROPE_THETA = 500000.0
ROPE_DIM = ATT_HEAD_DIM // 4
RWKV_HEAD_DIM = 64
GN_EPS = 64e-5
N_GROUPS = 8
EXPERTS_PER_GROUP = 8
N_EXPERTS = N_GROUPS * EXPERTS_PER_GROUP
TOP_K = 2
EXPERT_BLOCK = 128
RMS_EPS = 1e-6
NEG = -1e30
SCAN_CHUNK = 64
Q_SCALE = ATT_HEAD_DIM ** -0.5 * math.log2(math.e)


def _params(*sem):
    return pltpu.CompilerParams(dimension_semantics=sem, vmem_limit_bytes=VMEM_LIMIT)


def _rms(z, g):
    return z * lax.rsqrt(jnp.mean(z * z, axis=-1, keepdims=True) + RMS_EPS) * g


def _sigmoid(z):
    return 1.0 / (1.0 + jnp.exp(-z))


def _dot(a, b):
    return jnp.dot(a.astype(BF16), b.astype(BF16), preferred_element_type=F32)


def _dot_nt(a, b):
    return lax.dot_general(a.astype(BF16), b.astype(BF16), (((1,), (1,)), ((), ())),
                           preferred_element_type=F32)


def _dot_tn(a, b):
    return lax.dot_general(a.astype(BF16), b.astype(BF16), (((0,), (0,)), ((), ())),
                           preferred_element_type=F32)


def _dot_hi(a, b):
    return jnp.dot(a, b, precision=HI, preferred_element_type=F32)


def _split3(a):
    hi = a.astype(BF16)
    r1 = a - hi.astype(F32)
    mid = r1.astype(BF16)
    lo = (r1 - mid.astype(F32)).astype(BF16)
    return hi, mid, lo


def _dot_split(a, b01):
    b = b01.astype(BF16)
    hi, mid, _ = _split3(a)
    return jnp.dot(jnp.concatenate([hi, mid], axis=1), jnp.concatenate([b, b], axis=0),
                   preferred_element_type=F32)


def _dot_split_t(b01, a):
    b = b01.astype(BF16)
    return jnp.dot(jnp.concatenate([b, b, b], axis=1), jnp.concatenate(_split3(a), axis=0),
                   preferred_element_type=F32)


def _shift_rows(z, prev_row):
    rolled = pltpu.roll(z, 1, 0)
    row = lax.broadcasted_iota(I32, z.shape, 0)
    return jnp.where(row == 0, prev_row, rolled)


def _ada_kernel(c_ref, w_ref, b_ref, o_ref):
    o_ref[...] = jnp.sum(c_ref[...] * w_ref[...], axis=0, keepdims=True) + b_ref[...]


def _ada(c_col, w_ada, b_ada):
    d, n = w_ada.shape
    tn = 1024
    return pl.pallas_call(
        _ada_kernel,
        grid=(n // tn,),
        in_specs=[pl.BlockSpec((d, 1), lambda j: (0, 0)),
                  pl.BlockSpec((d, tn), lambda j: (0, j)),
                  pl.BlockSpec((1, tn), lambda j: (0, j))],
        out_specs=pl.BlockSpec((1, tn), lambda j: (0, j)),
        out_shape=jax.ShapeDtypeStruct((1, n), F32),
        name="ada",
        compiler_params=_params("parallel"),
    )(c_col, w_ada, b_ada)


def _prenorm_kernel(x_ref, xp_ref, g_ref, sc_ref, sh_ref, muw_ref, mua_ref, mug_ref,
                    ww1_ref, wa1_ref, wg1_ref, h_ref, lw_ref, la_ref, lg_ref):
    i = pl.program_id(0)
    g, sc, sh = g_ref[...], sc_ref[...], sh_ref[...]
    h = _rms(x_ref[...], g) * (1.0 + sc) + sh
    hp = _rms(xp_ref[SUBLANES - 1:SUBLANES, :], g) * (1.0 + sc) + sh
    hp = jnp.where(i == 0, 0.0, hp)
    dh = _shift_rows(h, hp) - h
    h_ref[...] = h.astype(BF16)
    lw_ref[...] = jnp.tanh(_dot(h + dh * muw_ref[...], ww1_ref[...]))
    la_ref[...] = _dot(h + dh * mua_ref[...], wa1_ref[...])
    lg_ref[...] = _sigmoid(_dot(h + dh * mug_ref[...], wg1_ref[...]))


def _prenorm(x, g, sc, sh, mu_w, mu_a, mu_g, w_w1, w_a1, w_g1):
    s, d = x.shape
    tm = 256
    rpb = tm // SUBLANES
    vec = pl.BlockSpec((1, d), lambda i: (0, 0))
    full = lambda a: pl.BlockSpec(a.shape, lambda i: (0, 0))
    lw, la, lg = w_w1.shape[1], w_a1.shape[1], w_g1.shape[1]
    return pl.pallas_call(
        _prenorm_kernel,
        grid=(s // tm,),
        in_specs=[pl.BlockSpec((tm, d), lambda i: (i, 0)),
                  pl.BlockSpec((SUBLANES, d), lambda i: (jnp.maximum(i * rpb - 1, 0), 0)),
                  vec, vec, vec, vec, vec, vec, full(w_w1), full(w_a1), full(w_g1)],
        out_specs=[pl.BlockSpec((tm, d), lambda i: (i, 0)),
                   pl.BlockSpec((tm, lw), lambda i: (i, 0)),
                   pl.BlockSpec((tm, la), lambda i: (i, 0)),
                   pl.BlockSpec((tm, lg), lambda i: (i, 0))],
        out_shape=[jax.ShapeDtypeStruct((s, d), BF16),
                   jax.ShapeDtypeStruct((s, lw), F32),
                   jax.ShapeDtypeStruct((s, la), F32),
                   jax.ShapeDtypeStruct((s, lg), F32)],
        name="prenorm_lora",
        compiler_params=_params("parallel"),
    )(x, x, g, sc, sh, mu_w, mu_a, mu_g, w_w1, w_a1, w_g1)


def _mm_kernel(a_ref, w_ref, o_ref, wb_ref):
    @pl.when(pl.program_id(1) == 0)
    def _():
        wb_ref[...] = w_ref[...].astype(BF16)

    o_ref[...] = jnp.dot(a_ref[...], wb_ref[...], preferred_element_type=F32).astype(o_ref.dtype)


def _matmul(a, w, out_dtype, tm=512, tn=1024):
    m, k = a.shape
    n = w.shape[1]
    tn = min(tn, n)
    return pl.pallas_call(
        _mm_kernel,
        grid=(n // tn, m // tm),
        in_specs=[pl.BlockSpec((tm, k), lambda j, i: (i, 0)),
                  pl.BlockSpec((k, tn), lambda j, i: (0, j))],
        out_specs=pl.BlockSpec((tm, tn), lambda j, i: (i, j)),
        out_shape=jax.ShapeDtypeStruct((m, n), out_dtype),
        scratch_shapes=[pltpu.VMEM((k, tn), BF16)],
        name="matmul",
        compiler_params=_params("arbitrary", "arbitrary"),
    )(a, w)


def _rope_gate_kernel(p_ref, c_ref, s1_ref, s2_ref, qa_ref, ka_ref, vt_ref, km_ref):
    i = pl.program_id(0)
    bs = MOBA_BLOCK
    nbp = km_ref.shape[1]

    @pl.when(i == 0)
    def _():
        km_ref[...] = jnp.zeros_like(km_ref)

    c, s1, s2 = c_ref[...], s1_ref[...], s2_ref[...]

    def rope(z):
        return z * c + pltpu.roll(z, LANES - ROPE_DIM // 2, 1) * s1 + pltpu.roll(z, ROPE_DIM // 2, 1) * s2

    row = lax.broadcasted_iota(I32, (nbp, bs), 0)
    lane = lax.broadcasted_iota(I32, (bs, LANES), 1)
    onehot = jnp.where(lane == i, 1.0, 0.0).astype(BF16)
    for h in range(ATT_HEADS):
        q = rope(p_ref[:, h * LANES:(h + 1) * LANES].astype(F32))
        k = rope(p_ref[:, (ATT_HEADS + h) * LANES:(ATT_HEADS + h + 1) * LANES].astype(F32))
        g = lax.dot_general(km_ref[h], q, (((1,), (1,)), ((), ())), precision=HI, preferred_element_type=F32)
        g = jnp.where(row < i, g, NEG)
        sel_t = jnp.zeros(g.shape, F32)
        for _ in range(MOBA_TOPK):
            mx = jnp.max(g, axis=0, keepdims=True)
            idx = jnp.min(jnp.where(g == mx, row, nbp), axis=0, keepdims=True)
            hit = row == idx
            sel_t = jnp.where(hit & (row < i), 1.0, sel_t)
            g = jnp.where(hit, -jnp.inf, g)
        if nbp < LANES:
            sel_t = jnp.concatenate([sel_t, jnp.zeros((LANES - nbp, bs), F32)], axis=0)
        w = 2 * LANES
        qa_ref[h, :LANES, :] = (q * Q_SCALE).T.astype(BF16)
        qa_ref[h, LANES:, :] = jnp.where(sel_t > 0.5, 0.0, NEG).astype(BF16)
        ka_ref[:, h * w:h * w + LANES] = k.astype(BF16)
        ka_ref[:, h * w + LANES:(h + 1) * w] = onehot
        v = p_ref[:, (2 * ATT_HEADS + h) * LANES:(2 * ATT_HEADS + h + 1) * LANES].astype(F32)
        vt_ref[h, 0] = v.T.astype(BF16)
        km_ref[h, pl.ds(i, 1), :] = jnp.mean(k, axis=0, keepdims=True)


def _rope_gate(proj, cos_t, sin1_t, sin2_t):
    s = proj.shape[0]
    nb = s // MOBA_BLOCK
    assert nb <= LANES
    nbp = -(-nb // SUBLANES) * SUBLANES
    w_in = 3 * ATT_HEADS * ATT_HEAD_DIM
    w_out = 2 * ATT_HEADS * LANES
    tab = pl.BlockSpec((MOBA_BLOCK, LANES), lambda i: (i, 0))
    return pl.pallas_call(
        _rope_gate_kernel,
        grid=(nb,),
        in_specs=[pl.BlockSpec((MOBA_BLOCK, w_in), lambda i: (i, 0)), tab, tab, tab],
        out_specs=[pl.BlockSpec((ATT_HEADS, 2 * LANES, MOBA_BLOCK), lambda i: (0, 0, i)),
                   pl.BlockSpec((MOBA_BLOCK, w_out), lambda i: (i, 0)),
                   pl.BlockSpec((ATT_HEADS, 1, LANES, MOBA_BLOCK), lambda i: (0, i, 0, 0))],
        out_shape=[jax.ShapeDtypeStruct((ATT_HEADS, 2 * LANES, s), BF16),
                   jax.ShapeDtypeStruct((s, w_out), BF16),
                   jax.ShapeDtypeStruct((ATT_HEADS, nb, LANES, MOBA_BLOCK), BF16)],
        scratch_shapes=[pltpu.VMEM((ATT_HEADS, nbp, LANES), F32)],
        name="rope_gate",
        compiler_params=_params("arbitrary"),
    )(proj, cos_t, sin1_t, sin2_t)


def _attn_kernel(qa_ref, ka_ref, vt_ref, o_ref, s_ref):
    qi = pl.program_id(1)
    bs = MOBA_BLOCK
    grp = ATT_GROUP * bs
    w = 2 * LANES
    heads = range(ATT_HEADS_PER_STEP)

    n_groups = ka_ref.shape[0] // grp
    assert n_groups % 2 == 0

    def issue_scores(g, buf):
        base = pl.multiple_of(jnp.minimum(g, n_groups - 1) * grp, grp)
        for h in heads:
            s_ref[buf, h] = jnp.dot(ka_ref[pl.ds(base, grp), h * w:(h + 1) * w], qa_ref[h],
                                    preferred_element_type=F32)

    issue_scores(0, 0)

    own = pl.multiple_of(qi * bs, bs)
    k_i = lax.broadcasted_iota(I32, (bs, bs), 0)
    q_i = lax.broadcasted_iota(I32, (bs, bs), 1)
    carry = []
    for h in heads:
        s = jnp.dot(ka_ref[pl.ds(own, bs), h * w:h * w + LANES], qa_ref[h, :LANES, :],
                    preferred_element_type=F32)
        s = jnp.where(k_i <= q_i, s, NEG)
        m = jnp.max(s, axis=0, keepdims=True)
        p = jnp.exp2(s - m)
        l = jnp.sum(p, axis=0, keepdims=True)
        acc = jnp.dot(vt_ref[h, qi], p.astype(BF16), preferred_element_type=F32)
        carry += [m, l, acc]

    def absorb(g, buf, carry):
        out = []
        for h in heads:
            m, l, acc = carry[3 * h:3 * h + 3]
            s = s_ref[buf, h]
            mn = jnp.maximum(m, jnp.max(s, axis=0, keepdims=True))
            alpha = jnp.exp2(m - mn)
            p = jnp.exp2(s - mn)
            l = alpha * l + jnp.sum(p, axis=0, keepdims=True)
            p = p.astype(BF16)
            acc = alpha * acc
            for jb in range(ATT_GROUP):
                acc = acc + jnp.dot(vt_ref[h, g * ATT_GROUP + jb], p[jb * bs:(jb + 1) * bs],
                                    preferred_element_type=F32)
            out += [mn, l, acc]
        return out

    def body(t, carry):
        issue_scores(2 * t + 1, 1)
        carry = absorb(2 * t, 0, carry)
        issue_scores(2 * t + 2, 0)
        return tuple(absorb(2 * t + 1, 1, carry))

    n_used = (qi + ATT_GROUP - 1) // ATT_GROUP
    carry = lax.fori_loop(0, (n_used + 1) // 2, body, tuple(carry))
    for h in heads:
        m, l, acc = carry[3 * h:3 * h + 3]
        o_ref[:, h * LANES:(h + 1) * LANES] = (acc / l).T.astype(BF16)


def _attention(q_aug_t, k_aug, v_t):
    s = k_aug.shape[0]
    assert s % (ATT_GROUP * MOBA_BLOCK) == 0
    nb = s // MOBA_BLOCK
    hps = ATT_HEADS_PER_STEP
    once = pl.Buffered(1)
    return pl.pallas_call(
        _attn_kernel,
        grid=(ATT_HEADS // hps, nb),
        in_specs=[pl.BlockSpec((hps, 2 * LANES, MOBA_BLOCK), lambda h, i: (h, 0, i)),
                  pl.BlockSpec((s, hps * 2 * LANES), lambda h, i: (0, h), pipeline_mode=once),
                  pl.BlockSpec((hps, nb, LANES, MOBA_BLOCK), lambda h, i: (h, 0, 0, 0), pipeline_mode=once)],
        out_specs=pl.BlockSpec((MOBA_BLOCK, hps * LANES), lambda h, i: (i, h)),
        out_shape=jax.ShapeDtypeStruct((s, ATT_HEADS * ATT_HEAD_DIM), BF16),
        scratch_shapes=[pltpu.VMEM((2, hps, ATT_GROUP * MOBA_BLOCK, MOBA_BLOCK), F32)],
        name="moba_attention",
        compiler_params=_params("parallel", "arbitrary"),
    )(q_aug_t, k_aug, v_t)


def _rwkv_kernel(r_ref, k_ref, v_ref, lw_ref, la_ref, lg_ref, ww2_ref, wa2_ref, wg2_ref,
                 mur_ref, muk_ref, muv_ref, w0_ref, a0_ref, kk_ref, ka_ref, rk_ref, gnw_ref, gnb_ref,
                 o_ref, st_ref, prev_ref):
    j = pl.program_id(1)
    t = SCAN_CHUNK
    n = RWKV_HEAD_DIM
    ts = r_ref.shape[0]
    chunks = range(ts // t)

    @pl.when(j == 0)
    def _():
        st_ref[...] = jnp.zeros_like(st_ref)
        prev_ref[...] = jnp.zeros_like(prev_ref)

    rp, kp, vp = r_ref[...].astype(F32), k_ref[...].astype(F32), v_ref[...].astype(F32)
    r = rp + (_shift_rows(rp, prev_ref[0:1, :]) - rp) * mur_ref[...]
    k = kp + (_shift_rows(kp, prev_ref[1:2, :]) - kp) * muk_ref[...]
    v = vp + (_shift_rows(vp, prev_ref[2:3, :]) - vp) * muv_ref[...]
    prev_ref[0:1, :] = rp[ts - 1:ts, :]
    prev_ref[1:2, :] = kp[ts - 1:ts, :]
    prev_ref[2:3, :] = vp[ts - 1:ts, :]

    d = w0_ref[...] + _dot(lw_ref[...], ww2_ref[...])
    logw = -math.exp(-0.5) * _sigmoid(d)
    a = _sigmoid(a0_ref[...] + _dot(la_ref[...], wa2_ref[...]))
    g = _dot(lg_ref[...], wg2_ref[...])

    li = lax.broadcasted_iota(I32, (LANES, LANES), 0)
    lj = lax.broadcasted_iota(I32, (LANES, LANES), 1)
    same_head = (li // n) == (lj // n)
    head_sum = jnp.where(same_head, 1.0, 0.0)

    kk = k * kk_ref[...]
    kk = kk / jnp.maximum(jnp.sqrt(_dot_split(kk * kk, head_sum)), 1e-12)
    kt = k * (1.0 + (a - 1.0) * ka_ref[...])
    bonus = _dot_split(r * kt * rk_ref[...], head_sum) * v

    same_blk = (li // t) == (lj // t)
    m_strict = same_blk & (lj < li)
    m_incl = same_blk & (lj <= li)
    ti = lax.broadcasted_iota(I32, (t, t), 0)
    tj = lax.broadcasted_iota(I32, (t, t), 1)
    tril_incl = jnp.where(tj <= ti, 1.0, 0.0)
    lane_a = lax.broadcasted_iota(I32, (t, LANES), 1) < n
    eye = jnp.where(li == lj, 1.0, 0.0)

    def stack_masked(z):
        return jnp.concatenate([jnp.where(lane_a, z, 0.0), jnp.where(lane_a, 0.0, z)], axis=0)

    def stack_plain(z):
        return jnp.concatenate([z, z], axis=0)

    def rows(z, c):
        return z[c * t:(c + 1) * t]

    cum = jnp.concatenate([_dot_split_t(tril_incl, rows(logw, c)) for c in chunks], axis=0)
    g_t = jnp.exp(cum)
    g_inv = jnp.exp(-cum)
    xa_f = -kk * jnp.exp(cum - logw)
    xr_f = r * g_t
    yb_f = kk * a * g_inv
    yk_f = kt * g_inv

    xa = [stack_masked(rows(xa_f, c)) for c in chunks]
    xr = [stack_masked(rows(xr_f, c)) for c in chunks]
    yb = [stack_plain(rows(yb_f, c)) for c in chunks]
    vs = [stack_masked(rows(v, c)) for c in chunks]
    ybk = [jnp.concatenate([yb[c], stack_plain(rows(yk_f, c))], axis=0) for c in chunks]
    sc = [_dot_nt(jnp.concatenate([xa[c], xr[c]], axis=0), ybk[c]) for c in chunks]
    a_ab = [jnp.where(m_strict, sc[c][:2 * t, :2 * t], 0.0) for c in chunks]
    a_ak = [jnp.where(m_strict, sc[c][:2 * t, 2 * t:], 0.0) for c in chunks]
    a_rb = [jnp.where(m_incl, sc[c][2 * t:, :2 * t], 0.0) for c in chunks]
    a_rk = [jnp.where(m_incl, sc[c][2 * t:, 2 * t:], 0.0) for c in chunks]
    inv = [eye + a_ab[c] for c in chunks]
    pw = [_dot(a_ab[c], a_ab[c]) for c in chunks]
    for _ in range(int(math.log2(t)) - 2):
        both = [_dot(pw[c], jnp.concatenate([pw[c], inv[c]], axis=1)) for c in chunks]
        pw = [both[c][:, :LANES] for c in chunks]
        inv = [inv[c] + both[c][:, LANES:] for c in chunks]
    inv = [inv[c] + _dot(pw[c], inv[c]) for c in chunks]
    av = [_dot(a_ak[c], vs[c]) for c in chunks]
    mw = [_dot(inv[c], jnp.concatenate([xa[c], av[c]], axis=1)) for c in chunks]
    zeros = jnp.zeros((2 * t, LANES), F32)
    rw = [_dot(jnp.concatenate([a_rb[c], a_rk[c]], axis=1),
               jnp.concatenate([mw[c], jnp.concatenate([zeros, vs[c]], axis=1)], axis=0)) for c in chunks]
    r_hat = [xr[c] + rw[c][:, :LANES] for c in chunks]
    y_hat = [rw[c][:, LANES:] for c in chunks]
    pm = [jnp.where(same_head, _dot_tn(mw[c][:, :LANES], yb[c]), 0.0) for c in chunks]
    qm = [jnp.where(same_head, _dot_tn(jnp.concatenate([mw[c][:, LANES:], vs[c]], axis=0), ybk[c]), 0.0)
          for c in chunks]

    st = st_ref[...]
    ys = []
    for c in chunks:
        y2 = _dot_nt(r_hat[c], st) + y_hat[c]
        ys.append(y2[:t] + y2[t:])
        st = (st + _dot(st, pm[c]) + qm[c]) * g_t[(c + 1) * t - 1:(c + 1) * t, :]
    st_ref[...] = st
    y = jnp.concatenate(ys, axis=0)

    mean = _dot_split(y, head_sum) * (1.0 / n)
    yc = y - mean
    var = _dot_split(yc * yc, head_sum) * (1.0 / n)
    yn = yc * lax.rsqrt(var + GN_EPS) * gnw_ref[...] + gnb_ref[...]
    o_ref[...] = ((yn + bonus) * g).astype(BF16)


def _rwkv(proj, lw, la, lg, w_w2, w_a2, w_g2, mu_r, mu_k, mu_v, w0, a0, k_k, k_a, r_k, gn_w, gn_b,
          col0, width):
    s = proj.shape[0]
    ts = 512
    npair = width // LANES
    cb = col0 // LANES
    row = lambda a: pl.BlockSpec((ts, a.shape[1]), lambda p, j: (j, 0))
    wcol = lambda a: pl.BlockSpec((a.shape[0], LANES), lambda p, j: (0, p))
    vec = pl.BlockSpec((1, LANES), lambda p, j: (0, p))
    return pl.pallas_call(
        _rwkv_kernel,
        grid=(npair, s // ts),
        in_specs=[pl.BlockSpec((ts, LANES), lambda p, j: (j, cb + p)),
                  pl.BlockSpec((ts, LANES), lambda p, j: (j, cb + npair + p)),
                  pl.BlockSpec((ts, LANES), lambda p, j: (j, cb + 2 * npair + p)),
                  row(lw), row(la), row(lg), wcol(w_w2), wcol(w_a2), wcol(w_g2)] + [vec] * 10,
        out_specs=pl.BlockSpec((ts, LANES), lambda p, j: (j, p)),
        out_shape=jax.ShapeDtypeStruct((s, width), BF16),
        scratch_shapes=[pltpu.VMEM((LANES, LANES), F32), pltpu.VMEM((SUBLANES, LANES), F32)],
        name="rwkv7_scan",
        compiler_params=_params("parallel", "arbitrary"),
    )(proj, proj, proj, lw, la, lg, w_w2, w_a2, w_g2, mu_r, mu_k, mu_v, w0, a0, k_k, k_a, r_k, gn_w, gn_b)


def _merge_kernel(oa_ref, or_ref, ga_ref, gr_ref, wa_ref, wr_ref, o_ref, wab_ref, wrb_ref):
    @pl.when(pl.program_id(1) == 0)
    def _():
        wab_ref[...] = wa_ref[...].astype(BF16)
        wrb_ref[...] = wr_ref[...].astype(BF16)

    ua = jnp.dot(oa_ref[...], wab_ref[...], preferred_element_type=F32)
    ur = jnp.dot(or_ref[...], wrb_ref[...], preferred_element_type=F32)
    mix = _sigmoid(ga_ref[...].astype(F32)) * ua + _sigmoid(gr_ref[...].astype(F32)) * ur
    o_ref[...] = mix.astype(BF16)


def _merge(o_att, o_rwkv, proj, w_up_att, w_up_rwkv, gate_col0):
    s, ka = o_att.shape
    kr = o_rwkv.shape[1]
    d = w_up_att.shape[1]
    tm, tn = 512, 1024
    gb = gate_col0 // tn
    return pl.pallas_call(
        _merge_kernel,
        grid=(d // tn, s // tm),
        in_specs=[pl.BlockSpec((tm, ka), lambda j, i: (i, 0)),
                  pl.BlockSpec((tm, kr), lambda j, i: (i, 0)),
                  pl.BlockSpec((tm, tn), lambda j, i: (i, gb + j)),
                  pl.BlockSpec((tm, tn), lambda j, i: (i, gb + d // tn + j)),
                  pl.BlockSpec((ka, tn), lambda j, i: (0, j)),
                  pl.BlockSpec((kr, tn), lambda j, i: (0, j))],
        out_specs=pl.BlockSpec((tm, tn), lambda j, i: (i, j)),
        out_shape=jax.ShapeDtypeStruct((s, d), BF16),
        scratch_shapes=[pltpu.VMEM((ka, tn), BF16), pltpu.VMEM((kr, tn), BF16)],
        name="gated_merge",
        compiler_params=_params("arbitrary", "arbitrary"),
    )(o_att, o_rwkv, proj, proj, w_up_att, w_up_rwkv)


def _route_kernel(x_ref, y_ref, gt_ref, gpost_ref, gpre_ref, sc_ref, sh_ref, wr_ref, br_ref,
                  x1_ref, h2_ref, ei_ref, wt_ref):
    x1 = x_ref[...] + gt_ref[...] * _rms(y_ref[...], gpost_ref[...])
    x1_ref[...] = x1
    h2 = _rms(x1, gpre_ref[...]) * (1.0 + sc_ref[...]) + sh_ref[...]
    h2_ref[...] = h2
    logits = _dot_hi(h2, wr_ref[...]) + br_ref[...]
    lane = lax.broadcasted_iota(I32, logits.shape, 1)
    big = jnp.int32(4 * LANES)
    gmask = (lane >= N_EXPERTS) & (lane < N_EXPERTS + N_GROUPS)
    mg = jnp.max(jnp.where(gmask, logits, -jnp.inf), axis=1, keepdims=True)
    eg = jnp.where(gmask, jnp.exp(logits - mg), 0.0)
    pg = eg / jnp.sum(eg, axis=1, keepdims=True)
    pg_top = jnp.max(pg, axis=1, keepdims=True)
    g_idx = jnp.min(jnp.where(gmask & (pg == pg_top), lane, big), axis=1, keepdims=True) - N_EXPERTS
    emask = (lane >= g_idx * EXPERTS_PER_GROUP) & (lane < (g_idx + 1) * EXPERTS_PER_GROUP)
    me = jnp.max(jnp.where(emask, logits, -jnp.inf), axis=1, keepdims=True)
    ee = jnp.where(emask, jnp.exp(logits - me), 0.0)
    pe = ee / jnp.sum(ee, axis=1, keepdims=True)
    p1 = jnp.max(pe, axis=1, keepdims=True)
    i1 = jnp.min(jnp.where(emask & (pe == p1), lane, big), axis=1, keepdims=True)
    rest = emask & (lane != i1)
    p2 = jnp.max(jnp.where(rest, pe, -jnp.inf), axis=1, keepdims=True)
    i2 = jnp.min(jnp.where(rest & (pe == p2), lane, big), axis=1, keepdims=True)
    den = p1 + p2
    ei_ref[...] = jnp.where(lane == 0, i1, jnp.where(lane == 1, i2, 0))
    wt_ref[...] = jnp.where(lane == 0, pg_top * p1 / den, jnp.where(lane == 1, pg_top * p2 / den, 0.0))


def _route(x, y, gt1, g_post, g_pre, sc2, sh2, w_router, b_router):
    s, d = x.shape
    tm = 256
    vec = pl.BlockSpec((1, d), lambda i: (0, 0))
    rowblk = pl.BlockSpec((tm, d), lambda i: (i, 0))
    small = pl.BlockSpec((tm, LANES), lambda i: (i, 0))
    return pl.pallas_call(
        _route_kernel,
        grid=(s // tm,),
        in_specs=[rowblk, rowblk, vec, vec, vec, vec, vec,
                  pl.BlockSpec((d, LANES), lambda i: (0, 0)),
                  pl.BlockSpec((1, LANES), lambda i: (0, 0))],
        out_specs=[rowblk, rowblk, small, small],
        out_shape=[jax.ShapeDtypeStruct((s, d), F32), jax.ShapeDtypeStruct((s, d), F32),
                   jax.ShapeDtypeStruct((s, LANES), I32), jax.ShapeDtypeStruct((s, LANES), F32)],
        name="norm_route",
        compiler_params=_params("parallel"),
    )(x, y, gt1, g_post, g_pre, sc2, sh2, w_router, b_router)


def _row_copy(src_hbm, row, dst, dst_row, sem):
    return pltpu.make_async_copy(src_hbm.at[pl.ds(row, 1), :], dst.at[pl.ds(dst_row, 1), :], sem)


def _expert_kernel(bstart_ref, bcount_ref, tok_ref, nused_ref, h_hbm, wg_ref, wu_ref, wd_ref, y_hbm,
                   xg_ref, yo_ref, wgb_ref, wub_ref, wdb_ref, gsem, osem):
    e = pl.program_id(0)
    nused = nused_ref[0]
    rows = EXPERT_BLOCK
    n_blk = y_hbm.shape[0] // rows
    first = bstart_ref[e]
    count = bcount_ref[e]

    def start_rows(blk, slot):
        for i in range(rows):
            _row_copy(h_hbm, tok_ref[blk * rows + i], xg_ref.at[slot], i, gsem.at[slot]).start()

    def wait_rows(slot):
        pltpu.make_async_copy(h_hbm.at[pl.ds(0, rows), :], xg_ref.at[slot], gsem.at[slot]).wait()

    def out_copy(blk, slot):
        return pltpu.make_async_copy(yo_ref.at[slot], y_hbm.at[pl.ds(blk * rows, rows), :], osem.at[slot])

    @pl.when(e == 0)
    def _():
        start_rows(0, 0)

    @pl.when(count > 0)
    def _():
        wgb_ref[...] = wg_ref[0].astype(BF16)
        wub_ref[...] = wu_ref[0].astype(BF16)
        wdb_ref[...] = wd_ref[0].astype(BF16)

    def block(j, _):
        blk = first + j
        slot = blk % 2

        @pl.when(blk >= 2)
        def _():
            out_copy(blk - 2, slot).wait()

        wait_rows(slot)
        start_rows(jnp.minimum(blk + 1, nused - 1), 1 - slot)
        xb = xg_ref[slot].astype(BF16)
        hg = jnp.dot(xb, wgb_ref[...], preferred_element_type=F32)
        hu = jnp.dot(xb, wub_ref[...], preferred_element_type=F32)
        hid = hg * _sigmoid(hg) * hu
        yo_ref[slot] = jnp.dot(hid.astype(BF16), wdb_ref[...], preferred_element_type=F32)
        out_copy(blk, slot).start()

        @pl.when(blk == nused - 1)
        def _():
            wait_rows(1 - slot)

        return 0

    lax.fori_loop(0, count, block, 0)

    @pl.when(e == pl.num_programs(0) - 1)
    def _():
        @pl.when(nused >= 2)
        def _():
            out_copy(nused - 2, nused % 2).wait()

        out_copy(nused - 1, (nused - 1) % 2).wait()
        yo_ref[0] = jnp.zeros(yo_ref.shape[1:], F32)

        def fill(blk, _):
            out_copy(blk, 0).start()
            return 0

        def drain(blk, _):
            out_copy(blk, 0).wait()
            return 0

        lax.fori_loop(nused, n_blk, fill, 0)
        lax.fori_loop(nused, n_blk, drain, 0)


def _experts(h2, bstart, bcount, buf_tok, nused, w_gate_e, w_up_e, w_down_e):
    n, d = h2.shape
    n_exp, _, f = w_gate_e.shape
    n_slots = buf_tok.shape[0]
    grid_spec = pltpu.PrefetchScalarGridSpec(
        num_scalar_prefetch=4,
        grid=(n_exp,),
        in_specs=[pl.BlockSpec(memory_space=pl.ANY),
                  pl.BlockSpec((1, d, f), lambda e, *_: (e, 0, 0)),
                  pl.BlockSpec((1, d, f), lambda e, *_: (e, 0, 0)),
                  pl.BlockSpec((1, f, d), lambda e, *_: (e, 0, 0))],
        out_specs=pl.BlockSpec(memory_space=pl.ANY),
        scratch_shapes=[pltpu.VMEM((2, EXPERT_BLOCK, d), F32), pltpu.VMEM((2, EXPERT_BLOCK, d), F32),
                        pltpu.VMEM((d, f), BF16), pltpu.VMEM((d, f), BF16), pltpu.VMEM((f, d), BF16),
                        pltpu.SemaphoreType.DMA((2,)), pltpu.SemaphoreType.DMA((2,))],
    )
    return pl.pallas_call(
        _expert_kernel,
        grid_spec=grid_spec,
        out_shape=jax.ShapeDtypeStruct((n_slots, d), F32),
        name="experts",
        compiler_params=_params("arbitrary"),
    )(bstart, bcount, buf_tok, nused, h2, w_gate_e, w_up_e, w_down_e)


def _combine_kernel(pos_ref, y_hbm, wt_ref, x1_ref, gt_ref, gpost_ref, o_ref, rows_ref, sem):
    i = pl.program_id(0)
    nsteps = pl.num_programs(0)
    tm = x1_ref.shape[0]

    def start_rows(step, slot):
        for r in range(tm):
            for kk in range(TOP_K):
                _row_copy(y_hbm, pos_ref[(step * tm + r) * TOP_K + kk], rows_ref.at[slot, kk], r,
                          sem.at[slot]).start()

    def wait_rows(slot):
        for kk in range(TOP_K):
            pltpu.make_async_copy(y_hbm.at[pl.ds(0, tm), :], rows_ref.at[slot, kk], sem.at[slot]).wait()

    @pl.when(i == 0)
    def _():
        start_rows(0, 0)

    slot = i % 2
    wait_rows(slot)
    start_rows(jnp.minimum(i + 1, nsteps - 1), 1 - slot)
    wt = wt_ref[...]
    y = rows_ref[slot, 0] * wt[:, 0:1] + rows_ref[slot, 1] * wt[:, 1:2]
    o_ref[...] = x1_ref[...] + gt_ref[...] * _rms(y, gpost_ref[...])

    @pl.when(i == nsteps - 1)
    def _():
        wait_rows(1 - slot)


def _combine(pos, y_buf, wts, x1, gt2, g_post):
    n, d = x1.shape
    tm = 128
    vec = pl.BlockSpec((1, d), lambda i, p: (0, 0))
    grid_spec = pltpu.PrefetchScalarGridSpec(
        num_scalar_prefetch=1,
        grid=(n // tm,),
        in_specs=[pl.BlockSpec(memory_space=pl.ANY),
                  pl.BlockSpec((tm, LANES), lambda i, p: (i, 0)),
                  pl.BlockSpec((tm, d), lambda i, p: (i, 0)), vec, vec],
        out_specs=pl.BlockSpec((tm, d), lambda i, p: (i, 0)),
        scratch_shapes=[pltpu.VMEM((2, TOP_K, tm, d), F32), pltpu.SemaphoreType.DMA((2,))],
    )
    return pl.pallas_call(
        _combine_kernel,
        grid_spec=grid_spec,
        out_shape=jax.ShapeDtypeStruct((n, d), F32),
        name="combine",
        compiler_params=_params("arbitrary"),
    )(pos, y_buf, wts, x1, gt2, g_post)


def _dispatch_tables(e_idx):
    n = e_idx.shape[0]
    p = n * TOP_K
    n_blk = (p + N_EXPERTS * (EXPERT_BLOCK - 1) + EXPERT_BLOCK - 1) // EXPERT_BLOCK
    bits = (p - 1).bit_length()
    mask = (1 << bits) - 1
    e_flat = e_idx.reshape(p)
    pair = jnp.arange(p, dtype=I32)
    keys = jnp.sort((e_flat << bits) | pair)
    pair_s, e_s = keys & mask, keys >> bits
    rank = jnp.sort((pair_s << bits) | pair) & mask
    counts = jnp.sum((e_flat[:, None] == jnp.arange(N_EXPERTS, dtype=I32)[None, :]).astype(I32), axis=0)
    start = jnp.cumsum(counts) - counts
    pcounts = (counts + EXPERT_BLOCK - 1) // EXPERT_BLOCK * EXPERT_BLOCK
    pend = jnp.cumsum(pcounts)
    pstart = pend - pcounts
    pos = pstart[e_flat] + (rank - start[e_flat])
    slot = jnp.arange(n_blk * EXPERT_BLOCK, dtype=I32)
    slot_e = jnp.clip(jnp.searchsorted(pend, slot, side='right'), 0, N_EXPERTS - 1).astype(I32)
    local = slot - pstart[slot_e]
    valid = (slot < pend[-1]) & (local < counts[slot_e])
    src = jnp.clip(start[slot_e] + local, 0, p - 1)
    buf_tok = jnp.where(valid, pair_s[src] // TOP_K, 0)
    nused = (pend[-1] // EXPERT_BLOCK).astype(I32)
    return pstart // EXPERT_BLOCK, pcounts // EXPERT_BLOCK, buf_tok, nused.reshape(1), pos


def _rope_tables(s):
    half = ROPE_DIM // 2
    inv = ROPE_THETA ** (-jnp.arange(half, dtype=F32) / half)
    ang = jnp.arange(s, dtype=F32)[:, None] * inv[None, :]
    cos, sin = lax.optimization_barrier((jnp.cos(ang), jnp.sin(ang)))
    pad = jnp.zeros((s, LANES - ROPE_DIM), F32)
    zero = jnp.zeros((s, half), F32)
    cos_t = jnp.concatenate([cos, cos, pad + 1.0], axis=1)
    sin1_t = jnp.concatenate([-sin, zero, pad], axis=1)
    sin2_t = jnp.concatenate([zero, sin, pad], axis=1)
    return cos_t, sin1_t, sin2_t


def _layer(x, c_col, w_ada, b_ada, g_pre_mix, g_post_mix, g_pre_ffn, g_post_ffn, w_in, mu_r, mu_k, mu_v,
           mu_w, mu_a, mu_g, w0, w_w1, w_w2, a0, w_a1, w_a2, w_g1, w_g2, k_k, k_a, r_k, gn_w, gn_b,
           w_up_att, w_up_rwkv, w_o, w_rg, b_rg, w_re, b_re, w_gate_e, w_up_e, w_down_e):
    s, d = x.shape
    att_w = ATT_HEADS * ATT_HEAD_DIM
    rwkv_w = w_up_rwkv.shape[0]
    row = lambda a: a.reshape(1, -1)

    ada = _ada(c_col, w_ada, row(b_ada))
    sh1, sc1, gt1, sh2, sc2, gt2 = (ada[:, i * d:(i + 1) * d] for i in range(6))

    h, lw, la, lg = _prenorm(x, row(g_pre_mix), sc1, sh1, row(mu_w), row(mu_a), row(mu_g), w_w1, w_a1, w_g1)
    proj = _matmul(h, w_in, BF16)

    q_aug_t, k_aug, v_t = _rope_gate(proj, *_rope_tables(s))
    o_att = _attention(q_aug_t, k_aug, v_t)

    o_rwkv = _rwkv(proj, lw, la, lg, w_w2, w_a2, w_g2, row(mu_r), row(mu_k), row(mu_v), row(w0), row(a0),
                   row(k_k), row(k_a), row(r_k), row(gn_w), row(gn_b), col0=3 * att_w, width=rwkv_w)

    mix = _merge(o_att, o_rwkv, proj, w_up_att, w_up_rwkv, gate_col0=3 * att_w + 3 * rwkv_w)
    y = _matmul(mix, w_o, F32)

    w_router = jnp.pad(jnp.concatenate([w_re, w_rg], axis=1), ((0, 0), (0, LANES - N_EXPERTS - N_GROUPS)))
    b_router = jnp.pad(jnp.concatenate([b_re, b_rg]), (0, LANES - N_EXPERTS - N_GROUPS)).reshape(1, LANES)
    x1, h2, e_idx, wts = _route(x, y, gt1, row(g_post_mix), row(g_pre_ffn), sc2, sh2, w_router, b_router)

    bstart, bcount, buf_tok, nused, pos = _dispatch_tables(e_idx[:, :TOP_K])
    y_buf = _experts(h2, bstart, bcount, buf_tok, nused, w_gate_e, w_up_e, w_down_e)
    return _combine(pos, y_buf, wts, x1, gt2, row(g_post_ffn))


def kernel(x, c, w_ada, b_ada, g_pre_mix, g_post_mix, g_pre_ffn, g_post_ffn, w_in, mu_r, mu_k, mu_v, mu_w, mu_a, mu_g, w0, w_w1, w_w2, a0, w_a1, w_a2, w_g1, w_g2, k_k, k_a, r_k, gn_w, gn_b, w_up_att, w_up_rwkv, w_o, w_rg, b_rg, w_re, b_re, w_gate_e, w_up_e, w_down_e):
    b, s, d = x.shape
    assert b == 1, "one sequence per call"
    params = (w_ada, b_ada, g_pre_mix, g_post_mix, g_pre_ffn, g_post_ffn, w_in, mu_r, mu_k, mu_v, mu_w, mu_a,
              mu_g, w0, w_w1, w_w2, a0, w_a1, w_a2, w_g1, w_g2, k_k, k_a, r_k, gn_w, gn_b, w_up_att,
              w_up_rwkv, w_o, w_rg, b_rg, w_re, b_re, w_gate_e, w_up_e, w_down_e)
    xs = x.reshape(s, d)
    c_col = c.reshape(d, 1)
    for l in range(w_ada.shape[0]):
        xs = _layer(xs, c_col, *(p[l] for p in params))
    return xs.reshape(b, s, d)
```

```python
import math

import jax
import jax.numpy as jnp
from jax import lax
from jax.experimental import pallas as pl
from jax.experimental.pallas import tpu as pltpu

F32 = jnp.float32
BF16 = jnp.bfloat16
I32 = jnp.int32
HI = lax.Precision.HIGHEST

LANES = 128
SUBLANES = 8
VMEM_LIMIT = 56 * 1024 * 1024

ATT_HEADS = 8
ATT_HEAD_DIM = 128
MOBA_BLOCK = 256
MOBA_TOPK = 3
ATT_GROUP = 4
ATT_HEADS_PER_STEP = 2
ROPE_THETA = 500000.0
ROPE_DIM = ATT_HEAD_DIM // 4
RWKV_HEAD_DIM = 64
GN_EPS = 64e-5
N_GROUPS = 8
EXPERTS_PER_GROUP = 8
N_EXPERTS = N_GROUPS * EXPERTS_PER_GROUP
TOP_K = 2
EXPERT_BLOCK = 128
RMS_EPS = 1e-6
NEG = -1e30
SCAN_CHUNK = 64
Q_SCALE = ATT_HEAD_DIM ** -0.5 * math.log2(math.e)


def _params(*sem):
    return pltpu.CompilerParams(dimension_semantics=sem, vmem_limit_bytes=VMEM_LIMIT)


def _rms(z, g):
    return z * lax.rsqrt(jnp.mean(z * z, axis=-1, keepdims=True) + RMS_EPS) * g


def _sigmoid(z):
    return 1.0 / (1.0 + jnp.exp(-z))


def _dot(a, b):
    return jnp.dot(a.astype(BF16), b.astype(BF16), preferred_element_type=F32)


def _dot_nt(a, b):
    return lax.dot_general(a.astype(BF16), b.astype(BF16), (((1,), (1,)), ((), ())),
                           preferred_element_type=F32)


def _dot_tn(a, b):
    return lax.dot_general(a.astype(BF16), b.astype(BF16), (((0,), (0,)), ((), ())),
                           preferred_element_type=F32)


def _dot_hi(a, b):
    return jnp.dot(a, b, precision=HI, preferred_element_type=F32)


def _split3(a):
    hi = a.astype(BF16)
    r1 = a - hi.astype(F32)
    mid = r1.astype(BF16)
    lo = (r1 - mid.astype(F32)).astype(BF16)
    return hi, mid, lo


def _dot_split(a, b01):
    b = b01.astype(BF16)
    hi, mid, _ = _split3(a)
    return jnp.dot(jnp.concatenate([hi, mid], axis=1), jnp.concatenate([b, b], axis=0),
                   preferred_element_type=F32)


def _dot_split_t(b01, a):
    b = b01.astype(BF16)
    return jnp.dot(jnp.concatenate([b, b, b], axis=1), jnp.concatenate(_split3(a), axis=0),
                   preferred_element_type=F32)


def _shift_rows(z, prev_row):
    rolled = pltpu.roll(z, 1, 0)
    row = lax.broadcasted_iota(I32, z.shape, 0)
    return jnp.where(row == 0, prev_row, rolled)


def _ada_kernel(c_ref, w_ref, b_ref, o_ref):
    o_ref[...] = jnp.sum(c_ref[...] * w_ref[...], axis=0, keepdims=True) + b_ref[...]


def _ada(c_col, w_ada, b_ada):
    d, n = w_ada.shape
    tn = 1024
    return pl.pallas_call(
        _ada_kernel,
        grid=(n // tn,),
        in_specs=[pl.BlockSpec((d, 1), lambda j: (0, 0)),
                  pl.BlockSpec((d, tn), lambda j: (0, j)),
                  pl.BlockSpec((1, tn), lambda j: (0, j))],
        out_specs=pl.BlockSpec((1, tn), lambda j: (0, j)),
        out_shape=jax.ShapeDtypeStruct((1, n), F32),
        name="ada",
        compiler_params=_params("parallel"),
    )(c_col, w_ada, b_ada)


def _prenorm_kernel(x_ref, xp_ref, g_ref, sc_ref, sh_ref, muw_ref, mua_ref, mug_ref,
                    ww1_ref, wa1_ref, wg1_ref, h_ref, lw_ref, la_ref, lg_ref):
    i = pl.program_id(0)
    g, sc, sh = g_ref[...], sc_ref[...], sh_ref[...]
    h = _rms(x_ref[...], g) * (1.0 + sc) + sh
    hp = _rms(xp_ref[SUBLANES - 1:SUBLANES, :], g) * (1.0 + sc) + sh
    hp = jnp.where(i == 0, 0.0, hp)
    dh = _shift_rows(h, hp) - h
    h_ref[...] = h.astype(BF16)
    lw_ref[...] = jnp.tanh(_dot(h + dh * muw_ref[...], ww1_ref[...]))
    la_ref[...] = _dot(h + dh * mua_ref[...], wa1_ref[...])
    lg_ref[...] = _sigmoid(_dot(h + dh * mug_ref[...], wg1_ref[...]))


def _prenorm(x, g, sc, sh, mu_w, mu_a, mu_g, w_w1, w_a1, w_g1):
    s, d = x.shape
    tm = 256
    rpb = tm // SUBLANES
    vec = pl.BlockSpec((1, d), lambda i: (0, 0))
    full = lambda a: pl.BlockSpec(a.shape, lambda i: (0, 0))
    lw, la, lg = w_w1.shape[1], w_a1.shape[1], w_g1.shape[1]
    return pl.pallas_call(
        _prenorm_kernel,
        grid=(s // tm,),
        in_specs=[pl.BlockSpec((tm, d), lambda i: (i, 0)),
                  pl.BlockSpec((SUBLANES, d), lambda i: (jnp.maximum(i * rpb - 1, 0), 0)),
                  vec, vec, vec, vec, vec, vec, full(w_w1), full(w_a1), full(w_g1)],
        out_specs=[pl.BlockSpec((tm, d), lambda i: (i, 0)),
                   pl.BlockSpec((tm, lw), lambda i: (i, 0)),
                   pl.BlockSpec((tm, la), lambda i: (i, 0)),
                   pl.BlockSpec((tm, lg), lambda i: (i, 0))],
        out_shape=[jax.ShapeDtypeStruct((s, d), BF16),
                   jax.ShapeDtypeStruct((s, lw), F32),
                   jax.ShapeDtypeStruct((s, la), F32),
                   jax.ShapeDtypeStruct((s, lg), F32)],
        name="prenorm_lora",
        compiler_params=_params("parallel"),
    )(x, x, g, sc, sh, mu_w, mu_a, mu_g, w_w1, w_a1, w_g1)


def _mm_kernel(a_ref, w_ref, o_ref, wb_ref):
    @pl.when(pl.program_id(1) == 0)
    def _():
        wb_ref[...] = w_ref[...].astype(BF16)

    o_ref[...] = jnp.dot(a_ref[...], wb_ref[...], preferred_element_type=F32).astype(o_ref.dtype)


def _matmul(a, w, out_dtype, tm=512, tn=1024):
    m, k = a.shape
    n = w.shape[1]
    tn = min(tn, n)
    return pl.pallas_call(
        _mm_kernel,
        grid=(n // tn, m // tm),
        in_specs=[pl.BlockSpec((tm, k), lambda j, i: (i, 0)),
                  pl.BlockSpec((k, tn), lambda j, i: (0, j))],
        out_specs=pl.BlockSpec((tm, tn), lambda j, i: (i, j)),
        out_shape=jax.ShapeDtypeStruct((m, n), out_dtype),
        scratch_shapes=[pltpu.VMEM((k, tn), BF16)],
        name="matmul",
        compiler_params=_params("arbitrary", "arbitrary"),
    )(a, w)


def _rope_gate_kernel(p_ref, c_ref, s1_ref, s2_ref, qa_ref, ka_ref, vt_ref, km_ref):
    i = pl.program_id(0)
    bs = MOBA_BLOCK
    nbp = km_ref.shape[1]

    @pl.when(i == 0)
    def _():
        km_ref[...] = jnp.zeros_like(km_ref)

    c, s1, s2 = c_ref[...], s1_ref[...], s2_ref[...]

    def rope(z):
        return z * c + pltpu.roll(z, LANES - ROPE_DIM // 2, 1) * s1 + pltpu.roll(z, ROPE_DIM // 2, 1) * s2

    row = lax.broadcasted_iota(I32, (nbp, bs), 0)
    lane = lax.broadcasted_iota(I32, (bs, LANES), 1)
    onehot = jnp.where(lane == i, 1.0, 0.0).astype(BF16)
    for h in range(ATT_HEADS):
        q = rope(p_ref[:, h * LANES:(h + 1) * LANES].astype(F32))
        k = rope(p_ref[:, (ATT_HEADS + h) * LANES:(ATT_HEADS + h + 1) * LANES].astype(F32))
        g = lax.dot_general(km_ref[h], q, (((1,), (1,)), ((), ())), precision=HI, preferred_element_type=F32)
        g = jnp.where(row < i, g, NEG)
        sel_t = jnp.zeros(g.shape, F32)
        for _ in range(MOBA_TOPK):
            mx = jnp.max(g, axis=0, keepdims=True)
            idx = jnp.min(jnp.where(g == mx, row, nbp), axis=0, keepdims=True)
            hit = row == idx
            sel_t = jnp.where(hit & (row < i), 1.0, sel_t)
            g = jnp.where(hit, -jnp.inf, g)
        if nbp < LANES:
            sel_t = jnp.concatenate([sel_t, jnp.zeros((LANES - nbp, bs), F32)], axis=0)
        w = 2 * LANES
        qa_ref[h, :LANES, :] = (q * Q_SCALE).T.astype(BF16)
        qa_ref[h, LANES:, :] = jnp.where(sel_t > 0.5, 0.0, NEG).astype(BF16)
        ka_ref[:, h * w:h * w + LANES] = k.astype(BF16)
        ka_ref[:, h * w + LANES:(h + 1) * w] = onehot
        v = p_ref[:, (2 * ATT_HEADS + h) * LANES:(2 * ATT_HEADS + h + 1) * LANES].astype(F32)
        vt_ref[h, 0] = v.T.astype(BF16)
        km_ref[h, pl.ds(i, 1), :] = jnp.mean(k, axis=0, keepdims=True)


def _rope_gate(proj, cos_t, sin1_t, sin2_t):
    s = proj.shape[0]
    nb = s // MOBA_BLOCK
    assert nb <= LANES
    nbp = -(-nb // SUBLANES) * SUBLANES
    w_in = 3 * ATT_HEADS * ATT_HEAD_DIM
    w_out = 2 * ATT_HEADS * LANES
    tab = pl.BlockSpec((MOBA_BLOCK, LANES), lambda i: (i, 0))
    return pl.pallas_call(
        _rope_gate_kernel,
        grid=(nb,),
        in_specs=[pl.BlockSpec((MOBA_BLOCK, w_in), lambda i: (i, 0)), tab, tab, tab],
        out_specs=[pl.BlockSpec((ATT_HEADS, 2 * LANES, MOBA_BLOCK), lambda i: (0, 0, i)),
                   pl.BlockSpec((MOBA_BLOCK, w_out), lambda i: (i, 0)),
                   pl.BlockSpec((ATT_HEADS, 1, LANES, MOBA_BLOCK), lambda i: (0, i, 0, 0))],
        out_shape=[jax.ShapeDtypeStruct((ATT_HEADS, 2 * LANES, s), BF16),
                   jax.ShapeDtypeStruct((s, w_out), BF16),
                   jax.ShapeDtypeStruct((ATT_HEADS, nb, LANES, MOBA_BLOCK), BF16)],
        scratch_shapes=[pltpu.VMEM((ATT_HEADS, nbp, LANES), F32)],
        name="rope_gate",
        compiler_params=_params("arbitrary"),
    )(proj, cos_t, sin1_t, sin2_t)


def _attn_kernel(qa_ref, ka_ref, vt_ref, o_ref, s_ref):
    qi = pl.program_id(1)
    bs = MOBA_BLOCK
    grp = ATT_GROUP * bs
    w = 2 * LANES
    heads = range(ATT_HEADS_PER_STEP)

    n_groups = ka_ref.shape[0] // grp
    assert n_groups % 2 == 0

    def issue_scores(g, buf):
        base = pl.multiple_of(jnp.minimum(g, n_groups - 1) * grp, grp)
        for h in heads:
            s_ref[buf, h] = jnp.dot(ka_ref[pl.ds(base, grp), h * w:(h + 1) * w], qa_ref[h],
                                    preferred_element_type=F32)

    issue_scores(0, 0)

    own = pl.multiple_of(qi * bs, bs)
    k_i = lax.broadcasted_iota(I32, (bs, bs), 0)
    q_i = lax.broadcasted_iota(I32, (bs, bs), 1)
    carry = []
    for h in heads:
        s = jnp.dot(ka_ref[pl.ds(own, bs), h * w:h * w + LANES], qa_ref[h, :LANES, :],
                    preferred_element_type=F32)
        s = jnp.where(k_i <= q_i, s, NEG)
        m = jnp.max(s, axis=0, keepdims=True)
        p = jnp.exp2(s - m)
        l = jnp.sum(p, axis=0, keepdims=True)
        acc = jnp.dot(vt_ref[h, qi], p.astype(BF16), preferred_element_type=F32)
        carry += [m, l, acc]

    def absorb(g, buf, carry):
        out = []
        for h in heads:
            m, l, acc = carry[3 * h:3 * h + 3]
            s = s_ref[buf, h]
            mn = jnp.maximum(m, jnp.max(s, axis=0, keepdims=True))
            alpha = jnp.exp2(m - mn)
            p = jnp.exp2(s - mn)
            l = alpha * l + jnp.sum(p, axis=0, keepdims=True)
            p = p.astype(BF16)
            acc = alpha * acc
            for jb in range(ATT_GROUP):
                acc = acc + jnp.dot(vt_ref[h, g * ATT_GROUP + jb], p[jb * bs:(jb + 1) * bs],
                                    preferred_element_type=F32)
            out += [mn, l, acc]
        return out

    def body(t, carry):
        issue_scores(2 * t + 1, 1)
        carry = absorb(2 * t, 0, carry)
        issue_scores(2 * t + 2, 0)
        return tuple(absorb(2 * t + 1, 1, carry))

    n_used = (qi + ATT_GROUP - 1) // ATT_GROUP
    carry = lax.fori_loop(0, (n_used + 1) // 2, body, tuple(carry))
    for h in heads:
        m, l, acc = carry[3 * h:3 * h + 3]
        o_ref[:, h * LANES:(h + 1) * LANES] = (acc / l).T.astype(BF16)


def _attention(q_aug_t, k_aug, v_t):
    s = k_aug.shape[0]
    assert s % (ATT_GROUP * MOBA_BLOCK) == 0
    nb = s // MOBA_BLOCK
    hps = ATT_HEADS_PER_STEP
    once = pl.Buffered(1)
    return pl.pallas_call(
        _attn_kernel,
        grid=(ATT_HEADS // hps, nb),
        in_specs=[pl.BlockSpec((hps, 2 * LANES, MOBA_BLOCK), lambda h, i: (h, 0, i)),
                  pl.BlockSpec((s, hps * 2 * LANES), lambda h, i: (0, h), pipeline_mode=once),
                  pl.BlockSpec((hps, nb, LANES, MOBA_BLOCK), lambda h, i: (h, 0, 0, 0), pipeline_mode=once)],
        out_specs=pl.BlockSpec((MOBA_BLOCK, hps * LANES), lambda h, i: (i, h)),
        out_shape=jax.ShapeDtypeStruct((s, ATT_HEADS * ATT_HEAD_DIM), BF16),
        scratch_shapes=[pltpu.VMEM((2, hps, ATT_GROUP * MOBA_BLOCK, MOBA_BLOCK), F32)],
        name="moba_attention",
        compiler_params=_params("parallel", "arbitrary"),
    )(q_aug_t, k_aug, v_t)


def _rwkv_kernel(r_ref, k_ref, v_ref, lw_ref, la_ref, lg_ref, ww2_ref, wa2_ref, wg2_ref,
                 mur_ref, muk_ref, muv_ref, w0_ref, a0_ref, kk_ref, ka_ref, rk_ref, gnw_ref, gnb_ref,
                 o_ref, st_ref, prev_ref):
    j = pl.program_id(1)
    t = SCAN_CHUNK
    n = RWKV_HEAD_DIM
    ts = r_ref.shape[0]
    chunks = range(ts // t)

    @pl.when(j == 0)
    def _():
        st_ref[...] = jnp.zeros_like(st_ref)
        prev_ref[...] = jnp.zeros_like(prev_ref)

    rp, kp, vp = r_ref[...].astype(F32), k_ref[...].astype(F32), v_ref[...].astype(F32)
    r = rp + (_shift_rows(rp, prev_ref[0:1, :]) - rp) * mur_ref[...]
    k = kp + (_shift_rows(kp, prev_ref[1:2, :]) - kp) * muk_ref[...]
    v = vp + (_shift_rows(vp, prev_ref[2:3, :]) - vp) * muv_ref[...]
    prev_ref[0:1, :] = rp[ts - 1:ts, :]
    prev_ref[1:2, :] = kp[ts - 1:ts, :]
    prev_ref[2:3, :] = vp[ts - 1:ts, :]

    d = w0_ref[...] + _dot(lw_ref[...], ww2_ref[...])
    logw = -math.exp(-0.5) * _sigmoid(d)
    a = _sigmoid(a0_ref[...] + _dot(la_ref[...], wa2_ref[...]))
    g = _dot(lg_ref[...], wg2_ref[...])

    li = lax.broadcasted_iota(I32, (LANES, LANES), 0)
    lj = lax.broadcasted_iota(I32, (LANES, LANES), 1)
    same_head = (li // n) == (lj // n)
    head_sum = jnp.where(same_head, 1.0, 0.0)

    kk = k * kk_ref[...]
    kk = kk / jnp.maximum(jnp.sqrt(_dot_split(kk * kk, head_sum)), 1e-12)
    kt = k * (1.0 + (a - 1.0) * ka_ref[...])
    bonus = _dot_split(r * kt * rk_ref[...], head_sum) * v

    same_blk = (li // t) == (lj // t)
    m_strict = same_blk & (lj < li)
    m_incl = same_blk & (lj <= li)
    ti = lax.broadcasted_iota(I32, (t, t), 0)
    tj = lax.broadcasted_iota(I32, (t, t), 1)
    tril_incl = jnp.where(tj <= ti, 1.0, 0.0)
    lane_a = lax.broadcasted_iota(I32, (t, LANES), 1) < n
    eye = jnp.where(li == lj, 1.0, 0.0)

    def stack_masked(z):
        return jnp.concatenate([jnp.where(lane_a, z, 0.0), jnp.where(lane_a, 0.0, z)], axis=0)

    def stack_plain(z):
        return jnp.concatenate([z, z], axis=0)

    def rows(z, c):
        return z[c * t:(c + 1) * t]

    cum = jnp.concatenate([_dot_split_t(tril_incl, rows(logw, c)) for c in chunks], axis=0)
    g_t = jnp.exp(cum)
    g_inv = jnp.exp(-cum)
    xa_f = -kk * jnp.exp(cum - logw)
    xr_f = r * g_t
    yb_f = kk * a * g_inv
    yk_f = kt * g_inv

    xa = [stack_masked(rows(xa_f, c)) for c in chunks]
    xr = [stack_masked(rows(xr_f, c)) for c in chunks]
    yb = [stack_plain(rows(yb_f, c)) for c in chunks]
    vs = [stack_masked(rows(v, c)) for c in chunks]
    ybk = [jnp.concatenate([yb[c], stack_plain(rows(yk_f, c))], axis=0) for c in chunks]
    sc = [_dot_nt(jnp.concatenate([xa[c], xr[c]], axis=0), ybk[c]) for c in chunks]
    a_ab = [jnp.where(m_strict, sc[c][:2 * t, :2 * t], 0.0) for c in chunks]
    a_ak = [jnp.where(m_strict, sc[c][:2 * t, 2 * t:], 0.0) for c in chunks]
    a_rb = [jnp.where(m_incl, sc[c][2 * t:, :2 * t], 0.0) for c in chunks]
    a_rk = [jnp.where(m_incl, sc[c][2 * t:, 2 * t:], 0.0) for c in chunks]
    inv = [eye + a_ab[c] for c in chunks]
    pw = [_dot(a_ab[c], a_ab[c]) for c in chunks]
    for _ in range(int(math.log2(t)) - 2):
        both = [_dot(pw[c], jnp.concatenate([pw[c], inv[c]], axis=1)) for c in chunks]
        pw = [both[c][:, :LANES] for c in chunks]
        inv = [inv[c] + both[c][:, LANES:] for c in chunks]
    inv = [inv[c] + _dot(pw[c], inv[c]) for c in chunks]
    av = [_dot(a_ak[c], vs[c]) for c in chunks]
    mw = [_dot(inv[c], jnp.concatenate([xa[c], av[c]], axis=1)) for c in chunks]
    zeros = jnp.zeros((2 * t, LANES), F32)
    rw = [_dot(jnp.concatenate([a_rb[c], a_rk[c]], axis=1),
               jnp.concatenate([mw[c], jnp.concatenate([zeros, vs[c]], axis=1)], axis=0)) for c in chunks]
    r_hat = [xr[c] + rw[c][:, :LANES] for c in chunks]
    y_hat = [rw[c][:, LANES:] for c in chunks]
    pm = [jnp.where(same_head, _dot_tn(mw[c][:, :LANES], yb[c]), 0.0) for c in chunks]
    qm = [jnp.where(same_head, _dot_tn(jnp.concatenate([mw[c][:, LANES:], vs[c]], axis=0), ybk[c]), 0.0)
          for c in chunks]

    st = st_ref[...]
    ys = []
    for c in chunks:
        y2 = _dot_nt(r_hat[c], st) + y_hat[c]
        ys.append(y2[:t] + y2[t:])
        st = (st + _dot(st, pm[c]) + qm[c]) * g_t[(c + 1) * t - 1:(c + 1) * t, :]
    st_ref[...] = st
    y = jnp.concatenate(ys, axis=0)

    mean = _dot_split(y, head_sum) * (1.0 / n)
    yc = y - mean
    var = _dot_split(yc * yc, head_sum) * (1.0 / n)
    yn = yc * lax.rsqrt(var + GN_EPS) * gnw_ref[...] + gnb_ref[...]
    o_ref[...] = ((yn + bonus) * g).astype(BF16)


def _rwkv(proj, lw, la, lg, w_w2, w_a2, w_g2, mu_r, mu_k, mu_v, w0, a0, k_k, k_a, r_k, gn_w, gn_b,
          col0, width):
    s = proj.shape[0]
    ts = 512
    npair = width // LANES
    cb = col0 // LANES
    row = lambda a: pl.BlockSpec((ts, a.shape[1]), lambda p, j: (j, 0))
    wcol = lambda a: pl.BlockSpec((a.shape[0], LANES), lambda p, j: (0, p))
    vec = pl.BlockSpec((1, LANES), lambda p, j: (0, p))
    return pl.pallas_call(
        _rwkv_kernel,
        grid=(npair, s // ts),
        in_specs=[pl.BlockSpec((ts, LANES), lambda p, j: (j, cb + p)),
                  pl.BlockSpec((ts, LANES), lambda p, j: (j, cb + npair + p)),
                  pl.BlockSpec((ts, LANES), lambda p, j: (j, cb + 2 * npair + p)),
                  row(lw), row(la), row(lg), wcol(w_w2), wcol(w_a2), wcol(w_g2)] + [vec] * 10,
        out_specs=pl.BlockSpec((ts, LANES), lambda p, j: (j, p)),
        out_shape=jax.ShapeDtypeStruct((s, width), BF16),
        scratch_shapes=[pltpu.VMEM((LANES, LANES), F32), pltpu.VMEM((SUBLANES, LANES), F32)],
        name="rwkv7_scan",
        compiler_params=_params("parallel", "arbitrary"),
    )(proj, proj, proj, lw, la, lg, w_w2, w_a2, w_g2, mu_r, mu_k, mu_v, w0, a0, k_k, k_a, r_k, gn_w, gn_b)


def _merge_kernel(oa_ref, or_ref, ga_ref, gr_ref, wa_ref, wr_ref, o_ref, wab_ref, wrb_ref):
    @pl.when(pl.program_id(1) == 0)
    def _():
        wab_ref[...] = wa_ref[...].astype(BF16)
        wrb_ref[...] = wr_ref[...].astype(BF16)

    ua = jnp.dot(oa_ref[...], wab_ref[...], preferred_element_type=F32)
    ur = jnp.dot(or_ref[...], wrb_ref[...], preferred_element_type=F32)
    mix = _sigmoid(ga_ref[...].astype(F32)) * ua + _sigmoid(gr_ref[...].astype(F32)) * ur
    o_ref[...] = mix.astype(BF16)


def _merge(o_att, o_rwkv, proj, w_up_att, w_up_rwkv, gate_col0):
    s, ka = o_att.shape
    kr = o_rwkv.shape[1]
    d = w_up_att.shape[1]
    tm, tn = 512, 1024
    gb = gate_col0 // tn
    return pl.pallas_call(
        _merge_kernel,
        grid=(d // tn, s // tm),
        in_specs=[pl.BlockSpec((tm, ka), lambda j, i: (i, 0)),
                  pl.BlockSpec((tm, kr), lambda j, i: (i, 0)),
                  pl.BlockSpec((tm, tn), lambda j, i: (i, gb + j)),
                  pl.BlockSpec((tm, tn), lambda j, i: (i, gb + d // tn + j)),
                  pl.BlockSpec((ka, tn), lambda j, i: (0, j)),
                  pl.BlockSpec((kr, tn), lambda j, i: (0, j))],
        out_specs=pl.BlockSpec((tm, tn), lambda j, i: (i, j)),
        out_shape=jax.ShapeDtypeStruct((s, d), BF16),
        scratch_shapes=[pltpu.VMEM((ka, tn), BF16), pltpu.VMEM((kr, tn), BF16)],
        name="gated_merge",
        compiler_params=_params("arbitrary", "arbitrary"),
    )(o_att, o_rwkv, proj, proj, w_up_att, w_up_rwkv)


def _route_kernel(x_ref, y_ref, gt_ref, gpost_ref, gpre_ref, sc_ref, sh_ref, wr_ref, br_ref,
                  x1_ref, h2_ref, ei_ref, wt_ref, rk_ref, cnt_ref, run_ref):
    @pl.when(pl.program_id(0) == 0)
    def _():
        run_ref[...] = jnp.zeros_like(run_ref)

    x1 = x_ref[...] + gt_ref[...] * _rms(y_ref[...], gpost_ref[...])
    x1_ref[...] = x1
    h2 = _rms(x1, gpre_ref[...]) * (1.0 + sc_ref[...]) + sh_ref[...]
    h2_ref[...] = h2
    logits = _dot_hi(h2, wr_ref[...]) + br_ref[...]
    lane = lax.broadcasted_iota(I32, logits.shape, 1)
    big = jnp.int32(4 * LANES)
    gmask = (lane >= N_EXPERTS) & (lane < N_EXPERTS + N_GROUPS)
    mg = jnp.max(jnp.where(gmask, logits, -jnp.inf), axis=1, keepdims=True)
    eg = jnp.where(gmask, jnp.exp(logits - mg), 0.0)
    pg = eg / jnp.sum(eg, axis=1, keepdims=True)
    pg_top = jnp.max(pg, axis=1, keepdims=True)
    g_idx = jnp.min(jnp.where(gmask & (pg == pg_top), lane, big), axis=1, keepdims=True) - N_EXPERTS
    emask = (lane >= g_idx * EXPERTS_PER_GROUP) & (lane < (g_idx + 1) * EXPERTS_PER_GROUP)
    me = jnp.max(jnp.where(emask, logits, -jnp.inf), axis=1, keepdims=True)
    ee = jnp.where(emask, jnp.exp(logits - me), 0.0)
    pe = ee / jnp.sum(ee, axis=1, keepdims=True)
    p1 = jnp.max(pe, axis=1, keepdims=True)
    i1 = jnp.min(jnp.where(emask & (pe == p1), lane, big), axis=1, keepdims=True)
    rest = emask & (lane != i1)
    p2 = jnp.max(jnp.where(rest, pe, -jnp.inf), axis=1, keepdims=True)
    i2 = jnp.min(jnp.where(rest & (pe == p2), lane, big), axis=1, keepdims=True)
    den = p1 + p2
    ei_ref[...] = jnp.where(lane == 0, i1, jnp.where(lane == 1, i2, 0))
    wt_ref[...] = jnp.where(lane == 0, pg_top * p1 / den, jnp.where(lane == 1, pg_top * p2 / den, 0.0))
    tm = logits.shape[0]
    chosen = jnp.where((lane == i1) | (lane == i2), 1.0, 0.0)
    t_i = lax.broadcasted_iota(I32, (tm, tm), 0)
    t_j = lax.broadcasted_iota(I32, (tm, tm), 1)
    before = _dot(jnp.where(t_j < t_i, 1.0, 0.0), chosen) + run_ref[...]
    r1 = jnp.sum(jnp.where(lane == i1, before, 0.0), axis=1, keepdims=True)
    r2 = jnp.sum(jnp.where(lane == i2, before, 0.0), axis=1, keepdims=True)
    rk_ref[...] = jnp.where(lane == 0, r1, jnp.where(lane == 1, r2, 0.0)).astype(I32)
    run_ref[...] += jnp.sum(chosen, axis=0, keepdims=True)
    cnt_ref[...] = run_ref[...].astype(I32)


def _route(x, y, gt1, g_post, g_pre, sc2, sh2, w_router, b_router):
    s, d = x.shape
    tm = 256
    vec = pl.BlockSpec((1, d), lambda i: (0, 0))
    rowblk = pl.BlockSpec((tm, d), lambda i: (i, 0))
    small = pl.BlockSpec((tm, LANES), lambda i: (i, 0))
    lane_row = pl.BlockSpec((1, LANES), lambda i: (0, 0))
    return pl.pallas_call(
        _route_kernel,
        grid=(s // tm,),
        in_specs=[rowblk, rowblk, vec, vec, vec, vec, vec, pl.BlockSpec((d, LANES), lambda i: (0, 0)), lane_row],
        out_specs=[rowblk, rowblk, small, small, small, lane_row],
        out_shape=[jax.ShapeDtypeStruct((s, d), F32), jax.ShapeDtypeStruct((s, d), F32),
                   jax.ShapeDtypeStruct((s, LANES), I32), jax.ShapeDtypeStruct((s, LANES), F32),
                   jax.ShapeDtypeStruct((s, LANES), I32), jax.ShapeDtypeStruct((1, LANES), I32)],
        scratch_shapes=[pltpu.VMEM((1, LANES), F32)],
        name="norm_route",
        compiler_params=_params("arbitrary"),
    )(x, y, gt1, g_post, g_pre, sc2, sh2, w_router, b_router)


def _slots_kernel(ei_ref, rk_ref, ps_ref, pos_ref):
    lane = lax.broadcasted_iota(I32, ei_ref.shape, 1)
    ei, rk = ei_ref[...], rk_ref[...]
    ps = ps_ref[...]
    cols = []
    for kk in range(TOP_K):
        e = ei[:, kk:kk + 1]
        cols.append(jnp.sum(jnp.where(lane == e, ps, 0), axis=1, keepdims=True) + rk[:, kk:kk + 1])
    pos_ref[...] = jnp.where(lane == 0, cols[0], jnp.where(lane == 1, cols[1], 0))


def _slots(e_idx, rank, pstart_row):
    n = e_idx.shape[0]
    tm = 1024
    blk = pl.BlockSpec((tm, LANES), lambda i: (i, 0))
    return pl.pallas_call(
        _slots_kernel,
        grid=(n // tm,),
        in_specs=[blk, blk, pl.BlockSpec((1, LANES), lambda i: (0, 0))],
        out_specs=blk,
        out_shape=jax.ShapeDtypeStruct((n, LANES), I32),
        name="dispatch_slots",
        compiler_params=_params("parallel"),
    )(e_idx, rank, pstart_row)


def _slot_tokens_kernel(pos_ref, tok_ref):
    n_pairs = pos_ref.shape[0]

    def clear(s, _):
        tok_ref[s] = 0
        return 0

    def place(p, _):
        tok_ref[pos_ref[p]] = p // TOP_K
        return 0

    lax.fori_loop(0, tok_ref.shape[0], clear, 0)
    lax.fori_loop(0, n_pairs, place, 0)


def _slot_tokens(pos_flat, n_slots):
    return pl.pallas_call(
        _slot_tokens_kernel,
        in_specs=[pl.BlockSpec(memory_space=pltpu.SMEM)],
        out_specs=pl.BlockSpec(memory_space=pltpu.SMEM),
        out_shape=jax.ShapeDtypeStruct((n_slots,), I32),
        name="dispatch_slot_tokens",
    )(pos_flat)


def _row_copy(src_hbm, row, dst, dst_row, sem):
    return pltpu.make_async_copy(src_hbm.at[pl.ds(row, 1), :], dst.at[pl.ds(dst_row, 1), :], sem)


def _expert_kernel(bstart_ref, bcount_ref, tok_ref, nused_ref, h_hbm, wg_ref, wu_ref, wd_ref, y_hbm,
                   xg_ref, yo_ref, wgb_ref, wub_ref, wdb_ref, gsem, osem):
    e = pl.program_id(0)
    nused = nused_ref[0]
    rows = EXPERT_BLOCK
    n_blk = y_hbm.shape[0] // rows
    first = bstart_ref[e]
    count = bcount_ref[e]

    def start_rows(blk, slot):
        for i in range(rows):
            _row_copy(h_hbm, tok_ref[blk * rows + i], xg_ref.at[slot], i, gsem.at[slot]).start(priority=1)

    def wait_rows(slot):
        pltpu.make_async_copy(h_hbm.at[pl.ds(0, rows), :], xg_ref.at[slot], gsem.at[slot]).wait()

    def out_copy(blk, slot):
        return pltpu.make_async_copy(yo_ref.at[slot], y_hbm.at[pl.ds(blk * rows, rows), :], osem.at[slot])

    @pl.when(e == 0)
    def _():
        start_rows(0, 0)

    @pl.when(count > 0)
    def _():
        wgb_ref[...] = wg_ref[0].astype(BF16)
        wub_ref[...] = wu_ref[0].astype(BF16)
        wdb_ref[...] = wd_ref[0].astype(BF16)

    def block(j, _):
        blk = first + j
        slot = blk % 2

        @pl.when(blk >= 2)
        def _():
            out_copy(blk - 2, slot).wait()

        wait_rows(slot)
        start_rows(jnp.minimum(blk + 1, nused - 1), 1 - slot)
        xb = xg_ref[slot].astype(BF16)
        hg = jnp.dot(xb, wgb_ref[...], preferred_element_type=F32)
        hu = jnp.dot(xb, wub_ref[...], preferred_element_type=F32)
        hid = hg * _sigmoid(hg) * hu
        yo_ref[slot] = jnp.dot(hid.astype(BF16), wdb_ref[...], preferred_element_type=F32)
        out_copy(blk, slot).start()

        @pl.when(blk == nused - 1)
        def _():
            wait_rows(1 - slot)

        return 0

    lax.fori_loop(0, count, block, 0)

    @pl.when(e == pl.num_programs(0) - 1)
    def _():
        @pl.when(nused >= 2)
        def _():
            out_copy(nused - 2, nused % 2).wait()

        out_copy(nused - 1, (nused - 1) % 2).wait()
        yo_ref[0] = jnp.zeros(yo_ref.shape[1:], F32)

        def fill(blk, _):
            out_copy(blk, 0).start()
            return 0

        def drain(blk, _):
            out_copy(blk, 0).wait()
            return 0

        lax.fori_loop(nused, n_blk, fill, 0)
        lax.fori_loop(nused, n_blk, drain, 0)


def _experts(h2, bstart, bcount, buf_tok, nused, w_gate_e, w_up_e, w_down_e):
    n, d = h2.shape
    n_exp, _, f = w_gate_e.shape
    n_slots = buf_tok.shape[0]
    grid_spec = pltpu.PrefetchScalarGridSpec(
        num_scalar_prefetch=4,
        grid=(n_exp,),
        in_specs=[pl.BlockSpec(memory_space=pl.ANY),
                  pl.BlockSpec((1, d, f), lambda e, *_: (e, 0, 0)),
                  pl.BlockSpec((1, d, f), lambda e, *_: (e, 0, 0)),
                  pl.BlockSpec((1, f, d), lambda e, *_: (e, 0, 0))],
        out_specs=pl.BlockSpec(memory_space=pl.ANY),
        scratch_shapes=[pltpu.VMEM((2, EXPERT_BLOCK, d), F32), pltpu.VMEM((2, EXPERT_BLOCK, d), F32),
                        pltpu.VMEM((d, f), BF16), pltpu.VMEM((d, f), BF16), pltpu.VMEM((f, d), BF16),
                        pltpu.SemaphoreType.DMA((2,)), pltpu.SemaphoreType.DMA((2,))],
    )
    return pl.pallas_call(
        _expert_kernel,
        grid_spec=grid_spec,
        out_shape=jax.ShapeDtypeStruct((n_slots, d), F32),
        name="experts",
        compiler_params=_params("arbitrary"),
    )(bstart, bcount, buf_tok, nused, h2, w_gate_e, w_up_e, w_down_e)


def _combine_kernel(pos_ref, y_hbm, wt_ref, x1_ref, gt_ref, gpost_ref, o_ref, rows_ref, sem):
    i = pl.program_id(0)
    nsteps = pl.num_programs(0)
    tm = x1_ref.shape[0]

    def start_rows(step, slot):
        for r in range(tm):
            for kk in range(TOP_K):
                _row_copy(y_hbm, pos_ref[(step * tm + r) * TOP_K + kk], rows_ref.at[slot, kk], r,
                          sem.at[slot]).start()

    def wait_rows(slot):
        for kk in range(TOP_K):
            pltpu.make_async_copy(y_hbm.at[pl.ds(0, tm), :], rows_ref.at[slot, kk], sem.at[slot]).wait()

    @pl.when(i == 0)
    def _():
        start_rows(0, 0)

    slot = i % 2
    wait_rows(slot)
    start_rows(jnp.minimum(i + 1, nsteps - 1), 1 - slot)
    wt = wt_ref[...]
    y = rows_ref[slot, 0] * wt[:, 0:1] + rows_ref[slot, 1] * wt[:, 1:2]
    o_ref[...] = x1_ref[...] + gt_ref[...] * _rms(y, gpost_ref[...])

    @pl.when(i == nsteps - 1)
    def _():
        wait_rows(1 - slot)


def _combine(pos, y_buf, wts, x1, gt2, g_post):
    n, d = x1.shape
    tm = 128
    vec = pl.BlockSpec((1, d), lambda i, p: (0, 0))
    grid_spec = pltpu.PrefetchScalarGridSpec(
        num_scalar_prefetch=1,
        grid=(n // tm,),
        in_specs=[pl.BlockSpec(memory_space=pl.ANY),
                  pl.BlockSpec((tm, LANES), lambda i, p: (i, 0)),
                  pl.BlockSpec((tm, d), lambda i, p: (i, 0)), vec, vec],
        out_specs=pl.BlockSpec((tm, d), lambda i, p: (i, 0)),
        scratch_shapes=[pltpu.VMEM((2, TOP_K, tm, d), F32), pltpu.SemaphoreType.DMA((2,))],
    )
    return pl.pallas_call(
        _combine_kernel,
        grid_spec=grid_spec,
        out_shape=jax.ShapeDtypeStruct((n, d), F32),
        name="combine",
        compiler_params=_params("arbitrary"),
    )(pos, y_buf, wts, x1, gt2, g_post)


def _segment_tables(counts_row):
    counts = counts_row[0, :N_EXPERTS]
    pcounts = (counts + EXPERT_BLOCK - 1) // EXPERT_BLOCK * EXPERT_BLOCK
    pend = jnp.cumsum(pcounts)
    pstart = pend - pcounts
    nused = (pend[-1] // EXPERT_BLOCK).astype(I32)
    pstart_row = jnp.pad(pstart, (0, LANES - N_EXPERTS)).reshape(1, LANES)
    return pstart_row, pstart // EXPERT_BLOCK, pcounts // EXPERT_BLOCK, nused.reshape(1)


def _rope_tables(s):
    half = ROPE_DIM // 2
    inv = ROPE_THETA ** (-jnp.arange(half, dtype=F32) / half)
    ang = jnp.arange(s, dtype=F32)[:, None] * inv[None, :]
    cos, sin = lax.optimization_barrier((jnp.cos(ang), jnp.sin(ang)))
    pad = jnp.zeros((s, LANES - ROPE_DIM), F32)
    zero = jnp.zeros((s, half), F32)
    cos_t = jnp.concatenate([cos, cos, pad + 1.0], axis=1)
    sin1_t = jnp.concatenate([-sin, zero, pad], axis=1)
    sin2_t = jnp.concatenate([zero, sin, pad], axis=1)
    return cos_t, sin1_t, sin2_t


def _layer(x, c_col, w_ada, b_ada, g_pre_mix, g_post_mix, g_pre_ffn, g_post_ffn, w_in, mu_r, mu_k, mu_v,
           mu_w, mu_a, mu_g, w0, w_w1, w_w2, a0, w_a1, w_a2, w_g1, w_g2, k_k, k_a, r_k, gn_w, gn_b,
           w_up_att, w_up_rwkv, w_o, w_rg, b_rg, w_re, b_re, w_gate_e, w_up_e, w_down_e):
    s, d = x.shape
    att_w = ATT_HEADS * ATT_HEAD_DIM
    rwkv_w = w_up_rwkv.shape[0]
    row = lambda a: a.reshape(1, -1)

    ada = _ada(c_col, w_ada, row(b_ada))
    sh1, sc1, gt1, sh2, sc2, gt2 = (ada[:, i * d:(i + 1) * d] for i in range(6))

    h, lw, la, lg = _prenorm(x, row(g_pre_mix), sc1, sh1, row(mu_w), row(mu_a), row(mu_g), w_w1, w_a1, w_g1)
    proj = _matmul(h, w_in, BF16)

    q_aug_t, k_aug, v_t = _rope_gate(proj, *_rope_tables(s))
    o_att = _attention(q_aug_t, k_aug, v_t)

    o_rwkv = _rwkv(proj, lw, la, lg, w_w2, w_a2, w_g2, row(mu_r), row(mu_k), row(mu_v), row(w0), row(a0),
                   row(k_k), row(k_a), row(r_k), row(gn_w), row(gn_b), col0=3 * att_w, width=rwkv_w)

    mix = _merge(o_att, o_rwkv, proj, w_up_att, w_up_rwkv, gate_col0=3 * att_w + 3 * rwkv_w)
    y = _matmul(mix, w_o, F32)

    w_router = jnp.pad(jnp.concatenate([w_re, w_rg], axis=1), ((0, 0), (0, LANES - N_EXPERTS - N_GROUPS)))
    b_router = jnp.pad(jnp.concatenate([b_re, b_rg]), (0, LANES - N_EXPERTS - N_GROUPS)).reshape(1, LANES)
    x1, h2, e_idx, wts, rank, counts = _route(x, y, gt1, row(g_post_mix), row(g_pre_ffn), sc2, sh2,
                                              w_router, b_router)

    n_pairs = s * TOP_K
    n_blk = (n_pairs + N_EXPERTS * (EXPERT_BLOCK - 1) + EXPERT_BLOCK - 1) // EXPERT_BLOCK
    pstart_row, bstart, bcount, nused = _segment_tables(counts)
    pos = _slots(e_idx, rank, pstart_row)[:, :TOP_K].reshape(n_pairs)
    buf_tok = _slot_tokens(pos, n_blk * EXPERT_BLOCK)
    y_buf = _experts(h2, bstart, bcount, buf_tok, nused, w_gate_e, w_up_e, w_down_e)
    return _combine(pos, y_buf, wts, x1, gt2, row(g_post_ffn))


def kernel(x, c, w_ada, b_ada, g_pre_mix, g_post_mix, g_pre_ffn, g_post_ffn, w_in, mu_r, mu_k, mu_v, mu_w, mu_a, mu_g, w0, w_w1, w_w2, a0, w_a1, w_a2, w_g1, w_g2, k_k, k_a, r_k, gn_w, gn_b, w_up_att, w_up_rwkv, w_o, w_rg, b_rg, w_re, b_re, w_gate_e, w_up_e, w_down_e):
    b, s, d = x.shape
    assert b == 1, "one sequence per call"
    params = (w_ada, b_ada, g_pre_mix, g_post_mix, g_pre_ffn, g_post_ffn, w_in, mu_r, mu_k, mu_v, mu_w, mu_a,
              mu_g, w0, w_w1, w_w2, a0, w_a1, w_a2, w_g1, w_g2, k_k, k_a, r_k, gn_w, gn_b, w_up_att,
              w_up_rwkv, w_o, w_rg, b_rg, w_re, b_re, w_gate_e, w_up_e, w_down_e)
    xs = x.reshape(s, d)
    c_col = c.reshape(d, 1)
    for l in range(w_ada.shape[0]):
        xs = _layer(xs, c_col, *(p[l] for p in params))
    return xs.reshape(b, s, d)
```

```python
import math

import jax
import jax.numpy as jnp
from jax import lax
from jax.experimental import pallas as pl
from jax.experimental.pallas import tpu as pltpu

F32 = jnp.float32
BF16 = jnp.bfloat16
I32 = jnp.int32
HI = lax.Precision.HIGHEST

LANES = 128
SUBLANES = 8
VMEM_LIMIT = 56 * 1024 * 1024

ATT_HEADS = 8
ATT_HEAD_DIM = 128
MOBA_BLOCK = 256
MOBA_TOPK = 3
ATT_GROUP = 4
ATT_HEADS_PER_STEP = 2
ROPE_THETA = 500000.0
ROPE_DIM = ATT_HEAD_DIM // 4
RWKV_HEAD_DIM = 64
GN_EPS = 64e-5
N_GROUPS = 8
EXPERTS_PER_GROUP = 8
N_EXPERTS = N_GROUPS * EXPERTS_PER_GROUP
TOP_K = 2
EXPERT_BLOCK = 128
RMS_EPS = 1e-6
NEG = -1e30
SCAN_CHUNK = 64
Q_SCALE = ATT_HEAD_DIM ** -0.5 * math.log2(math.e)


def _params(*sem):
    return pltpu.CompilerParams(dimension_semantics=sem, vmem_limit_bytes=VMEM_LIMIT)


def _rms(z, g):
    return z * lax.rsqrt(jnp.mean(z * z, axis=-1, keepdims=True) + RMS_EPS) * g


def _sigmoid(z):
    return 1.0 / (1.0 + jnp.exp(-z))


def _dot(a, b):
    return jnp.dot(a.astype(BF16), b.astype(BF16), preferred_element_type=F32)


def _dot_nt(a, b):
    return lax.dot_general(a.astype(BF16), b.astype(BF16), (((1,), (1,)), ((), ())),
                           preferred_element_type=F32)


def _dot_tn(a, b):
    return lax.dot_general(a.astype(BF16), b.astype(BF16), (((0,), (0,)), ((), ())),
                           preferred_element_type=F32)


def _dot_hi(a, b):
    return jnp.dot(a, b, precision=HI, preferred_element_type=F32)


def _split3(a):
    hi = a.astype(BF16)
    r1 = a - hi.astype(F32)
    mid = r1.astype(BF16)
    lo = (r1 - mid.astype(F32)).astype(BF16)
    return hi, mid, lo


def _dot_split(a, b01):
    b = b01.astype(BF16)
    hi, mid, _ = _split3(a)
    return jnp.dot(jnp.concatenate([hi, mid], axis=1), jnp.concatenate([b, b], axis=0),
                   preferred_element_type=F32)


def _dot_split_t(b01, a):
    b = b01.astype(BF16)
    return jnp.dot(jnp.concatenate([b, b, b], axis=1), jnp.concatenate(_split3(a), axis=0),
                   preferred_element_type=F32)


def _shift_rows(z, prev_row):
    rolled = pltpu.roll(z, 1, 0)
    row = lax.broadcasted_iota(I32, z.shape, 0)
    return jnp.where(row == 0, prev_row, rolled)


def _ada_kernel(c_ref, w_ref, b_ref, o_ref):
    o_ref[...] = jnp.sum(c_ref[...] * w_ref[...], axis=0, keepdims=True) + b_ref[...]


def _ada(c_col, w_ada, b_ada):
    d, n = w_ada.shape
    tn = 1024
    return pl.pallas_call(
        _ada_kernel,
        grid=(n // tn,),
        in_specs=[pl.BlockSpec((d, 1), lambda j: (0, 0)),
                  pl.BlockSpec((d, tn), lambda j: (0, j)),
                  pl.BlockSpec((1, tn), lambda j: (0, j))],
        out_specs=pl.BlockSpec((1, tn), lambda j: (0, j)),
        out_shape=jax.ShapeDtypeStruct((1, n), F32),
        name="ada",
        compiler_params=_params("parallel"),
    )(c_col, w_ada, b_ada)


def _prenorm_kernel(x_ref, xp_ref, g_ref, sc_ref, sh_ref, muw_ref, mua_ref, mug_ref,
                    ww1_ref, wa1_ref, wg1_ref, h_ref, lw_ref, la_ref, lg_ref):
    i = pl.program_id(0)
    g, sc, sh = g_ref[...], sc_ref[...], sh_ref[...]
    h = _rms(x_ref[...], g) * (1.0 + sc) + sh
    hp = _rms(xp_ref[SUBLANES - 1:SUBLANES, :], g) * (1.0 + sc) + sh
    hp = jnp.where(i == 0, 0.0, hp)
    dh = _shift_rows(h, hp) - h
    h_ref[...] = h.astype(BF16)
    lw_ref[...] = jnp.tanh(_dot(h + dh * muw_ref[...], ww1_ref[...]))
    la_ref[...] = _dot(h + dh * mua_ref[...], wa1_ref[...])
    lg_ref[...] = _sigmoid(_dot(h + dh * mug_ref[...], wg1_ref[...]))


def _prenorm(x, g, sc, sh, mu_w, mu_a, mu_g, w_w1, w_a1, w_g1):
    s, d = x.shape
    tm = 256
    rpb = tm // SUBLANES
    vec = pl.BlockSpec((1, d), lambda i: (0, 0))
    full = lambda a: pl.BlockSpec(a.shape, lambda i: (0, 0))
    lw, la, lg = w_w1.shape[1], w_a1.shape[1], w_g1.shape[1]
    return pl.pallas_call(
        _prenorm_kernel,
        grid=(s // tm,),
        in_specs=[pl.BlockSpec((tm, d), lambda i: (i, 0)),
                  pl.BlockSpec((SUBLANES, d), lambda i: (jnp.maximum(i * rpb - 1, 0), 0)),
                  vec, vec, vec, vec, vec, vec, full(w_w1), full(w_a1), full(w_g1)],
        out_specs=[pl.BlockSpec((tm, d), lambda i: (i, 0)),
                   pl.BlockSpec((tm, lw), lambda i: (i, 0)),
                   pl.BlockSpec((tm, la), lambda i: (i, 0)),
                   pl.BlockSpec((tm, lg), lambda i: (i, 0))],
        out_shape=[jax.ShapeDtypeStruct((s, d), BF16),
                   jax.ShapeDtypeStruct((s, lw), F32),
                   jax.ShapeDtypeStruct((s, la), F32),
                   jax.ShapeDtypeStruct((s, lg), F32)],
        name="prenorm_lora",
        compiler_params=_params("parallel"),
    )(x, x, g, sc, sh, mu_w, mu_a, mu_g, w_w1, w_a1, w_g1)


def _mm_kernel(a_ref, w_ref, o_ref, wb_ref):
    @pl.when(pl.program_id(1) == 0)
    def _():
        wb_ref[...] = w_ref[...].astype(BF16)

    o_ref[...] = jnp.dot(a_ref[...], wb_ref[...], preferred_element_type=F32).astype(o_ref.dtype)


def _matmul(a, w, out_dtype, tm=512, tn=1024):
    m, k = a.shape
    n = w.shape[1]
    tn = min(tn, n)
    return pl.pallas_call(
        _mm_kernel,
        grid=(n // tn, m // tm),
        in_specs=[pl.BlockSpec((tm, k), lambda j, i: (i, 0)),
                  pl.BlockSpec((k, tn), lambda j, i: (0, j))],
        out_specs=pl.BlockSpec((tm, tn), lambda j, i: (i, j)),
        out_shape=jax.ShapeDtypeStruct((m, n), out_dtype),
        scratch_shapes=[pltpu.VMEM((k, tn), BF16)],
        name="matmul",
        compiler_params=_params("arbitrary", "arbitrary"),
    )(a, w)


def _rope_gate_kernel(p_ref, c_ref, s1_ref, s2_ref, qa_ref, ka_ref, vt_ref, km_ref):
    i = pl.program_id(0)
    bs = MOBA_BLOCK
    nbp = km_ref.shape[1]

    @pl.when(i == 0)
    def _():
        km_ref[...] = jnp.zeros_like(km_ref)

    c, s1, s2 = c_ref[...], s1_ref[...], s2_ref[...]

    def rope(z):
        return z * c + pltpu.roll(z, LANES - ROPE_DIM // 2, 1) * s1 + pltpu.roll(z, ROPE_DIM // 2, 1) * s2

    row = lax.broadcasted_iota(I32, (nbp, bs), 0)
    lane = lax.broadcasted_iota(I32, (bs, LANES), 1)
    onehot = jnp.where(lane == i, 1.0, 0.0).astype(BF16)
    for h in range(ATT_HEADS):
        q = rope(p_ref[:, h * LANES:(h + 1) * LANES].astype(F32))
        k = rope(p_ref[:, (ATT_HEADS + h) * LANES:(ATT_HEADS + h + 1) * LANES].astype(F32))
        g = lax.dot_general(km_ref[h], q, (((1,), (1,)), ((), ())), precision=HI, preferred_element_type=F32)
        g = jnp.where(row < i, g, NEG)
        sel_t = jnp.zeros(g.shape, F32)
        for _ in range(MOBA_TOPK):
            mx = jnp.max(g, axis=0, keepdims=True)
            idx = jnp.min(jnp.where(g == mx, row, nbp), axis=0, keepdims=True)
            hit = row == idx
            sel_t = jnp.where(hit & (row < i), 1.0, sel_t)
            g = jnp.where(hit, -jnp.inf, g)
        if nbp < LANES:
            sel_t = jnp.concatenate([sel_t, jnp.zeros((LANES - nbp, bs), F32)], axis=0)
        w = 2 * LANES
        qa_ref[h, :LANES, :] = (q * Q_SCALE).T.astype(BF16)
        qa_ref[h, LANES:, :] = jnp.where(sel_t > 0.5, 0.0, NEG).astype(BF16)
        ka_ref[:, h * w:h * w + LANES] = k.astype(BF16)
        ka_ref[:, h * w + LANES:(h + 1) * w] = onehot
        v = p_ref[:, (2 * ATT_HEADS + h) * LANES:(2 * ATT_HEADS + h + 1) * LANES].astype(F32)
        vt_ref[h, 0] = v.T.astype(BF16)
        km_ref[h, pl.ds(i, 1), :] = jnp.mean(k, axis=0, keepdims=True)


def _rope_gate(proj, cos_t, sin1_t, sin2_t):
    s = proj.shape[0]
    nb = s // MOBA_BLOCK
    assert nb <= LANES
    nbp = -(-nb // SUBLANES) * SUBLANES
    w_in = 3 * ATT_HEADS * ATT_HEAD_DIM
    w_out = 2 * ATT_HEADS * LANES
    tab = pl.BlockSpec((MOBA_BLOCK, LANES), lambda i: (i, 0))
    return pl.pallas_call(
        _rope_gate_kernel,
        grid=(nb,),
        in_specs=[pl.BlockSpec((MOBA_BLOCK, w_in), lambda i: (i, 0)), tab, tab, tab],
        out_specs=[pl.BlockSpec((ATT_HEADS, 2 * LANES, MOBA_BLOCK), lambda i: (0, 0, i)),
                   pl.BlockSpec((MOBA_BLOCK, w_out), lambda i: (i, 0)),
                   pl.BlockSpec((ATT_HEADS, 1, LANES, MOBA_BLOCK), lambda i: (0, i, 0, 0))],
        out_shape=[jax.ShapeDtypeStruct((ATT_HEADS, 2 * LANES, s), BF16),
                   jax.ShapeDtypeStruct((s, w_out), BF16),
                   jax.ShapeDtypeStruct((ATT_HEADS, nb, LANES, MOBA_BLOCK), BF16)],
        scratch_shapes=[pltpu.VMEM((ATT_HEADS, nbp, LANES), F32)],
        name="rope_gate",
        compiler_params=_params("arbitrary"),
    )(proj, cos_t, sin1_t, sin2_t)


def _attn_kernel(qa_ref, ka_ref, vt_ref, o_ref, s_ref):
    qi = pl.program_id(1)
    bs = MOBA_BLOCK
    grp = ATT_GROUP * bs
    w = 2 * LANES
    heads = range(ATT_HEADS_PER_STEP)

    n_groups = ka_ref.shape[0] // grp
    assert n_groups % 2 == 0

    def issue_scores(g, buf):
        base = pl.multiple_of(jnp.minimum(g, n_groups - 1) * grp, grp)
        for h in heads:
            s_ref[buf, h] = jnp.dot(ka_ref[pl.ds(base, grp), h * w:(h + 1) * w], qa_ref[h],
                                    preferred_element_type=F32)

    issue_scores(0, 0)

    own = pl.multiple_of(qi * bs, bs)
    k_i = lax.broadcasted_iota(I32, (bs, bs), 0)
    q_i = lax.broadcasted_iota(I32, (bs, bs), 1)
    carry = []
    for h in heads:
        s = jnp.dot(ka_ref[pl.ds(own, bs), h * w:h * w + LANES], qa_ref[h, :LANES, :],
                    preferred_element_type=F32)
        s = jnp.where(k_i <= q_i, s, NEG)
        m = jnp.max(s, axis=0, keepdims=True)
        p = jnp.exp2(s - m)
        l = jnp.sum(p, axis=0, keepdims=True)
        acc = jnp.dot(vt_ref[h, qi], p.astype(BF16), preferred_element_type=F32)
        carry += [m, l, acc]

    def absorb(g, buf, carry):
        out = []
        for h in heads:
            m, l, acc = carry[3 * h:3 * h + 3]
            s = s_ref[buf, h]
            mn = jnp.maximum(m, jnp.max(s, axis=0, keepdims=True))
            alpha = jnp.exp2(m - mn)
            p = jnp.exp2(s - mn)
            l = alpha * l + jnp.sum(p, axis=0, keepdims=True)
            p = p.astype(BF16)
            acc = alpha * acc
            for jb in range(ATT_GROUP):
                acc = acc + jnp.dot(vt_ref[h, g * ATT_GROUP + jb], p[jb * bs:(jb + 1) * bs],
                                    preferred_element_type=F32)
            out += [mn, l, acc]
        return out

    def body(t, carry):
        issue_scores(2 * t + 1, 1)
        carry = absorb(2 * t, 0, carry)
        issue_scores(2 * t + 2, 0)
        return tuple(absorb(2 * t + 1, 1, carry))

    n_used = (qi + ATT_GROUP - 1) // ATT_GROUP
    carry = lax.fori_loop(0, (n_used + 1) // 2, body, tuple(carry))
    for h in heads:
        m, l, acc = carry[3 * h:3 * h + 3]
        o_ref[:, h * LANES:(h + 1) * LANES] = (acc / l).T.astype(BF16)


def _attention(q_aug_t, k_aug, v_t):
    s = k_aug.shape[0]
    assert s % (ATT_GROUP * MOBA_BLOCK) == 0
    nb = s // MOBA_BLOCK
    hps = ATT_HEADS_PER_STEP
    once = pl.Buffered(1)
    return pl.pallas_call(
        _attn_kernel,
        grid=(ATT_HEADS // hps, nb),
        in_specs=[pl.BlockSpec((hps, 2 * LANES, MOBA_BLOCK), lambda h, i: (h, 0, i)),
                  pl.BlockSpec((s, hps * 2 * LANES), lambda h, i: (0, h), pipeline_mode=once),
                  pl.BlockSpec((hps, nb, LANES, MOBA_BLOCK), lambda h, i: (h, 0, 0, 0), pipeline_mode=once)],
        out_specs=pl.BlockSpec((MOBA_BLOCK, hps * LANES), lambda h, i: (i, h)),
        out_shape=jax.ShapeDtypeStruct((s, ATT_HEADS * ATT_HEAD_DIM), BF16),
        scratch_shapes=[pltpu.VMEM((2, hps, ATT_GROUP * MOBA_BLOCK, MOBA_BLOCK), F32)],
        name="moba_attention",
        compiler_params=_params("parallel", "arbitrary"),
    )(q_aug_t, k_aug, v_t)


def _rwkv_kernel(r_ref, k_ref, v_ref, lw_ref, la_ref, lg_ref, ww2_ref, wa2_ref, wg2_ref,
                 mur_ref, muk_ref, muv_ref, w0_ref, a0_ref, kk_ref, ka_ref, rk_ref, gnw_ref, gnb_ref,
                 o_ref, st_ref, prev_ref):
    j = pl.program_id(1)
    t = SCAN_CHUNK
    n = RWKV_HEAD_DIM
    ts = r_ref.shape[0]
    chunks = range(ts // t)

    @pl.when(j == 0)
    def _():
        st_ref[...] = jnp.zeros_like(st_ref)
        prev_ref[...] = jnp.zeros_like(prev_ref)

    rp, kp, vp = r_ref[...].astype(F32), k_ref[...].astype(F32), v_ref[...].astype(F32)
    r = rp + (_shift_rows(rp, prev_ref[0:1, :]) - rp) * mur_ref[...]
    k = kp + (_shift_rows(kp, prev_ref[1:2, :]) - kp) * muk_ref[...]
    v = vp + (_shift_rows(vp, prev_ref[2:3, :]) - vp) * muv_ref[...]
    prev_ref[0:1, :] = rp[ts - 1:ts, :]
    prev_ref[1:2, :] = kp[ts - 1:ts, :]
    prev_ref[2:3, :] = vp[ts - 1:ts, :]

    d = w0_ref[...] + _dot(lw_ref[...], ww2_ref[...])
    logw = -math.exp(-0.5) * _sigmoid(d)
    a = _sigmoid(a0_ref[...] + _dot(la_ref[...], wa2_ref[...]))
    g = _dot(lg_ref[...], wg2_ref[...])

    li = lax.broadcasted_iota(I32, (LANES, LANES), 0)
    lj = lax.broadcasted_iota(I32, (LANES, LANES), 1)
    same_head = (li // n) == (lj // n)
    head_sum = jnp.where(same_head, 1.0, 0.0)

    kk = k * kk_ref[...]
    kk = kk / jnp.maximum(jnp.sqrt(_dot_split(kk * kk, head_sum)), 1e-12)
    kt = k * (1.0 + (a - 1.0) * ka_ref[...])
    bonus = _dot_split(r * kt * rk_ref[...], head_sum) * v

    same_blk = (li // t) == (lj // t)
    m_strict = same_blk & (lj < li)
    m_incl = same_blk & (lj <= li)
    ti = lax.broadcasted_iota(I32, (t, t), 0)
    tj = lax.broadcasted_iota(I32, (t, t), 1)
    tril_incl = jnp.where(tj <= ti, 1.0, 0.0)
    lane_a = lax.broadcasted_iota(I32, (t, LANES), 1) < n
    eye = jnp.where(li == lj, 1.0, 0.0)

    def stack_masked(z):
        return jnp.concatenate([jnp.where(lane_a, z, 0.0), jnp.where(lane_a, 0.0, z)], axis=0)

    def stack_plain(z):
        return jnp.concatenate([z, z], axis=0)

    def rows(z, c):
        return z[c * t:(c + 1) * t]

    cum = jnp.concatenate([_dot_split_t(tril_incl, rows(logw, c)) for c in chunks], axis=0)
    g_t = jnp.exp(cum)
    g_inv = jnp.exp(-cum)
    xa_f = -kk * jnp.exp(cum - logw)
    xr_f = r * g_t
    yb_f = kk * a * g_inv
    yk_f = kt * g_inv

    xa = [stack_masked(rows(xa_f, c)) for c in chunks]
    xr = [stack_masked(rows(xr_f, c)) for c in chunks]
    yb = [stack_plain(rows(yb_f, c)) for c in chunks]
    vs = [stack_masked(rows(v, c)) for c in chunks]
    ybk = [jnp.concatenate([yb[c], stack_plain(rows(yk_f, c))], axis=0) for c in chunks]
    sc = [_dot_nt(jnp.concatenate([xa[c], xr[c]], axis=0), ybk[c]) for c in chunks]
    a_ab = [jnp.where(m_strict, sc[c][:2 * t, :2 * t], 0.0) for c in chunks]
    a_ak = [jnp.where(m_strict, sc[c][:2 * t, 2 * t:], 0.0) for c in chunks]
    a_rb = [jnp.where(m_incl, sc[c][2 * t:, :2 * t], 0.0) for c in chunks]
    a_rk = [jnp.where(m_incl, sc[c][2 * t:, 2 * t:], 0.0) for c in chunks]
    inv = [eye + a_ab[c] for c in chunks]
    pw = [_dot(a_ab[c], a_ab[c]) for c in chunks]
    for _ in range(int(math.log2(t)) - 2):
        both = [_dot(pw[c], jnp.concatenate([pw[c], inv[c]], axis=1)) for c in chunks]
        pw = [both[c][:, :LANES] for c in chunks]
        inv = [inv[c] + both[c][:, LANES:] for c in chunks]
    inv = [inv[c] + _dot(pw[c], inv[c]) for c in chunks]
    av = [_dot(a_ak[c], vs[c]) for c in chunks]
    mw = [_dot(inv[c], jnp.concatenate([xa[c], av[c]], axis=1)) for c in chunks]
    zeros = jnp.zeros((2 * t, LANES), F32)
    rw = [_dot(jnp.concatenate([a_rb[c], a_rk[c]], axis=1),
               jnp.concatenate([mw[c], jnp.concatenate([zeros, vs[c]], axis=1)], axis=0)) for c in chunks]
    r_hat = [xr[c] + rw[c][:, :LANES] for c in chunks]
    y_hat = [rw[c][:, LANES:] for c in chunks]
    pm = [jnp.where(same_head, _dot_tn(mw[c][:, :LANES], yb[c]), 0.0) for c in chunks]
    qm = [jnp.where(same_head, _dot_tn(jnp.concatenate([mw[c][:, LANES:], vs[c]], axis=0), ybk[c]), 0.0)
          for c in chunks]

    st = st_ref[...]
    ys = []
    for c in chunks:
        y2 = _dot_nt(r_hat[c], st) + y_hat[c]
        ys.append(y2[:t] + y2[t:])
        st = (st + _dot(st, pm[c]) + qm[c]) * g_t[(c + 1) * t - 1:(c + 1) * t, :]
    st_ref[...] = st
    y = jnp.concatenate(ys, axis=0)

    mean = _dot_split(y, head_sum) * (1.0 / n)
    yc = y - mean
    var = _dot_split(yc * yc, head_sum) * (1.0 / n)
    yn = yc * lax.rsqrt(var + GN_EPS) * gnw_ref[...] + gnb_ref[...]
    o_ref[...] = ((yn + bonus) * g).astype(BF16)


def _rwkv(proj, lw, la, lg, w_w2, w_a2, w_g2, mu_r, mu_k, mu_v, w0, a0, k_k, k_a, r_k, gn_w, gn_b,
          col0, width):
    s = proj.shape[0]
    ts = 512
    npair = width // LANES
    cb = col0 // LANES
    row = lambda a: pl.BlockSpec((ts, a.shape[1]), lambda p, j: (j, 0))
    wcol = lambda a: pl.BlockSpec((a.shape[0], LANES), lambda p, j: (0, p))
    vec = pl.BlockSpec((1, LANES), lambda p, j: (0, p))
    return pl.pallas_call(
        _rwkv_kernel,
        grid=(npair, s // ts),
        in_specs=[pl.BlockSpec((ts, LANES), lambda p, j: (j, cb + p)),
                  pl.BlockSpec((ts, LANES), lambda p, j: (j, cb + npair + p)),
                  pl.BlockSpec((ts, LANES), lambda p, j: (j, cb + 2 * npair + p)),
                  row(lw), row(la), row(lg), wcol(w_w2), wcol(w_a2), wcol(w_g2)] + [vec] * 10,
        out_specs=pl.BlockSpec((ts, LANES), lambda p, j: (j, p)),
        out_shape=jax.ShapeDtypeStruct((s, width), BF16),
        scratch_shapes=[pltpu.VMEM((LANES, LANES), F32), pltpu.VMEM((SUBLANES, LANES), F32)],
        name="rwkv7_scan",
        compiler_params=_params("parallel", "arbitrary"),
    )(proj, proj, proj, lw, la, lg, w_w2, w_a2, w_g2, mu_r, mu_k, mu_v, w0, a0, k_k, k_a, r_k, gn_w, gn_b)


def _merge_kernel(oa_ref, or_ref, ga_ref, gr_ref, wa_ref, wr_ref, o_ref, wab_ref, wrb_ref):
    @pl.when(pl.program_id(1) == 0)
    def _():
        wab_ref[...] = wa_ref[...].astype(BF16)
        wrb_ref[...] = wr_ref[...].astype(BF16)

    ua = jnp.dot(oa_ref[...], wab_ref[...], preferred_element_type=F32)
    ur = jnp.dot(or_ref[...], wrb_ref[...], preferred_element_type=F32)
    mix = _sigmoid(ga_ref[...].astype(F32)) * ua + _sigmoid(gr_ref[...].astype(F32)) * ur
    o_ref[...] = mix.astype(BF16)


def _merge(o_att, o_rwkv, proj, w_up_att, w_up_rwkv, gate_col0):
    s, ka = o_att.shape
    kr = o_rwkv.shape[1]
    d = w_up_att.shape[1]
    tm, tn = 512, 1024
    gb = gate_col0 // tn
    return pl.pallas_call(
        _merge_kernel,
        grid=(d // tn, s // tm),
        in_specs=[pl.BlockSpec((tm, ka), lambda j, i: (i, 0)),
                  pl.BlockSpec((tm, kr), lambda j, i: (i, 0)),
                  pl.BlockSpec((tm, tn), lambda j, i: (i, gb + j)),
                  pl.BlockSpec((tm, tn), lambda j, i: (i, gb + d // tn + j)),
                  pl.BlockSpec((ka, tn), lambda j, i: (0, j)),
                  pl.BlockSpec((kr, tn), lambda j, i: (0, j))],
        out_specs=pl.BlockSpec((tm, tn), lambda j, i: (i, j)),
        out_shape=jax.ShapeDtypeStruct((s, d), BF16),
        scratch_shapes=[pltpu.VMEM((ka, tn), BF16), pltpu.VMEM((kr, tn), BF16)],
        name="gated_merge",
        compiler_params=_params("arbitrary", "arbitrary"),
    )(o_att, o_rwkv, proj, proj, w_up_att, w_up_rwkv)


def _route_kernel(x_ref, y_ref, gt_ref, gpost_ref, gpre_ref, sc_ref, sh_ref, wr_ref, br_ref,
                  x1_ref, h2_ref, ei_ref, wt_ref, rk_ref, cnt_ref, run_ref):
    @pl.when(pl.program_id(0) == 0)
    def _():
        run_ref[...] = jnp.zeros_like(run_ref)

    x1 = x_ref[...] + gt_ref[...] * _rms(y_ref[...], gpost_ref[...])
    x1_ref[...] = x1
    h2 = _rms(x1, gpre_ref[...]) * (1.0 + sc_ref[...]) + sh_ref[...]
    h2_ref[...] = h2
    logits = _dot_hi(h2, wr_ref[...]) + br_ref[...]
    lane = lax.broadcasted_iota(I32, logits.shape, 1)
    big = jnp.int32(4 * LANES)
    gmask = (lane >= N_EXPERTS) & (lane < N_EXPERTS + N_GROUPS)
    mg = jnp.max(jnp.where(gmask, logits, -jnp.inf), axis=1, keepdims=True)
    eg = jnp.where(gmask, jnp.exp(logits - mg), 0.0)
    pg = eg / jnp.sum(eg, axis=1, keepdims=True)
    pg_top = jnp.max(pg, axis=1, keepdims=True)
    g_idx = jnp.min(jnp.where(gmask & (pg == pg_top), lane, big), axis=1, keepdims=True) - N_EXPERTS
    emask = (lane >= g_idx * EXPERTS_PER_GROUP) & (lane < (g_idx + 1) * EXPERTS_PER_GROUP)
    me = jnp.max(jnp.where(emask, logits, -jnp.inf), axis=1, keepdims=True)
    ee = jnp.where(emask, jnp.exp(logits - me), 0.0)
    pe = ee / jnp.sum(ee, axis=1, keepdims=True)
    p1 = jnp.max(pe, axis=1, keepdims=True)
    i1 = jnp.min(jnp.where(emask & (pe == p1), lane, big), axis=1, keepdims=True)
    rest = emask & (lane != i1)
    p2 = jnp.max(jnp.where(rest, pe, -jnp.inf), axis=1, keepdims=True)
    i2 = jnp.min(jnp.where(rest & (pe == p2), lane, big), axis=1, keepdims=True)
    den = p1 + p2
    ei_ref[...] = jnp.where(lane == 0, i1, jnp.where(lane == 1, i2, 0))
    wt_ref[...] = jnp.where(lane == 0, pg_top * p1 / den, jnp.where(lane == 1, pg_top * p2 / den, 0.0))
    tm = logits.shape[0]
    chosen = jnp.where((lane == i1) | (lane == i2), 1.0, 0.0)
    t_i = lax.broadcasted_iota(I32, (tm, tm), 0)
    t_j = lax.broadcasted_iota(I32, (tm, tm), 1)
    before = _dot(jnp.where(t_j < t_i, 1.0, 0.0), chosen) + run_ref[...]
    r1 = jnp.sum(jnp.where(lane == i1, before, 0.0), axis=1, keepdims=True)
    r2 = jnp.sum(jnp.where(lane == i2, before, 0.0), axis=1, keepdims=True)
    rk_ref[...] = jnp.where(lane == 0, r1, jnp.where(lane == 1, r2, 0.0)).astype(I32)
    run_ref[...] += jnp.sum(chosen, axis=0, keepdims=True)
    cnt_ref[...] = run_ref[...].astype(I32)


def _route(x, y, gt1, g_post, g_pre, sc2, sh2, w_router, b_router):
    s, d = x.shape
    tm = 256
    vec = pl.BlockSpec((1, d), lambda i: (0, 0))
    rowblk = pl.BlockSpec((tm, d), lambda i: (i, 0))
    small = pl.BlockSpec((tm, LANES), lambda i: (i, 0))
    lane_row = pl.BlockSpec((1, LANES), lambda i: (0, 0))
    return pl.pallas_call(
        _route_kernel,
        grid=(s // tm,),
        in_specs=[rowblk, rowblk, vec, vec, vec, vec, vec, pl.BlockSpec((d, LANES), lambda i: (0, 0)), lane_row],
        out_specs=[rowblk, rowblk, small, small, small, lane_row],
        out_shape=[jax.ShapeDtypeStruct((s, d), F32), jax.ShapeDtypeStruct((s, d), F32),
                   jax.ShapeDtypeStruct((s, LANES), I32), jax.ShapeDtypeStruct((s, LANES), F32),
                   jax.ShapeDtypeStruct((s, LANES), I32), jax.ShapeDtypeStruct((1, LANES), I32)],
        scratch_shapes=[pltpu.VMEM((1, LANES), F32)],
        name="norm_route",
        compiler_params=_params("arbitrary"),
    )(x, y, gt1, g_post, g_pre, sc2, sh2, w_router, b_router)


def _slots_kernel(ei_ref, rk_ref, ps_ref, pos_ref):
    lane = lax.broadcasted_iota(I32, ei_ref.shape, 1)
    ei, rk = ei_ref[...], rk_ref[...]
    ps = ps_ref[...]
    cols = []
    for kk in range(TOP_K):
        e = ei[:, kk:kk + 1]
        cols.append(jnp.sum(jnp.where(lane == e, ps, 0), axis=1, keepdims=True) + rk[:, kk:kk + 1])
    pos_ref[...] = jnp.where(lane == 0, cols[0], jnp.where(lane == 1, cols[1], 0))


def _slots(e_idx, rank, pstart_row):
    n = e_idx.shape[0]
    tm = 1024
    blk = pl.BlockSpec((tm, LANES), lambda i: (i, 0))
    return pl.pallas_call(
        _slots_kernel,
        grid=(n // tm,),
        in_specs=[blk, blk, pl.BlockSpec((1, LANES), lambda i: (0, 0))],
        out_specs=blk,
        out_shape=jax.ShapeDtypeStruct((n, LANES), I32),
        name="dispatch_slots",
        compiler_params=_params("parallel"),
    )(e_idx, rank, pstart_row)


def _slot_tokens_kernel(pos_ref, tok_ref):
    n_pairs = pos_ref.shape[0]

    def clear(s, _):
        tok_ref[s] = 0
        return 0

    def place(p, _):
        tok_ref[pos_ref[p]] = lax.div(p, TOP_K)
        return 0

    lax.fori_loop(0, tok_ref.shape[0], clear, 0, unroll=16)
    lax.fori_loop(0, n_pairs, place, 0, unroll=16)


def _slot_tokens(pos_flat, n_slots):
    return pl.pallas_call(
        _slot_tokens_kernel,
        in_specs=[pl.BlockSpec(memory_space=pltpu.SMEM)],
        out_specs=pl.BlockSpec(memory_space=pltpu.SMEM),
        out_shape=jax.ShapeDtypeStruct((n_slots,), I32),
        name="dispatch_slot_tokens",
    )(pos_flat)


def _row_copy(src_hbm, row, dst, dst_row, sem):
    return pltpu.make_async_copy(src_hbm.at[pl.ds(row, 1), :], dst.at[pl.ds(dst_row, 1), :], sem)


def _expert_kernel(bstart_ref, bcount_ref, tok_ref, nused_ref, h_hbm, wg_hbm, wu_hbm, wd_hbm, y_hbm,
                   xg_ref, yo_ref, wg_ref, wu_ref, wd_ref, wgb_ref, wub_ref, wdb_ref, gsem, osem, wsem):
    e = pl.program_id(0)
    n_exp = pl.num_programs(0)
    nused = nused_ref[0]
    rows = EXPERT_BLOCK
    n_blk = y_hbm.shape[0] // rows
    first = bstart_ref[e]
    count = bcount_ref[e]

    def weight_copies(expert, slot):
        return [pltpu.make_async_copy(src.at[expert], dst.at[slot], wsem.at[slot])
                for src, dst in ((wg_hbm, wg_ref), (wu_hbm, wu_ref), (wd_hbm, wd_ref))]

    def start_rows(blk, slot):
        for i in range(rows):
            _row_copy(h_hbm, tok_ref[blk * rows + i], xg_ref.at[slot], i, gsem.at[slot]).start()

    def wait_rows(slot):
        pltpu.make_async_copy(h_hbm.at[pl.ds(0, rows), :], xg_ref.at[slot], gsem.at[slot]).wait()

    def out_copy(blk, slot):
        return pltpu.make_async_copy(yo_ref.at[slot], y_hbm.at[pl.ds(blk * rows, rows), :], osem.at[slot])

    wslot = e % 2

    @pl.when(e == 0)
    def _():
        for cp in weight_copies(0, 0):
            cp.start(priority=1)
        start_rows(0, 0)

    @pl.when(e + 1 < n_exp)
    def _():
        for cp in weight_copies(e + 1, 1 - wslot):
            cp.start(priority=1)

    for cp in weight_copies(e, wslot):
        cp.wait()

    @pl.when(count > 0)
    def _():
        wgb_ref[...] = wg_ref[wslot].astype(BF16)
        wub_ref[...] = wu_ref[wslot].astype(BF16)
        wdb_ref[...] = wd_ref[wslot].astype(BF16)

    def block(j, _):
        blk = first + j
        slot = blk % 2

        @pl.when(blk >= 2)
        def _():
            out_copy(blk - 2, slot).wait()

        wait_rows(slot)
        start_rows(jnp.minimum(blk + 1, nused - 1), 1 - slot)
        xb = xg_ref[slot].astype(BF16)
        hg = jnp.dot(xb, wgb_ref[...], preferred_element_type=F32)
        hu = jnp.dot(xb, wub_ref[...], preferred_element_type=F32)
        hid = hg * _sigmoid(hg) * hu
        yo_ref[slot] = jnp.dot(hid.astype(BF16), wdb_ref[...], preferred_element_type=F32)
        out_copy(blk, slot).start()

        @pl.when(blk == nused - 1)
        def _():
            wait_rows(1 - slot)

        return 0

    lax.fori_loop(0, count, block, 0)

    @pl.when(e == pl.num_programs(0) - 1)
    def _():
        @pl.when(nused >= 2)
        def _():
            out_copy(nused - 2, nused % 2).wait()

        out_copy(nused - 1, (nused - 1) % 2).wait()
        yo_ref[0] = jnp.zeros(yo_ref.shape[1:], F32)

        def fill(blk, _):
            out_copy(blk, 0).start()
            return 0

        def drain(blk, _):
            out_copy(blk, 0).wait()
            return 0

        lax.fori_loop(nused, n_blk, fill, 0)
        lax.fori_loop(nused, n_blk, drain, 0)


def _experts(h2, bstart, bcount, buf_tok, nused, w_gate_e, w_up_e, w_down_e):
    n, d = h2.shape
    n_exp, _, f = w_gate_e.shape
    n_slots = buf_tok.shape[0]
    grid_spec = pltpu.PrefetchScalarGridSpec(
        num_scalar_prefetch=4,
        grid=(n_exp,),
        in_specs=[pl.BlockSpec(memory_space=pl.ANY)] * 4,
        out_specs=pl.BlockSpec(memory_space=pl.ANY),
        scratch_shapes=[pltpu.VMEM((2, EXPERT_BLOCK, d), F32), pltpu.VMEM((2, EXPERT_BLOCK, d), F32),
                        pltpu.VMEM((2, d, f), F32), pltpu.VMEM((2, d, f), F32), pltpu.VMEM((2, f, d), F32),
                        pltpu.VMEM((d, f), BF16), pltpu.VMEM((d, f), BF16), pltpu.VMEM((f, d), BF16),
                        pltpu.SemaphoreType.DMA((2,)), pltpu.SemaphoreType.DMA((2,)),
                        pltpu.SemaphoreType.DMA((2,))],
    )
    return pl.pallas_call(
        _expert_kernel,
        grid_spec=grid_spec,
        out_shape=jax.ShapeDtypeStruct((n_slots, d), F32),
        name="experts",
        compiler_params=_params("arbitrary"),
    )(bstart, bcount, buf_tok, nused, h2, w_gate_e, w_up_e, w_down_e)


def _combine_kernel(pos_ref, y_hbm, wt_ref, x1_ref, gt_ref, gpost_ref, o_ref, rows_ref, sem):
    i = pl.program_id(0)
    nsteps = pl.num_programs(0)
    tm = x1_ref.shape[0]

    def start_rows(step, slot):
        for r in range(tm):
            for kk in range(TOP_K):
                _row_copy(y_hbm, pos_ref[(step * tm + r) * TOP_K + kk], rows_ref.at[slot, kk], r,
                          sem.at[slot]).start()

    def wait_rows(slot):
        for kk in range(TOP_K):
            pltpu.make_async_copy(y_hbm.at[pl.ds(0, tm), :], rows_ref.at[slot, kk], sem.at[slot]).wait()

    @pl.when(i == 0)
    def _():
        start_rows(0, 0)

    slot = i % 2
    wait_rows(slot)
    start_rows(jnp.minimum(i + 1, nsteps - 1), 1 - slot)
    wt = wt_ref[...]
    y = rows_ref[slot, 0] * wt[:, 0:1] + rows_ref[slot, 1] * wt[:, 1:2]
    o_ref[...] = x1_ref[...] + gt_ref[...] * _rms(y, gpost_ref[...])

    @pl.when(i == nsteps - 1)
    def _():
        wait_rows(1 - slot)


def _combine(pos, y_buf, wts, x1, gt2, g_post):
    n, d = x1.shape
    tm = 128
    vec = pl.BlockSpec((1, d), lambda i, p: (0, 0))
    grid_spec = pltpu.PrefetchScalarGridSpec(
        num_scalar_prefetch=1,
        grid=(n // tm,),
        in_specs=[pl.BlockSpec(memory_space=pl.ANY),
                  pl.BlockSpec((tm, LANES), lambda i, p: (i, 0)),
                  pl.BlockSpec((tm, d), lambda i, p: (i, 0)), vec, vec],
        out_specs=pl.BlockSpec((tm, d), lambda i, p: (i, 0)),
        scratch_shapes=[pltpu.VMEM((2, TOP_K, tm, d), F32), pltpu.SemaphoreType.DMA((2,))],
    )
    return pl.pallas_call(
        _combine_kernel,
        grid_spec=grid_spec,
        out_shape=jax.ShapeDtypeStruct((n, d), F32),
        name="combine",
        compiler_params=_params("arbitrary"),
    )(pos, y_buf, wts, x1, gt2, g_post)


def _segment_tables(counts_row):
    counts = counts_row[0, :N_EXPERTS]
    pcounts = (counts + EXPERT_BLOCK - 1) // EXPERT_BLOCK * EXPERT_BLOCK
    pend = jnp.cumsum(pcounts)
    pstart = pend - pcounts
    nused = (pend[-1] // EXPERT_BLOCK).astype(I32)
    pstart_row = jnp.pad(pstart, (0, LANES - N_EXPERTS)).reshape(1, LANES)
    return pstart_row, pstart // EXPERT_BLOCK, pcounts // EXPERT_BLOCK, nused.reshape(1)


def _rope_tables(s):
    half = ROPE_DIM // 2
    inv = ROPE_THETA ** (-jnp.arange(half, dtype=F32) / half)
    ang = jnp.arange(s, dtype=F32)[:, None] * inv[None, :]
    cos, sin = lax.optimization_barrier((jnp.cos(ang), jnp.sin(ang)))
    pad = jnp.zeros((s, LANES - ROPE_DIM), F32)
    zero = jnp.zeros((s, half), F32)
    cos_t = jnp.concatenate([cos, cos, pad + 1.0], axis=1)
    sin1_t = jnp.concatenate([-sin, zero, pad], axis=1)
    sin2_t = jnp.concatenate([zero, sin, pad], axis=1)
    return cos_t, sin1_t, sin2_t


def _layer(x, c_col, w_ada, b_ada, g_pre_mix, g_post_mix, g_pre_ffn, g_post_ffn, w_in, mu_r, mu_k, mu_v,
           mu_w, mu_a, mu_g, w0, w_w1, w_w2, a0, w_a1, w_a2, w_g1, w_g2, k_k, k_a, r_k, gn_w, gn_b,
           w_up_att, w_up_rwkv, w_o, w_rg, b_rg, w_re, b_re, w_gate_e, w_up_e, w_down_e):
    s, d = x.shape
    att_w = ATT_HEADS * ATT_HEAD_DIM
    rwkv_w = w_up_rwkv.shape[0]
    row = lambda a: a.reshape(1, -1)

    ada = _ada(c_col, w_ada, row(b_ada))
    sh1, sc1, gt1, sh2, sc2, gt2 = (ada[:, i * d:(i + 1) * d] for i in range(6))

    h, lw, la, lg = _prenorm(x, row(g_pre_mix), sc1, sh1, row(mu_w), row(mu_a), row(mu_g), w_w1, w_a1, w_g1)
    proj = _matmul(h, w_in, BF16)

    q_aug_t, k_aug, v_t = _rope_gate(proj, *_rope_tables(s))
    o_att = _attention(q_aug_t, k_aug, v_t)

    o_rwkv = _rwkv(proj, lw, la, lg, w_w2, w_a2, w_g2, row(mu_r), row(mu_k), row(mu_v), row(w0), row(a0),
                   row(k_k), row(k_a), row(r_k), row(gn_w), row(gn_b), col0=3 * att_w, width=rwkv_w)

    mix = _merge(o_att, o_rwkv, proj, w_up_att, w_up_rwkv, gate_col0=3 * att_w + 3 * rwkv_w)
    y = _matmul(mix, w_o, F32)

    w_router = jnp.pad(jnp.concatenate([w_re, w_rg], axis=1), ((0, 0), (0, LANES - N_EXPERTS - N_GROUPS)))
    b_router = jnp.pad(jnp.concatenate([b_re, b_rg]), (0, LANES - N_EXPERTS - N_GROUPS)).reshape(1, LANES)
    x1, h2, e_idx, wts, rank, counts = _route(x, y, gt1, row(g_post_mix), row(g_pre_ffn), sc2, sh2,
                                              w_router, b_router)

    n_pairs = s * TOP_K
    n_blk = (n_pairs + N_EXPERTS * (EXPERT_BLOCK - 1) + EXPERT_BLOCK - 1) // EXPERT_BLOCK
    pstart_row, bstart, bcount, nused = _segment_tables(counts)
    pos = _slots(e_idx, rank, pstart_row)[:, :TOP_K].reshape(n_pairs)
    buf_tok = _slot_tokens(pos, n_blk * EXPERT_BLOCK)
    y_buf = _experts(h2, bstart, bcount, buf_tok, nused, w_gate_e, w_up_e, w_down_e)
    return _combine(pos, y_buf, wts, x1, gt2, row(g_post_ffn))


def kernel(x, c, w_ada, b_ada, g_pre_mix, g_post_mix, g_pre_ffn, g_post_ffn, w_in, mu_r, mu_k, mu_v, mu_w, mu_a, mu_g, w0, w_w1, w_w2, a0, w_a1, w_a2, w_g1, w_g2, k_k, k_a, r_k, gn_w, gn_b, w_up_att, w_up_rwkv, w_o, w_rg, b_rg, w_re, b_re, w_gate_e, w_up_e, w_down_e):
    b, s, d = x.shape
    assert b == 1, "one sequence per call"
    params = (w_ada, b_ada, g_pre_mix, g_post_mix, g_pre_ffn, g_post_ffn, w_in, mu_r, mu_k, mu_v, mu_w, mu_a,
              mu_g, w0, w_w1, w_w2, a0, w_a1, w_a2, w_g1, w_g2, k_k, k_a, r_k, gn_w, gn_b, w_up_att,
              w_up_rwkv, w_o, w_rg, b_rg, w_re, b_re, w_gate_e, w_up_e, w_down_e)
    xs = x.reshape(s, d)
    c_col = c.reshape(d, 1)
    for l in range(w_ada.shape[0]):
        xs = _layer(xs, c_col, *(p[l] for p in params))
    return xs.reshape(b, s, d)
```

```python
import math

import jax
import jax.numpy as jnp
from jax import lax
from jax.experimental import pallas as pl
from jax.experimental.pallas import tpu as pltpu

F32 = jnp.float32
BF16 = jnp.bfloat16
I32 = jnp.int32
HI = lax.Precision.HIGHEST

LANES = 128
SUBLANES = 8
VMEM_LIMIT = 56 * 1024 * 1024

ATT_HEADS = 8
ATT_HEAD_DIM = 128
MOBA_BLOCK = 256
MOBA_TOPK = 3
ATT_GROUP = 4
ATT_HEADS_PER_STEP = 2
ROPE_THETA = 500000.0
ROPE_DIM = ATT_HEAD_DIM // 4
RWKV_HEAD_DIM = 64
GN_EPS = 64e-5
N_GROUPS = 8
EXPERTS_PER_GROUP = 8
N_EXPERTS = N_GROUPS * EXPERTS_PER_GROUP
TOP_K = 2
EXPERT_BLOCK = 128
RMS_EPS = 1e-6
NEG = -1e30
SCAN_CHUNK = 64
Q_SCALE = ATT_HEAD_DIM ** -0.5 * math.log2(math.e)


def _params(*sem):
    return pltpu.CompilerParams(dimension_semantics=sem, vmem_limit_bytes=VMEM_LIMIT)


def _rms(z, g):
    return z * lax.rsqrt(jnp.mean(z * z, axis=-1, keepdims=True) + RMS_EPS) * g


def _sigmoid(z):
    return 1.0 / (1.0 + jnp.exp(-z))


def _dot(a, b):
    return jnp.dot(a.astype(BF16), b.astype(BF16), preferred_element_type=F32)


def _dot_nt(a, b):
    return lax.dot_general(a.astype(BF16), b.astype(BF16), (((1,), (1,)), ((), ())),
                           preferred_element_type=F32)


def _dot_tn(a, b):
    return lax.dot_general(a.astype(BF16), b.astype(BF16), (((0,), (0,)), ((), ())),
                           preferred_element_type=F32)


def _dot_hi(a, b):
    return jnp.dot(a, b, precision=HI, preferred_element_type=F32)


def _split3(a):
    hi = a.astype(BF16)
    r1 = a - hi.astype(F32)
    mid = r1.astype(BF16)
    lo = (r1 - mid.astype(F32)).astype(BF16)
    return hi, mid, lo


def _dot_split(a, b01):
    b = b01.astype(BF16)
    hi, mid, _ = _split3(a)
    return jnp.dot(jnp.concatenate([hi, mid], axis=1), jnp.concatenate([b, b], axis=0),
                   preferred_element_type=F32)


def _dot_split_t(b01, a):
    b = b01.astype(BF16)
    return jnp.dot(jnp.concatenate([b, b, b], axis=1), jnp.concatenate(_split3(a), axis=0),
                   preferred_element_type=F32)


def _shift_rows(z, prev_row):
    rolled = pltpu.roll(z, 1, 0)
    row = lax.broadcasted_iota(I32, z.shape, 0)
    return jnp.where(row == 0, prev_row, rolled)


def _ada_kernel(c_ref, w_ref, b_ref, o_ref):
    o_ref[...] = jnp.sum(c_ref[...] * w_ref[...], axis=0, keepdims=True) + b_ref[...]


def _ada(c_col, w_ada, b_ada):
    d, n = w_ada.shape
    tn = 1024
    return pl.pallas_call(
        _ada_kernel,
        grid=(n // tn,),
        in_specs=[pl.BlockSpec((d, 1), lambda j: (0, 0)),
                  pl.BlockSpec((d, tn), lambda j: (0, j)),
                  pl.BlockSpec((1, tn), lambda j: (0, j))],
        out_specs=pl.BlockSpec((1, tn), lambda j: (0, j)),
        out_shape=jax.ShapeDtypeStruct((1, n), F32),
        name="ada",
        compiler_params=_params("parallel"),
    )(c_col, w_ada, b_ada)


def _prenorm_kernel(x_ref, xp_ref, g_ref, sc_ref, sh_ref, muw_ref, mua_ref, mug_ref,
                    ww1_ref, wa1_ref, wg1_ref, h_ref, lw_ref, la_ref, lg_ref):
    i = pl.program_id(0)
    g, sc, sh = g_ref[...], sc_ref[...], sh_ref[...]
    h = _rms(x_ref[...], g) * (1.0 + sc) + sh
    hp = _rms(xp_ref[SUBLANES - 1:SUBLANES, :], g) * (1.0 + sc) + sh
    hp = jnp.where(i == 0, 0.0, hp)
    dh = _shift_rows(h, hp) - h
    h_ref[...] = h.astype(BF16)
    lw_ref[...] = jnp.tanh(_dot(h + dh * muw_ref[...], ww1_ref[...]))
    la_ref[...] = _dot(h + dh * mua_ref[...], wa1_ref[...])
    lg_ref[...] = _sigmoid(_dot(h + dh * mug_ref[...], wg1_ref[...]))


def _prenorm(x, g, sc, sh, mu_w, mu_a, mu_g, w_w1, w_a1, w_g1):
    s, d = x.shape
    tm = 256
    rpb = tm // SUBLANES
    vec = pl.BlockSpec((1, d), lambda i: (0, 0))
    full = lambda a: pl.BlockSpec(a.shape, lambda i: (0, 0))
    lw, la, lg = w_w1.shape[1], w_a1.shape[1], w_g1.shape[1]
    return pl.pallas_call(
        _prenorm_kernel,
        grid=(s // tm,),
        in_specs=[pl.BlockSpec((tm, d), lambda i: (i, 0)),
                  pl.BlockSpec((SUBLANES, d), lambda i: (jnp.maximum(i * rpb - 1, 0), 0)),
                  vec, vec, vec, vec, vec, vec, full(w_w1), full(w_a1), full(w_g1)],
        out_specs=[pl.BlockSpec((tm, d), lambda i: (i, 0)),
                   pl.BlockSpec((tm, lw), lambda i: (i, 0)),
                   pl.BlockSpec((tm, la), lambda i: (i, 0)),
                   pl.BlockSpec((tm, lg), lambda i: (i, 0))],
        out_shape=[jax.ShapeDtypeStruct((s, d), BF16),
                   jax.ShapeDtypeStruct((s, lw), F32),
                   jax.ShapeDtypeStruct((s, la), F32),
                   jax.ShapeDtypeStruct((s, lg), F32)],
        name="prenorm_lora",
        compiler_params=_params("parallel"),
    )(x, x, g, sc, sh, mu_w, mu_a, mu_g, w_w1, w_a1, w_g1)


def _mm_kernel(a_ref, w_ref, o_ref, wb_ref):
    @pl.when(pl.program_id(1) == 0)
    def _():
        wb_ref[...] = w_ref[...].astype(BF16)

    o_ref[...] = jnp.dot(a_ref[...], wb_ref[...], preferred_element_type=F32).astype(o_ref.dtype)


def _matmul(a, w, out_dtype, tm=512, tn=1024):
    m, k = a.shape
    n = w.shape[1]
    tn = min(tn, n)
    return pl.pallas_call(
        _mm_kernel,
        grid=(n // tn, m // tm),
        in_specs=[pl.BlockSpec((tm, k), lambda j, i: (i, 0)),
                  pl.BlockSpec((k, tn), lambda j, i: (0, j))],
        out_specs=pl.BlockSpec((tm, tn), lambda j, i: (i, j)),
        out_shape=jax.ShapeDtypeStruct((m, n), out_dtype),
        scratch_shapes=[pltpu.VMEM((k, tn), BF16)],
        name="matmul",
        compiler_params=_params("arbitrary", "arbitrary"),
    )(a, w)


def _rope_gate_kernel(p_ref, c_ref, s1_ref, s2_ref, qa_ref, ka_ref, vt_ref, km_ref):
    i = pl.program_id(0)
    bs = MOBA_BLOCK
    nbp = km_ref.shape[1]

    @pl.when(i == 0)
    def _():
        km_ref[...] = jnp.zeros_like(km_ref)

    c, s1, s2 = c_ref[...], s1_ref[...], s2_ref[...]

    def rope(z):
        return z * c + pltpu.roll(z, LANES - ROPE_DIM // 2, 1) * s1 + pltpu.roll(z, ROPE_DIM // 2, 1) * s2

    row = lax.broadcasted_iota(I32, (nbp, bs), 0)
    lane = lax.broadcasted_iota(I32, (bs, LANES), 1)
    onehot = jnp.where(lane == i, 1.0, 0.0).astype(BF16)
    for h in range(ATT_HEADS):
        q = rope(p_ref[:, h * LANES:(h + 1) * LANES].astype(F32))
        k = rope(p_ref[:, (ATT_HEADS + h) * LANES:(ATT_HEADS + h + 1) * LANES].astype(F32))
        g = lax.dot_general(km_ref[h], q, (((1,), (1,)), ((), ())), precision=HI, preferred_element_type=F32)
        g = jnp.where(row < i, g, NEG)
        sel_t = jnp.zeros(g.shape, F32)
        for _ in range(MOBA_TOPK):
            mx = jnp.max(g, axis=0, keepdims=True)
            idx = jnp.min(jnp.where(g == mx, row, nbp), axis=0, keepdims=True)
            hit = row == idx
            sel_t = jnp.where(hit & (row < i), 1.0, sel_t)
            g = jnp.where(hit, -jnp.inf, g)
        if nbp < LANES:
            sel_t = jnp.concatenate([sel_t, jnp.zeros((LANES - nbp, bs), F32)], axis=0)
        w = 2 * LANES
        qa_ref[h, :LANES, :] = (q * Q_SCALE).T.astype(BF16)
        qa_ref[h, LANES:, :] = jnp.where(sel_t > 0.5, 0.0, NEG).astype(BF16)
        ka_ref[:, h * w:h * w + LANES] = k.astype(BF16)
        ka_ref[:, h * w + LANES:(h + 1) * w] = onehot
        v = p_ref[:, (2 * ATT_HEADS + h) * LANES:(2 * ATT_HEADS + h + 1) * LANES].astype(F32)
        vt_ref[h, 0] = v.T.astype(BF16)
        km_ref[h, pl.ds(i, 1), :] = jnp.mean(k, axis=0, keepdims=True)


def _rope_gate(proj, cos_t, sin1_t, sin2_t):
    s = proj.shape[0]
    nb = s // MOBA_BLOCK
    assert nb <= LANES
    nbp = -(-nb // SUBLANES) * SUBLANES
    w_in = 3 * ATT_HEADS * ATT_HEAD_DIM
    w_out = 2 * ATT_HEADS * LANES
    tab = pl.BlockSpec((MOBA_BLOCK, LANES), lambda i: (i, 0))
    return pl.pallas_call(
        _rope_gate_kernel,
        grid=(nb,),
        in_specs=[pl.BlockSpec((MOBA_BLOCK, w_in), lambda i: (i, 0)), tab, tab, tab],
        out_specs=[pl.BlockSpec((ATT_HEADS, 2 * LANES, MOBA_BLOCK), lambda i: (0, 0, i)),
                   pl.BlockSpec((MOBA_BLOCK, w_out), lambda i: (i, 0)),
                   pl.BlockSpec((ATT_HEADS, 1, LANES, MOBA_BLOCK), lambda i: (0, i, 0, 0))],
        out_shape=[jax.ShapeDtypeStruct((ATT_HEADS, 2 * LANES, s), BF16),
                   jax.ShapeDtypeStruct((s, w_out), BF16),
                   jax.ShapeDtypeStruct((ATT_HEADS, nb, LANES, MOBA_BLOCK), BF16)],
        scratch_shapes=[pltpu.VMEM((ATT_HEADS, nbp, LANES), F32)],
        name="rope_gate",
        compiler_params=_params("arbitrary"),
    )(proj, cos_t, sin1_t, sin2_t)


def _attn_kernel(qa_ref, ka_ref, vt_ref, o_ref, s_ref):
    qi = pl.program_id(1)
    bs = MOBA_BLOCK
    grp = ATT_GROUP * bs
    w = 2 * LANES
    heads = range(ATT_HEADS_PER_STEP)

    n_groups = ka_ref.shape[0] // grp
    assert n_groups % 2 == 0

    def issue_scores(g, buf):
        base = pl.multiple_of(jnp.minimum(g, n_groups - 1) * grp, grp)
        for h in heads:
            s_ref[buf, h] = jnp.dot(ka_ref[pl.ds(base, grp), h * w:(h + 1) * w], qa_ref[h],
                                    preferred_element_type=F32)

    issue_scores(0, 0)

    own = pl.multiple_of(qi * bs, bs)
    k_i = lax.broadcasted_iota(I32, (bs, bs), 0)
    q_i = lax.broadcasted_iota(I32, (bs, bs), 1)
    carry = []
    for h in heads:
        s = jnp.dot(ka_ref[pl.ds(own, bs), h * w:h * w + LANES], qa_ref[h, :LANES, :],
                    preferred_element_type=F32)
        s = jnp.where(k_i <= q_i, s, NEG)
        m = jnp.max(s, axis=0, keepdims=True)
        p = jnp.exp2(s - m)
        l = jnp.sum(p, axis=0, keepdims=True)
        acc = jnp.dot(vt_ref[h, qi], p.astype(BF16), preferred_element_type=F32)
        carry += [m, l, acc]

    def absorb(g, buf, carry):
        out = []
        for h in heads:
            m, l, acc = carry[3 * h:3 * h + 3]
            s = s_ref[buf, h]
            mn = jnp.maximum(m, jnp.max(s, axis=0, keepdims=True))
            alpha = jnp.exp2(m - mn)
            p = jnp.exp2(s - mn)
            l = alpha * l + jnp.sum(p, axis=0, keepdims=True)
            p = p.astype(BF16)
            acc = alpha * acc
            for jb in range(ATT_GROUP):
                acc = acc + jnp.dot(vt_ref[h, g * ATT_GROUP + jb], p[jb * bs:(jb + 1) * bs],
                                    preferred_element_type=F32)
            out += [mn, l, acc]
        return out

    def body(t, carry):
        issue_scores(2 * t + 1, 1)
        carry = absorb(2 * t, 0, carry)
        issue_scores(2 * t + 2, 0)
        return tuple(absorb(2 * t + 1, 1, carry))

    n_used = (qi + ATT_GROUP - 1) // ATT_GROUP
    carry = lax.fori_loop(0, (n_used + 1) // 2, body, tuple(carry))
    for h in heads:
        m, l, acc = carry[3 * h:3 * h + 3]
        o_ref[:, h * LANES:(h + 1) * LANES] = (acc / l).T.astype(BF16)


def _attention(q_aug_t, k_aug, v_t):
    s = k_aug.shape[0]
    assert s % (ATT_GROUP * MOBA_BLOCK) == 0
    nb = s // MOBA_BLOCK
    hps = ATT_HEADS_PER_STEP
    once = pl.Buffered(1)
    return pl.pallas_call(
        _attn_kernel,
        grid=(ATT_HEADS // hps, nb),
        in_specs=[pl.BlockSpec((hps, 2 * LANES, MOBA_BLOCK), lambda h, i: (h, 0, i)),
                  pl.BlockSpec((s, hps * 2 * LANES), lambda h, i: (0, h), pipeline_mode=once),
                  pl.BlockSpec((hps, nb, LANES, MOBA_BLOCK), lambda h, i: (h, 0, 0, 0), pipeline_mode=once)],
        out_specs=pl.BlockSpec((MOBA_BLOCK, hps * LANES), lambda h, i: (i, h)),
        out_shape=jax.ShapeDtypeStruct((s, ATT_HEADS * ATT_HEAD_DIM), BF16),
        scratch_shapes=[pltpu.VMEM((2, hps, ATT_GROUP * MOBA_BLOCK, MOBA_BLOCK), F32)],
        name="moba_attention",
        compiler_params=_params("parallel", "arbitrary"),
    )(q_aug_t, k_aug, v_t)


def _rwkv_kernel(r_ref, k_ref, v_ref, lw_ref, la_ref, lg_ref, ww2_ref, wa2_ref, wg2_ref,
                 mur_ref, muk_ref, muv_ref, w0_ref, a0_ref, kk_ref, ka_ref, rk_ref, gnw_ref, gnb_ref,
                 o_ref, st_ref, prev_ref):
    j = pl.program_id(1)
    t = SCAN_CHUNK
    n = RWKV_HEAD_DIM
    ts = r_ref.shape[0]
    chunks = range(ts // t)

    @pl.when(j == 0)
    def _():
        st_ref[...] = jnp.zeros_like(st_ref)
        prev_ref[...] = jnp.zeros_like(prev_ref)

    rp, kp, vp = r_ref[...].astype(F32), k_ref[...].astype(F32), v_ref[...].astype(F32)
    r = rp + (_shift_rows(rp, prev_ref[0:1, :]) - rp) * mur_ref[...]
    k = kp + (_shift_rows(kp, prev_ref[1:2, :]) - kp) * muk_ref[...]
    v = vp + (_shift_rows(vp, prev_ref[2:3, :]) - vp) * muv_ref[...]
    prev_ref[0:1, :] = rp[ts - 1:ts, :]
    prev_ref[1:2, :] = kp[ts - 1:ts, :]
    prev_ref[2:3, :] = vp[ts - 1:ts, :]

    d = w0_ref[...] + _dot(lw_ref[...], ww2_ref[...])
    logw = -math.exp(-0.5) * _sigmoid(d)
    a = _sigmoid(a0_ref[...] + _dot(la_ref[...], wa2_ref[...]))
    g = _dot(lg_ref[...], wg2_ref[...])

    li = lax.broadcasted_iota(I32, (LANES, LANES), 0)
    lj = lax.broadcasted_iota(I32, (LANES, LANES), 1)
    same_head = (li // n) == (lj // n)
    head_sum = jnp.where(same_head, 1.0, 0.0)

    kk = k * kk_ref[...]
    kk = kk / jnp.maximum(jnp.sqrt(_dot_split(kk * kk, head_sum)), 1e-12)
    kt = k * (1.0 + (a - 1.0) * ka_ref[...])
    bonus = _dot_split(r * kt * rk_ref[...], head_sum) * v

    same_blk = (li // t) == (lj // t)
    m_strict = same_blk & (lj < li)
    m_incl = same_blk & (lj <= li)
    ti = lax.broadcasted_iota(I32, (t, t), 0)
    tj = lax.broadcasted_iota(I32, (t, t), 1)
    tril_incl = jnp.where(tj <= ti, 1.0, 0.0)
    lane_a = lax.broadcasted_iota(I32, (t, LANES), 1) < n
    eye = jnp.where(li == lj, 1.0, 0.0)

    def stack_masked(z):
        return jnp.concatenate([jnp.where(lane_a, z, 0.0), jnp.where(lane_a, 0.0, z)], axis=0)

    def stack_plain(z):
        return jnp.concatenate([z, z], axis=0)

    def rows(z, c):
        return z[c * t:(c + 1) * t]

    cum = jnp.concatenate([_dot_split_t(tril_incl, rows(logw, c)) for c in chunks], axis=0)
    g_t = jnp.exp(cum)
    g_inv = jnp.exp(-cum)
    xa_f = -kk * jnp.exp(cum - logw)
    xr_f = r * g_t
    yb_f = kk * a * g_inv
    yk_f = kt * g_inv

    xa = [stack_masked(rows(xa_f, c)) for c in chunks]
    xr = [stack_masked(rows(xr_f, c)) for c in chunks]
    yb = [stack_plain(rows(yb_f, c)) for c in chunks]
    vs = [stack_masked(rows(v, c)) for c in chunks]
    ybk = [jnp.concatenate([yb[c], stack_plain(rows(yk_f, c))], axis=0) for c in chunks]
    sc = [_dot_nt(jnp.concatenate([xa[c], xr[c]], axis=0), ybk[c]) for c in chunks]
    a_ab = [jnp.where(m_strict, sc[c][:2 * t, :2 * t], 0.0) for c in chunks]
    a_ak = [jnp.where(m_strict, sc[c][:2 * t, 2 * t:], 0.0) for c in chunks]
    a_rb = [jnp.where(m_incl, sc[c][2 * t:, :2 * t], 0.0) for c in chunks]
    a_rk = [jnp.where(m_incl, sc[c][2 * t:, 2 * t:], 0.0) for c in chunks]
    inv = [eye + a_ab[c] for c in chunks]
    pw = [_dot(a_ab[c], a_ab[c]) for c in chunks]
    for _ in range(int(math.log2(t)) - 2):
        both = [_dot(pw[c], jnp.concatenate([pw[c], inv[c]], axis=1)) for c in chunks]
        pw = [both[c][:, :LANES] for c in chunks]
        inv = [inv[c] + both[c][:, LANES:] for c in chunks]
    inv = [inv[c] + _dot(pw[c], inv[c]) for c in chunks]
    av = [_dot(a_ak[c], vs[c]) for c in chunks]
    mw = [_dot(inv[c], jnp.concatenate([xa[c], av[c]], axis=1)) for c in chunks]
    zeros = jnp.zeros((2 * t, LANES), F32)
    rw = [_dot(jnp.concatenate([a_rb[c], a_rk[c]], axis=1),
               jnp.concatenate([mw[c], jnp.concatenate([zeros, vs[c]], axis=1)], axis=0)) for c in chunks]
    r_hat = [xr[c] + rw[c][:, :LANES] for c in chunks]
    y_hat = [rw[c][:, LANES:] for c in chunks]
    pm = [jnp.where(same_head, _dot_tn(mw[c][:, :LANES], yb[c]), 0.0) for c in chunks]
    qm = [jnp.where(same_head, _dot_tn(jnp.concatenate([mw[c][:, LANES:], vs[c]], axis=0), ybk[c]), 0.0)
          for c in chunks]

    st = st_ref[...]
    ys = []
    for c in chunks:
        y2 = _dot_nt(r_hat[c], st) + y_hat[c]
        ys.append(y2[:t] + y2[t:])
        st = (st + _dot(st, pm[c]) + qm[c]) * g_t[(c + 1) * t - 1:(c + 1) * t, :]
    st_ref[...] = st
    y = jnp.concatenate(ys, axis=0)

    mean = _dot_split(y, head_sum) * (1.0 / n)
    yc = y - mean
    var = _dot_split(yc * yc, head_sum) * (1.0 / n)
    yn = yc * lax.rsqrt(var + GN_EPS) * gnw_ref[...] + gnb_ref[...]
    o_ref[...] = ((yn + bonus) * g).astype(BF16)


def _rwkv(proj, lw, la, lg, w_w2, w_a2, w_g2, mu_r, mu_k, mu_v, w0, a0, k_k, k_a, r_k, gn_w, gn_b,
          col0, width):
    s = proj.shape[0]
    ts = 512
    npair = width // LANES
    cb = col0 // LANES
    row = lambda a: pl.BlockSpec((ts, a.shape[1]), lambda p, j: (j, 0))
    wcol = lambda a: pl.BlockSpec((a.shape[0], LANES), lambda p, j: (0, p))
    vec = pl.BlockSpec((1, LANES), lambda p, j: (0, p))
    return pl.pallas_call(
        _rwkv_kernel,
        grid=(npair, s // ts),
        in_specs=[pl.BlockSpec((ts, LANES), lambda p, j: (j, cb + p)),
                  pl.BlockSpec((ts, LANES), lambda p, j: (j, cb + npair + p)),
                  pl.BlockSpec((ts, LANES), lambda p, j: (j, cb + 2 * npair + p)),
                  row(lw), row(la), row(lg), wcol(w_w2), wcol(w_a2), wcol(w_g2)] + [vec] * 10,
        out_specs=pl.BlockSpec((ts, LANES), lambda p, j: (j, p)),
        out_shape=jax.ShapeDtypeStruct((s, width), BF16),
        scratch_shapes=[pltpu.VMEM((LANES, LANES), F32), pltpu.VMEM((SUBLANES, LANES), F32)],
        name="rwkv7_scan",
        compiler_params=_params("parallel", "arbitrary"),
    )(proj, proj, proj, lw, la, lg, w_w2, w_a2, w_g2, mu_r, mu_k, mu_v, w0, a0, k_k, k_a, r_k, gn_w, gn_b)


def _merge_kernel(oa_ref, or_ref, ga_ref, gr_ref, wa_ref, wr_ref, o_ref, wab_ref, wrb_ref):
    @pl.when(pl.program_id(1) == 0)
    def _():
        wab_ref[...] = wa_ref[...].astype(BF16)
        wrb_ref[...] = wr_ref[...].astype(BF16)

    ua = jnp.dot(oa_ref[...], wab_ref[...], preferred_element_type=F32)
    ur = jnp.dot(or_ref[...], wrb_ref[...], preferred_element_type=F32)
    mix = _sigmoid(ga_ref[...].astype(F32)) * ua + _sigmoid(gr_ref[...].astype(F32)) * ur
    o_ref[...] = mix.astype(BF16)


def _merge(o_att, o_rwkv, proj, w_up_att, w_up_rwkv, gate_col0):
    s, ka = o_att.shape
    kr = o_rwkv.shape[1]
    d = w_up_att.shape[1]
    tm, tn = 512, 1024
    gb = gate_col0 // tn
    return pl.pallas_call(
        _merge_kernel,
        grid=(d // tn, s // tm),
        in_specs=[pl.BlockSpec((tm, ka), lambda j, i: (i, 0)),
                  pl.BlockSpec((tm, kr), lambda j, i: (i, 0)),
                  pl.BlockSpec((tm, tn), lambda j, i: (i, gb + j)),
                  pl.BlockSpec((tm, tn), lambda j, i: (i, gb + d // tn + j)),
                  pl.BlockSpec((ka, tn), lambda j, i: (0, j)),
                  pl.BlockSpec((kr, tn), lambda j, i: (0, j))],
        out_specs=pl.BlockSpec((tm, tn), lambda j, i: (i, j)),
        out_shape=jax.ShapeDtypeStruct((s, d), BF16),
        scratch_shapes=[pltpu.VMEM((ka, tn), BF16), pltpu.VMEM((kr, tn), BF16)],
        name="gated_merge",
        compiler_params=_params("arbitrary", "arbitrary"),
    )(o_att, o_rwkv, proj, proj, w_up_att, w_up_rwkv)


def _route_kernel(x_ref, y_ref, gt_ref, gpost_ref, gpre_ref, sc_ref, sh_ref, wr_ref, br_ref,
                  x1_ref, h2_ref, ei_ref, wt_ref, rk_ref, cnt_ref, run_ref):
    @pl.when(pl.program_id(0) == 0)
    def _():
        run_ref[...] = jnp.zeros_like(run_ref)

    x1 = x_ref[...] + gt_ref[...] * _rms(y_ref[...], gpost_ref[...])
    x1_ref[...] = x1
    h2 = _rms(x1, gpre_ref[...]) * (1.0 + sc_ref[...]) + sh_ref[...]
    h2_ref[...] = h2
    logits = _dot_hi(h2, wr_ref[...]) + br_ref[...]
    lane = lax.broadcasted_iota(I32, logits.shape, 1)
    big = jnp.int32(4 * LANES)
    gmask = (lane >= N_EXPERTS) & (lane < N_EXPERTS + N_GROUPS)
    mg = jnp.max(jnp.where(gmask, logits, -jnp.inf), axis=1, keepdims=True)
    eg = jnp.where(gmask, jnp.exp(logits - mg), 0.0)
    pg = eg / jnp.sum(eg, axis=1, keepdims=True)
    pg_top = jnp.max(pg, axis=1, keepdims=True)
    g_idx = jnp.min(jnp.where(gmask & (pg == pg_top), lane, big), axis=1, keepdims=True) - N_EXPERTS
    emask = (lane >= g_idx * EXPERTS_PER_GROUP) & (lane < (g_idx + 1) * EXPERTS_PER_GROUP)
    me = jnp.max(jnp.where(emask, logits, -jnp.inf), axis=1, keepdims=True)
    ee = jnp.where(emask, jnp.exp(logits - me), 0.0)
    pe = ee / jnp.sum(ee, axis=1, keepdims=True)
    p1 = jnp.max(pe, axis=1, keepdims=True)
    i1 = jnp.min(jnp.where(emask & (pe == p1), lane, big), axis=1, keepdims=True)
    rest = emask & (lane != i1)
    p2 = jnp.max(jnp.where(rest, pe, -jnp.inf), axis=1, keepdims=True)
    i2 = jnp.min(jnp.where(rest & (pe == p2), lane, big), axis=1, keepdims=True)
    den = p1 + p2
    ei_ref[...] = jnp.where(lane == 0, i1, jnp.where(lane == 1, i2, 0))
    wt_ref[...] = jnp.where(lane == 0, pg_top * p1 / den, jnp.where(lane == 1, pg_top * p2 / den, 0.0))
    tm = logits.shape[0]
    chosen = jnp.where((lane == i1) | (lane == i2), 1.0, 0.0)
    t_i = lax.broadcasted_iota(I32, (tm, tm), 0)
    t_j = lax.broadcasted_iota(I32, (tm, tm), 1)
    before = _dot(jnp.where(t_j < t_i, 1.0, 0.0), chosen) + run_ref[...]
    r1 = jnp.sum(jnp.where(lane == i1, before, 0.0), axis=1, keepdims=True)
    r2 = jnp.sum(jnp.where(lane == i2, before, 0.0), axis=1, keepdims=True)
    rk_ref[...] = jnp.where(lane == 0, r1, jnp.where(lane == 1, r2, 0.0)).astype(I32)
    run_ref[...] += jnp.sum(chosen, axis=0, keepdims=True)
    cnt_ref[...] = run_ref[...].astype(I32)


def _route(x, y, gt1, g_post, g_pre, sc2, sh2, w_router, b_router):
    s, d = x.shape
    tm = 256
    vec = pl.BlockSpec((1, d), lambda i: (0, 0))
    rowblk = pl.BlockSpec((tm, d), lambda i: (i, 0))
    small = pl.BlockSpec((tm, LANES), lambda i: (i, 0))
    lane_row = pl.BlockSpec((1, LANES), lambda i: (0, 0))
    return pl.pallas_call(
        _route_kernel,
        grid=(s // tm,),
        in_specs=[rowblk, rowblk, vec, vec, vec, vec, vec, pl.BlockSpec((d, LANES), lambda i: (0, 0)), lane_row],
        out_specs=[rowblk, rowblk, small, small, small, lane_row],
        out_shape=[jax.ShapeDtypeStruct((s, d), F32), jax.ShapeDtypeStruct((s, d), F32),
                   jax.ShapeDtypeStruct((s, LANES), I32), jax.ShapeDtypeStruct((s, LANES), F32),
                   jax.ShapeDtypeStruct((s, LANES), I32), jax.ShapeDtypeStruct((1, LANES), I32)],
        scratch_shapes=[pltpu.VMEM((1, LANES), F32)],
        name="norm_route",
        compiler_params=_params("arbitrary"),
    )(x, y, gt1, g_post, g_pre, sc2, sh2, w_router, b_router)


def _slots_kernel(ei_ref, rk_ref, ps_ref, pos_ref):
    lane = lax.broadcasted_iota(I32, ei_ref.shape, 1)
    ei, rk = ei_ref[...], rk_ref[...]
    ps = ps_ref[...]
    cols = []
    for kk in range(TOP_K):
        e = ei[:, kk:kk + 1]
        cols.append(jnp.sum(jnp.where(lane == e, ps, 0), axis=1, keepdims=True) + rk[:, kk:kk + 1])
    pos_ref[...] = jnp.where(lane == 0, cols[0], jnp.where(lane == 1, cols[1], 0))


def _slots(e_idx, rank, pstart_row):
    n = e_idx.shape[0]
    tm = 1024
    blk = pl.BlockSpec((tm, LANES), lambda i: (i, 0))
    return pl.pallas_call(
        _slots_kernel,
        grid=(n // tm,),
        in_specs=[blk, blk, pl.BlockSpec((1, LANES), lambda i: (0, 0))],
        out_specs=blk,
        out_shape=jax.ShapeDtypeStruct((n, LANES), I32),
        name="dispatch_slots",
        compiler_params=_params("parallel"),
    )(e_idx, rank, pstart_row)


def _row_copy(src, row, dst, dst_row, sem):
    return pltpu.make_async_copy(src.at[pl.ds(row, 1), :], dst.at[pl.ds(dst_row, 1), :], sem)


def _dispatch_kernel(pos_ref, cnt_ref, pst_ref, nused_ref, h_hbm, x_hbm, zero_ref, sem, zsem):
    n_tok = h_hbm.shape[0]
    n_blk = x_hbm.shape[0] // EXPERT_BLOCK
    nused = nused_ref[0]
    chunk = EXPERT_BLOCK
    zero_ref[...] = jnp.zeros_like(zero_ref)

    def scatter(n, _):
        for kk in range(TOP_K):
            _row_copy(h_hbm, n, x_hbm, pos_ref[n * TOP_K + kk], sem).start()
        return 0

    lax.fori_loop(0, n_tok, scatter, 0, unroll=8)

    def pad_expert(e, total):
        lo = pst_ref[e] + cnt_ref[e]
        hi = pst_ref[e] + (cnt_ref[e] + EXPERT_BLOCK - 1) // EXPERT_BLOCK * EXPERT_BLOCK

        def pad_row(s, _):
            _row_copy(zero_ref, 0, x_hbm, s, zsem).start()
            return 0

        lax.fori_loop(lo, hi, pad_row, 0)
        return total + (hi - lo)

    n_pad = lax.fori_loop(0, cnt_ref.shape[0], pad_expert, 0)

    def tail_copy(blk):
        return pltpu.make_async_copy(zero_ref, x_hbm.at[pl.ds(blk * EXPERT_BLOCK, EXPERT_BLOCK), :], zsem)

    def tail_start(blk, _):
        tail_copy(blk).start()
        return 0

    lax.fori_loop(nused, n_blk, tail_start, 0)

    def drain(i, _):
        pltpu.make_async_copy(x_hbm.at[pl.ds(0, chunk), :], x_hbm.at[pl.ds(0, chunk), :], sem).wait()
        return 0

    lax.fori_loop(0, n_tok * TOP_K // chunk, drain, 0)

    def pad_wait(i, _):
        _row_copy(zero_ref, 0, x_hbm, 0, zsem).wait()
        return 0

    lax.fori_loop(0, n_pad, pad_wait, 0)

    def tail_wait(blk, _):
        tail_copy(blk).wait()
        return 0

    lax.fori_loop(nused, n_blk, tail_wait, 0)


def _dispatch(pos_flat, counts, pstart, nused, h2, n_slots):
    n, d = h2.shape
    assert (n * TOP_K) % EXPERT_BLOCK == 0
    smem = pl.BlockSpec(memory_space=pltpu.SMEM)
    return pl.pallas_call(
        _dispatch_kernel,
        in_specs=[smem, smem, smem, smem, pl.BlockSpec(memory_space=pl.ANY)],
        out_specs=pl.BlockSpec(memory_space=pl.ANY),
        out_shape=jax.ShapeDtypeStruct((n_slots, d), F32),
        scratch_shapes=[pltpu.VMEM((EXPERT_BLOCK, d), F32), pltpu.SemaphoreType.DMA(()),
                        pltpu.SemaphoreType.DMA(())],
        name="dispatch_rows",
    )(pos_flat, counts, pstart, nused, h2)


EXPERT_X_SLOTS = 4
EXPERT_Y_SLOTS = 3


def _expert_kernel(bstart_ref, bcount_ref, nused_ref, x_hbm, wg_ref, wu_ref, wd_ref, y_hbm,
                   xs_ref, yo_ref, wgb_ref, wub_ref, wdb_ref, xsem, osem):
    e = pl.program_id(0)
    nused = nused_ref[0]
    rows = EXPERT_BLOCK
    n_blk = y_hbm.shape[0] // rows
    nx, ny = EXPERT_X_SLOTS, EXPERT_Y_SLOTS
    first = bstart_ref[e]
    count = bcount_ref[e]

    def x_copy(blk):
        src = x_hbm.at[pl.ds(jnp.minimum(blk, n_blk - 1) * rows, rows), :]
        return pltpu.make_async_copy(src, xs_ref.at[blk % nx], xsem.at[blk % nx])

    def out_copy(blk):
        return pltpu.make_async_copy(yo_ref.at[blk % ny], y_hbm.at[pl.ds(blk * rows, rows), :], osem.at[blk % ny])

    @pl.when(e == 0)
    def _():
        for j in range(nx - 1):
            x_copy(j).start()

    @pl.when(count > 0)
    def _():
        wgb_ref[...] = wg_ref[0].astype(BF16)
        wub_ref[...] = wu_ref[0].astype(BF16)
        wdb_ref[...] = wd_ref[0].astype(BF16)

    def block(j, _):
        blk = first + j

        @pl.when(blk >= ny)
        def _():
            out_copy(blk - ny).wait()

        x_copy(blk).wait()
        x_copy(blk + nx - 1).start()
        xb = xs_ref[blk % nx].astype(BF16)
        hg = jnp.dot(xb, wgb_ref[...], preferred_element_type=F32)
        hu = jnp.dot(xb, wub_ref[...], preferred_element_type=F32)
        hid = hg * _sigmoid(hg) * hu
        yo_ref[blk % ny] = jnp.dot(hid.astype(BF16), wdb_ref[...], preferred_element_type=F32)
        out_copy(blk).start()

        @pl.when(blk == nused - 1)
        def _():
            for ahead in range(1, nx):
                x_copy(blk + ahead).wait()

        return 0

    lax.fori_loop(0, count, block, 0)

    @pl.when(e == pl.num_programs(0) - 1)
    def _():
        for back in range(1, ny + 1):
            @pl.when(nused - back >= 0)
            def _():
                out_copy(nused - back).wait()

        yo_ref[0] = jnp.zeros(yo_ref.shape[1:], F32)

        def zero_copy(blk):
            return pltpu.make_async_copy(yo_ref.at[0], y_hbm.at[pl.ds(blk * rows, rows), :], osem.at[0])

        def fill(blk, _):
            zero_copy(blk).start()
            return 0

        def drain(blk, _):
            zero_copy(blk).wait()
            return 0

        lax.fori_loop(nused, n_blk, fill, 0)
        lax.fori_loop(nused, n_blk, drain, 0)


def _experts(x_buf, bstart, bcount, nused, w_gate_e, w_up_e, w_down_e):
    n_slots, d = x_buf.shape
    n_exp, _, f = w_gate_e.shape
    grid_spec = pltpu.PrefetchScalarGridSpec(
        num_scalar_prefetch=3,
        grid=(n_exp,),
        in_specs=[pl.BlockSpec(memory_space=pl.ANY),
                  pl.BlockSpec((1, d, f), lambda e, *_: (e, 0, 0)),
                  pl.BlockSpec((1, d, f), lambda e, *_: (e, 0, 0)),
                  pl.BlockSpec((1, f, d), lambda e, *_: (e, 0, 0))],
        out_specs=pl.BlockSpec(memory_space=pl.ANY),
        scratch_shapes=[pltpu.VMEM((EXPERT_X_SLOTS, EXPERT_BLOCK, d), F32),
                        pltpu.VMEM((EXPERT_Y_SLOTS, EXPERT_BLOCK, d), F32),
                        pltpu.VMEM((d, f), BF16), pltpu.VMEM((d, f), BF16), pltpu.VMEM((f, d), BF16),
                        pltpu.SemaphoreType.DMA((EXPERT_X_SLOTS,)), pltpu.SemaphoreType.DMA((EXPERT_Y_SLOTS,))],
    )
    return pl.pallas_call(
        _expert_kernel,
        grid_spec=grid_spec,
        out_shape=jax.ShapeDtypeStruct((n_slots, d), F32),
        name="experts",
        compiler_params=_params("arbitrary"),
    )(bstart, bcount, nused, x_buf, w_gate_e, w_up_e, w_down_e)


def _combine_kernel(pos_ref, y_hbm, wt_ref, x1_ref, gt_ref, gpost_ref, o_ref, rows_ref, sem):
    i = pl.program_id(0)
    nsteps = pl.num_programs(0)
    tm = x1_ref.shape[0]

    def start_rows(step, slot):
        for r in range(tm):
            for kk in range(TOP_K):
                _row_copy(y_hbm, pos_ref[(step * tm + r) * TOP_K + kk], rows_ref.at[slot, kk], r,
                          sem.at[slot]).start()

    def wait_rows(slot):
        for kk in range(TOP_K):
            pltpu.make_async_copy(y_hbm.at[pl.ds(0, tm), :], rows_ref.at[slot, kk], sem.at[slot]).wait()

    @pl.when(i == 0)
    def _():
        start_rows(0, 0)

    slot = i % 2
    wait_rows(slot)
    start_rows(jnp.minimum(i + 1, nsteps - 1), 1 - slot)
    wt = wt_ref[...]
    y = rows_ref[slot, 0] * wt[:, 0:1] + rows_ref[slot, 1] * wt[:, 1:2]
    o_ref[...] = x1_ref[...] + gt_ref[...] * _rms(y, gpost_ref[...])

    @pl.when(i == nsteps - 1)
    def _():
        wait_rows(1 - slot)


def _combine(pos, y_buf, wts, x1, gt2, g_post):
    n, d = x1.shape
    tm = 128
    vec = pl.BlockSpec((1, d), lambda i, p: (0, 0))
    grid_spec = pltpu.PrefetchScalarGridSpec(
        num_scalar_prefetch=1,
        grid=(n // tm,),
        in_specs=[pl.BlockSpec(memory_space=pl.ANY),
                  pl.BlockSpec((tm, LANES), lambda i, p: (i, 0)),
                  pl.BlockSpec((tm, d), lambda i, p: (i, 0)), vec, vec],
        out_specs=pl.BlockSpec((tm, d), lambda i, p: (i, 0)),
        scratch_shapes=[pltpu.VMEM((2, TOP_K, tm, d), F32), pltpu.SemaphoreType.DMA((2,))],
    )
    return pl.pallas_call(
        _combine_kernel,
        grid_spec=grid_spec,
        out_shape=jax.ShapeDtypeStruct((n, d), F32),
        name="combine",
        compiler_params=_params("arbitrary"),
    )(pos, y_buf, wts, x1, gt2, g_post)


def _segment_tables(counts_row):
    counts = counts_row[0, :N_EXPERTS]
    pcounts = (counts + EXPERT_BLOCK - 1) // EXPERT_BLOCK * EXPERT_BLOCK
    pend = jnp.cumsum(pcounts)
    pstart = pend - pcounts
    nused = (pend[-1] // EXPERT_BLOCK).astype(I32)
    pstart_row = jnp.pad(pstart, (0, LANES - N_EXPERTS)).reshape(1, LANES)
    return counts, pstart, pstart_row, pstart // EXPERT_BLOCK, pcounts // EXPERT_BLOCK, nused.reshape(1)


def _rope_tables(s):
    half = ROPE_DIM // 2
    inv = ROPE_THETA ** (-jnp.arange(half, dtype=F32) / half)
    ang = jnp.arange(s, dtype=F32)[:, None] * inv[None, :]
    cos, sin = lax.optimization_barrier((jnp.cos(ang), jnp.sin(ang)))
    pad = jnp.zeros((s, LANES - ROPE_DIM), F32)
    zero = jnp.zeros((s, half), F32)
    cos_t = jnp.concatenate([cos, cos, pad + 1.0], axis=1)
    sin1_t = jnp.concatenate([-sin, zero, pad], axis=1)
    sin2_t = jnp.concatenate([zero, sin, pad], axis=1)
    return cos_t, sin1_t, sin2_t


def _layer(x, c_col, w_ada, b_ada, g_pre_mix, g_post_mix, g_pre_ffn, g_post_ffn, w_in, mu_r, mu_k, mu_v,
           mu_w, mu_a, mu_g, w0, w_w1, w_w2, a0, w_a1, w_a2, w_g1, w_g2, k_k, k_a, r_k, gn_w, gn_b,
           w_up_att, w_up_rwkv, w_o, w_rg, b_rg, w_re, b_re, w_gate_e, w_up_e, w_down_e):
    s, d = x.shape
    att_w = ATT_HEADS * ATT_HEAD_DIM
    rwkv_w = w_up_rwkv.shape[0]
    row = lambda a: a.reshape(1, -1)

    ada = _ada(c_col, w_ada, row(b_ada))
    sh1, sc1, gt1, sh2, sc2, gt2 = (ada[:, i * d:(i + 1) * d] for i in range(6))

    h, lw, la, lg = _prenorm(x, row(g_pre_mix), sc1, sh1, row(mu_w), row(mu_a), row(mu_g), w_w1, w_a1, w_g1)
    proj = _matmul(h, w_in, BF16)

    q_aug_t, k_aug, v_t = _rope_gate(proj, *_rope_tables(s))
    o_att = _attention(q_aug_t, k_aug, v_t)

    o_rwkv = _rwkv(proj, lw, la, lg, w_w2, w_a2, w_g2, row(mu_r), row(mu_k), row(mu_v), row(w0), row(a0),
                   row(k_k), row(k_a), row(r_k), row(gn_w), row(gn_b), col0=3 * att_w, width=rwkv_w)

    mix = _merge(o_att, o_rwkv, proj, w_up_att, w_up_rwkv, gate_col0=3 * att_w + 3 * rwkv_w)
    y = _matmul(mix, w_o, F32)

    w_router = jnp.pad(jnp.concatenate([w_re, w_rg], axis=1), ((0, 0), (0, LANES - N_EXPERTS - N_GROUPS)))
    b_router = jnp.pad(jnp.concatenate([b_re, b_rg]), (0, LANES - N_EXPERTS - N_GROUPS)).reshape(1, LANES)
    x1, h2, e_idx, wts, rank, counts = _route(x, y, gt1, row(g_post_mix), row(g_pre_ffn), sc2, sh2,
                                              w_router, b_router)

    n_pairs = s * TOP_K
    n_blk = (n_pairs + N_EXPERTS * (EXPERT_BLOCK - 1) + EXPERT_BLOCK - 1) // EXPERT_BLOCK
    counts, pstart, pstart_row, bstart, bcount, nused = _segment_tables(counts)
    pos = _slots(e_idx, rank, pstart_row)[:, :TOP_K].reshape(n_pairs)
    x_buf = _dispatch(pos, counts, pstart, nused, h2, n_blk * EXPERT_BLOCK)
    y_buf = _experts(x_buf, bstart, bcount, nused, w_gate_e, w_up_e, w_down_e)
    return _combine(pos, y_buf, wts, x1, gt2, row(g_post_ffn))


def kernel(x, c, w_ada, b_ada, g_pre_mix, g_post_mix, g_pre_ffn, g_post_ffn, w_in, mu_r, mu_k, mu_v, mu_w, mu_a, mu_g, w0, w_w1, w_w2, a0, w_a1, w_a2, w_g1, w_g2, k_k, k_a, r_k, gn_w, gn_b, w_up_att, w_up_rwkv, w_o, w_rg, b_rg, w_re, b_re, w_gate_e, w_up_e, w_down_e):
    b, s, d = x.shape
    assert b == 1, "one sequence per call"
    params = (w_ada, b_ada, g_pre_mix, g_post_mix, g_pre_ffn, g_post_ffn, w_in, mu_r, mu_k, mu_v, mu_w, mu_a,
              mu_g, w0, w_w1, w_w2, a0, w_a1, w_a2, w_g1, w_g2, k_k, k_a, r_k, gn_w, gn_b, w_up_att,
              w_up_rwkv, w_o, w_rg, b_rg, w_re, b_re, w_gate_e, w_up_e, w_down_e)
    xs = x.reshape(s, d)
    c_col = c.reshape(d, 1)
    for l in range(w_ada.shape[0]):
        xs = _layer(xs, c_col, *(p[l] for p in params))
    return xs.reshape(b, s, d)
```

```python
import math

import jax
import jax.numpy as jnp
from jax import lax
from jax.experimental import pallas as pl
from jax.experimental.pallas import tpu as pltpu

F32 = jnp.float32
BF16 = jnp.bfloat16
I32 = jnp.int32
HI = lax.Precision.HIGHEST

LANES = 128
SUBLANES = 8
VMEM_LIMIT = 56 * 1024 * 1024

ATT_HEADS = 8
ATT_HEAD_DIM = 128
MOBA_BLOCK = 256
MOBA_TOPK = 3
ATT_GROUP = 4
ATT_HEADS_PER_STEP = 2
ROPE_THETA = 500000.0
ROPE_DIM = ATT_HEAD_DIM // 4
RWKV_HEAD_DIM = 64
GN_EPS = 64e-5
N_GROUPS = 8
EXPERTS_PER_GROUP = 8
N_EXPERTS = N_GROUPS * EXPERTS_PER_GROUP
TOP_K = 2
EXPERT_BLOCK = 128
RMS_EPS = 1e-6
NEG = -1e30
SCAN_CHUNK = 64
Q_SCALE = ATT_HEAD_DIM ** -0.5 * math.log2(math.e)


def _params(*sem):
    return pltpu.CompilerParams(dimension_semantics=sem, vmem_limit_bytes=VMEM_LIMIT)


def _rms(z, g):
    return z * lax.rsqrt(jnp.mean(z * z, axis=-1, keepdims=True) + RMS_EPS) * g


def _sigmoid(z):
    return 1.0 / (1.0 + jnp.exp(-z))


def _dot(a, b):
    return jnp.dot(a.astype(BF16), b.astype(BF16), preferred_element_type=F32)


def _dot_nt(a, b):
    return lax.dot_general(a.astype(BF16), b.astype(BF16), (((1,), (1,)), ((), ())),
                           preferred_element_type=F32)


def _dot_tn(a, b):
    return lax.dot_general(a.astype(BF16), b.astype(BF16), (((0,), (0,)), ((), ())),
                           preferred_element_type=F32)


def _dot_hi(a, b):
    return jnp.dot(a, b, precision=HI, preferred_element_type=F32)


def _split3(a):
    hi = a.astype(BF16)
    r1 = a - hi.astype(F32)
    mid = r1.astype(BF16)
    lo = (r1 - mid.astype(F32)).astype(BF16)
    return hi, mid, lo


def _dot_split(a, b01):
    b = b01.astype(BF16)
    hi, mid, _ = _split3(a)
    return jnp.dot(jnp.concatenate([hi, mid], axis=1), jnp.concatenate([b, b], axis=0),
                   preferred_element_type=F32)


def _dot_split_t(b01, a):
    b = b01.astype(BF16)
    return jnp.dot(jnp.concatenate([b, b, b], axis=1), jnp.concatenate(_split3(a), axis=0),
                   preferred_element_type=F32)


def _shift_rows(z, prev_row):
    rolled = pltpu.roll(z, 1, 0)
    row = lax.broadcasted_iota(I32, z.shape, 0)
    return jnp.where(row == 0, prev_row, rolled)


def _ada_kernel(c_ref, w_ref, b_ref, o_ref):
    o_ref[...] = jnp.sum(c_ref[...] * w_ref[...], axis=0, keepdims=True) + b_ref[...]


def _ada(c_col, w_ada, b_ada):
    d, n = w_ada.shape
    tn = 1024
    return pl.pallas_call(
        _ada_kernel,
        grid=(n // tn,),
        in_specs=[pl.BlockSpec((d, 1), lambda j: (0, 0)),
                  pl.BlockSpec((d, tn), lambda j: (0, j)),
                  pl.BlockSpec((1, tn), lambda j: (0, j))],
        out_specs=pl.BlockSpec((1, tn), lambda j: (0, j)),
        out_shape=jax.ShapeDtypeStruct((1, n), F32),
        name="ada",
        compiler_params=_params("parallel"),
    )(c_col, w_ada, b_ada)


def _prenorm_kernel(x_ref, xp_ref, g_ref, sc_ref, sh_ref, muw_ref, mua_ref, mug_ref,
                    ww1_ref, wa1_ref, wg1_ref, h_ref, lw_ref, la_ref, lg_ref):
    i = pl.program_id(0)
    g, sc, sh = g_ref[...], sc_ref[...], sh_ref[...]
    h = _rms(x_ref[...], g) * (1.0 + sc) + sh
    hp = _rms(xp_ref[SUBLANES - 1:SUBLANES, :], g) * (1.0 + sc) + sh
    hp = jnp.where(i == 0, 0.0, hp)
    dh = _shift_rows(h, hp) - h
    h_ref[...] = h.astype(BF16)
    lw_ref[...] = jnp.tanh(_dot(h + dh * muw_ref[...], ww1_ref[...]))
    la_ref[...] = _dot(h + dh * mua_ref[...], wa1_ref[...])
    lg_ref[...] = _sigmoid(_dot(h + dh * mug_ref[...], wg1_ref[...]))


def _prenorm(x, g, sc, sh, mu_w, mu_a, mu_g, w_w1, w_a1, w_g1):
    s, d = x.shape
    tm = 256
    rpb = tm // SUBLANES
    vec = pl.BlockSpec((1, d), lambda i: (0, 0))
    full = lambda a: pl.BlockSpec(a.shape, lambda i: (0, 0))
    lw, la, lg = w_w1.shape[1], w_a1.shape[1], w_g1.shape[1]
    return pl.pallas_call(
        _prenorm_kernel,
        grid=(s // tm,),
        in_specs=[pl.BlockSpec((tm, d), lambda i: (i, 0)),
                  pl.BlockSpec((SUBLANES, d), lambda i: (jnp.maximum(i * rpb - 1, 0), 0)),
                  vec, vec, vec, vec, vec, vec, full(w_w1), full(w_a1), full(w_g1)],
        out_specs=[pl.BlockSpec((tm, d), lambda i: (i, 0)),
                   pl.BlockSpec((tm, lw), lambda i: (i, 0)),
                   pl.BlockSpec((tm, la), lambda i: (i, 0)),
                   pl.BlockSpec((tm, lg), lambda i: (i, 0))],
        out_shape=[jax.ShapeDtypeStruct((s, d), BF16),
                   jax.ShapeDtypeStruct((s, lw), F32),
                   jax.ShapeDtypeStruct((s, la), F32),
                   jax.ShapeDtypeStruct((s, lg), F32)],
        name="prenorm_lora",
        compiler_params=_params("parallel"),
    )(x, x, g, sc, sh, mu_w, mu_a, mu_g, w_w1, w_a1, w_g1)


def _mm_kernel(a_ref, w_ref, o_ref, wb_ref):
    @pl.when(pl.program_id(1) == 0)
    def _():
        wb_ref[...] = w_ref[...].astype(BF16)

    o_ref[...] = jnp.dot(a_ref[...], wb_ref[...], preferred_element_type=F32).astype(o_ref.dtype)


def _matmul(a, w, out_dtype, tm=512, tn=1024):
    m, k = a.shape
    n = w.shape[1]
    tn = min(tn, n)
    return pl.pallas_call(
        _mm_kernel,
        grid=(n // tn, m // tm),
        in_specs=[pl.BlockSpec((tm, k), lambda j, i: (i, 0)),
                  pl.BlockSpec((k, tn), lambda j, i: (0, j))],
        out_specs=pl.BlockSpec((tm, tn), lambda j, i: (i, j)),
        out_shape=jax.ShapeDtypeStruct((m, n), out_dtype),
        scratch_shapes=[pltpu.VMEM((k, tn), BF16)],
        name="matmul",
        compiler_params=_params("arbitrary", "arbitrary"),
    )(a, w)


def _rope_gate_kernel(p_ref, c_ref, s1_ref, s2_ref, qa_ref, ka_ref, vt_ref, km_ref):
    i = pl.program_id(0)
    bs = MOBA_BLOCK
    nbp = km_ref.shape[1]

    @pl.when(i == 0)
    def _():
        km_ref[...] = jnp.zeros_like(km_ref)

    c, s1, s2 = c_ref[...], s1_ref[...], s2_ref[...]

    def rope(z):
        return z * c + pltpu.roll(z, LANES - ROPE_DIM // 2, 1) * s1 + pltpu.roll(z, ROPE_DIM // 2, 1) * s2

    row = lax.broadcasted_iota(I32, (nbp, bs), 0)
    lane = lax.broadcasted_iota(I32, (bs, LANES), 1)
    onehot = jnp.where(lane == i, 1.0, 0.0).astype(BF16)
    for h in range(ATT_HEADS):
        q = rope(p_ref[:, h * LANES:(h + 1) * LANES].astype(F32))
        k = rope(p_ref[:, (ATT_HEADS + h) * LANES:(ATT_HEADS + h + 1) * LANES].astype(F32))
        g = lax.dot_general(km_ref[h], q, (((1,), (1,)), ((), ())), precision=HI, preferred_element_type=F32)
        g = jnp.where(row < i, g, NEG)
        sel_t = jnp.zeros(g.shape, F32)
        for _ in range(MOBA_TOPK):
            mx = jnp.max(g, axis=0, keepdims=True)
            idx = jnp.min(jnp.where(g == mx, row, nbp), axis=0, keepdims=True)
            hit = row == idx
            sel_t = jnp.where(hit & (row < i), 1.0, sel_t)
            g = jnp.where(hit, -jnp.inf, g)
        if nbp < LANES:
            sel_t = jnp.concatenate([sel_t, jnp.zeros((LANES - nbp, bs), F32)], axis=0)
        w = 2 * LANES
        qa_ref[h, :LANES, :] = (q * Q_SCALE).T.astype(BF16)
        qa_ref[h, LANES:, :] = jnp.where(sel_t > 0.5, 0.0, NEG).astype(BF16)
        ka_ref[:, h * w:h * w + LANES] = k.astype(BF16)
        ka_ref[:, h * w + LANES:(h + 1) * w] = onehot
        v = p_ref[:, (2 * ATT_HEADS + h) * LANES:(2 * ATT_HEADS + h + 1) * LANES].astype(F32)
        vt_ref[h, 0] = v.T.astype(BF16)
        km_ref[h, pl.ds(i, 1), :] = jnp.mean(k, axis=0, keepdims=True)


def _rope_gate(proj, cos_t, sin1_t, sin2_t):
    s = proj.shape[0]
    nb = s // MOBA_BLOCK
    assert nb <= LANES
    nbp = -(-nb // SUBLANES) * SUBLANES
    w_in = 3 * ATT_HEADS * ATT_HEAD_DIM
    w_out = 2 * ATT_HEADS * LANES
    tab = pl.BlockSpec((MOBA_BLOCK, LANES), lambda i: (i, 0))
    return pl.pallas_call(
        _rope_gate_kernel,
        grid=(nb,),
        in_specs=[pl.BlockSpec((MOBA_BLOCK, w_in), lambda i: (i, 0)), tab, tab, tab],
        out_specs=[pl.BlockSpec((ATT_HEADS, 2 * LANES, MOBA_BLOCK), lambda i: (0, 0, i)),
                   pl.BlockSpec((MOBA_BLOCK, w_out), lambda i: (i, 0)),
                   pl.BlockSpec((ATT_HEADS, 1, LANES, MOBA_BLOCK), lambda i: (0, i, 0, 0))],
        out_shape=[jax.ShapeDtypeStruct((ATT_HEADS, 2 * LANES, s), BF16),
                   jax.ShapeDtypeStruct((s, w_out), BF16),
                   jax.ShapeDtypeStruct((ATT_HEADS, nb, LANES, MOBA_BLOCK), BF16)],
        scratch_shapes=[pltpu.VMEM((ATT_HEADS, nbp, LANES), F32)],
        name="rope_gate",
        compiler_params=_params("arbitrary"),
    )(proj, cos_t, sin1_t, sin2_t)


def _attn_kernel(qa_ref, ka_ref, vt_ref, o_ref, s_ref):
    qi = pl.program_id(1)
    bs = MOBA_BLOCK
    grp = ATT_GROUP * bs
    w = 2 * LANES
    heads = range(ATT_HEADS_PER_STEP)

    n_groups = ka_ref.shape[0] // grp
    assert n_groups % 2 == 0

    def issue_scores(g, buf):
        base = pl.multiple_of(jnp.minimum(g, n_groups - 1) * grp, grp)
        for h in heads:
            s_ref[buf, h] = jnp.dot(ka_ref[pl.ds(base, grp), h * w:(h + 1) * w], qa_ref[h],
                                    preferred_element_type=F32)

    issue_scores(0, 0)

    own = pl.multiple_of(qi * bs, bs)
    k_i = lax.broadcasted_iota(I32, (bs, bs), 0)
    q_i = lax.broadcasted_iota(I32, (bs, bs), 1)
    carry = []
    for h in heads:
        s = jnp.dot(ka_ref[pl.ds(own, bs), h * w:h * w + LANES], qa_ref[h, :LANES, :],
                    preferred_element_type=F32)
        s = jnp.where(k_i <= q_i, s, NEG)
        m = jnp.max(s, axis=0, keepdims=True)
        p = jnp.exp2(s - m)
        l = jnp.sum(p, axis=0, keepdims=True)
        acc = jnp.dot(vt_ref[h, qi], p.astype(BF16), preferred_element_type=F32)
        carry += [m, l, acc]

    def absorb(g, buf, carry):
        out = []
        for h in heads:
            m, l, acc = carry[3 * h:3 * h + 3]
            s = s_ref[buf, h]
            mn = jnp.maximum(m, jnp.max(s, axis=0, keepdims=True))
            alpha = jnp.exp2(m - mn)
            p = jnp.exp2(s - mn)
            l = alpha * l + jnp.sum(p, axis=0, keepdims=True)
            p = p.astype(BF16)
            acc = alpha * acc
            for jb in range(ATT_GROUP):
                acc = acc + jnp.dot(vt_ref[h, g * ATT_GROUP + jb], p[jb * bs:(jb + 1) * bs],
                                    preferred_element_type=F32)
            out += [mn, l, acc]
        return out

    def body(t, carry):
        issue_scores(2 * t + 1, 1)
        carry = absorb(2 * t, 0, carry)
        issue_scores(2 * t + 2, 0)
        return tuple(absorb(2 * t + 1, 1, carry))

    n_used = (qi + ATT_GROUP - 1) // ATT_GROUP
    carry = lax.fori_loop(0, (n_used + 1) // 2, body, tuple(carry))
    for h in heads:
        m, l, acc = carry[3 * h:3 * h + 3]
        o_ref[:, h * LANES:(h + 1) * LANES] = (acc / l).T.astype(BF16)


def _attention(q_aug_t, k_aug, v_t):
    s = k_aug.shape[0]
    assert s % (ATT_GROUP * MOBA_BLOCK) == 0
    nb = s // MOBA_BLOCK
    hps = ATT_HEADS_PER_STEP
    once = pl.Buffered(1)
    return pl.pallas_call(
        _attn_kernel,
        grid=(ATT_HEADS // hps, nb),
        in_specs=[pl.BlockSpec((hps, 2 * LANES, MOBA_BLOCK), lambda h, i: (h, 0, i)),
                  pl.BlockSpec((s, hps * 2 * LANES), lambda h, i: (0, h), pipeline_mode=once),
                  pl.BlockSpec((hps, nb, LANES, MOBA_BLOCK), lambda h, i: (h, 0, 0, 0), pipeline_mode=once)],
        out_specs=pl.BlockSpec((MOBA_BLOCK, hps * LANES), lambda h, i: (i, h)),
        out_shape=jax.ShapeDtypeStruct((s, ATT_HEADS * ATT_HEAD_DIM), BF16),
        scratch_shapes=[pltpu.VMEM((2, hps, ATT_GROUP * MOBA_BLOCK, MOBA_BLOCK), F32)],
        name="moba_attention",
        compiler_params=_params("parallel", "arbitrary"),
    )(q_aug_t, k_aug, v_t)


def _rwkv_kernel(r_ref, k_ref, v_ref, lw_ref, la_ref, lg_ref, ww2_ref, wa2_ref, wg2_ref,
                 mur_ref, muk_ref, muv_ref, w0_ref, a0_ref, kk_ref, ka_ref, rk_ref, gnw_ref, gnb_ref,
                 o_ref, st_ref, prev_ref):
    j = pl.program_id(1)
    t = SCAN_CHUNK
    n = RWKV_HEAD_DIM
    ts = r_ref.shape[0]
    chunks = range(ts // t)

    @pl.when(j == 0)
    def _():
        st_ref[...] = jnp.zeros_like(st_ref)
        prev_ref[...] = jnp.zeros_like(prev_ref)

    rp, kp, vp = r_ref[...].astype(F32), k_ref[...].astype(F32), v_ref[...].astype(F32)
    r = rp + (_shift_rows(rp, prev_ref[0:1, :]) - rp) * mur_ref[...]
    k = kp + (_shift_rows(kp, prev_ref[1:2, :]) - kp) * muk_ref[...]
    v = vp + (_shift_rows(vp, prev_ref[2:3, :]) - vp) * muv_ref[...]
    prev_ref[0:1, :] = rp[ts - 1:ts, :]
    prev_ref[1:2, :] = kp[ts - 1:ts, :]
    prev_ref[2:3, :] = vp[ts - 1:ts, :]

    d = w0_ref[...] + _dot(lw_ref[...], ww2_ref[...])
    logw = -math.exp(-0.5) * _sigmoid(d)
    a = _sigmoid(a0_ref[...] + _dot(la_ref[...], wa2_ref[...]))
    g = _dot(lg_ref[...], wg2_ref[...])

    li = lax.broadcasted_iota(I32, (LANES, LANES), 0)
    lj = lax.broadcasted_iota(I32, (LANES, LANES), 1)
    same_head = (li // n) == (lj // n)
    head_sum = jnp.where(same_head, 1.0, 0.0)

    kk = k * kk_ref[...]
    kk = kk / jnp.maximum(jnp.sqrt(_dot_split(kk * kk, head_sum)), 1e-12)
    kt = k * (1.0 + (a - 1.0) * ka_ref[...])
    bonus = _dot_split(r * kt * rk_ref[...], head_sum) * v

    same_blk = (li // t) == (lj // t)
    m_strict = same_blk & (lj < li)
    m_incl = same_blk & (lj <= li)
    ti = lax.broadcasted_iota(I32, (t, t), 0)
    tj = lax.broadcasted_iota(I32, (t, t), 1)
    tril_incl = jnp.where(tj <= ti, 1.0, 0.0)
    lane_a = lax.broadcasted_iota(I32, (t, LANES), 1) < n
    eye = jnp.where(li == lj, 1.0, 0.0)

    def stack_masked(z):
        return jnp.concatenate([jnp.where(lane_a, z, 0.0), jnp.where(lane_a, 0.0, z)], axis=0)

    def stack_plain(z):
        return jnp.concatenate([z, z], axis=0)

    def rows(z, c):
        return z[c * t:(c + 1) * t]

    cum = jnp.concatenate([_dot_split_t(tril_incl, rows(logw, c)) for c in chunks], axis=0)
    g_t = jnp.exp(cum)
    g_inv = jnp.exp(-cum)
    xa_f = -kk * jnp.exp(cum - logw)
    xr_f = r * g_t
    yb_f = kk * a * g_inv
    yk_f = kt * g_inv

    xa = [stack_masked(rows(xa_f, c)) for c in chunks]
    xr = [stack_masked(rows(xr_f, c)) for c in chunks]
    yb = [stack_plain(rows(yb_f, c)) for c in chunks]
    vs = [stack_masked(rows(v, c)) for c in chunks]
    ybk = [jnp.concatenate([yb[c], stack_plain(rows(yk_f, c))], axis=0) for c in chunks]
    sc = [_dot_nt(jnp.concatenate([xa[c], xr[c]], axis=0), ybk[c]) for c in chunks]
    a_ab = [jnp.where(m_strict, sc[c][:2 * t, :2 * t], 0.0) for c in chunks]
    a_ak = [jnp.where(m_strict, sc[c][:2 * t, 2 * t:], 0.0) for c in chunks]
    a_rb = [jnp.where(m_incl, sc[c][2 * t:, :2 * t], 0.0) for c in chunks]
    a_rk = [jnp.where(m_incl, sc[c][2 * t:, 2 * t:], 0.0) for c in chunks]
    inv = [eye + a_ab[c] for c in chunks]
    pw = [_dot(a_ab[c], a_ab[c]) for c in chunks]
    for _ in range(int(math.log2(t)) - 2):
        both = [_dot(pw[c], jnp.concatenate([pw[c], inv[c]], axis=1)) for c in chunks]
        pw = [both[c][:, :LANES] for c in chunks]
        inv = [inv[c] + both[c][:, LANES:] for c in chunks]
    inv = [inv[c] + _dot(pw[c], inv[c]) for c in chunks]
    av = [_dot(a_ak[c], vs[c]) for c in chunks]
    mw = [_dot(inv[c], jnp.concatenate([xa[c], av[c]], axis=1)) for c in chunks]
    zeros = jnp.zeros((2 * t, LANES), F32)
    rw = [_dot(jnp.concatenate([a_rb[c], a_rk[c]], axis=1),
               jnp.concatenate([mw[c], jnp.concatenate([zeros, vs[c]], axis=1)], axis=0)) for c in chunks]
    r_hat = [xr[c] + rw[c][:, :LANES] for c in chunks]
    y_hat = [rw[c][:, LANES:] for c in chunks]
    pm = [jnp.where(same_head, _dot_tn(mw[c][:, :LANES], yb[c]), 0.0) for c in chunks]
    qm = [jnp.where(same_head, _dot_tn(jnp.concatenate([mw[c][:, LANES:], vs[c]], axis=0), ybk[c]), 0.0)
          for c in chunks]

    st = st_ref[...]
    ys = []
    for c in chunks:
        y2 = _dot_nt(r_hat[c], st) + y_hat[c]
        ys.append(y2[:t] + y2[t:])
        st = (st + _dot(st, pm[c]) + qm[c]) * g_t[(c + 1) * t - 1:(c + 1) * t, :]
    st_ref[...] = st
    y = jnp.concatenate(ys, axis=0)

    mean = _dot_split(y, head_sum) * (1.0 / n)
    yc = y - mean
    var = _dot_split(yc * yc, head_sum) * (1.0 / n)
    yn = yc * lax.rsqrt(var + GN_EPS) * gnw_ref[...] + gnb_ref[...]
    o_ref[...] = ((yn + bonus) * g).astype(BF16)


def _rwkv(proj, lw, la, lg, w_w2, w_a2, w_g2, mu_r, mu_k, mu_v, w0, a0, k_k, k_a, r_k, gn_w, gn_b,
          col0, width):
    s = proj.shape[0]
    ts = 512
    npair = width // LANES
    cb = col0 // LANES
    row = lambda a: pl.BlockSpec((ts, a.shape[1]), lambda p, j: (j, 0))
    wcol = lambda a: pl.BlockSpec((a.shape[0], LANES), lambda p, j: (0, p))
    vec = pl.BlockSpec((1, LANES), lambda p, j: (0, p))
    return pl.pallas_call(
        _rwkv_kernel,
        grid=(npair, s // ts),
        in_specs=[pl.BlockSpec((ts, LANES), lambda p, j: (j, cb + p)),
                  pl.BlockSpec((ts, LANES), lambda p, j: (j, cb + npair + p)),
                  pl.BlockSpec((ts, LANES), lambda p, j: (j, cb + 2 * npair + p)),
                  row(lw), row(la), row(lg), wcol(w_w2), wcol(w_a2), wcol(w_g2)] + [vec] * 10,
        out_specs=pl.BlockSpec((ts, LANES), lambda p, j: (j, p)),
        out_shape=jax.ShapeDtypeStruct((s, width), BF16),
        scratch_shapes=[pltpu.VMEM((LANES, LANES), F32), pltpu.VMEM((SUBLANES, LANES), F32)],
        name="rwkv7_scan",
        compiler_params=_params("parallel", "arbitrary"),
    )(proj, proj, proj, lw, la, lg, w_w2, w_a2, w_g2, mu_r, mu_k, mu_v, w0, a0, k_k, k_a, r_k, gn_w, gn_b)


def _merge_kernel(oa_ref, or_ref, ga_ref, gr_ref, wa_ref, wr_ref, o_ref, wab_ref, wrb_ref):
    @pl.when(pl.program_id(1) == 0)
    def _():
        wab_ref[...] = wa_ref[...].astype(BF16)
        wrb_ref[...] = wr_ref[...].astype(BF16)

    ua = jnp.dot(oa_ref[...], wab_ref[...], preferred_element_type=F32)
    ur = jnp.dot(or_ref[...], wrb_ref[...], preferred_element_type=F32)
    mix = _sigmoid(ga_ref[...].astype(F32)) * ua + _sigmoid(gr_ref[...].astype(F32)) * ur
    o_ref[...] = mix.astype(BF16)


def _merge(o_att, o_rwkv, proj, w_up_att, w_up_rwkv, gate_col0):
    s, ka = o_att.shape
    kr = o_rwkv.shape[1]
    d = w_up_att.shape[1]
    tm, tn = 512, 1024
    gb = gate_col0 // tn
    return pl.pallas_call(
        _merge_kernel,
        grid=(d // tn, s // tm),
        in_specs=[pl.BlockSpec((tm, ka), lambda j, i: (i, 0)),
                  pl.BlockSpec((tm, kr), lambda j, i: (i, 0)),
                  pl.BlockSpec((tm, tn), lambda j, i: (i, gb + j)),
                  pl.BlockSpec((tm, tn), lambda j, i: (i, gb + d // tn + j)),
                  pl.BlockSpec((ka, tn), lambda j, i: (0, j)),
                  pl.BlockSpec((kr, tn), lambda j, i: (0, j))],
        out_specs=pl.BlockSpec((tm, tn), lambda j, i: (i, j)),
        out_shape=jax.ShapeDtypeStruct((s, d), BF16),
        scratch_shapes=[pltpu.VMEM((ka, tn), BF16), pltpu.VMEM((kr, tn), BF16)],
        name="gated_merge",
        compiler_params=_params("arbitrary", "arbitrary"),
    )(o_att, o_rwkv, proj, proj, w_up_att, w_up_rwkv)


def _route_kernel(x_ref, y_ref, gt_ref, gpost_ref, gpre_ref, sc_ref, sh_ref, wr_ref, br_ref,
                  x1_ref, h2_ref, ei_ref, wt_ref, rk_ref, cnt_ref, run_ref):
    @pl.when(pl.program_id(0) == 0)
    def _():
        run_ref[...] = jnp.zeros_like(run_ref)

    x1 = x_ref[...] + gt_ref[...] * _rms(y_ref[...], gpost_ref[...])
    x1_ref[...] = x1
    h2 = _rms(x1, gpre_ref[...]) * (1.0 + sc_ref[...]) + sh_ref[...]
    h2_ref[...] = h2
    logits = _dot_hi(h2, wr_ref[...]) + br_ref[...]
    lane = lax.broadcasted_iota(I32, logits.shape, 1)
    big = jnp.int32(4 * LANES)
    gmask = (lane >= N_EXPERTS) & (lane < N_EXPERTS + N_GROUPS)
    mg = jnp.max(jnp.where(gmask, logits, -jnp.inf), axis=1, keepdims=True)
    eg = jnp.where(gmask, jnp.exp(logits - mg), 0.0)
    pg = eg / jnp.sum(eg, axis=1, keepdims=True)
    pg_top = jnp.max(pg, axis=1, keepdims=True)
    g_idx = jnp.min(jnp.where(gmask & (pg == pg_top), lane, big), axis=1, keepdims=True) - N_EXPERTS
    emask = (lane >= g_idx * EXPERTS_PER_GROUP) & (lane < (g_idx + 1) * EXPERTS_PER_GROUP)
    me = jnp.max(jnp.where(emask, logits, -jnp.inf), axis=1, keepdims=True)
    ee = jnp.where(emask, jnp.exp(logits - me), 0.0)
    pe = ee / jnp.sum(ee, axis=1, keepdims=True)
    p1 = jnp.max(pe, axis=1, keepdims=True)
    i1 = jnp.min(jnp.where(emask & (pe == p1), lane, big), axis=1, keepdims=True)
    rest = emask & (lane != i1)
    p2 = jnp.max(jnp.where(rest, pe, -jnp.inf), axis=1, keepdims=True)
    i2 = jnp.min(jnp.where(rest & (pe == p2), lane, big), axis=1, keepdims=True)
    den = p1 + p2
    ei_ref[...] = jnp.where(lane == 0, i1, jnp.where(lane == 1, i2, 0))
    wt_ref[...] = jnp.where(lane == 0, pg_top * p1 / den, jnp.where(lane == 1, pg_top * p2 / den, 0.0))
    tm = logits.shape[0]
    chosen = jnp.where((lane == i1) | (lane == i2), 1.0, 0.0)
    t_i = lax.broadcasted_iota(I32, (tm, tm), 0)
    t_j = lax.broadcasted_iota(I32, (tm, tm), 1)
    before = _dot(jnp.where(t_j < t_i, 1.0, 0.0), chosen) + run_ref[...]
    r1 = jnp.sum(jnp.where(lane == i1, before, 0.0), axis=1, keepdims=True)
    r2 = jnp.sum(jnp.where(lane == i2, before, 0.0), axis=1, keepdims=True)
    rk_ref[...] = jnp.where(lane == 0, r1, jnp.where(lane == 1, r2, 0.0)).astype(I32)
    run_ref[...] += jnp.sum(chosen, axis=0, keepdims=True)
    cnt_ref[...] = run_ref[...].astype(I32)


def _route(x, y, gt1, g_post, g_pre, sc2, sh2, w_router, b_router):
    s, d = x.shape
    tm = 256
    vec = pl.BlockSpec((1, d), lambda i: (0, 0))
    rowblk = pl.BlockSpec((tm, d), lambda i: (i, 0))
    small = pl.BlockSpec((tm, LANES), lambda i: (i, 0))
    lane_row = pl.BlockSpec((1, LANES), lambda i: (0, 0))
    return pl.pallas_call(
        _route_kernel,
        grid=(s // tm,),
        in_specs=[rowblk, rowblk, vec, vec, vec, vec, vec, pl.BlockSpec((d, LANES), lambda i: (0, 0)), lane_row],
        out_specs=[rowblk, rowblk, small, small, small, lane_row],
        out_shape=[jax.ShapeDtypeStruct((s, d), F32), jax.ShapeDtypeStruct((s, d), F32),
                   jax.ShapeDtypeStruct((s, LANES), I32), jax.ShapeDtypeStruct((s, LANES), F32),
                   jax.ShapeDtypeStruct((s, LANES), I32), jax.ShapeDtypeStruct((1, LANES), I32)],
        scratch_shapes=[pltpu.VMEM((1, LANES), F32)],
        name="norm_route",
        compiler_params=_params("arbitrary"),
    )(x, y, gt1, g_post, g_pre, sc2, sh2, w_router, b_router)


def _slots_kernel(ei_ref, rk_ref, ps_ref, pos_ref):
    lane = lax.broadcasted_iota(I32, ei_ref.shape, 1)
    ei, rk = ei_ref[...], rk_ref[...]
    ps = ps_ref[...]
    cols = []
    for kk in range(TOP_K):
        e = ei[:, kk:kk + 1]
        cols.append(jnp.sum(jnp.where(lane == e, ps, 0), axis=1, keepdims=True) + rk[:, kk:kk + 1])
    pos_ref[...] = jnp.where(lane == 0, cols[0], jnp.where(lane == 1, cols[1], 0))


def _slots(e_idx, rank, pstart_row):
    n = e_idx.shape[0]
    tm = 1024
    blk = pl.BlockSpec((tm, LANES), lambda i: (i, 0))
    return pl.pallas_call(
        _slots_kernel,
        grid=(n // tm,),
        in_specs=[blk, blk, pl.BlockSpec((1, LANES), lambda i: (0, 0))],
        out_specs=blk,
        out_shape=jax.ShapeDtypeStruct((n, LANES), I32),
        name="dispatch_slots",
        compiler_params=_params("parallel"),
    )(e_idx, rank, pstart_row)


def _row_copy(src, row, dst, dst_row, sem):
    return pltpu.make_async_copy(src.at[pl.ds(row, 1), :], dst.at[pl.ds(dst_row, 1), :], sem)


def _dispatch_kernel(pos_ref, cnt_ref, pst_ref, nused_ref, h_ref, x_hbm, stage_ref, zero_ref, sem, zsem):
    i = pl.program_id(0)
    tm = h_ref.shape[0]
    n_blk = x_hbm.shape[0] // EXPERT_BLOCK
    nused = nused_ref[0]
    slot = i % 2

    def wait_tile(s):
        for _ in range(TOP_K):
            pltpu.make_async_copy(stage_ref.at[s], x_hbm.at[pl.ds(0, tm), :], sem.at[s]).wait()

    @pl.when(i >= 2)
    def _():
        wait_tile(slot)

    stage_ref[slot] = h_ref[...]
    for r in range(tm):
        for kk in range(TOP_K):
            _row_copy(stage_ref.at[slot], r, x_hbm, pos_ref[(i * tm + r) * TOP_K + kk], sem.at[slot]).start()

    @pl.when(i == 0)
    def _():
        zero_ref[...] = jnp.zeros_like(zero_ref)

        def pad_expert(e, total):
            lo = pst_ref[e] + cnt_ref[e]
            hi = pst_ref[e] + (cnt_ref[e] + EXPERT_BLOCK - 1) // EXPERT_BLOCK * EXPERT_BLOCK

            def pad_row(s, _):
                _row_copy(zero_ref, 0, x_hbm, s, zsem).start()
                return 0

            lax.fori_loop(lo, hi, pad_row, 0)
            return total + (hi - lo)

        n_pad = lax.fori_loop(0, cnt_ref.shape[0], pad_expert, 0)

        def tail_copy(blk):
            return pltpu.make_async_copy(zero_ref, x_hbm.at[pl.ds(blk * EXPERT_BLOCK, EXPERT_BLOCK), :], zsem)

        def tail_start(blk, _):
            tail_copy(blk).start()
            return 0

        def pad_wait(_, c):
            _row_copy(zero_ref, 0, x_hbm, 0, zsem).wait()
            return c

        def tail_wait(blk, _):
            tail_copy(blk).wait()
            return 0

        lax.fori_loop(nused, n_blk, tail_start, 0)
        lax.fori_loop(0, n_pad, pad_wait, 0)
        lax.fori_loop(nused, n_blk, tail_wait, 0)

    @pl.when(i == pl.num_programs(0) - 1)
    def _():
        @pl.when(i >= 1)
        def _():
            wait_tile(1 - slot)

        wait_tile(slot)


def _dispatch(pos_flat, counts, pstart, nused, h2, n_slots):
    n, d = h2.shape
    tm = 128
    grid_spec = pltpu.PrefetchScalarGridSpec(
        num_scalar_prefetch=4,
        grid=(n // tm,),
        in_specs=[pl.BlockSpec((tm, d), lambda i, *_: (i, 0))],
        out_specs=pl.BlockSpec(memory_space=pl.ANY),
        scratch_shapes=[pltpu.VMEM((2, tm, d), F32), pltpu.VMEM((EXPERT_BLOCK, d), F32),
                        pltpu.SemaphoreType.DMA((2,)), pltpu.SemaphoreType.DMA(())],
    )
    return pl.pallas_call(
        _dispatch_kernel,
        grid_spec=grid_spec,
        out_shape=jax.ShapeDtypeStruct((n_slots, d), F32),
        name="dispatch_rows",
        compiler_params=_params("arbitrary"),
    )(pos_flat, counts, pstart, nused, h2)


EXPERT_X_SLOTS = 4
EXPERT_Y_SLOTS = 3


def _expert_kernel(bstart_ref, bcount_ref, nused_ref, x_hbm, wg_ref, wu_ref, wd_ref, y_hbm,
                   xs_ref, yo_ref, wgb_ref, wub_ref, wdb_ref, xsem, osem):
    e = pl.program_id(0)
    nused = nused_ref[0]
    rows = EXPERT_BLOCK
    n_blk = y_hbm.shape[0] // rows
    nx, ny = EXPERT_X_SLOTS, EXPERT_Y_SLOTS
    first = bstart_ref[e]
    count = bcount_ref[e]

    def x_copy(blk):
        src = x_hbm.at[pl.ds(jnp.minimum(blk, n_blk - 1) * rows, rows), :]
        return pltpu.make_async_copy(src, xs_ref.at[blk % nx], xsem.at[blk % nx])

    def out_copy(blk):
        return pltpu.make_async_copy(yo_ref.at[blk % ny], y_hbm.at[pl.ds(blk * rows, rows), :], osem.at[blk % ny])

    @pl.when(e == 0)
    def _():
        for j in range(nx - 1):
            x_copy(j).start()

    @pl.when(count > 0)
    def _():
        wgb_ref[...] = wg_ref[0].astype(BF16)
        wub_ref[...] = wu_ref[0].astype(BF16)
        wdb_ref[...] = wd_ref[0].astype(BF16)

    def block(j, _):
        blk = first + j

        @pl.when(blk >= ny)
        def _():
            out_copy(blk - ny).wait()

        x_copy(blk).wait()
        x_copy(blk + nx - 1).start()
        xb = xs_ref[blk % nx].astype(BF16)
        hg = jnp.dot(xb, wgb_ref[...], preferred_element_type=F32)
        hu = jnp.dot(xb, wub_ref[...], preferred_element_type=F32)
        hid = hg * _sigmoid(hg) * hu
        yo_ref[blk % ny] = jnp.dot(hid.astype(BF16), wdb_ref[...], preferred_element_type=F32)
        out_copy(blk).start()

        @pl.when(blk == nused - 1)
        def _():
            for ahead in range(1, nx):
                x_copy(blk + ahead).wait()

        return 0

    lax.fori_loop(0, count, block, 0)

    @pl.when(e == pl.num_programs(0) - 1)
    def _():
        for back in range(1, ny + 1):
            @pl.when(nused - back >= 0)
            def _():
                out_copy(nused - back).wait()

        yo_ref[0] = jnp.zeros(yo_ref.shape[1:], F32)

        def zero_copy(blk):
            return pltpu.make_async_copy(yo_ref.at[0], y_hbm.at[pl.ds(blk * rows, rows), :], osem.at[0])

        def fill(blk, _):
            zero_copy(blk).start()
            return 0

        def drain(blk, _):
            zero_copy(blk).wait()
            return 0

        lax.fori_loop(nused, n_blk, fill, 0)
        lax.fori_loop(nused, n_blk, drain, 0)


def _experts(x_buf, bstart, bcount, nused, w_gate_e, w_up_e, w_down_e):
    n_slots, d = x_buf.shape
    n_exp, _, f = w_gate_e.shape
    grid_spec = pltpu.PrefetchScalarGridSpec(
        num_scalar_prefetch=3,
        grid=(n_exp,),
        in_specs=[pl.BlockSpec(memory_space=pl.ANY),
                  pl.BlockSpec((1, d, f), lambda e, *_: (e, 0, 0)),
                  pl.BlockSpec((1, d, f), lambda e, *_: (e, 0, 0)),
                  pl.BlockSpec((1, f, d), lambda e, *_: (e, 0, 0))],
        out_specs=pl.BlockSpec(memory_space=pl.ANY),
        scratch_shapes=[pltpu.VMEM((EXPERT_X_SLOTS, EXPERT_BLOCK, d), F32),
                        pltpu.VMEM((EXPERT_Y_SLOTS, EXPERT_BLOCK, d), F32),
                        pltpu.VMEM((d, f), BF16), pltpu.VMEM((d, f), BF16), pltpu.VMEM((f, d), BF16),
                        pltpu.SemaphoreType.DMA((EXPERT_X_SLOTS,)), pltpu.SemaphoreType.DMA((EXPERT_Y_SLOTS,))],
    )
    return pl.pallas_call(
        _expert_kernel,
        grid_spec=grid_spec,
        out_shape=jax.ShapeDtypeStruct((n_slots, d), F32),
        name="experts",
        compiler_params=_params("arbitrary"),
    )(bstart, bcount, nused, x_buf, w_gate_e, w_up_e, w_down_e)


def _combine_kernel(pos_ref, y_hbm, wt_ref, x1_ref, gt_ref, gpost_ref, o_ref, rows_ref, sem):
    i = pl.program_id(0)
    nsteps = pl.num_programs(0)
    tm = x1_ref.shape[0]

    def start_rows(step, slot):
        for r in range(tm):
            for kk in range(TOP_K):
                _row_copy(y_hbm, pos_ref[(step * tm + r) * TOP_K + kk], rows_ref.at[slot, kk], r,
                          sem.at[slot]).start()

    def wait_rows(slot):
        for kk in range(TOP_K):
            pltpu.make_async_copy(y_hbm.at[pl.ds(0, tm), :], rows_ref.at[slot, kk], sem.at[slot]).wait()

    @pl.when(i == 0)
    def _():
        start_rows(0, 0)

    slot = i % 2
    wait_rows(slot)
    start_rows(jnp.minimum(i + 1, nsteps - 1), 1 - slot)
    wt = wt_ref[...]
    y = rows_ref[slot, 0] * wt[:, 0:1] + rows_ref[slot, 1] * wt[:, 1:2]
    o_ref[...] = x1_ref[...] + gt_ref[...] * _rms(y, gpost_ref[...])

    @pl.when(i == nsteps - 1)
    def _():
        wait_rows(1 - slot)


def _combine(pos, y_buf, wts, x1, gt2, g_post):
    n, d = x1.shape
    tm = 128
    vec = pl.BlockSpec((1, d), lambda i, p: (0, 0))
    grid_spec = pltpu.PrefetchScalarGridSpec(
        num_scalar_prefetch=1,
        grid=(n // tm,),
        in_specs=[pl.BlockSpec(memory_space=pl.ANY),
                  pl.BlockSpec((tm, LANES), lambda i, p: (i, 0)),
                  pl.BlockSpec((tm, d), lambda i, p: (i, 0)), vec, vec],
        out_specs=pl.BlockSpec((tm, d), lambda i, p: (i, 0)),
        scratch_shapes=[pltpu.VMEM((2, TOP_K, tm, d), F32), pltpu.SemaphoreType.DMA((2,))],
    )
    return pl.pallas_call(
        _combine_kernel,
        grid_spec=grid_spec,
        out_shape=jax.ShapeDtypeStruct((n, d), F32),
        name="combine",
        compiler_params=_params("arbitrary"),
    )(pos, y_buf, wts, x1, gt2, g_post)


def _segment_tables(counts_row):
    counts = counts_row[0, :N_EXPERTS]
    pcounts = (counts + EXPERT_BLOCK - 1) // EXPERT_BLOCK * EXPERT_BLOCK
    pend = jnp.cumsum(pcounts)
    pstart = pend - pcounts
    nused = (pend[-1] // EXPERT_BLOCK).astype(I32)
    pstart_row = jnp.pad(pstart, (0, LANES - N_EXPERTS)).reshape(1, LANES)
    return counts, pstart, pstart_row, pstart // EXPERT_BLOCK, pcounts // EXPERT_BLOCK, nused.reshape(1)


def _rope_tables(s):
    half = ROPE_DIM // 2
    inv = ROPE_THETA ** (-jnp.arange(half, dtype=F32) / half)
    ang = jnp.arange(s, dtype=F32)[:, None] * inv[None, :]
    cos, sin = lax.optimization_barrier((jnp.cos(ang), jnp.sin(ang)))
    pad = jnp.zeros((s, LANES - ROPE_DIM), F32)
    zero = jnp.zeros((s, half), F32)
    cos_t = jnp.concatenate([cos, cos, pad + 1.0], axis=1)
    sin1_t = jnp.concatenate([-sin, zero, pad], axis=1)
    sin2_t = jnp.concatenate([zero, sin, pad], axis=1)
    return cos_t, sin1_t, sin2_t


def _layer(x, c_col, w_ada, b_ada, g_pre_mix, g_post_mix, g_pre_ffn, g_post_ffn, w_in, mu_r, mu_k, mu_v,
           mu_w, mu_a, mu_g, w0, w_w1, w_w2, a0, w_a1, w_a2, w_g1, w_g2, k_k, k_a, r_k, gn_w, gn_b,
           w_up_att, w_up_rwkv, w_o, w_rg, b_rg, w_re, b_re, w_gate_e, w_up_e, w_down_e):
    s, d = x.shape
    att_w = ATT_HEADS * ATT_HEAD_DIM
    rwkv_w = w_up_rwkv.shape[0]
    row = lambda a: a.reshape(1, -1)

    ada = _ada(c_col, w_ada, row(b_ada))
    sh1, sc1, gt1, sh2, sc2, gt2 = (ada[:, i * d:(i + 1) * d] for i in range(6))

    h, lw, la, lg = _prenorm(x, row(g_pre_mix), sc1, sh1, row(mu_w), row(mu_a), row(mu_g), w_w1, w_a1, w_g1)
    proj = _matmul(h, w_in, BF16)

    q_aug_t, k_aug, v_t = _rope_gate(proj, *_rope_tables(s))
    o_att = _attention(q_aug_t, k_aug, v_t)

    o_rwkv = _rwkv(proj, lw, la, lg, w_w2, w_a2, w_g2, row(mu_r), row(mu_k), row(mu_v), row(w0), row(a0),
                   row(k_k), row(k_a), row(r_k), row(gn_w), row(gn_b), col0=3 * att_w, width=rwkv_w)

    mix = _merge(o_att, o_rwkv, proj, w_up_att, w_up_rwkv, gate_col0=3 * att_w + 3 * rwkv_w)
    y = _matmul(mix, w_o, F32)

    w_router = jnp.pad(jnp.concatenate([w_re, w_rg], axis=1), ((0, 0), (0, LANES - N_EXPERTS - N_GROUPS)))
    b_router = jnp.pad(jnp.concatenate([b_re, b_rg]), (0, LANES - N_EXPERTS - N_GROUPS)).reshape(1, LANES)
    x1, h2, e_idx, wts, rank, counts = _route(x, y, gt1, row(g_post_mix), row(g_pre_ffn), sc2, sh2,
                                              w_router, b_router)

    n_pairs = s * TOP_K
    n_blk = (n_pairs + N_EXPERTS * (EXPERT_BLOCK - 1) + EXPERT_BLOCK - 1) // EXPERT_BLOCK
    counts, pstart, pstart_row, bstart, bcount, nused = _segment_tables(counts)
    pos = _slots(e_idx, rank, pstart_row)[:, :TOP_K].reshape(n_pairs)
    x_buf = _dispatch(pos, counts, pstart, nused, h2, n_blk * EXPERT_BLOCK)
    y_buf = _experts(x_buf, bstart, bcount, nused, w_gate_e, w_up_e, w_down_e)
    return _combine(pos, y_buf, wts, x1, gt2, row(g_post_ffn))


def kernel(x, c, w_ada, b_ada, g_pre_mix, g_post_mix, g_pre_ffn, g_post_ffn, w_in, mu_r, mu_k, mu_v, mu_w, mu_a, mu_g, w0, w_w1, w_w2, a0, w_a1, w_a2, w_g1, w_g2, k_k, k_a, r_k, gn_w, gn_b, w_up_att, w_up_rwkv, w_o, w_rg, b_rg, w_re, b_re, w_gate_e, w_up_e, w_down_e):
    b, s, d = x.shape
    assert b == 1, "one sequence per call"
    params = (w_ada, b_ada, g_pre_mix, g_post_mix, g_pre_ffn, g_post_ffn, w_in, mu_r, mu_k, mu_v, mu_w, mu_a,
              mu_g, w0, w_w1, w_w2, a0, w_a1, w_a2, w_g1, w_g2, k_k, k_a, r_k, gn_w, gn_b, w_up_att,
              w_up_rwkv, w_o, w_rg, b_rg, w_re, b_re, w_gate_e, w_up_e, w_down_e)
    xs = x.reshape(s, d)
    c_col = c.reshape(d, 1)
    for l in range(w_ada.shape[0]):
        xs = _layer(xs, c_col, *(p[l] for p in params))
    return xs.reshape(b, s, d)
```

```python
import math

import jax
import jax.numpy as jnp
from jax import lax
from jax.experimental import pallas as pl
from jax.experimental.pallas import tpu as pltpu

F32 = jnp.float32
BF16 = jnp.bfloat16
I32 = jnp.int32
HI = lax.Precision.HIGHEST

LANES = 128
SUBLANES = 8
VMEM_LIMIT = 56 * 1024 * 1024

ATT_HEADS = 8
ATT_HEAD_DIM = 128
MOBA_BLOCK = 256
MOBA_TOPK = 3
ATT_GROUP = 4
ATT_HEADS_PER_STEP = 2
ROPE_THETA = 500000.0
ROPE_DIM = ATT_HEAD_DIM // 4
RWKV_HEAD_DIM = 64
GN_EPS = 64e-5
N_GROUPS = 8
EXPERTS_PER_GROUP = 8
N_EXPERTS = N_GROUPS * EXPERTS_PER_GROUP
TOP_K = 2
EXPERT_BLOCK = 128
RMS_EPS = 1e-6
NEG = -1e30
SCAN_CHUNK = 64
RWKV_CHUNK_GROUP = 8
RWKV_ROWS_PER_STEP = 2048
Q_SCALE = ATT_HEAD_DIM ** -0.5 * math.log2(math.e)


def _params(*sem):
    return pltpu.CompilerParams(dimension_semantics=sem, vmem_limit_bytes=VMEM_LIMIT)


def _rms(z, g):
    return z * lax.rsqrt(jnp.mean(z * z, axis=-1, keepdims=True) + RMS_EPS) * g


def _sigmoid(z):
    return 1.0 / (1.0 + jnp.exp(-z))


def _dot(a, b):
    return jnp.dot(a.astype(BF16), b.astype(BF16), preferred_element_type=F32)


def _dot_nt(a, b):
    return lax.dot_general(a.astype(BF16), b.astype(BF16), (((1,), (1,)), ((), ())),
                           preferred_element_type=F32)


def _dot_tn(a, b):
    return lax.dot_general(a.astype(BF16), b.astype(BF16), (((0,), (0,)), ((), ())),
                           preferred_element_type=F32)


def _dot_hi(a, b):
    return jnp.dot(a, b, precision=HI, preferred_element_type=F32)


def _dot_x3(a, b):
    ah, al, _ = _split3(a)
    bh, bl, _ = _split3(b)
    return (jnp.dot(ah, bh, preferred_element_type=F32) + jnp.dot(ah, bl, preferred_element_type=F32)
            + jnp.dot(al, bh, preferred_element_type=F32))


def _split3(a):
    hi = a.astype(BF16)
    r1 = a - hi.astype(F32)
    mid = r1.astype(BF16)
    lo = (r1 - mid.astype(F32)).astype(BF16)
    return hi, mid, lo


def _dot_split(a, b01):
    b = b01.astype(BF16)
    hi, mid, _ = _split3(a)
    return jnp.dot(jnp.concatenate([hi, mid], axis=1), jnp.concatenate([b, b], axis=0),
                   preferred_element_type=F32)


def _dot_split_t(b01, a):
    b = b01.astype(BF16)
    return jnp.dot(jnp.concatenate([b, b, b], axis=1), jnp.concatenate(_split3(a), axis=0),
                   preferred_element_type=F32)


def _shift_rows(z, prev_row):
    rolled = pltpu.roll(z, 1, 0)
    row = lax.broadcasted_iota(I32, z.shape, 0)
    return jnp.where(row == 0, prev_row, rolled)


def _ada_kernel(c_ref, w_ref, b_ref, o_ref):
    o_ref[...] = jnp.sum(c_ref[...] * w_ref[...], axis=0, keepdims=True) + b_ref[...]


def _ada(c_col, w_ada, b_ada):
    d, n = w_ada.shape
    tn = 1024
    return pl.pallas_call(
        _ada_kernel,
        grid=(n // tn,),
        in_specs=[pl.BlockSpec((d, 1), lambda j: (0, 0)),
                  pl.BlockSpec((d, tn), lambda j: (0, j)),
                  pl.BlockSpec((1, tn), lambda j: (0, j))],
        out_specs=pl.BlockSpec((1, tn), lambda j: (0, j)),
        out_shape=jax.ShapeDtypeStruct((1, n), F32),
        name="ada",
        compiler_params=_params("parallel"),
    )(c_col, w_ada, b_ada)


def _prenorm_kernel(x_ref, xp_ref, g_ref, sc_ref, sh_ref, muw_ref, mua_ref, mug_ref,
                    ww1_ref, wa1_ref, wg1_ref, h_ref, lw_ref, la_ref, lg_ref):
    i = pl.program_id(0)
    g, sc, sh = g_ref[...], sc_ref[...], sh_ref[...]
    h = _rms(x_ref[...], g) * (1.0 + sc) + sh
    hp = _rms(xp_ref[SUBLANES - 1:SUBLANES, :], g) * (1.0 + sc) + sh
    hp = jnp.where(i == 0, 0.0, hp)
    dh = _shift_rows(h, hp) - h
    h_ref[...] = h.astype(BF16)
    lw_ref[...] = jnp.tanh(_dot(h + dh * muw_ref[...], ww1_ref[...]))
    la_ref[...] = _dot(h + dh * mua_ref[...], wa1_ref[...])
    lg_ref[...] = _sigmoid(_dot(h + dh * mug_ref[...], wg1_ref[...]))


def _prenorm(x, g, sc, sh, mu_w, mu_a, mu_g, w_w1, w_a1, w_g1):
    s, d = x.shape
    tm = 256
    rpb = tm // SUBLANES
    vec = pl.BlockSpec((1, d), lambda i: (0, 0))
    full = lambda a: pl.BlockSpec(a.shape, lambda i: (0, 0))
    lw, la, lg = w_w1.shape[1], w_a1.shape[1], w_g1.shape[1]
    return pl.pallas_call(
        _prenorm_kernel,
        grid=(s // tm,),
        in_specs=[pl.BlockSpec((tm, d), lambda i: (i, 0)),
                  pl.BlockSpec((SUBLANES, d), lambda i: (jnp.maximum(i * rpb - 1, 0), 0)),
                  vec, vec, vec, vec, vec, vec, full(w_w1), full(w_a1), full(w_g1)],
        out_specs=[pl.BlockSpec((tm, d), lambda i: (i, 0)),
                   pl.BlockSpec((tm, lw), lambda i: (i, 0)),
                   pl.BlockSpec((tm, la), lambda i: (i, 0)),
                   pl.BlockSpec((tm, lg), lambda i: (i, 0))],
        out_shape=[jax.ShapeDtypeStruct((s, d), BF16),
                   jax.ShapeDtypeStruct((s, lw), F32),
                   jax.ShapeDtypeStruct((s, la), F32),
                   jax.ShapeDtypeStruct((s, lg), F32)],
        name="prenorm_lora",
        compiler_params=_params("parallel"),
    )(x, x, g, sc, sh, mu_w, mu_a, mu_g, w_w1, w_a1, w_g1)


def _mm_kernel(a_ref, w_ref, o_ref, wb_ref):
    @pl.when(pl.program_id(1) == 0)
    def _():
        wb_ref[...] = w_ref[...].astype(BF16)

    o_ref[...] = jnp.dot(a_ref[...], wb_ref[...], preferred_element_type=F32).astype(o_ref.dtype)


def _matmul(a, w, out_dtype, tm=512, tn=1024):
    m, k = a.shape
    n = w.shape[1]
    tn = min(tn, n)
    return pl.pallas_call(
        _mm_kernel,
        grid=(n // tn, m // tm),
        in_specs=[pl.BlockSpec((tm, k), lambda j, i: (i, 0)),
                  pl.BlockSpec((k, tn), lambda j, i: (0, j))],
        out_specs=pl.BlockSpec((tm, tn), lambda j, i: (i, j)),
        out_shape=jax.ShapeDtypeStruct((m, n), out_dtype),
        scratch_shapes=[pltpu.VMEM((k, tn), BF16)],
        name="matmul",
        compiler_params=_params("arbitrary", "arbitrary"),
    )(a, w)


def _rope_gate_kernel(p_ref, c_ref, s1_ref, s2_ref, qa_ref, ka_ref, vt_ref, km_ref):
    i = pl.program_id(0)
    bs = MOBA_BLOCK
    nbp = km_ref.shape[1]

    @pl.when(i == 0)
    def _():
        km_ref[...] = jnp.zeros_like(km_ref)

    c, s1, s2 = c_ref[...], s1_ref[...], s2_ref[...]

    def rope(z):
        return z * c + pltpu.roll(z, LANES - ROPE_DIM // 2, 1) * s1 + pltpu.roll(z, ROPE_DIM // 2, 1) * s2

    row = lax.broadcasted_iota(I32, (nbp, bs), 0)
    lane = lax.broadcasted_iota(I32, (bs, LANES), 1)
    onehot = jnp.where(lane == i, 1.0, 0.0).astype(BF16)
    for h in range(ATT_HEADS):
        q = rope(p_ref[:, h * LANES:(h + 1) * LANES].astype(F32))
        k = rope(p_ref[:, (ATT_HEADS + h) * LANES:(ATT_HEADS + h + 1) * LANES].astype(F32))
        g = lax.dot_general(km_ref[h], q, (((1,), (1,)), ((), ())), precision=HI, preferred_element_type=F32)
        g = jnp.where(row < i, g, NEG)
        sel_t = jnp.zeros(g.shape, F32)
        for _ in range(MOBA_TOPK):
            mx = jnp.max(g, axis=0, keepdims=True)
            idx = jnp.min(jnp.where(g == mx, row, nbp), axis=0, keepdims=True)
            hit = row == idx
            sel_t = jnp.where(hit & (row < i), 1.0, sel_t)
            g = jnp.where(hit, -jnp.inf, g)
        if nbp < LANES:
            sel_t = jnp.concatenate([sel_t, jnp.zeros((LANES - nbp, bs), F32)], axis=0)
        w = 2 * LANES
        qa_ref[h, :LANES, :] = (q * Q_SCALE).T.astype(BF16)
        qa_ref[h, LANES:, :] = jnp.where(sel_t > 0.5, 0.0, NEG).astype(BF16)
        ka_ref[:, h * w:h * w + LANES] = k.astype(BF16)
        ka_ref[:, h * w + LANES:(h + 1) * w] = onehot
        v = p_ref[:, (2 * ATT_HEADS + h) * LANES:(2 * ATT_HEADS + h + 1) * LANES].astype(F32)
        vt_ref[h, 0] = v.T.astype(BF16)
        km_ref[h, pl.ds(i, 1), :] = jnp.mean(k, axis=0, keepdims=True)


def _rope_gate(proj, cos_t, sin1_t, sin2_t):
    s = proj.shape[0]
    nb = s // MOBA_BLOCK
    assert nb <= LANES
    nbp = -(-nb // SUBLANES) * SUBLANES
    w_in = 3 * ATT_HEADS * ATT_HEAD_DIM
    w_out = 2 * ATT_HEADS * LANES
    tab = pl.BlockSpec((MOBA_BLOCK, LANES), lambda i: (i, 0))
    return pl.pallas_call(
        _rope_gate_kernel,
        grid=(nb,),
        in_specs=[pl.BlockSpec((MOBA_BLOCK, w_in), lambda i: (i, 0)), tab, tab, tab],
        out_specs=[pl.BlockSpec((ATT_HEADS, 2 * LANES, MOBA_BLOCK), lambda i: (0, 0, i)),
                   pl.BlockSpec((MOBA_BLOCK, w_out), lambda i: (i, 0)),
                   pl.BlockSpec((ATT_HEADS, 1, LANES, MOBA_BLOCK), lambda i: (0, i, 0, 0))],
        out_shape=[jax.ShapeDtypeStruct((ATT_HEADS, 2 * LANES, s), BF16),
                   jax.ShapeDtypeStruct((s, w_out), BF16),
                   jax.ShapeDtypeStruct((ATT_HEADS, nb, LANES, MOBA_BLOCK), BF16)],
        scratch_shapes=[pltpu.VMEM((ATT_HEADS, nbp, LANES), F32)],
        name="rope_gate",
        compiler_params=_params("arbitrary"),
    )(proj, cos_t, sin1_t, sin2_t)


def _attn_kernel(qa_ref, ka_ref, vt_ref, o_ref, s_ref):
    qi = pl.program_id(1)
    bs = MOBA_BLOCK
    grp = ATT_GROUP * bs
    w = 2 * LANES
    heads = range(ATT_HEADS_PER_STEP)

    n_groups = ka_ref.shape[0] // grp
    assert n_groups % 2 == 0

    def issue_scores(g, buf):
        base = pl.multiple_of(jnp.minimum(g, n_groups - 1) * grp, grp)
        for h in heads:
            s_ref[buf, h] = jnp.dot(ka_ref[pl.ds(base, grp), h * w:(h + 1) * w], qa_ref[h],
                                    preferred_element_type=F32)

    issue_scores(0, 0)

    own = pl.multiple_of(qi * bs, bs)
    k_i = lax.broadcasted_iota(I32, (bs, bs), 0)
    q_i = lax.broadcasted_iota(I32, (bs, bs), 1)
    carry = []
    for h in heads:
        s = jnp.dot(ka_ref[pl.ds(own, bs), h * w:h * w + LANES], qa_ref[h, :LANES, :],
                    preferred_element_type=F32)
        s = jnp.where(k_i <= q_i, s, NEG)
        m = jnp.max(s, axis=0, keepdims=True)
        p = jnp.exp2(s - m)
        l = jnp.sum(p, axis=0, keepdims=True)
        acc = jnp.dot(vt_ref[h, qi], p.astype(BF16), preferred_element_type=F32)
        carry += [m, l, acc]

    def absorb(g, buf, carry):
        out = []
        for h in heads:
            m, l, acc = carry[3 * h:3 * h + 3]
            s = s_ref[buf, h]
            mn = jnp.maximum(m, jnp.max(s, axis=0, keepdims=True))
            alpha = jnp.exp2(m - mn)
            p = jnp.exp2(s - mn)
            l = alpha * l + jnp.sum(p, axis=0, keepdims=True)
            p = p.astype(BF16)
            acc = alpha * acc
            for jb in range(ATT_GROUP):
                acc = acc + jnp.dot(vt_ref[h, g * ATT_GROUP + jb], p[jb * bs:(jb + 1) * bs],
                                    preferred_element_type=F32)
            out += [mn, l, acc]
        return out

    def body(t, carry):
        issue_scores(2 * t + 1, 1)
        carry = absorb(2 * t, 0, carry)
        issue_scores(2 * t + 2, 0)
        return tuple(absorb(2 * t + 1, 1, carry))

    n_used = (qi + ATT_GROUP - 1) // ATT_GROUP
    carry = lax.fori_loop(0, (n_used + 1) // 2, body, tuple(carry))
    for h in heads:
        m, l, acc = carry[3 * h:3 * h + 3]
        o_ref[:, h * LANES:(h + 1) * LANES] = (acc / l).T.astype(BF16)


def _attention(q_aug_t, k_aug, v_t):
    s = k_aug.shape[0]
    assert s % (ATT_GROUP * MOBA_BLOCK) == 0
    nb = s // MOBA_BLOCK
    hps = ATT_HEADS_PER_STEP
    once = pl.Buffered(1)
    return pl.pallas_call(
        _attn_kernel,
        grid=(ATT_HEADS // hps, nb),
        in_specs=[pl.BlockSpec((hps, 2 * LANES, MOBA_BLOCK), lambda h, i: (h, 0, i)),
                  pl.BlockSpec((s, hps * 2 * LANES), lambda h, i: (0, h), pipeline_mode=once),
                  pl.BlockSpec((hps, nb, LANES, MOBA_BLOCK), lambda h, i: (h, 0, 0, 0), pipeline_mode=once)],
        out_specs=pl.BlockSpec((MOBA_BLOCK, hps * LANES), lambda h, i: (i, h)),
        out_shape=jax.ShapeDtypeStruct((s, ATT_HEADS * ATT_HEAD_DIM), BF16),
        scratch_shapes=[pltpu.VMEM((2, hps, ATT_GROUP * MOBA_BLOCK, MOBA_BLOCK), F32)],
        name="moba_attention",
        compiler_params=_params("parallel", "arbitrary"),
    )(q_aug_t, k_aug, v_t)


def _rwkv_kernel(r_ref, k_ref, v_ref, lw_ref, la_ref, lg_ref, ww2_ref, wa2_ref, wg2_ref,
                 mur_ref, muk_ref, muv_ref, w0_ref, a0_ref, kk_ref, ka_ref, rk_ref, gnw_ref, gnb_ref,
                 o_ref, st_ref, prev_ref):
    j = pl.program_id(1)
    t = SCAN_CHUNK
    n = RWKV_HEAD_DIM
    ts = r_ref.shape[0]
    chunks = range(ts // t)

    @pl.when(j == 0)
    def _():
        st_ref[...] = jnp.zeros_like(st_ref)
        prev_ref[...] = jnp.zeros_like(prev_ref)

    rp, kp, vp = r_ref[...].astype(F32), k_ref[...].astype(F32), v_ref[...].astype(F32)
    r = rp + (_shift_rows(rp, prev_ref[0:1, :]) - rp) * mur_ref[...]
    k = kp + (_shift_rows(kp, prev_ref[1:2, :]) - kp) * muk_ref[...]
    v = vp + (_shift_rows(vp, prev_ref[2:3, :]) - vp) * muv_ref[...]
    prev_ref[0:1, :] = rp[ts - 1:ts, :]
    prev_ref[1:2, :] = kp[ts - 1:ts, :]
    prev_ref[2:3, :] = vp[ts - 1:ts, :]

    d = w0_ref[...] + _dot(lw_ref[...], ww2_ref[...])
    logw = -math.exp(-0.5) * _sigmoid(d)
    a = _sigmoid(a0_ref[...] + _dot(la_ref[...], wa2_ref[...]))
    g = _dot(lg_ref[...], wg2_ref[...])

    li = lax.broadcasted_iota(I32, (LANES, LANES), 0)
    lj = lax.broadcasted_iota(I32, (LANES, LANES), 1)
    same_head = (li // n) == (lj // n)
    head_sum = jnp.where(same_head, 1.0, 0.0)

    kk = k * kk_ref[...]
    kk = kk / jnp.maximum(jnp.sqrt(_dot_split(kk * kk, head_sum)), 1e-12)
    kt = k * (1.0 + (a - 1.0) * ka_ref[...])
    bonus = _dot_split(r * kt * rk_ref[...], head_sum) * v

    same_blk = (li // t) == (lj // t)
    m_strict = same_blk & (lj < li)
    m_incl = same_blk & (lj <= li)
    ti = lax.broadcasted_iota(I32, (t, t), 0)
    tj = lax.broadcasted_iota(I32, (t, t), 1)
    tril_incl = jnp.where(tj <= ti, 1.0, 0.0)
    lane_a = lax.broadcasted_iota(I32, (t, LANES), 1) < n
    eye = jnp.where(li == lj, 1.0, 0.0)

    def stack_masked(z):
        return jnp.concatenate([jnp.where(lane_a, z, 0.0), jnp.where(lane_a, 0.0, z)], axis=0)

    def stack_plain(z):
        return jnp.concatenate([z, z], axis=0)

    def rows(z, c):
        return z[c * t:(c + 1) * t]

    cum = jnp.concatenate([_dot_split_t(tril_incl, rows(logw, c)) for c in chunks], axis=0)
    g_t = jnp.exp(cum)
    g_inv = jnp.exp(-cum)
    xa_f = -kk * jnp.exp(cum - logw)
    xr_f = r * g_t
    yb_f = kk * a * g_inv
    yk_f = kt * g_inv

    zeros = jnp.zeros((2 * t, LANES), F32)
    state = [st_ref[...]]
    ys = {}

    def chain_step(c, r_hat, y_hat, pm, qm):
        def run():
            st = state[0]
            y2 = _dot_nt(r_hat, st) + y_hat
            ys[c] = y2[:t] + y2[t:]
            state[0] = (st + _dot(st, pm) + qm) * g_t[(c + 1) * t - 1:(c + 1) * t, :]
        return run

    def chunk_local(grp, between):
        idx = range(len(grp))
        xa = [stack_masked(rows(xa_f, c)) for c in grp]
        xr = [stack_masked(rows(xr_f, c)) for c in grp]
        yb = [stack_plain(rows(yb_f, c)) for c in grp]
        vs = [stack_masked(rows(v, c)) for c in grp]
        ybk = [jnp.concatenate([yb[i], stack_plain(rows(yk_f, c))], axis=0) for i, c in enumerate(grp)]
        sc = [_dot_nt(jnp.concatenate([xa[i], xr[i]], axis=0), ybk[i]) for i in idx]
        between()
        a_ab = [jnp.where(m_strict, sc[i][:2 * t, :2 * t], 0.0) for i in idx]
        a_ak = [jnp.where(m_strict, sc[i][:2 * t, 2 * t:], 0.0) for i in idx]
        a_rb = [jnp.where(m_incl, sc[i][2 * t:, :2 * t], 0.0) for i in idx]
        a_rk = [jnp.where(m_incl, sc[i][2 * t:, 2 * t:], 0.0) for i in idx]
        inv = [eye + a_ab[i] for i in idx]
        pw = [_dot(a_ab[i], a_ab[i]) for i in idx]
        for _ in range(int(math.log2(t)) - 2):
            between()
            both = [_dot(pw[i], jnp.concatenate([pw[i], inv[i]], axis=1)) for i in idx]
            pw = [both[i][:, :LANES] for i in idx]
            inv = [inv[i] + both[i][:, LANES:] for i in idx]
        between()
        inv = [inv[i] + _dot(pw[i], inv[i]) for i in idx]
        av = [_dot(a_ak[i], vs[i]) for i in idx]
        between()
        mw = [_dot(inv[i], jnp.concatenate([xa[i], av[i]], axis=1)) for i in idx]
        between()
        rw = [_dot(jnp.concatenate([a_rb[i], a_rk[i]], axis=1),
                   jnp.concatenate([mw[i], jnp.concatenate([zeros, vs[i]], axis=1)], axis=0)) for i in idx]
        between()
        pm = [jnp.where(same_head, _dot_tn(mw[i][:, :LANES], yb[i]), 0.0) for i in idx]
        qm = [jnp.where(same_head, _dot_tn(jnp.concatenate([mw[i][:, LANES:], vs[i]], axis=0), ybk[i]), 0.0)
              for i in idx]
        return [chain_step(c, xr[i] + rw[i][:, :LANES], rw[i][:, LANES:], pm[i], qm[i])
                for i, c in enumerate(grp)]

    pending = []

    def between():
        if pending:
            pending.pop(0)()

    group = RWKV_CHUNK_GROUP
    for g0 in range(0, len(chunks), group):
        steps = chunk_local(list(chunks[g0:g0 + group]), between)
        while pending:
            pending.pop(0)()
        pending.extend(steps)
    while pending:
        pending.pop(0)()
    st_ref[...] = state[0]
    y = jnp.concatenate([ys[c] for c in chunks], axis=0)

    mean = _dot_split(y, head_sum) * (1.0 / n)
    yc = y - mean
    var = _dot_split(yc * yc, head_sum) * (1.0 / n)
    yn = yc * lax.rsqrt(var + GN_EPS) * gnw_ref[...] + gnb_ref[...]
    o_ref[...] = ((yn + bonus) * g).astype(BF16)


def _rwkv(proj, lw, la, lg, w_w2, w_a2, w_g2, mu_r, mu_k, mu_v, w0, a0, k_k, k_a, r_k, gn_w, gn_b,
          col0, width):
    s = proj.shape[0]
    ts = min(RWKV_ROWS_PER_STEP, s)
    npair = width // LANES
    cb = col0 // LANES
    row = lambda a: pl.BlockSpec((ts, a.shape[1]), lambda p, j: (j, 0))
    wcol = lambda a: pl.BlockSpec((a.shape[0], LANES), lambda p, j: (0, p))
    vec = pl.BlockSpec((1, LANES), lambda p, j: (0, p))
    return pl.pallas_call(
        _rwkv_kernel,
        grid=(npair, s // ts),
        in_specs=[pl.BlockSpec((ts, LANES), lambda p, j: (j, cb + p)),
                  pl.BlockSpec((ts, LANES), lambda p, j: (j, cb + npair + p)),
                  pl.BlockSpec((ts, LANES), lambda p, j: (j, cb + 2 * npair + p)),
                  row(lw), row(la), row(lg), wcol(w_w2), wcol(w_a2), wcol(w_g2)] + [vec] * 10,
        out_specs=pl.BlockSpec((ts, LANES), lambda p, j: (j, p)),
        out_shape=jax.ShapeDtypeStruct((s, width), BF16),
        scratch_shapes=[pltpu.VMEM((LANES, LANES), F32), pltpu.VMEM((SUBLANES, LANES), F32)],
        name="rwkv7_scan",
        compiler_params=_params("parallel", "arbitrary"),
    )(proj, proj, proj, lw, la, lg, w_w2, w_a2, w_g2, mu_r, mu_k, mu_v, w0, a0, k_k, k_a, r_k, gn_w, gn_b)


def _merge_kernel(oa_ref, or_ref, ga_ref, gr_ref, wa_ref, wr_ref, o_ref, wab_ref, wrb_ref):
    @pl.when(pl.program_id(1) == 0)
    def _():
        wab_ref[...] = wa_ref[...].astype(BF16)
        wrb_ref[...] = wr_ref[...].astype(BF16)

    ua = jnp.dot(oa_ref[...], wab_ref[...], preferred_element_type=F32)
    ur = jnp.dot(or_ref[...], wrb_ref[...], preferred_element_type=F32)
    mix = _sigmoid(ga_ref[...].astype(F32)) * ua + _sigmoid(gr_ref[...].astype(F32)) * ur
    o_ref[...] = mix.astype(BF16)


def _merge(o_att, o_rwkv, proj, w_up_att, w_up_rwkv, gate_col0):
    s, ka = o_att.shape
    kr = o_rwkv.shape[1]
    d = w_up_att.shape[1]
    tm, tn = 512, 1024
    gb = gate_col0 // tn
    return pl.pallas_call(
        _merge_kernel,
        grid=(d // tn, s // tm),
        in_specs=[pl.BlockSpec((tm, ka), lambda j, i: (i, 0)),
                  pl.BlockSpec((tm, kr), lambda j, i: (i, 0)),
                  pl.BlockSpec((tm, tn), lambda j, i: (i, gb + j)),
                  pl.BlockSpec((tm, tn), lambda j, i: (i, gb + d // tn + j)),
                  pl.BlockSpec((ka, tn), lambda j, i: (0, j)),
                  pl.BlockSpec((kr, tn), lambda j, i: (0, j))],
        out_specs=pl.BlockSpec((tm, tn), lambda j, i: (i, j)),
        out_shape=jax.ShapeDtypeStruct((s, d), BF16),
        scratch_shapes=[pltpu.VMEM((ka, tn), BF16), pltpu.VMEM((kr, tn), BF16)],
        name="gated_merge",
        compiler_params=_params("arbitrary", "arbitrary"),
    )(o_att, o_rwkv, proj, proj, w_up_att, w_up_rwkv)


def _route_kernel(x_ref, y_ref, gt_ref, gpost_ref, gpre_ref, sc_ref, sh_ref, wr_ref, br_ref,
                  x1_ref, h2_ref, ei_ref, wt_ref, rk_ref, cnt_ref, run_ref):
    @pl.when(pl.program_id(0) == 0)
    def _():
        run_ref[...] = jnp.zeros_like(run_ref)

    x1 = x_ref[...] + gt_ref[...] * _rms(y_ref[...], gpost_ref[...])
    x1_ref[...] = x1
    h2 = _rms(x1, gpre_ref[...]) * (1.0 + sc_ref[...]) + sh_ref[...]
    h2_ref[...] = h2
    logits = _dot_x3(h2, wr_ref[...]) + br_ref[...]
    lane = lax.broadcasted_iota(I32, logits.shape, 1)
    big = jnp.int32(4 * LANES)
    gmask = (lane >= N_EXPERTS) & (lane < N_EXPERTS + N_GROUPS)
    mg = jnp.max(jnp.where(gmask, logits, -jnp.inf), axis=1, keepdims=True)
    eg = jnp.where(gmask, jnp.exp(logits - mg), 0.0)
    pg = eg / jnp.sum(eg, axis=1, keepdims=True)
    pg_top = jnp.max(pg, axis=1, keepdims=True)
    g_idx = jnp.min(jnp.where(gmask & (pg == pg_top), lane, big), axis=1, keepdims=True) - N_EXPERTS
    emask = (lane >= g_idx * EXPERTS_PER_GROUP) & (lane < (g_idx + 1) * EXPERTS_PER_GROUP)
    me = jnp.max(jnp.where(emask, logits, -jnp.inf), axis=1, keepdims=True)
    ee = jnp.where(emask, jnp.exp(logits - me), 0.0)
    pe = ee / jnp.sum(ee, axis=1, keepdims=True)
    p1 = jnp.max(pe, axis=1, keepdims=True)
    i1 = jnp.min(jnp.where(emask & (pe == p1), lane, big), axis=1, keepdims=True)
    rest = emask & (lane != i1)
    p2 = jnp.max(jnp.where(rest, pe, -jnp.inf), axis=1, keepdims=True)
    i2 = jnp.min(jnp.where(rest & (pe == p2), lane, big), axis=1, keepdims=True)
    den = p1 + p2
    ei_ref[...] = jnp.where(lane == 0, i1, jnp.where(lane == 1, i2, 0))
    wt_ref[...] = jnp.where(lane == 0, pg_top * p1 / den, jnp.where(lane == 1, pg_top * p2 / den, 0.0))
    tm = logits.shape[0]
    chosen = jnp.where((lane == i1) | (lane == i2), 1.0, 0.0)
    t_i = lax.broadcasted_iota(I32, (tm, tm), 0)
    t_j = lax.broadcasted_iota(I32, (tm, tm), 1)
    before = _dot(jnp.where(t_j < t_i, 1.0, 0.0), chosen) + run_ref[...]
    r1 = jnp.sum(jnp.where(lane == i1, before, 0.0), axis=1, keepdims=True)
    r2 = jnp.sum(jnp.where(lane == i2, before, 0.0), axis=1, keepdims=True)
    rk_ref[...] = jnp.where(lane == 0, r1, jnp.where(lane == 1, r2, 0.0)).astype(I32)
    run_ref[...] += jnp.sum(chosen, axis=0, keepdims=True)
    cnt_ref[...] = run_ref[...].astype(I32)


def _route(x, y, gt1, g_post, g_pre, sc2, sh2, w_router, b_router):
    s, d = x.shape
    tm = 256
    vec = pl.BlockSpec((1, d), lambda i: (0, 0))
    rowblk = pl.BlockSpec((tm, d), lambda i: (i, 0))
    small = pl.BlockSpec((tm, LANES), lambda i: (i, 0))
    lane_row = pl.BlockSpec((1, LANES), lambda i: (0, 0))
    return pl.pallas_call(
        _route_kernel,
        grid=(s // tm,),
        in_specs=[rowblk, rowblk, vec, vec, vec, vec, vec, pl.BlockSpec((d, LANES), lambda i: (0, 0)), lane_row],
        out_specs=[rowblk, rowblk, small, small, small, lane_row],
        out_shape=[jax.ShapeDtypeStruct((s, d), F32), jax.ShapeDtypeStruct((s, d), F32),
                   jax.ShapeDtypeStruct((s, LANES), I32), jax.ShapeDtypeStruct((s, LANES), F32),
                   jax.ShapeDtypeStruct((s, LANES), I32), jax.ShapeDtypeStruct((1, LANES), I32)],
        scratch_shapes=[pltpu.VMEM((1, LANES), F32)],
        name="norm_route",
        compiler_params=_params("arbitrary"),
    )(x, y, gt1, g_post, g_pre, sc2, sh2, w_router, b_router)


def _slots_kernel(ei_ref, rk_ref, ps_ref, pos_ref):
    lane = lax.broadcasted_iota(I32, ei_ref.shape, 1)
    ei, rk = ei_ref[...], rk_ref[...]
    ps = ps_ref[...]
    cols = []
    for kk in range(TOP_K):
        e = ei[:, kk:kk + 1]
        cols.append(jnp.sum(jnp.where(lane == e, ps, 0), axis=1, keepdims=True) + rk[:, kk:kk + 1])
    pos_ref[...] = jnp.where(lane == 0, cols[0], jnp.where(lane == 1, cols[1], 0))


def _slots(e_idx, rank, pstart_row):
    n = e_idx.shape[0]
    tm = 1024
    blk = pl.BlockSpec((tm, LANES), lambda i: (i, 0))
    return pl.pallas_call(
        _slots_kernel,
        grid=(n // tm,),
        in_specs=[blk, blk, pl.BlockSpec((1, LANES), lambda i: (0, 0))],
        out_specs=blk,
        out_shape=jax.ShapeDtypeStruct((n, LANES), I32),
        name="dispatch_slots",
        compiler_params=_params("parallel"),
    )(e_idx, rank, pstart_row)


def _row_copy(src, row, dst, dst_row, sem):
    return pltpu.make_async_copy(src.at[pl.ds(row, 1), :], dst.at[pl.ds(dst_row, 1), :], sem)


def _dispatch_kernel(pos_ref, cnt_ref, pst_ref, nused_ref, h_ref, x_hbm, stage_ref, zero_ref, sem, zsem):
    i = pl.program_id(0)
    tm = h_ref.shape[0]
    n_blk = x_hbm.shape[0] // EXPERT_BLOCK
    nused = nused_ref[0]
    slot = i % 2

    def wait_tile(s):
        for _ in range(TOP_K):
            pltpu.make_async_copy(stage_ref.at[s], x_hbm.at[pl.ds(0, tm), :], sem.at[s]).wait()

    @pl.when(i >= 2)
    def _():
        wait_tile(slot)

    stage_ref[slot] = h_ref[...]
    for r in range(tm):
        for kk in range(TOP_K):
            _row_copy(stage_ref.at[slot], r, x_hbm, pos_ref[(i * tm + r) * TOP_K + kk],
                      sem.at[slot]).start(priority=(r * TOP_K + kk) % 2)

    @pl.when(i == 0)
    def _():
        zero_ref[...] = jnp.zeros_like(zero_ref)

        def pad_expert(e, total):
            lo = pst_ref[e] + cnt_ref[e]
            hi = pst_ref[e] + (cnt_ref[e] + EXPERT_BLOCK - 1) // EXPERT_BLOCK * EXPERT_BLOCK

            def pad_row(s, _):
                _row_copy(zero_ref, 0, x_hbm, s, zsem).start()
                return 0

            lax.fori_loop(lo, hi, pad_row, 0)
            return total + (hi - lo)

        n_pad = lax.fori_loop(0, cnt_ref.shape[0], pad_expert, 0)

        def tail_copy(blk):
            return pltpu.make_async_copy(zero_ref, x_hbm.at[pl.ds(blk * EXPERT_BLOCK, EXPERT_BLOCK), :], zsem)

        def tail_start(blk, _):
            tail_copy(blk).start()
            return 0

        def pad_wait(_, c):
            _row_copy(zero_ref, 0, x_hbm, 0, zsem).wait()
            return c

        def tail_wait(blk, _):
            tail_copy(blk).wait()
            return 0

        lax.fori_loop(nused, n_blk, tail_start, 0)
        lax.fori_loop(0, n_pad, pad_wait, 0)
        lax.fori_loop(nused, n_blk, tail_wait, 0)

    @pl.when(i == pl.num_programs(0) - 1)
    def _():
        @pl.when(i >= 1)
        def _():
            wait_tile(1 - slot)

        wait_tile(slot)


def _dispatch(pos_flat, counts, pstart, nused, h2, n_slots):
    n, d = h2.shape
    tm = 128
    grid_spec = pltpu.PrefetchScalarGridSpec(
        num_scalar_prefetch=4,
        grid=(n // tm,),
        in_specs=[pl.BlockSpec((tm, d), lambda i, *_: (i, 0))],
        out_specs=pl.BlockSpec(memory_space=pl.ANY),
        scratch_shapes=[pltpu.VMEM((2, tm, d), F32), pltpu.VMEM((EXPERT_BLOCK, d), F32),
                        pltpu.SemaphoreType.DMA((2,)), pltpu.SemaphoreType.DMA(())],
    )
    return pl.pallas_call(
        _dispatch_kernel,
        grid_spec=grid_spec,
        out_shape=jax.ShapeDtypeStruct((n_slots, d), F32),
        name="dispatch_rows",
        compiler_params=_params("arbitrary"),
    )(pos_flat, counts, pstart, nused, h2)


EXPERT_X_SLOTS = 4
EXPERT_Y_SLOTS = 3


def _expert_kernel(bstart_ref, bcount_ref, nused_ref, x_hbm, wg_ref, wu_ref, wd_ref, y_hbm,
                   xs_ref, yo_ref, wgb_ref, wub_ref, wdb_ref, xsem, osem):
    e = pl.program_id(0)
    nused = nused_ref[0]
    rows = EXPERT_BLOCK
    n_blk = y_hbm.shape[0] // rows
    nx, ny = EXPERT_X_SLOTS, EXPERT_Y_SLOTS
    first = bstart_ref[e]
    count = bcount_ref[e]

    def x_copy(blk):
        src = x_hbm.at[pl.ds(jnp.minimum(blk, n_blk - 1) * rows, rows), :]
        return pltpu.make_async_copy(src, xs_ref.at[blk % nx], xsem.at[blk % nx])

    def out_copy(blk):
        return pltpu.make_async_copy(yo_ref.at[blk % ny], y_hbm.at[pl.ds(blk * rows, rows), :], osem.at[blk % ny])

    @pl.when(e == 0)
    def _():
        for j in range(nx - 1):
            x_copy(j).start()

    @pl.when(count > 0)
    def _():
        wgb_ref[...] = wg_ref[0].astype(BF16)
        wub_ref[...] = wu_ref[0].astype(BF16)
        wdb_ref[...] = wd_ref[0].astype(BF16)

    def block(j, _):
        blk = first + j

        @pl.when(blk >= ny)
        def _():
            out_copy(blk - ny).wait()

        x_copy(blk).wait()
        x_copy(blk + nx - 1).start()
        xb = xs_ref[blk % nx].astype(BF16)
        hg = jnp.dot(xb, wgb_ref[...], preferred_element_type=F32)
        hu = jnp.dot(xb, wub_ref[...], preferred_element_type=F32)
        hid = hg * _sigmoid(hg) * hu
        yo_ref[blk % ny] = jnp.dot(hid.astype(BF16), wdb_ref[...], preferred_element_type=F32)
        out_copy(blk).start()

        @pl.when(blk == nused - 1)
        def _():
            for ahead in range(1, nx):
                x_copy(blk + ahead).wait()

        return 0

    lax.fori_loop(0, count, block, 0)

    @pl.when(e == pl.num_programs(0) - 1)
    def _():
        for back in range(1, ny + 1):
            @pl.when(nused - back >= 0)
            def _():
                out_copy(nused - back).wait()

        yo_ref[0] = jnp.zeros(yo_ref.shape[1:], F32)

        def zero_copy(blk):
            return pltpu.make_async_copy(yo_ref.at[0], y_hbm.at[pl.ds(blk * rows, rows), :], osem.at[0])

        def fill(blk, _):
            zero_copy(blk).start()
            return 0

        def drain(blk, _):
            zero_copy(blk).wait()
            return 0

        lax.fori_loop(nused, n_blk, fill, 0)
        lax.fori_loop(nused, n_blk, drain, 0)


def _experts(x_buf, bstart, bcount, nused, w_gate_e, w_up_e, w_down_e):
    n_slots, d = x_buf.shape
    n_exp, _, f = w_gate_e.shape
    grid_spec = pltpu.PrefetchScalarGridSpec(
        num_scalar_prefetch=3,
        grid=(n_exp,),
        in_specs=[pl.BlockSpec(memory_space=pl.ANY),
                  pl.BlockSpec((1, d, f), lambda e, *_: (e, 0, 0)),
                  pl.BlockSpec((1, d, f), lambda e, *_: (e, 0, 0)),
                  pl.BlockSpec((1, f, d), lambda e, *_: (e, 0, 0))],
        out_specs=pl.BlockSpec(memory_space=pl.ANY),
        scratch_shapes=[pltpu.VMEM((EXPERT_X_SLOTS, EXPERT_BLOCK, d), F32),
                        pltpu.VMEM((EXPERT_Y_SLOTS, EXPERT_BLOCK, d), F32),
                        pltpu.VMEM((d, f), BF16), pltpu.VMEM((d, f), BF16), pltpu.VMEM((f, d), BF16),
                        pltpu.SemaphoreType.DMA((EXPERT_X_SLOTS,)), pltpu.SemaphoreType.DMA((EXPERT_Y_SLOTS,))],
    )
    return pl.pallas_call(
        _expert_kernel,
        grid_spec=grid_spec,
        out_shape=jax.ShapeDtypeStruct((n_slots, d), F32),
        name="experts",
        compiler_params=_params("arbitrary"),
    )(bstart, bcount, nused, x_buf, w_gate_e, w_up_e, w_down_e)


def _combine_kernel(pos_ref, y_hbm, wt_ref, x1_ref, gt_ref, gpost_ref, o_ref, rows_ref, sem):
    i = pl.program_id(0)
    nsteps = pl.num_programs(0)
    tm = x1_ref.shape[0]

    def start_rows(step, slot):
        for r in range(tm):
            for kk in range(TOP_K):
                _row_copy(y_hbm, pos_ref[(step * tm + r) * TOP_K + kk], rows_ref.at[slot, kk], r,
                          sem.at[slot]).start(priority=(r * TOP_K + kk) % 2)

    def wait_rows(slot):
        for kk in range(TOP_K):
            pltpu.make_async_copy(y_hbm.at[pl.ds(0, tm), :], rows_ref.at[slot, kk], sem.at[slot]).wait()

    @pl.when(i == 0)
    def _():
        start_rows(0, 0)

    slot = i % 2
    wait_rows(slot)
    start_rows(jnp.minimum(i + 1, nsteps - 1), 1 - slot)
    wt = wt_ref[...]
    y = rows_ref[slot, 0] * wt[:, 0:1] + rows_ref[slot, 1] * wt[:, 1:2]
    o_ref[...] = x1_ref[...] + gt_ref[...] * _rms(y, gpost_ref[...])

    @pl.when(i == nsteps - 1)
    def _():
        wait_rows(1 - slot)


def _combine(pos, y_buf, wts, x1, gt2, g_post):
    n, d = x1.shape
    tm = 128
    vec = pl.BlockSpec((1, d), lambda i, p: (0, 0))
    grid_spec = pltpu.PrefetchScalarGridSpec(
        num_scalar_prefetch=1,
        grid=(n // tm,),
        in_specs=[pl.BlockSpec(memory_space=pl.ANY),
                  pl.BlockSpec((tm, LANES), lambda i, p: (i, 0)),
                  pl.BlockSpec((tm, d), lambda i, p: (i, 0)), vec, vec],
        out_specs=pl.BlockSpec((tm, d), lambda i, p: (i, 0)),
        scratch_shapes=[pltpu.VMEM((2, TOP_K, tm, d), F32), pltpu.SemaphoreType.DMA((2,))],
    )
    return pl.pallas_call(
        _combine_kernel,
        grid_spec=grid_spec,
        out_shape=jax.ShapeDtypeStruct((n, d), F32),
        name="combine",
        compiler_params=_params("arbitrary"),
    )(pos, y_buf, wts, x1, gt2, g_post)


def _segment_tables(counts_row):
    counts = counts_row[0, :N_EXPERTS]
    pcounts = (counts + EXPERT_BLOCK - 1) // EXPERT_BLOCK * EXPERT_BLOCK
    pend = jnp.cumsum(pcounts)
    pstart = pend - pcounts
    nused = (pend[-1] // EXPERT_BLOCK).astype(I32)
    pstart_row = jnp.pad(pstart, (0, LANES - N_EXPERTS)).reshape(1, LANES)
    return counts, pstart, pstart_row, pstart // EXPERT_BLOCK, pcounts // EXPERT_BLOCK, nused.reshape(1)


def _rope_tables(s):
    half = ROPE_DIM // 2
    inv = ROPE_THETA ** (-jnp.arange(half, dtype=F32) / half)
    ang = jnp.arange(s, dtype=F32)[:, None] * inv[None, :]
    cos, sin = lax.optimization_barrier((jnp.cos(ang), jnp.sin(ang)))
    pad = jnp.zeros((s, LANES - ROPE_DIM), F32)
    zero = jnp.zeros((s, half), F32)
    cos_t = jnp.concatenate([cos, cos, pad + 1.0], axis=1)
    sin1_t = jnp.concatenate([-sin, zero, pad], axis=1)
    sin2_t = jnp.concatenate([zero, sin, pad], axis=1)
    return cos_t, sin1_t, sin2_t


def _layer(x, c_col, w_ada, b_ada, g_pre_mix, g_post_mix, g_pre_ffn, g_post_ffn, w_in, mu_r, mu_k, mu_v,
           mu_w, mu_a, mu_g, w0, w_w1, w_w2, a0, w_a1, w_a2, w_g1, w_g2, k_k, k_a, r_k, gn_w, gn_b,
           w_up_att, w_up_rwkv, w_o, w_rg, b_rg, w_re, b_re, w_gate_e, w_up_e, w_down_e):
    s, d = x.shape
    att_w = ATT_HEADS * ATT_HEAD_DIM
    rwkv_w = w_up_rwkv.shape[0]
    row = lambda a: a.reshape(1, -1)

    ada = _ada(c_col, w_ada, row(b_ada))
    sh1, sc1, gt1, sh2, sc2, gt2 = (ada[:, i * d:(i + 1) * d] for i in range(6))

    h, lw, la, lg = _prenorm(x, row(g_pre_mix), sc1, sh1, row(mu_w), row(mu_a), row(mu_g), w_w1, w_a1, w_g1)
    proj = _matmul(h, w_in, BF16)

    q_aug_t, k_aug, v_t = _rope_gate(proj, *_rope_tables(s))
    o_att = _attention(q_aug_t, k_aug, v_t)

    o_rwkv = _rwkv(proj, lw, la, lg, w_w2, w_a2, w_g2, row(mu_r), row(mu_k), row(mu_v), row(w0), row(a0),
                   row(k_k), row(k_a), row(r_k), row(gn_w), row(gn_b), col0=3 * att_w, width=rwkv_w)

    mix = _merge(o_att, o_rwkv, proj, w_up_att, w_up_rwkv, gate_col0=3 * att_w + 3 * rwkv_w)
    y = _matmul(mix, w_o, F32)

    w_router = jnp.pad(jnp.concatenate([w_re, w_rg], axis=1), ((0, 0), (0, LANES - N_EXPERTS - N_GROUPS)))
    b_router = jnp.pad(jnp.concatenate([b_re, b_rg]), (0, LANES - N_EXPERTS - N_GROUPS)).reshape(1, LANES)
    x1, h2, e_idx, wts, rank, counts = _route(x, y, gt1, row(g_post_mix), row(g_pre_ffn), sc2, sh2,
                                              w_router, b_router)

    n_pairs = s * TOP_K
    n_blk = (n_pairs + N_EXPERTS * (EXPERT_BLOCK - 1) + EXPERT_BLOCK - 1) // EXPERT_BLOCK
    counts, pstart, pstart_row, bstart, bcount, nused = _segment_tables(counts)
    pos = _slots(e_idx, rank, pstart_row)[:, :TOP_K].reshape(n_pairs)
    x_buf = _dispatch(pos, counts, pstart, nused, h2, n_blk * EXPERT_BLOCK)
    y_buf = _experts(x_buf, bstart, bcount, nused, w_gate_e, w_up_e, w_down_e)
    return _combine(pos, y_buf, wts, x1, gt2, row(g_post_ffn))


def kernel(x, c, w_ada, b_ada, g_pre_mix, g_post_mix, g_pre_ffn, g_post_ffn, w_in, mu_r, mu_k, mu_v, mu_w, mu_a, mu_g, w0, w_w1, w_w2, a0, w_a1, w_a2, w_g1, w_g2, k_k, k_a, r_k, gn_w, gn_b, w_up_att, w_up_rwkv, w_o, w_rg, b_rg, w_re, b_re, w_gate_e, w_up_e, w_down_e):
    b, s, d = x.shape
    assert b == 1, "one sequence per call"
    params = (w_ada, b_ada, g_pre_mix, g_post_mix, g_pre_ffn, g_post_ffn, w_in, mu_r, mu_k, mu_v, mu_w, mu_a,
              mu_g, w0, w_w1, w_w2, a0, w_a1, w_a2, w_g1, w_g2, k_k, k_a, r_k, gn_w, gn_b, w_up_att,
              w_up_rwkv, w_o, w_rg, b_rg, w_re, b_re, w_gate_e, w_up_e, w_down_e)
    xs = x.reshape(s, d)
    c_col = c.reshape(d, 1)
    for l in range(w_ada.shape[0]):
        xs = _layer(xs, c_col, *(p[l] for p in params))
    return xs.reshape(b, s, d)
```

```python
import math

import jax
import jax.numpy as jnp
from jax import lax
from jax.experimental import pallas as pl
from jax.experimental.pallas import tpu as pltpu

F32 = jnp.float32
BF16 = jnp.bfloat16
I32 = jnp.int32
HI = lax.Precision.HIGHEST

LANES = 128
SUBLANES = 8
VMEM_LIMIT = 56 * 1024 * 1024

ATT_HEADS = 8
ATT_HEAD_DIM = 128
MOBA_BLOCK = 256
MOBA_TOPK = 3
ATT_GROUP = 4
ATT_HEADS_PER_STEP = 2
ROPE_THETA = 500000.0
ROPE_DIM = ATT_HEAD_DIM // 4
RWKV_HEAD_DIM = 64
GN_EPS = 64e-5
N_GROUPS = 8
EXPERTS_PER_GROUP = 8
N_EXPERTS = N_GROUPS * EXPERTS_PER_GROUP
TOP_K = 2
EXPERT_BLOCK = 128
RMS_EPS = 1e-6
NEG = -1e30
SCAN_CHUNK = 64
ROUTE_SUB_ROWS = 256
RWKV_CHUNK_GROUP = 8
RWKV_ROWS_PER_STEP = 2048
Q_SCALE = ATT_HEAD_DIM ** -0.5 * math.log2(math.e)


def _params(*sem):
    return pltpu.CompilerParams(dimension_semantics=sem, vmem_limit_bytes=VMEM_LIMIT)


def _rms(z, g):
    return z * lax.rsqrt(jnp.mean(z * z, axis=-1, keepdims=True) + RMS_EPS) * g


def _sigmoid(z):
    return 1.0 / (1.0 + jnp.exp(-z))


def _dot(a, b):
    return jnp.dot(a.astype(BF16), b.astype(BF16), preferred_element_type=F32)


def _dot_nt(a, b):
    return lax.dot_general(a.astype(BF16), b.astype(BF16), (((1,), (1,)), ((), ())),
                           preferred_element_type=F32)


def _dot_tn(a, b):
    return lax.dot_general(a.astype(BF16), b.astype(BF16), (((0,), (0,)), ((), ())),
                           preferred_element_type=F32)


def _dot_hi(a, b):
    return jnp.dot(a, b, precision=HI, preferred_element_type=F32)


def _dot_x3(a, b):
    ah, al, _ = _split3(a)
    bh, bl, _ = _split3(b)
    return (jnp.dot(ah, bh, preferred_element_type=F32) + jnp.dot(ah, bl, preferred_element_type=F32)
            + jnp.dot(al, bh, preferred_element_type=F32))


def _split3(a):
    hi = a.astype(BF16)
    r1 = a - hi.astype(F32)
    mid = r1.astype(BF16)
    lo = (r1 - mid.astype(F32)).astype(BF16)
    return hi, mid, lo


def _dot_split(a, b01):
    b = b01.astype(BF16)
    hi, mid, _ = _split3(a)
    return jnp.dot(jnp.concatenate([hi, mid], axis=1), jnp.concatenate([b, b], axis=0),
                   preferred_element_type=F32)


def _dot_split_t(b01, a):
    b = b01.astype(BF16)
    return jnp.dot(jnp.concatenate([b, b, b], axis=1), jnp.concatenate(_split3(a), axis=0),
                   preferred_element_type=F32)


def _shift_rows(z, prev_row):
    rolled = pltpu.roll(z, 1, 0)
    row = lax.broadcasted_iota(I32, z.shape, 0)
    return jnp.where(row == 0, prev_row, rolled)


def _ada_kernel(c_ref, w_ref, b_ref, o_ref):
    o_ref[...] = jnp.sum(c_ref[...] * w_ref[...], axis=0, keepdims=True) + b_ref[...]


def _ada(c_col, w_ada, b_ada):
    d, n = w_ada.shape
    tn = 1024
    return pl.pallas_call(
        _ada_kernel,
        grid=(n // tn,),
        in_specs=[pl.BlockSpec((d, 1), lambda j: (0, 0)),
                  pl.BlockSpec((d, tn), lambda j: (0, j)),
                  pl.BlockSpec((1, tn), lambda j: (0, j))],
        out_specs=pl.BlockSpec((1, tn), lambda j: (0, j)),
        out_shape=jax.ShapeDtypeStruct((1, n), F32),
        name="ada",
        compiler_params=_params("parallel"),
    )(c_col, w_ada, b_ada)


def _prenorm_kernel(x_ref, xp_ref, g_ref, sc_ref, sh_ref, muw_ref, mua_ref, mug_ref,
                    ww1_ref, wa1_ref, wg1_ref, h_ref, lw_ref, la_ref, lg_ref):
    i = pl.program_id(0)
    g, sc, sh = g_ref[...], sc_ref[...], sh_ref[...]
    h = _rms(x_ref[...], g) * (1.0 + sc) + sh
    hp = _rms(xp_ref[SUBLANES - 1:SUBLANES, :], g) * (1.0 + sc) + sh
    hp = jnp.where(i == 0, 0.0, hp)
    dh = _shift_rows(h, hp) - h
    h_ref[...] = h.astype(BF16)
    lw_ref[...] = jnp.tanh(_dot(h + dh * muw_ref[...], ww1_ref[...]))
    la_ref[...] = _dot(h + dh * mua_ref[...], wa1_ref[...])
    lg_ref[...] = _sigmoid(_dot(h + dh * mug_ref[...], wg1_ref[...]))


def _prenorm(x, g, sc, sh, mu_w, mu_a, mu_g, w_w1, w_a1, w_g1):
    s, d = x.shape
    tm = 256
    rpb = tm // SUBLANES
    vec = pl.BlockSpec((1, d), lambda i: (0, 0))
    full = lambda a: pl.BlockSpec(a.shape, lambda i: (0, 0))
    lw, la, lg = w_w1.shape[1], w_a1.shape[1], w_g1.shape[1]
    return pl.pallas_call(
        _prenorm_kernel,
        grid=(s // tm,),
        in_specs=[pl.BlockSpec((tm, d), lambda i: (i, 0)),
                  pl.BlockSpec((SUBLANES, d), lambda i: (jnp.maximum(i * rpb - 1, 0), 0)),
                  vec, vec, vec, vec, vec, vec, full(w_w1), full(w_a1), full(w_g1)],
        out_specs=[pl.BlockSpec((tm, d), lambda i: (i, 0)),
                   pl.BlockSpec((tm, lw), lambda i: (i, 0)),
                   pl.BlockSpec((tm, la), lambda i: (i, 0)),
                   pl.BlockSpec((tm, lg), lambda i: (i, 0))],
        out_shape=[jax.ShapeDtypeStruct((s, d), BF16),
                   jax.ShapeDtypeStruct((s, lw), F32),
                   jax.ShapeDtypeStruct((s, la), F32),
                   jax.ShapeDtypeStruct((s, lg), F32)],
        name="prenorm_lora",
        compiler_params=_params("parallel"),
    )(x, x, g, sc, sh, mu_w, mu_a, mu_g, w_w1, w_a1, w_g1)


def _mm_kernel(a_ref, w_ref, o_ref, wb_ref):
    @pl.when(pl.program_id(1) == 0)
    def _():
        wb_ref[...] = w_ref[...].astype(BF16)

    o_ref[...] = jnp.dot(a_ref[...], wb_ref[...], preferred_element_type=F32).astype(o_ref.dtype)


def _matmul(a, w, out_dtype, tm=512, tn=1024):
    m, k = a.shape
    n = w.shape[1]
    tn = min(tn, n)
    return pl.pallas_call(
        _mm_kernel,
        grid=(n // tn, m // tm),
        in_specs=[pl.BlockSpec((tm, k), lambda j, i: (i, 0)),
                  pl.BlockSpec((k, tn), lambda j, i: (0, j))],
        out_specs=pl.BlockSpec((tm, tn), lambda j, i: (i, j)),
        out_shape=jax.ShapeDtypeStruct((m, n), out_dtype),
        scratch_shapes=[pltpu.VMEM((k, tn), BF16)],
        name="matmul",
        compiler_params=_params("arbitrary", "arbitrary"),
    )(a, w)


def _rope_gate_kernel(p_ref, c_ref, s1_ref, s2_ref, qa_ref, ka_ref, vt_ref, km_ref):
    i = pl.program_id(0)
    bs = MOBA_BLOCK
    nbp = km_ref.shape[1]

    @pl.when(i == 0)
    def _():
        km_ref[...] = jnp.zeros_like(km_ref)

    c, s1, s2 = c_ref[...], s1_ref[...], s2_ref[...]

    def rope(z):
        return z * c + pltpu.roll(z, LANES - ROPE_DIM // 2, 1) * s1 + pltpu.roll(z, ROPE_DIM // 2, 1) * s2

    row = lax.broadcasted_iota(I32, (nbp, bs), 0)
    lane = lax.broadcasted_iota(I32, (bs, LANES), 1)
    onehot = jnp.where(lane == i, 1.0, 0.0).astype(BF16)
    for h in range(ATT_HEADS):
        q = rope(p_ref[:, h * LANES:(h + 1) * LANES].astype(F32))
        k = rope(p_ref[:, (ATT_HEADS + h) * LANES:(ATT_HEADS + h + 1) * LANES].astype(F32))
        g = lax.dot_general(km_ref[h], q, (((1,), (1,)), ((), ())), precision=HI, preferred_element_type=F32)
        g = jnp.where(row < i, g, NEG)
        sel_t = jnp.zeros(g.shape, F32)
        for _ in range(MOBA_TOPK):
            mx = jnp.max(g, axis=0, keepdims=True)
            idx = jnp.min(jnp.where(g == mx, row, nbp), axis=0, keepdims=True)
            hit = row == idx
            sel_t = jnp.where(hit & (row < i), 1.0, sel_t)
            g = jnp.where(hit, -jnp.inf, g)
        if nbp < LANES:
            sel_t = jnp.concatenate([sel_t, jnp.zeros((LANES - nbp, bs), F32)], axis=0)
        w = 2 * LANES
        qa_ref[h, :LANES, :] = (q * Q_SCALE).T.astype(BF16)
        qa_ref[h, LANES:, :] = jnp.where(sel_t > 0.5, 0.0, NEG).astype(BF16)
        ka_ref[:, h * w:h * w + LANES] = k.astype(BF16)
        ka_ref[:, h * w + LANES:(h + 1) * w] = onehot
        v = p_ref[:, (2 * ATT_HEADS + h) * LANES:(2 * ATT_HEADS + h + 1) * LANES].astype(F32)
        vt_ref[h, 0] = v.T.astype(BF16)
        km_ref[h, pl.ds(i, 1), :] = jnp.mean(k, axis=0, keepdims=True)


def _rope_gate(proj, cos_t, sin1_t, sin2_t):
    s = proj.shape[0]
    nb = s // MOBA_BLOCK
    assert nb <= LANES
    nbp = -(-nb // SUBLANES) * SUBLANES
    w_in = 3 * ATT_HEADS * ATT_HEAD_DIM
    w_out = 2 * ATT_HEADS * LANES
    tab = pl.BlockSpec((MOBA_BLOCK, LANES), lambda i: (i, 0))
    return pl.pallas_call(
        _rope_gate_kernel,
        grid=(nb,),
        in_specs=[pl.BlockSpec((MOBA_BLOCK, w_in), lambda i: (i, 0)), tab, tab, tab],
        out_specs=[pl.BlockSpec((ATT_HEADS, 2 * LANES, MOBA_BLOCK), lambda i: (0, 0, i)),
                   pl.BlockSpec((MOBA_BLOCK, w_out), lambda i: (i, 0)),
                   pl.BlockSpec((ATT_HEADS, 1, LANES, MOBA_BLOCK), lambda i: (0, i, 0, 0))],
        out_shape=[jax.ShapeDtypeStruct((ATT_HEADS, 2 * LANES, s), BF16),
                   jax.ShapeDtypeStruct((s, w_out), BF16),
                   jax.ShapeDtypeStruct((ATT_HEADS, nb, LANES, MOBA_BLOCK), BF16)],
        scratch_shapes=[pltpu.VMEM((ATT_HEADS, nbp, LANES), F32)],
        name="rope_gate",
        compiler_params=_params("arbitrary"),
    )(proj, cos_t, sin1_t, sin2_t)


def _attn_kernel(qa_ref, ka_ref, vt_ref, o_ref, s_ref):
    qi = pl.program_id(1)
    bs = MOBA_BLOCK
    grp = ATT_GROUP * bs
    w = 2 * LANES
    heads = range(ATT_HEADS_PER_STEP)

    n_groups = ka_ref.shape[0] // grp
    assert n_groups % 2 == 0

    def issue_scores(g, buf):
        base = pl.multiple_of(jnp.minimum(g, n_groups - 1) * grp, grp)
        for h in heads:
            s_ref[buf, h] = jnp.dot(ka_ref[pl.ds(base, grp), h * w:(h + 1) * w], qa_ref[h],
                                    preferred_element_type=F32)

    issue_scores(0, 0)

    own = pl.multiple_of(qi * bs, bs)
    k_i = lax.broadcasted_iota(I32, (bs, bs), 0)
    q_i = lax.broadcasted_iota(I32, (bs, bs), 1)
    carry = []
    for h in heads:
        s = jnp.dot(ka_ref[pl.ds(own, bs), h * w:h * w + LANES], qa_ref[h, :LANES, :],
                    preferred_element_type=F32)
        s = jnp.where(k_i <= q_i, s, NEG)
        m = jnp.max(s, axis=0, keepdims=True)
        p = jnp.exp2(s - m)
        l = jnp.sum(p, axis=0, keepdims=True)
        acc = jnp.dot(vt_ref[h, qi], p.astype(BF16), preferred_element_type=F32)
        carry += [m, l, acc]

    def absorb(g, buf, carry):
        out = []
        for h in heads:
            m, l, acc = carry[3 * h:3 * h + 3]
            s = s_ref[buf, h]
            mn = jnp.maximum(m, jnp.max(s, axis=0, keepdims=True))
            alpha = jnp.exp2(m - mn)
            p = jnp.exp2(s - mn)
            l = alpha * l + jnp.sum(p, axis=0, keepdims=True)
            p = p.astype(BF16)
            acc = alpha * acc
            for jb in range(ATT_GROUP):
                acc = acc + jnp.dot(vt_ref[h, g * ATT_GROUP + jb], p[jb * bs:(jb + 1) * bs],
                                    preferred_element_type=F32)
            out += [mn, l, acc]
        return out

    def body(t, carry):
        issue_scores(2 * t + 1, 1)
        carry = absorb(2 * t, 0, carry)
        issue_scores(2 * t + 2, 0)
        return tuple(absorb(2 * t + 1, 1, carry))

    n_used = (qi + ATT_GROUP - 1) // ATT_GROUP
    carry = lax.fori_loop(0, (n_used + 1) // 2, body, tuple(carry))
    for h in heads:
        m, l, acc = carry[3 * h:3 * h + 3]
        o_ref[:, h * LANES:(h + 1) * LANES] = (acc / l).T.astype(BF16)


def _attention(q_aug_t, k_aug, v_t):
    s = k_aug.shape[0]
    assert s % (ATT_GROUP * MOBA_BLOCK) == 0
    nb = s // MOBA_BLOCK
    hps = ATT_HEADS_PER_STEP
    once = pl.Buffered(1)
    return pl.pallas_call(
        _attn_kernel,
        grid=(ATT_HEADS // hps, nb),
        in_specs=[pl.BlockSpec((hps, 2 * LANES, MOBA_BLOCK), lambda h, i: (h, 0, i)),
                  pl.BlockSpec((s, hps * 2 * LANES), lambda h, i: (0, h), pipeline_mode=once),
                  pl.BlockSpec((hps, nb, LANES, MOBA_BLOCK), lambda h, i: (h, 0, 0, 0), pipeline_mode=once)],
        out_specs=pl.BlockSpec((MOBA_BLOCK, hps * LANES), lambda h, i: (i, h)),
        out_shape=jax.ShapeDtypeStruct((s, ATT_HEADS * ATT_HEAD_DIM), BF16),
        scratch_shapes=[pltpu.VMEM((2, hps, ATT_GROUP * MOBA_BLOCK, MOBA_BLOCK), F32)],
        name="moba_attention",
        compiler_params=_params("parallel", "arbitrary"),
    )(q_aug_t, k_aug, v_t)


def _rwkv_kernel(r_ref, k_ref, v_ref, lw_ref, la_ref, lg_ref, ww2_ref, wa2_ref, wg2_ref,
                 mur_ref, muk_ref, muv_ref, w0_ref, a0_ref, kk_ref, ka_ref, rk_ref, gnw_ref, gnb_ref,
                 o_ref, st_ref, prev_ref):
    j = pl.program_id(1)
    t = SCAN_CHUNK
    n = RWKV_HEAD_DIM
    ts = r_ref.shape[0]
    chunks = range(ts // t)

    @pl.when(j == 0)
    def _():
        st_ref[...] = jnp.zeros_like(st_ref)
        prev_ref[...] = jnp.zeros_like(prev_ref)

    rp, kp, vp = r_ref[...].astype(F32), k_ref[...].astype(F32), v_ref[...].astype(F32)
    r = rp + (_shift_rows(rp, prev_ref[0:1, :]) - rp) * mur_ref[...]
    k = kp + (_shift_rows(kp, prev_ref[1:2, :]) - kp) * muk_ref[...]
    v = vp + (_shift_rows(vp, prev_ref[2:3, :]) - vp) * muv_ref[...]
    prev_ref[0:1, :] = rp[ts - 1:ts, :]
    prev_ref[1:2, :] = kp[ts - 1:ts, :]
    prev_ref[2:3, :] = vp[ts - 1:ts, :]

    d = w0_ref[...] + _dot(lw_ref[...], ww2_ref[...])
    logw = -math.exp(-0.5) * _sigmoid(d)
    a = _sigmoid(a0_ref[...] + _dot(la_ref[...], wa2_ref[...]))
    g = _dot(lg_ref[...], wg2_ref[...])

    li = lax.broadcasted_iota(I32, (LANES, LANES), 0)
    lj = lax.broadcasted_iota(I32, (LANES, LANES), 1)
    same_head = (li // n) == (lj // n)
    head_sum = jnp.where(same_head, 1.0, 0.0)

    kk = k * kk_ref[...]
    kk = kk / jnp.maximum(jnp.sqrt(_dot_split(kk * kk, head_sum)), 1e-12)
    kt = k * (1.0 + (a - 1.0) * ka_ref[...])
    bonus = _dot_split(r * kt * rk_ref[...], head_sum) * v

    same_blk = (li // t) == (lj // t)
    m_strict = same_blk & (lj < li)
    m_incl = same_blk & (lj <= li)
    ti = lax.broadcasted_iota(I32, (t, t), 0)
    tj = lax.broadcasted_iota(I32, (t, t), 1)
    tril_incl = jnp.where(tj <= ti, 1.0, 0.0)
    lane_a = lax.broadcasted_iota(I32, (t, LANES), 1) < n
    eye = jnp.where(li == lj, 1.0, 0.0)

    def stack_masked(z):
        return jnp.concatenate([jnp.where(lane_a, z, 0.0), jnp.where(lane_a, 0.0, z)], axis=0)

    def stack_plain(z):
        return jnp.concatenate([z, z], axis=0)

    def rows(z, c):
        return z[c * t:(c + 1) * t]

    cum = jnp.concatenate([_dot_split_t(tril_incl, rows(logw, c)) for c in chunks], axis=0)
    g_t = jnp.exp(cum)
    g_inv = jnp.exp(-cum)
    xa_f = -kk * jnp.exp(cum - logw)
    xr_f = r * g_t
    yb_f = kk * a * g_inv
    yk_f = kt * g_inv

    zeros = jnp.zeros((2 * t, LANES), F32)
    state = [st_ref[...]]
    ys = {}

    def chain_step(c, r_hat, y_hat, pm, qm):
        def run():
            st = state[0]
            y2 = _dot_nt(r_hat, st) + y_hat
            ys[c] = y2[:t] + y2[t:]
            state[0] = (st + _dot(st, pm) + qm) * g_t[(c + 1) * t - 1:(c + 1) * t, :]
        return run

    def chunk_local(grp, between):
        idx = range(len(grp))
        xa = [stack_masked(rows(xa_f, c)) for c in grp]
        xr = [stack_masked(rows(xr_f, c)) for c in grp]
        yb = [stack_plain(rows(yb_f, c)) for c in grp]
        vs = [stack_masked(rows(v, c)) for c in grp]
        ybk = [jnp.concatenate([yb[i], stack_plain(rows(yk_f, c))], axis=0) for i, c in enumerate(grp)]
        sc = [_dot_nt(jnp.concatenate([xa[i], xr[i]], axis=0), ybk[i]) for i in idx]
        between()
        a_ab = [jnp.where(m_strict, sc[i][:2 * t, :2 * t], 0.0) for i in idx]
        a_ak = [jnp.where(m_strict, sc[i][:2 * t, 2 * t:], 0.0) for i in idx]
        a_rb = [jnp.where(m_incl, sc[i][2 * t:, :2 * t], 0.0) for i in idx]
        a_rk = [jnp.where(m_incl, sc[i][2 * t:, 2 * t:], 0.0) for i in idx]
        inv = [eye + a_ab[i] for i in idx]
        pw = [_dot(a_ab[i], a_ab[i]) for i in idx]
        for _ in range(int(math.log2(t)) - 2):
            between()
            both = [_dot(pw[i], jnp.concatenate([pw[i], inv[i]], axis=1)) for i in idx]
            pw = [both[i][:, :LANES] for i in idx]
            inv = [inv[i] + both[i][:, LANES:] for i in idx]
        between()
        inv = [inv[i] + _dot(pw[i], inv[i]) for i in idx]
        av = [_dot(a_ak[i], vs[i]) for i in idx]
        between()
        mw = [_dot(inv[i], jnp.concatenate([xa[i], av[i]], axis=1)) for i in idx]
        between()
        rw = [_dot(jnp.concatenate([a_rb[i], a_rk[i]], axis=1),
                   jnp.concatenate([mw[i], jnp.concatenate([zeros, vs[i]], axis=1)], axis=0)) for i in idx]
        between()
        pm = [jnp.where(same_head, _dot_tn(mw[i][:, :LANES], yb[i]), 0.0) for i in idx]
        qm = [jnp.where(same_head, _dot_tn(jnp.concatenate([mw[i][:, LANES:], vs[i]], axis=0), ybk[i]), 0.0)
              for i in idx]
        return [chain_step(c, xr[i] + rw[i][:, :LANES], rw[i][:, LANES:], pm[i], qm[i])
                for i, c in enumerate(grp)]

    pending = []

    def between():
        if pending:
            pending.pop(0)()

    group = RWKV_CHUNK_GROUP
    for g0 in range(0, len(chunks), group):
        steps = chunk_local(list(chunks[g0:g0 + group]), between)
        while pending:
            pending.pop(0)()
        pending.extend(steps)
    while pending:
        pending.pop(0)()
    st_ref[...] = state[0]
    y = jnp.concatenate([ys[c] for c in chunks], axis=0)

    mean = _dot_split(y, head_sum) * (1.0 / n)
    yc = y - mean
    var = _dot_split(yc * yc, head_sum) * (1.0 / n)
    yn = yc * lax.rsqrt(var + GN_EPS) * gnw_ref[...] + gnb_ref[...]
    o_ref[...] = ((yn + bonus) * g).astype(BF16)


def _rwkv(proj, lw, la, lg, w_w2, w_a2, w_g2, mu_r, mu_k, mu_v, w0, a0, k_k, k_a, r_k, gn_w, gn_b,
          col0, width):
    s = proj.shape[0]
    ts = min(RWKV_ROWS_PER_STEP, s)
    npair = width // LANES
    cb = col0 // LANES
    row = lambda a: pl.BlockSpec((ts, a.shape[1]), lambda p, j: (j, 0))
    wcol = lambda a: pl.BlockSpec((a.shape[0], LANES), lambda p, j: (0, p))
    vec = pl.BlockSpec((1, LANES), lambda p, j: (0, p))
    return pl.pallas_call(
        _rwkv_kernel,
        grid=(npair, s // ts),
        in_specs=[pl.BlockSpec((ts, LANES), lambda p, j: (j, cb + p)),
                  pl.BlockSpec((ts, LANES), lambda p, j: (j, cb + npair + p)),
                  pl.BlockSpec((ts, LANES), lambda p, j: (j, cb + 2 * npair + p)),
                  row(lw), row(la), row(lg), wcol(w_w2), wcol(w_a2), wcol(w_g2)] + [vec] * 10,
        out_specs=pl.BlockSpec((ts, LANES), lambda p, j: (j, p)),
        out_shape=jax.ShapeDtypeStruct((s, width), BF16),
        scratch_shapes=[pltpu.VMEM((LANES, LANES), F32), pltpu.VMEM((SUBLANES, LANES), F32)],
        name="rwkv7_scan",
        compiler_params=_params("parallel", "arbitrary"),
    )(proj, proj, proj, lw, la, lg, w_w2, w_a2, w_g2, mu_r, mu_k, mu_v, w0, a0, k_k, k_a, r_k, gn_w, gn_b)


def _route_kernel(x_ref, oa_ref, or_ref, ga_ref, gr_ref, wua_ref, wur_ref, wo_ref,
                  gt_ref, gpost_ref, gpre_ref, sc_ref, sh_ref, wr_ref, br_ref,
                  x1_ref, h2_ref, ei_ref, wt_ref, rk_ref, cnt_ref, run_ref):
    @pl.when(pl.program_id(0) == 0)
    def _():
        run_ref[...] = jnp.zeros_like(run_ref)

    sub = ROUTE_SUB_ROWS
    spans = [slice(r0, r0 + sub) for r0 in range(0, x_ref.shape[0], sub)]

    def project(rs):
        ua = jnp.dot(oa_ref[rs, :], wua_ref[...], preferred_element_type=F32)
        ur = jnp.dot(or_ref[rs, :], wur_ref[...], preferred_element_type=F32)
        mix = _sigmoid(ga_ref[rs, :].astype(F32)) * ua + _sigmoid(gr_ref[rs, :].astype(F32)) * ur
        return jnp.dot(mix.astype(BF16), wo_ref[...], preferred_element_type=F32)

    ys = [project(rs) for rs in spans]
    for rs, y in zip(spans, ys):
        _route_rows(rs, y, x_ref, gt_ref, gpost_ref, gpre_ref, sc_ref, sh_ref, wr_ref, br_ref,
                    x1_ref, h2_ref, ei_ref, wt_ref, rk_ref, run_ref)
    cnt_ref[...] = run_ref[...].astype(I32)


def _route_rows(rs, y, x_ref, gt_ref, gpost_ref, gpre_ref, sc_ref, sh_ref, wr_ref, br_ref,
                x1_ref, h2_ref, ei_ref, wt_ref, rk_ref, run_ref):
    x1 = x_ref[rs, :] + gt_ref[...] * _rms(y, gpost_ref[...])
    x1_ref[rs, :] = x1
    h2 = _rms(x1, gpre_ref[...]) * (1.0 + sc_ref[...]) + sh_ref[...]
    h2_ref[rs, :] = h2
    logits = _dot_x3(h2, wr_ref[...]) + br_ref[...]
    lane = lax.broadcasted_iota(I32, logits.shape, 1)
    big = jnp.int32(4 * LANES)
    gmask = (lane >= N_EXPERTS) & (lane < N_EXPERTS + N_GROUPS)
    mg = jnp.max(jnp.where(gmask, logits, -jnp.inf), axis=1, keepdims=True)
    eg = jnp.where(gmask, jnp.exp(logits - mg), 0.0)
    pg = eg / jnp.sum(eg, axis=1, keepdims=True)
    pg_top = jnp.max(pg, axis=1, keepdims=True)
    g_idx = jnp.min(jnp.where(gmask & (pg == pg_top), lane, big), axis=1, keepdims=True) - N_EXPERTS
    emask = (lane >= g_idx * EXPERTS_PER_GROUP) & (lane < (g_idx + 1) * EXPERTS_PER_GROUP)
    me = jnp.max(jnp.where(emask, logits, -jnp.inf), axis=1, keepdims=True)
    ee = jnp.where(emask, jnp.exp(logits - me), 0.0)
    pe = ee / jnp.sum(ee, axis=1, keepdims=True)
    p1 = jnp.max(pe, axis=1, keepdims=True)
    i1 = jnp.min(jnp.where(emask & (pe == p1), lane, big), axis=1, keepdims=True)
    rest = emask & (lane != i1)
    p2 = jnp.max(jnp.where(rest, pe, -jnp.inf), axis=1, keepdims=True)
    i2 = jnp.min(jnp.where(rest & (pe == p2), lane, big), axis=1, keepdims=True)
    den = p1 + p2
    ei_ref[rs, :] = jnp.where(lane == 0, i1, jnp.where(lane == 1, i2, 0))
    wt_ref[rs, :] = jnp.where(lane == 0, pg_top * p1 / den, jnp.where(lane == 1, pg_top * p2 / den, 0.0))
    tm = logits.shape[0]
    chosen = jnp.where((lane == i1) | (lane == i2), 1.0, 0.0)
    t_i = lax.broadcasted_iota(I32, (tm, tm), 0)
    t_j = lax.broadcasted_iota(I32, (tm, tm), 1)
    before = _dot(jnp.where(t_j < t_i, 1.0, 0.0), chosen) + run_ref[...]
    r1 = jnp.sum(jnp.where(lane == i1, before, 0.0), axis=1, keepdims=True)
    r2 = jnp.sum(jnp.where(lane == i2, before, 0.0), axis=1, keepdims=True)
    rk_ref[rs, :] = jnp.where(lane == 0, r1, jnp.where(lane == 1, r2, 0.0)).astype(I32)
    run_ref[...] += jnp.sum(chosen, axis=0, keepdims=True)


def _cast_kernel(w_ref, o_ref):
    o_ref[...] = w_ref[...].astype(o_ref.dtype)


def _to_bf16(w):
    k, n = w.shape
    tk = 512
    return pl.pallas_call(
        _cast_kernel,
        grid=(k // tk,),
        in_specs=[pl.BlockSpec((tk, n), lambda i: (i, 0))],
        out_specs=pl.BlockSpec((tk, n), lambda i: (i, 0)),
        out_shape=jax.ShapeDtypeStruct((k, n), BF16),
        name="cast_bf16",
        compiler_params=_params("parallel"),
    )(w)


def _route(x, o_att, o_rwkv, proj, gate_col0, w_up_att, w_up_rwkv, w_o, gt1, g_post, g_pre, sc2, sh2,
           w_router, b_router):
    s, d = x.shape
    ka, kr = o_att.shape[1], o_rwkv.shape[1]
    tm = 256
    gb = gate_col0 // d
    vec = pl.BlockSpec((1, d), lambda i: (0, 0))
    rowblk = pl.BlockSpec((tm, d), lambda i: (i, 0))
    small = pl.BlockSpec((tm, LANES), lambda i: (i, 0))
    lane_row = pl.BlockSpec((1, LANES), lambda i: (0, 0))
    once = pl.Buffered(1)
    return pl.pallas_call(
        _route_kernel,
        grid=(s // tm,),
        in_specs=[rowblk,
                  pl.BlockSpec((tm, ka), lambda i: (i, 0)), pl.BlockSpec((tm, kr), lambda i: (i, 0)),
                  pl.BlockSpec((tm, d), lambda i: (i, gb)), pl.BlockSpec((tm, d), lambda i: (i, gb + 1)),
                  pl.BlockSpec((ka, d), lambda i: (0, 0), pipeline_mode=once),
                  pl.BlockSpec((kr, d), lambda i: (0, 0), pipeline_mode=once),
                  pl.BlockSpec((d, d), lambda i: (0, 0), pipeline_mode=once),
                  vec, vec, vec, vec, vec, pl.BlockSpec((d, LANES), lambda i: (0, 0)), lane_row],
        out_specs=[rowblk, rowblk, small, small, small, lane_row],
        out_shape=[jax.ShapeDtypeStruct((s, d), F32), jax.ShapeDtypeStruct((s, d), F32),
                   jax.ShapeDtypeStruct((s, LANES), I32), jax.ShapeDtypeStruct((s, LANES), F32),
                   jax.ShapeDtypeStruct((s, LANES), I32), jax.ShapeDtypeStruct((1, LANES), I32)],
        scratch_shapes=[pltpu.VMEM((1, LANES), F32)],
        name="merge_out_route",
        compiler_params=_params("arbitrary"),
    )(x, o_att, o_rwkv, proj, proj, _to_bf16(w_up_att), _to_bf16(w_up_rwkv), _to_bf16(w_o),
      gt1, g_post, g_pre, sc2, sh2, w_router, b_router)


def _slots_kernel(ei_ref, rk_ref, ps_ref, pos_ref):
    lane = lax.broadcasted_iota(I32, ei_ref.shape, 1)
    ei, rk = ei_ref[...], rk_ref[...]
    ps = ps_ref[...]
    cols = []
    for kk in range(TOP_K):
        e = ei[:, kk:kk + 1]
        cols.append(jnp.sum(jnp.where(lane == e, ps, 0), axis=1, keepdims=True) + rk[:, kk:kk + 1])
    pos_ref[...] = jnp.where(lane == 0, cols[0], jnp.where(lane == 1, cols[1], 0))


def _slots(e_idx, rank, pstart_row):
    n = e_idx.shape[0]
    tm = 1024
    blk = pl.BlockSpec((tm, LANES), lambda i: (i, 0))
    return pl.pallas_call(
        _slots_kernel,
        grid=(n // tm,),
        in_specs=[blk, blk, pl.BlockSpec((1, LANES), lambda i: (0, 0))],
        out_specs=blk,
        out_shape=jax.ShapeDtypeStruct((n, LANES), I32),
        name="dispatch_slots",
        compiler_params=_params("parallel"),
    )(e_idx, rank, pstart_row)


def _row_copy(src, row, dst, dst_row, sem):
    return pltpu.make_async_copy(src.at[pl.ds(row, 1)], dst.at[pl.ds(dst_row, 1)], sem)


def _dispatch_kernel(pos_ref, cnt_ref, pst_ref, nused_ref, h_ref, x_hbm, stage_ref, zero_ref, sem, zsem):
    i = pl.program_id(0)
    tm = h_ref.shape[0]
    n_blk = x_hbm.shape[0] // EXPERT_BLOCK
    nused = nused_ref[0]
    slot = i % 2

    def wait_tile(s):
        for _ in range(TOP_K):
            pltpu.make_async_copy(stage_ref.at[s], x_hbm.at[pl.ds(0, tm)], sem.at[s]).wait()

    @pl.when(i >= 2)
    def _():
        wait_tile(slot)

    stage_ref[slot] = h_ref[...]
    for r in range(tm):
        for kk in range(TOP_K):
            _row_copy(stage_ref.at[slot], r, x_hbm, pos_ref[(i * tm + r) * TOP_K + kk],
                      sem.at[slot]).start(priority=(r * TOP_K + kk) % 2)

    @pl.when(i == 0)
    def _():
        zero_ref[...] = jnp.zeros_like(zero_ref)

        def pad_expert(e, total):
            lo = pst_ref[e] + cnt_ref[e]
            hi = pst_ref[e] + (cnt_ref[e] + EXPERT_BLOCK - 1) // EXPERT_BLOCK * EXPERT_BLOCK

            def pad_row(s, _):
                _row_copy(zero_ref, 0, x_hbm, s, zsem).start()
                return 0

            lax.fori_loop(lo, hi, pad_row, 0)
            return total + (hi - lo)

        n_pad = lax.fori_loop(0, cnt_ref.shape[0], pad_expert, 0)

        def tail_copy(blk):
            return pltpu.make_async_copy(zero_ref, x_hbm.at[pl.ds(blk * EXPERT_BLOCK, EXPERT_BLOCK)], zsem)

        def tail_start(blk, _):
            tail_copy(blk).start()
            return 0

        def pad_wait(_, c):
            _row_copy(zero_ref, 0, x_hbm, 0, zsem).wait()
            return c

        def tail_wait(blk, _):
            tail_copy(blk).wait()
            return 0

        lax.fori_loop(nused, n_blk, tail_start, 0)
        lax.fori_loop(0, n_pad, pad_wait, 0)
        lax.fori_loop(nused, n_blk, tail_wait, 0)

    @pl.when(i == pl.num_programs(0) - 1)
    def _():
        @pl.when(i >= 1)
        def _():
            wait_tile(1 - slot)

        wait_tile(slot)


def _dispatch(pos_flat, counts, pstart, nused, h2, n_slots):
    n, w = h2.shape
    tm = 128
    grid_spec = pltpu.PrefetchScalarGridSpec(
        num_scalar_prefetch=4,
        grid=(n // tm,),
        in_specs=[pl.BlockSpec((tm, w), lambda i, *_: (i, 0))],
        out_specs=pl.BlockSpec(memory_space=pl.ANY),
        scratch_shapes=[pltpu.VMEM((2, tm, w), h2.dtype), pltpu.VMEM((EXPERT_BLOCK, w), h2.dtype),
                        pltpu.SemaphoreType.DMA((2,)), pltpu.SemaphoreType.DMA(())],
    )
    return pl.pallas_call(
        _dispatch_kernel,
        grid_spec=grid_spec,
        out_shape=jax.ShapeDtypeStruct((n_slots, w), h2.dtype),
        name="dispatch_rows",
        compiler_params=_params("arbitrary"),
    )(pos_flat, counts, pstart, nused, h2)


EXPERT_X_SLOTS = 4
EXPERT_Y_SLOTS = 3


def _expert_kernel(bstart_ref, bcount_ref, nused_ref, x_hbm, wg_ref, wu_ref, wd_ref, y_hbm,
                   xs_ref, yo_ref, wgb_ref, wub_ref, wdb_ref, xsem, osem):
    e = pl.program_id(0)
    nused = nused_ref[0]
    rows = EXPERT_BLOCK
    n_blk = y_hbm.shape[0] // rows
    nx, ny = EXPERT_X_SLOTS, EXPERT_Y_SLOTS
    first = bstart_ref[e]
    count = bcount_ref[e]

    def x_copy(blk):
        src = x_hbm.at[pl.ds(jnp.minimum(blk, n_blk - 1) * rows, rows)]
        return pltpu.make_async_copy(src, xs_ref.at[blk % nx], xsem.at[blk % nx])

    def out_copy(blk):
        return pltpu.make_async_copy(yo_ref.at[blk % ny], y_hbm.at[pl.ds(blk * rows, rows), :], osem.at[blk % ny])

    @pl.when(e == 0)
    def _():
        for j in range(nx - 1):
            x_copy(j).start()

    @pl.when(count > 0)
    def _():
        wgb_ref[...] = wg_ref[0].astype(BF16)
        wub_ref[...] = wu_ref[0].astype(BF16)
        wdb_ref[...] = wd_ref[0].astype(BF16)

    def block(j, _):
        blk = first + j

        @pl.when(blk >= ny)
        def _():
            out_copy(blk - ny).wait()

        x_copy(blk).wait()
        x_copy(blk + nx - 1).start()
        xb = xs_ref[blk % nx].astype(BF16)
        hg = jnp.dot(xb, wgb_ref[...], preferred_element_type=F32)
        hu = jnp.dot(xb, wub_ref[...], preferred_element_type=F32)
        hid = hg * _sigmoid(hg) * hu
        yo_ref[blk % ny] = jnp.dot(hid.astype(BF16), wdb_ref[...], preferred_element_type=F32)
        out_copy(blk).start()

        @pl.when(blk == nused - 1)
        def _():
            for ahead in range(1, nx):
                x_copy(blk + ahead).wait()

        return 0

    lax.fori_loop(0, count, block, 0)

    @pl.when(e == pl.num_programs(0) - 1)
    def _():
        for back in range(1, ny + 1):
            @pl.when(nused - back >= 0)
            def _():
                out_copy(nused - back).wait()

        yo_ref[0] = jnp.zeros(yo_ref.shape[1:], yo_ref.dtype)

        def zero_copy(blk):
            return pltpu.make_async_copy(yo_ref.at[0], y_hbm.at[pl.ds(blk * rows, rows), :], osem.at[0])

        def fill(blk, _):
            zero_copy(blk).start()
            return 0

        def drain(blk, _):
            zero_copy(blk).wait()
            return 0

        lax.fori_loop(nused, n_blk, fill, 0)
        lax.fori_loop(nused, n_blk, drain, 0)


def _experts(x_buf, bstart, bcount, nused, w_gate_e, w_up_e, w_down_e):
    n_slots = x_buf.shape[0]
    n_exp, d, f = w_gate_e.shape
    grid_spec = pltpu.PrefetchScalarGridSpec(
        num_scalar_prefetch=3,
        grid=(n_exp,),
        in_specs=[pl.BlockSpec(memory_space=pl.ANY),
                  pl.BlockSpec((1, d, f), lambda e, *_: (e, 0, 0)),
                  pl.BlockSpec((1, d, f), lambda e, *_: (e, 0, 0)),
                  pl.BlockSpec((1, f, d), lambda e, *_: (e, 0, 0))],
        out_specs=pl.BlockSpec(memory_space=pl.ANY),
        scratch_shapes=[pltpu.VMEM((EXPERT_X_SLOTS, EXPERT_BLOCK, d), F32),
                        pltpu.VMEM((EXPERT_Y_SLOTS, EXPERT_BLOCK, d), F32),
                        pltpu.VMEM((d, f), BF16), pltpu.VMEM((d, f), BF16), pltpu.VMEM((f, d), BF16),
                        pltpu.SemaphoreType.DMA((EXPERT_X_SLOTS,)), pltpu.SemaphoreType.DMA((EXPERT_Y_SLOTS,))],
    )
    return pl.pallas_call(
        _expert_kernel,
        grid_spec=grid_spec,
        out_shape=jax.ShapeDtypeStruct((n_slots, d), F32),
        name="experts",
        compiler_params=_params("arbitrary"),
    )(bstart, bcount, nused, x_buf, w_gate_e, w_up_e, w_down_e)


def _combine_kernel(pos_ref, y_hbm, wt_ref, x1_ref, gt_ref, gpost_ref, o_ref, rows_ref, sem):
    i = pl.program_id(0)
    nsteps = pl.num_programs(0)
    tm = x1_ref.shape[0]

    def start_rows(step, slot):
        for r in range(tm):
            for kk in range(TOP_K):
                _row_copy(y_hbm, pos_ref[(step * tm + r) * TOP_K + kk], rows_ref.at[slot, kk], r,
                          sem.at[slot]).start(priority=(r * TOP_K + kk) % 2)

    def wait_rows(slot):
        for kk in range(TOP_K):
            pltpu.make_async_copy(y_hbm.at[pl.ds(0, tm), :], rows_ref.at[slot, kk], sem.at[slot]).wait()

    @pl.when(i == 0)
    def _():
        start_rows(0, 0)

    slot = i % 2
    wait_rows(slot)
    start_rows(jnp.minimum(i + 1, nsteps - 1), 1 - slot)
    wt = wt_ref[...]
    y = rows_ref[slot, 0] * wt[:, 0:1] + rows_ref[slot, 1] * wt[:, 1:2]
    o_ref[...] = x1_ref[...] + gt_ref[...] * _rms(y, gpost_ref[...])

    @pl.when(i == nsteps - 1)
    def _():
        wait_rows(1 - slot)


def _combine(pos, y_buf, wts, x1, gt2, g_post):
    n, d = x1.shape
    tm = 128
    vec = pl.BlockSpec((1, d), lambda i, p: (0, 0))
    grid_spec = pltpu.PrefetchScalarGridSpec(
        num_scalar_prefetch=1,
        grid=(n // tm,),
        in_specs=[pl.BlockSpec(memory_space=pl.ANY),
                  pl.BlockSpec((tm, LANES), lambda i, p: (i, 0)),
                  pl.BlockSpec((tm, d), lambda i, p: (i, 0)), vec, vec],
        out_specs=pl.BlockSpec((tm, d), lambda i, p: (i, 0)),
        scratch_shapes=[pltpu.VMEM((2, TOP_K, tm, d), F32), pltpu.SemaphoreType.DMA((2,))],
    )
    return pl.pallas_call(
        _combine_kernel,
        grid_spec=grid_spec,
        out_shape=jax.ShapeDtypeStruct((n, d), F32),
        name="combine",
        compiler_params=_params("arbitrary"),
    )(pos, y_buf, wts, x1, gt2, g_post)


def _segment_tables(counts_row):
    counts = counts_row[0, :N_EXPERTS]
    pcounts = (counts + EXPERT_BLOCK - 1) // EXPERT_BLOCK * EXPERT_BLOCK
    pend = jnp.cumsum(pcounts)
    pstart = pend - pcounts
    nused = (pend[-1] // EXPERT_BLOCK).astype(I32)
    pstart_row = jnp.pad(pstart, (0, LANES - N_EXPERTS)).reshape(1, LANES)
    return counts, pstart, pstart_row, pstart // EXPERT_BLOCK, pcounts // EXPERT_BLOCK, nused.reshape(1)


def _rope_tables(s):
    half = ROPE_DIM // 2
    inv = ROPE_THETA ** (-jnp.arange(half, dtype=F32) / half)
    ang = jnp.arange(s, dtype=F32)[:, None] * inv[None, :]
    cos, sin = lax.optimization_barrier((jnp.cos(ang), jnp.sin(ang)))
    pad = jnp.zeros((s, LANES - ROPE_DIM), F32)
    zero = jnp.zeros((s, half), F32)
    cos_t = jnp.concatenate([cos, cos, pad + 1.0], axis=1)
    sin1_t = jnp.concatenate([-sin, zero, pad], axis=1)
    sin2_t = jnp.concatenate([zero, sin, pad], axis=1)
    return cos_t, sin1_t, sin2_t


def _layer(x, c_col, w_ada, b_ada, g_pre_mix, g_post_mix, g_pre_ffn, g_post_ffn, w_in, mu_r, mu_k, mu_v,
           mu_w, mu_a, mu_g, w0, w_w1, w_w2, a0, w_a1, w_a2, w_g1, w_g2, k_k, k_a, r_k, gn_w, gn_b,
           w_up_att, w_up_rwkv, w_o, w_rg, b_rg, w_re, b_re, w_gate_e, w_up_e, w_down_e):
    s, d = x.shape
    att_w = ATT_HEADS * ATT_HEAD_DIM
    rwkv_w = w_up_rwkv.shape[0]
    row = lambda a: a.reshape(1, -1)

    ada = _ada(c_col, w_ada, row(b_ada))
    sh1, sc1, gt1, sh2, sc2, gt2 = (ada[:, i * d:(i + 1) * d] for i in range(6))

    h, lw, la, lg = _prenorm(x, row(g_pre_mix), sc1, sh1, row(mu_w), row(mu_a), row(mu_g), w_w1, w_a1, w_g1)
    proj = _matmul(h, w_in, BF16)

    q_aug_t, k_aug, v_t = _rope_gate(proj, *_rope_tables(s))
    o_att = _attention(q_aug_t, k_aug, v_t)

    o_rwkv = _rwkv(proj, lw, la, lg, w_w2, w_a2, w_g2, row(mu_r), row(mu_k), row(mu_v), row(w0), row(a0),
                   row(k_k), row(k_a), row(r_k), row(gn_w), row(gn_b), col0=3 * att_w, width=rwkv_w)


    w_router = jnp.pad(jnp.concatenate([w_re, w_rg], axis=1), ((0, 0), (0, LANES - N_EXPERTS - N_GROUPS)))
    b_router = jnp.pad(jnp.concatenate([b_re, b_rg]), (0, LANES - N_EXPERTS - N_GROUPS)).reshape(1, LANES)
    x1, h2, e_idx, wts, rank, counts = _route(x, o_att, o_rwkv, proj, 3 * att_w + 3 * rwkv_w, w_up_att, w_up_rwkv,
                                              w_o, gt1, row(g_post_mix), row(g_pre_ffn), sc2, sh2,
                                              w_router, b_router)

    n_pairs = s * TOP_K
    n_blk = (n_pairs + N_EXPERTS * (EXPERT_BLOCK - 1) + EXPERT_BLOCK - 1) // EXPERT_BLOCK
    counts, pstart, pstart_row, bstart, bcount, nused = _segment_tables(counts)
    pos = _slots(e_idx, rank, pstart_row)[:, :TOP_K].reshape(n_pairs)
    x_buf = _dispatch(pos, counts, pstart, nused, h2, n_blk * EXPERT_BLOCK)
    y_buf = _experts(x_buf, bstart, bcount, nused, w_gate_e, w_up_e, w_down_e)
    return _combine(pos, y_buf, wts, x1, gt2, row(g_post_ffn))


def kernel(x, c, w_ada, b_ada, g_pre_mix, g_post_mix, g_pre_ffn, g_post_ffn, w_in, mu_r, mu_k, mu_v, mu_w, mu_a, mu_g, w0, w_w1, w_w2, a0, w_a1, w_a2, w_g1, w_g2, k_k, k_a, r_k, gn_w, gn_b, w_up_att, w_up_rwkv, w_o, w_rg, b_rg, w_re, b_re, w_gate_e, w_up_e, w_down_e):
    b, s, d = x.shape
    assert b == 1, "one sequence per call"
    params = (w_ada, b_ada, g_pre_mix, g_post_mix, g_pre_ffn, g_post_ffn, w_in, mu_r, mu_k, mu_v, mu_w, mu_a,
              mu_g, w0, w_w1, w_w2, a0, w_a1, w_a2, w_g1, w_g2, k_k, k_a, r_k, gn_w, gn_b, w_up_att,
              w_up_rwkv, w_o, w_rg, b_rg, w_re, b_re, w_gate_e, w_up_e, w_down_e)
    xs = x.reshape(s, d)
    c_col = c.reshape(d, 1)
    for l in range(w_ada.shape[0]):
        xs = _layer(xs, c_col, *(p[l] for p in params))
    return xs.reshape(b, s, d)
```

```python
import math

import jax
import jax.numpy as jnp
from jax import lax
from jax.experimental import pallas as pl
from jax.experimental.pallas import tpu as pltpu

F32 = jnp.float32
BF16 = jnp.bfloat16
I32 = jnp.int32
HI = lax.Precision.HIGHEST

LANES = 128
SUBLANES = 8
VMEM_LIMIT = 56 * 1024 * 1024

ATT_HEADS = 8
ATT_HEAD_DIM = 128
MOBA_BLOCK = 256
MOBA_TOPK = 3
ATT_GROUP = 4
ATT_HEADS_PER_STEP = 2
ROPE_THETA = 500000.0
ROPE_DIM = ATT_HEAD_DIM // 4
RWKV_HEAD_DIM = 64
GN_EPS = 64e-5
N_GROUPS = 8
EXPERTS_PER_GROUP = 8
N_EXPERTS = N_GROUPS * EXPERTS_PER_GROUP
TOP_K = 2
EXPERT_BLOCK = 128
RMS_EPS = 1e-6
NEG = -1e30
SCAN_CHUNK = 64
RWKV_CHUNK_GROUP = 8
RWKV_ROWS_PER_STEP = 2048
Q_SCALE = ATT_HEAD_DIM ** -0.5 * math.log2(math.e)


def _params(*sem):
    return pltpu.CompilerParams(dimension_semantics=sem, vmem_limit_bytes=VMEM_LIMIT)


def _rms(z, g):
    return z * lax.rsqrt(jnp.mean(z * z, axis=-1, keepdims=True) + RMS_EPS) * g


def _sigmoid(z):
    return 1.0 / (1.0 + jnp.exp(-z))


def _dot(a, b):
    return jnp.dot(a.astype(BF16), b.astype(BF16), preferred_element_type=F32)


def _dot_nt(a, b):
    return lax.dot_general(a.astype(BF16), b.astype(BF16), (((1,), (1,)), ((), ())),
                           preferred_element_type=F32)


def _dot_tn(a, b):
    return lax.dot_general(a.astype(BF16), b.astype(BF16), (((0,), (0,)), ((), ())),
                           preferred_element_type=F32)


def _dot_hi(a, b):
    return jnp.dot(a, b, precision=HI, preferred_element_type=F32)


def _dot_x3(a, b):
    ah, al, _ = _split3(a)
    bh, bl, _ = _split3(b)
    return (jnp.dot(ah, bh, preferred_element_type=F32) + jnp.dot(ah, bl, preferred_element_type=F32)
            + jnp.dot(al, bh, preferred_element_type=F32))


def _split3(a):
    hi = a.astype(BF16)
    r1 = a - hi.astype(F32)
    mid = r1.astype(BF16)
    lo = (r1 - mid.astype(F32)).astype(BF16)
    return hi, mid, lo


def _dot_split(a, b01):
    b = b01.astype(BF16)
    hi, mid, _ = _split3(a)
    return jnp.dot(jnp.concatenate([hi, mid], axis=1), jnp.concatenate([b, b], axis=0),
                   preferred_element_type=F32)


def _dot_split_t(b01, a):
    b = b01.astype(BF16)
    return jnp.dot(jnp.concatenate([b, b, b], axis=1), jnp.concatenate(_split3(a), axis=0),
                   preferred_element_type=F32)


def _shift_rows(z, prev_row):
    rolled = pltpu.roll(z, 1, 0)
    row = lax.broadcasted_iota(I32, z.shape, 0)
    return jnp.where(row == 0, prev_row, rolled)


def _ada_kernel(c_ref, w_ref, b_ref, o_ref):
    o_ref[...] = jnp.sum(c_ref[...] * w_ref[...], axis=0, keepdims=True) + b_ref[...]


def _ada(c_col, w_ada, b_ada):
    d, n = w_ada.shape
    tn = 1024
    return pl.pallas_call(
        _ada_kernel,
        grid=(n // tn,),
        in_specs=[pl.BlockSpec((d, 1), lambda j: (0, 0)),
                  pl.BlockSpec((d, tn), lambda j: (0, j)),
                  pl.BlockSpec((1, tn), lambda j: (0, j))],
        out_specs=pl.BlockSpec((1, tn), lambda j: (0, j)),
        out_shape=jax.ShapeDtypeStruct((1, n), F32),
        name="ada",
        compiler_params=_params("parallel"),
    )(c_col, w_ada, b_ada)


def _prenorm_kernel(x_ref, xp_ref, g_ref, sc_ref, sh_ref, muw_ref, mua_ref, mug_ref,
                    ww1_ref, wa1_ref, wg1_ref, h_ref, lw_ref, la_ref, lg_ref):
    i = pl.program_id(0)
    g, sc, sh = g_ref[...], sc_ref[...], sh_ref[...]
    h = _rms(x_ref[...], g) * (1.0 + sc) + sh
    hp = _rms(xp_ref[SUBLANES - 1:SUBLANES, :], g) * (1.0 + sc) + sh
    hp = jnp.where(i == 0, 0.0, hp)
    dh = _shift_rows(h, hp) - h
    h_ref[...] = h.astype(BF16)
    lw_ref[...] = jnp.tanh(_dot(h + dh * muw_ref[...], ww1_ref[...]))
    la_ref[...] = _dot(h + dh * mua_ref[...], wa1_ref[...])
    lg_ref[...] = _sigmoid(_dot(h + dh * mug_ref[...], wg1_ref[...]))


def _prenorm(x, g, sc, sh, mu_w, mu_a, mu_g, w_w1, w_a1, w_g1):
    s, d = x.shape
    tm = 256
    rpb = tm // SUBLANES
    vec = pl.BlockSpec((1, d), lambda i: (0, 0))
    full = lambda a: pl.BlockSpec(a.shape, lambda i: (0, 0))
    lw, la, lg = w_w1.shape[1], w_a1.shape[1], w_g1.shape[1]
    return pl.pallas_call(
        _prenorm_kernel,
        grid=(s // tm,),
        in_specs=[pl.BlockSpec((tm, d), lambda i: (i, 0)),
                  pl.BlockSpec((SUBLANES, d), lambda i: (jnp.maximum(i * rpb - 1, 0), 0)),
                  vec, vec, vec, vec, vec, vec, full(w_w1), full(w_a1), full(w_g1)],
        out_specs=[pl.BlockSpec((tm, d), lambda i: (i, 0)),
                   pl.BlockSpec((tm, lw), lambda i: (i, 0)),
                   pl.BlockSpec((tm, la), lambda i: (i, 0)),
                   pl.BlockSpec((tm, lg), lambda i: (i, 0))],
        out_shape=[jax.ShapeDtypeStruct((s, d), BF16),
                   jax.ShapeDtypeStruct((s, lw), F32),
                   jax.ShapeDtypeStruct((s, la), F32),
                   jax.ShapeDtypeStruct((s, lg), F32)],
        name="prenorm_lora",
        compiler_params=_params("parallel"),
    )(x, x, g, sc, sh, mu_w, mu_a, mu_g, w_w1, w_a1, w_g1)


def _mm_kernel(a_ref, w_ref, o_ref, wb_ref):
    @pl.when(pl.program_id(1) == 0)
    def _():
        wb_ref[...] = w_ref[...].astype(BF16)

    o_ref[...] = jnp.dot(a_ref[...], wb_ref[...], preferred_element_type=F32).astype(o_ref.dtype)


def _matmul(a, w, out_dtype, tm=512, tn=1024):
    m, k = a.shape
    n = w.shape[1]
    tn = min(tn, n)
    return pl.pallas_call(
        _mm_kernel,
        grid=(n // tn, m // tm),
        in_specs=[pl.BlockSpec((tm, k), lambda j, i: (i, 0)),
                  pl.BlockSpec((k, tn), lambda j, i: (0, j))],
        out_specs=pl.BlockSpec((tm, tn), lambda j, i: (i, j)),
        out_shape=jax.ShapeDtypeStruct((m, n), out_dtype),
        scratch_shapes=[pltpu.VMEM((k, tn), BF16)],
        name="matmul",
        compiler_params=_params("arbitrary", "arbitrary"),
    )(a, w)


def _rope_gate_kernel(p_ref, c_ref, s1_ref, s2_ref, qa_ref, ka_ref, vt_ref, km_ref):
    i = pl.program_id(0)
    bs = MOBA_BLOCK
    nbp = km_ref.shape[1]

    @pl.when(i == 0)
    def _():
        km_ref[...] = jnp.zeros_like(km_ref)

    c, s1, s2 = c_ref[...], s1_ref[...], s2_ref[...]

    def rope(z):
        return z * c + pltpu.roll(z, LANES - ROPE_DIM // 2, 1) * s1 + pltpu.roll(z, ROPE_DIM // 2, 1) * s2

    row = lax.broadcasted_iota(I32, (nbp, bs), 0)
    lane = lax.broadcasted_iota(I32, (bs, LANES), 1)
    onehot = jnp.where(lane == i, 1.0, 0.0).astype(BF16)
    for h in range(ATT_HEADS):
        q = rope(p_ref[:, h * LANES:(h + 1) * LANES].astype(F32))
        k = rope(p_ref[:, (ATT_HEADS + h) * LANES:(ATT_HEADS + h + 1) * LANES].astype(F32))
        g = lax.dot_general(km_ref[h], q, (((1,), (1,)), ((), ())), precision=HI, preferred_element_type=F32)
        g = jnp.where(row < i, g, NEG)
        sel_t = jnp.zeros(g.shape, F32)
        for _ in range(MOBA_TOPK):
            mx = jnp.max(g, axis=0, keepdims=True)
            idx = jnp.min(jnp.where(g == mx, row, nbp), axis=0, keepdims=True)
            hit = row == idx
            sel_t = jnp.where(hit & (row < i), 1.0, sel_t)
            g = jnp.where(hit, -jnp.inf, g)
        if nbp < LANES:
            sel_t = jnp.concatenate([sel_t, jnp.zeros((LANES - nbp, bs), F32)], axis=0)
        w = 2 * LANES
        qa_ref[h, :LANES, :] = (q * Q_SCALE).T.astype(BF16)
        qa_ref[h, LANES:, :] = jnp.where(sel_t > 0.5, 0.0, NEG).astype(BF16)
        ka_ref[:, h * w:h * w + LANES] = k.astype(BF16)
        ka_ref[:, h * w + LANES:(h + 1) * w] = onehot
        v = p_ref[:, (2 * ATT_HEADS + h) * LANES:(2 * ATT_HEADS + h + 1) * LANES].astype(F32)
        vt_ref[h, 0] = v.T.astype(BF16)
        km_ref[h, pl.ds(i, 1), :] = jnp.mean(k, axis=0, keepdims=True)


def _rope_gate(proj, cos_t, sin1_t, sin2_t):
    s = proj.shape[0]
    nb = s // MOBA_BLOCK
    assert nb <= LANES
    nbp = -(-nb // SUBLANES) * SUBLANES
    w_in = 3 * ATT_HEADS * ATT_HEAD_DIM
    w_out = 2 * ATT_HEADS * LANES
    tab = pl.BlockSpec((MOBA_BLOCK, LANES), lambda i: (i, 0))
    return pl.pallas_call(
        _rope_gate_kernel,
        grid=(nb,),
        in_specs=[pl.BlockSpec((MOBA_BLOCK, w_in), lambda i: (i, 0)), tab, tab, tab],
        out_specs=[pl.BlockSpec((ATT_HEADS, 2 * LANES, MOBA_BLOCK), lambda i: (0, 0, i)),
                   pl.BlockSpec((MOBA_BLOCK, w_out), lambda i: (i, 0)),
                   pl.BlockSpec((ATT_HEADS, 1, LANES, MOBA_BLOCK), lambda i: (0, i, 0, 0))],
        out_shape=[jax.ShapeDtypeStruct((ATT_HEADS, 2 * LANES, s), BF16),
                   jax.ShapeDtypeStruct((s, w_out), BF16),
                   jax.ShapeDtypeStruct((ATT_HEADS, nb, LANES, MOBA_BLOCK), BF16)],
        scratch_shapes=[pltpu.VMEM((ATT_HEADS, nbp, LANES), F32)],
        name="rope_gate",
        compiler_params=_params("arbitrary"),
    )(proj, cos_t, sin1_t, sin2_t)


def _attn_kernel(qa_ref, ka_ref, vt_ref, o_ref, s_ref):
    qi = pl.program_id(1)
    bs = MOBA_BLOCK
    grp = ATT_GROUP * bs
    w = 2 * LANES
    heads = range(ATT_HEADS_PER_STEP)

    n_groups = ka_ref.shape[0] // grp
    assert n_groups % 2 == 0

    def issue_scores(g, buf):
        base = pl.multiple_of(jnp.minimum(g, n_groups - 1) * grp, grp)
        for h in heads:
            s_ref[buf, h] = jnp.dot(ka_ref[pl.ds(base, grp), h * w:(h + 1) * w], qa_ref[h],
                                    preferred_element_type=F32)

    issue_scores(0, 0)

    own = pl.multiple_of(qi * bs, bs)
    k_i = lax.broadcasted_iota(I32, (bs, bs), 0)
    q_i = lax.broadcasted_iota(I32, (bs, bs), 1)
    carry = []
    for h in heads:
        s = jnp.dot(ka_ref[pl.ds(own, bs), h * w:h * w + LANES], qa_ref[h, :LANES, :],
                    preferred_element_type=F32)
        s = jnp.where(k_i <= q_i, s, NEG)
        m = jnp.max(s, axis=0, keepdims=True)
        p = jnp.exp2(s - m)
        l = jnp.sum(p, axis=0, keepdims=True)
        acc = jnp.dot(vt_ref[h, qi], p.astype(BF16), preferred_element_type=F32)
        carry += [m, l, acc]

    def absorb(g, buf, carry):
        out = []
        for h in heads:
            m, l, acc = carry[3 * h:3 * h + 3]
            s = s_ref[buf, h]
            mn = jnp.maximum(m, jnp.max(s, axis=0, keepdims=True))
            alpha = jnp.exp2(m - mn)
            p = jnp.exp2(s - mn)
            l = alpha * l + jnp.sum(p, axis=0, keepdims=True)
            p = p.astype(BF16)
            acc = alpha * acc
            for jb in range(ATT_GROUP):
                acc = acc + jnp.dot(vt_ref[h, g * ATT_GROUP + jb], p[jb * bs:(jb + 1) * bs],
                                    preferred_element_type=F32)
            out += [mn, l, acc]
        return out

    def body(t, carry):
        issue_scores(2 * t + 1, 1)
        carry = absorb(2 * t, 0, carry)
        issue_scores(2 * t + 2, 0)
        return tuple(absorb(2 * t + 1, 1, carry))

    n_used = (qi + ATT_GROUP - 1) // ATT_GROUP
    carry = lax.fori_loop(0, (n_used + 1) // 2, body, tuple(carry))
    for h in heads:
        m, l, acc = carry[3 * h:3 * h + 3]
        o_ref[:, h * LANES:(h + 1) * LANES] = (acc / l).T.astype(BF16)


def _attention(q_aug_t, k_aug, v_t):
    s = k_aug.shape[0]
    assert s % (ATT_GROUP * MOBA_BLOCK) == 0
    nb = s // MOBA_BLOCK
    hps = ATT_HEADS_PER_STEP
    once = pl.Buffered(1)
    return pl.pallas_call(
        _attn_kernel,
        grid=(ATT_HEADS // hps, nb),
        in_specs=[pl.BlockSpec((hps, 2 * LANES, MOBA_BLOCK), lambda h, i: (h, 0, i)),
                  pl.BlockSpec((s, hps * 2 * LANES), lambda h, i: (0, h), pipeline_mode=once),
                  pl.BlockSpec((hps, nb, LANES, MOBA_BLOCK), lambda h, i: (h, 0, 0, 0), pipeline_mode=once)],
        out_specs=pl.BlockSpec((MOBA_BLOCK, hps * LANES), lambda h, i: (i, h)),
        out_shape=jax.ShapeDtypeStruct((s, ATT_HEADS * ATT_HEAD_DIM), BF16),
        scratch_shapes=[pltpu.VMEM((2, hps, ATT_GROUP * MOBA_BLOCK, MOBA_BLOCK), F32)],
        name="moba_attention",
        compiler_params=_params("parallel", "arbitrary"),
    )(q_aug_t, k_aug, v_t)


def _rwkv_kernel(r_ref, k_ref, v_ref, lw_ref, la_ref, lg_ref, ww2_ref, wa2_ref, wg2_ref,
                 mur_ref, muk_ref, muv_ref, w0_ref, a0_ref, kk_ref, ka_ref, rk_ref, gnw_ref, gnb_ref,
                 o_ref, st_ref, prev_ref):
    j = pl.program_id(1)
    t = SCAN_CHUNK
    n = RWKV_HEAD_DIM
    ts = r_ref.shape[0]
    chunks = range(ts // t)

    @pl.when(j == 0)
    def _():
        st_ref[...] = jnp.zeros_like(st_ref)
        prev_ref[...] = jnp.zeros_like(prev_ref)

    rp, kp, vp = r_ref[...].astype(F32), k_ref[...].astype(F32), v_ref[...].astype(F32)
    r = rp + (_shift_rows(rp, prev_ref[0:1, :]) - rp) * mur_ref[...]
    k = kp + (_shift_rows(kp, prev_ref[1:2, :]) - kp) * muk_ref[...]
    v = vp + (_shift_rows(vp, prev_ref[2:3, :]) - vp) * muv_ref[...]
    prev_ref[0:1, :] = rp[ts - 1:ts, :]
    prev_ref[1:2, :] = kp[ts - 1:ts, :]
    prev_ref[2:3, :] = vp[ts - 1:ts, :]

    d = w0_ref[...] + _dot(lw_ref[...], ww2_ref[...])
    logw = -math.exp(-0.5) * _sigmoid(d)
    a = _sigmoid(a0_ref[...] + _dot(la_ref[...], wa2_ref[...]))
    g = _dot(lg_ref[...], wg2_ref[...])

    li = lax.broadcasted_iota(I32, (LANES, LANES), 0)
    lj = lax.broadcasted_iota(I32, (LANES, LANES), 1)
    same_head = (li // n) == (lj // n)
    head_sum = jnp.where(same_head, 1.0, 0.0)

    kk = k * kk_ref[...]
    kk = kk / jnp.maximum(jnp.sqrt(_dot_split(kk * kk, head_sum)), 1e-12)
    kt = k * (1.0 + (a - 1.0) * ka_ref[...])
    bonus = _dot_split(r * kt * rk_ref[...], head_sum) * v

    same_blk = (li // t) == (lj // t)
    m_strict = same_blk & (lj < li)
    m_incl = same_blk & (lj <= li)
    ti = lax.broadcasted_iota(I32, (t, t), 0)
    tj = lax.broadcasted_iota(I32, (t, t), 1)
    tril_incl = jnp.where(tj <= ti, 1.0, 0.0)
    lane_a = lax.broadcasted_iota(I32, (t, LANES), 1) < n
    eye = jnp.where(li == lj, 1.0, 0.0)

    def stack_masked(z):
        return jnp.concatenate([jnp.where(lane_a, z, 0.0), jnp.where(lane_a, 0.0, z)], axis=0)

    def stack_plain(z):
        return jnp.concatenate([z, z], axis=0)

    def rows(z, c):
        return z[c * t:(c + 1) * t]

    cum = jnp.concatenate([_dot_split_t(tril_incl, rows(logw, c)) for c in chunks], axis=0)
    g_t = jnp.exp(cum)
    g_inv = jnp.exp(-cum)
    xa_f = -kk * jnp.exp(cum - logw)
    xr_f = r * g_t
    yb_f = kk * a * g_inv
    yk_f = kt * g_inv

    zeros = jnp.zeros((2 * t, LANES), F32)
    state = [st_ref[...]]
    ys = {}

    def chain_step(c, r_hat, y_hat, pm, qm):
        def run():
            st = state[0]
            y2 = _dot_nt(r_hat, st) + y_hat
            ys[c] = y2[:t] + y2[t:]
            state[0] = (st + _dot(st, pm) + qm) * g_t[(c + 1) * t - 1:(c + 1) * t, :]
        return run

    def chunk_local(grp, between):
        idx = range(len(grp))
        xa = [stack_masked(rows(xa_f, c)) for c in grp]
        xr = [stack_masked(rows(xr_f, c)) for c in grp]
        yb = [stack_plain(rows(yb_f, c)) for c in grp]
        vs = [stack_masked(rows(v, c)) for c in grp]
        ybk = [jnp.concatenate([yb[i], stack_plain(rows(yk_f, c))], axis=0) for i, c in enumerate(grp)]
        sc = [_dot_nt(jnp.concatenate([xa[i], xr[i]], axis=0), ybk[i]) for i in idx]
        between()
        a_ab = [jnp.where(m_strict, sc[i][:2 * t, :2 * t], 0.0) for i in idx]
        a_ak = [jnp.where(m_strict, sc[i][:2 * t, 2 * t:], 0.0) for i in idx]
        a_rb = [jnp.where(m_incl, sc[i][2 * t:, :2 * t], 0.0) for i in idx]
        a_rk = [jnp.where(m_incl, sc[i][2 * t:, 2 * t:], 0.0) for i in idx]
        inv = [eye + a_ab[i] for i in idx]
        pw = [_dot(a_ab[i], a_ab[i]) for i in idx]
        for _ in range(int(math.log2(t)) - 2):
            between()
            both = [_dot(pw[i], jnp.concatenate([pw[i], inv[i]], axis=1)) for i in idx]
            pw = [both[i][:, :LANES] for i in idx]
            inv = [inv[i] + both[i][:, LANES:] for i in idx]
        between()
        inv = [inv[i] + _dot(pw[i], inv[i]) for i in idx]
        av = [_dot(a_ak[i], vs[i]) for i in idx]
        between()
        mw = [_dot(inv[i], jnp.concatenate([xa[i], av[i]], axis=1)) for i in idx]
        between()
        rw = [_dot(jnp.concatenate([a_rb[i], a_rk[i]], axis=1),
                   jnp.concatenate([mw[i], jnp.concatenate([zeros, vs[i]], axis=1)], axis=0)) for i in idx]
        between()
        pm = [jnp.where(same_head, _dot_tn(mw[i][:, :LANES], yb[i]), 0.0) for i in idx]
        qm = [jnp.where(same_head, _dot_tn(jnp.concatenate([mw[i][:, LANES:], vs[i]], axis=0), ybk[i]), 0.0)
              for i in idx]
        return [chain_step(c, xr[i] + rw[i][:, :LANES], rw[i][:, LANES:], pm[i], qm[i])
                for i, c in enumerate(grp)]

    pending = []

    def between():
        if pending:
            pending.pop(0)()

    group = RWKV_CHUNK_GROUP
    for g0 in range(0, len(chunks), group):
        steps = chunk_local(list(chunks[g0:g0 + group]), between)
        while pending:
            pending.pop(0)()
        pending.extend(steps)
    while pending:
        pending.pop(0)()
    st_ref[...] = state[0]
    y = jnp.concatenate([ys[c] for c in chunks], axis=0)

    mean = _dot_split(y, head_sum) * (1.0 / n)
    yc = y - mean
    var = _dot_split(yc * yc, head_sum) * (1.0 / n)
    yn = yc * lax.rsqrt(var + GN_EPS) * gnw_ref[...] + gnb_ref[...]
    o_ref[...] = ((yn + bonus) * g).astype(BF16)


def _rwkv(proj, lw, la, lg, w_w2, w_a2, w_g2, mu_r, mu_k, mu_v, w0, a0, k_k, k_a, r_k, gn_w, gn_b,
          col0, width):
    s = proj.shape[0]
    ts = min(RWKV_ROWS_PER_STEP, s)
    npair = width // LANES
    cb = col0 // LANES
    row = lambda a: pl.BlockSpec((ts, a.shape[1]), lambda p, j: (j, 0))
    wcol = lambda a: pl.BlockSpec((a.shape[0], LANES), lambda p, j: (0, p))
    vec = pl.BlockSpec((1, LANES), lambda p, j: (0, p))
    return pl.pallas_call(
        _rwkv_kernel,
        grid=(npair, s // ts),
        in_specs=[pl.BlockSpec((ts, LANES), lambda p, j: (j, cb + p)),
                  pl.BlockSpec((ts, LANES), lambda p, j: (j, cb + npair + p)),
                  pl.BlockSpec((ts, LANES), lambda p, j: (j, cb + 2 * npair + p)),
                  row(lw), row(la), row(lg), wcol(w_w2), wcol(w_a2), wcol(w_g2)] + [vec] * 10,
        out_specs=pl.BlockSpec((ts, LANES), lambda p, j: (j, p)),
        out_shape=jax.ShapeDtypeStruct((s, width), BF16),
        scratch_shapes=[pltpu.VMEM((LANES, LANES), F32), pltpu.VMEM((SUBLANES, LANES), F32)],
        name="rwkv7_scan",
        compiler_params=_params("parallel", "arbitrary"),
    )(proj, proj, proj, lw, la, lg, w_w2, w_a2, w_g2, mu_r, mu_k, mu_v, w0, a0, k_k, k_a, r_k, gn_w, gn_b)


def _route_kernel(x_ref, oa_ref, or_ref, ga_ref, gr_ref, wua_ref, wur_ref, wo_ref,
                  gt_ref, gpost_ref, gpre_ref, sc_ref, sh_ref, wr_ref, br_ref,
                  x1_ref, h2_ref, ei_ref, wt_ref, rk_ref, cnt_ref, run_ref, y_ref):
    i = pl.program_id(0)

    @pl.when(i == 0)
    def _():
        run_ref[...] = jnp.zeros_like(run_ref)
        y_ref[...] = jnp.zeros_like(y_ref)

    ua = jnp.dot(oa_ref[...], wua_ref[...], preferred_element_type=F32)
    ur = jnp.dot(or_ref[...], wur_ref[...], preferred_element_type=F32)
    logits = _route_logits(y_ref[...], x_ref, gt_ref, gpost_ref, gpre_ref, sc_ref, sh_ref, wr_ref, br_ref,
                           x1_ref, h2_ref)
    mix = _sigmoid(ga_ref[...].astype(F32)) * ua + _sigmoid(gr_ref[...].astype(F32)) * ur
    y_new = jnp.dot(mix.astype(BF16), wo_ref[...], preferred_element_type=F32)
    _route_choose(logits, jnp.where(i > 0, 1.0, 0.0), ei_ref, wt_ref, rk_ref, run_ref)
    cnt_ref[...] = run_ref[...].astype(I32)
    y_ref[...] = y_new


def _route_logits(y, x_ref, gt_ref, gpost_ref, gpre_ref, sc_ref, sh_ref, wr_ref, br_ref, x1_ref, h2_ref):
    x1 = x_ref[...] + gt_ref[...] * _rms(y, gpost_ref[...])
    x1_ref[...] = x1
    h2 = _rms(x1, gpre_ref[...]) * (1.0 + sc_ref[...]) + sh_ref[...]
    h2_ref[...] = h2
    return _dot_x3(h2, wr_ref[...]) + br_ref[...]


def _route_choose(logits, valid, ei_ref, wt_ref, rk_ref, run_ref):
    rs = slice(None)
    lane = lax.broadcasted_iota(I32, logits.shape, 1)
    big = jnp.int32(4 * LANES)
    gmask = (lane >= N_EXPERTS) & (lane < N_EXPERTS + N_GROUPS)
    mg = jnp.max(jnp.where(gmask, logits, -jnp.inf), axis=1, keepdims=True)
    eg = jnp.where(gmask, jnp.exp(logits - mg), 0.0)
    pg = eg / jnp.sum(eg, axis=1, keepdims=True)
    pg_top = jnp.max(pg, axis=1, keepdims=True)
    g_idx = jnp.min(jnp.where(gmask & (pg == pg_top), lane, big), axis=1, keepdims=True) - N_EXPERTS
    emask = (lane >= g_idx * EXPERTS_PER_GROUP) & (lane < (g_idx + 1) * EXPERTS_PER_GROUP)
    me = jnp.max(jnp.where(emask, logits, -jnp.inf), axis=1, keepdims=True)
    ee = jnp.where(emask, jnp.exp(logits - me), 0.0)
    pe = ee / jnp.sum(ee, axis=1, keepdims=True)
    p1 = jnp.max(pe, axis=1, keepdims=True)
    i1 = jnp.min(jnp.where(emask & (pe == p1), lane, big), axis=1, keepdims=True)
    rest = emask & (lane != i1)
    p2 = jnp.max(jnp.where(rest, pe, -jnp.inf), axis=1, keepdims=True)
    i2 = jnp.min(jnp.where(rest & (pe == p2), lane, big), axis=1, keepdims=True)
    den = p1 + p2
    ei_ref[rs, :] = jnp.where(lane == 0, i1, jnp.where(lane == 1, i2, 0))
    wt_ref[rs, :] = jnp.where(lane == 0, pg_top * p1 / den, jnp.where(lane == 1, pg_top * p2 / den, 0.0))
    tm = logits.shape[0]
    chosen = jnp.where((lane == i1) | (lane == i2), valid, 0.0)
    t_i = lax.broadcasted_iota(I32, (tm, tm), 0)
    t_j = lax.broadcasted_iota(I32, (tm, tm), 1)
    before = _dot(jnp.where(t_j < t_i, 1.0, 0.0), chosen) + run_ref[...]
    r1 = jnp.sum(jnp.where(lane == i1, before, 0.0), axis=1, keepdims=True)
    r2 = jnp.sum(jnp.where(lane == i2, before, 0.0), axis=1, keepdims=True)
    rk_ref[rs, :] = jnp.where(lane == 0, r1, jnp.where(lane == 1, r2, 0.0)).astype(I32)
    run_ref[...] += jnp.sum(chosen, axis=0, keepdims=True)


def _cast_kernel(w_ref, o_ref):
    o_ref[...] = w_ref[...].astype(o_ref.dtype)


def _to_bf16(w):
    k, n = w.shape
    tk = 512
    return pl.pallas_call(
        _cast_kernel,
        grid=(k // tk,),
        in_specs=[pl.BlockSpec((tk, n), lambda i: (i, 0))],
        out_specs=pl.BlockSpec((tk, n), lambda i: (i, 0)),
        out_shape=jax.ShapeDtypeStruct((k, n), BF16),
        name="cast_bf16",
        compiler_params=_params("parallel"),
    )(w)


def _route(x, o_att, o_rwkv, proj, gate_col0, w_up_att, w_up_rwkv, w_o, gt1, g_post, g_pre, sc2, sh2,
           w_router, b_router):
    s, d = x.shape
    ka, kr = o_att.shape[1], o_rwkv.shape[1]
    tm = 256
    gb = gate_col0 // d
    nt = s // tm
    proj_tile = lambda i: jnp.minimum(i, nt - 1)
    route_tile = lambda i: jnp.maximum(i - 1, 0)
    vec = pl.BlockSpec((1, d), lambda i: (0, 0))
    rowblk = pl.BlockSpec((tm, d), lambda i: (route_tile(i), 0))
    small = pl.BlockSpec((tm, LANES), lambda i: (route_tile(i), 0))
    lane_row = pl.BlockSpec((1, LANES), lambda i: (0, 0))
    once = pl.Buffered(1)
    return pl.pallas_call(
        _route_kernel,
        grid=(nt + 1,),
        in_specs=[rowblk,
                  pl.BlockSpec((tm, ka), lambda i: (proj_tile(i), 0)),
                  pl.BlockSpec((tm, kr), lambda i: (proj_tile(i), 0)),
                  pl.BlockSpec((tm, d), lambda i: (proj_tile(i), gb)),
                  pl.BlockSpec((tm, d), lambda i: (proj_tile(i), gb + 1)),
                  pl.BlockSpec((ka, d), lambda i: (0, 0), pipeline_mode=once),
                  pl.BlockSpec((kr, d), lambda i: (0, 0), pipeline_mode=once),
                  pl.BlockSpec((d, d), lambda i: (0, 0), pipeline_mode=once),
                  vec, vec, vec, vec, vec, pl.BlockSpec((d, LANES), lambda i: (0, 0)), lane_row],
        out_specs=[rowblk, rowblk, small, small, small, lane_row],
        out_shape=[jax.ShapeDtypeStruct((s, d), F32), jax.ShapeDtypeStruct((s, d), F32),
                   jax.ShapeDtypeStruct((s, LANES), I32), jax.ShapeDtypeStruct((s, LANES), F32),
                   jax.ShapeDtypeStruct((s, LANES), I32), jax.ShapeDtypeStruct((1, LANES), I32)],
        scratch_shapes=[pltpu.VMEM((1, LANES), F32), pltpu.VMEM((tm, d), F32)],
        name="merge_out_route",
        compiler_params=_params("arbitrary"),
    )(x, o_att, o_rwkv, proj, proj, _to_bf16(w_up_att), _to_bf16(w_up_rwkv), _to_bf16(w_o),
      gt1, g_post, g_pre, sc2, sh2, w_router, b_router)


def _slots_kernel(ei_ref, rk_ref, ps_ref, pos_ref):
    lane = lax.broadcasted_iota(I32, ei_ref.shape, 1)
    ei, rk = ei_ref[...], rk_ref[...]
    ps = ps_ref[...]
    cols = []
    for kk in range(TOP_K):
        e = ei[:, kk:kk + 1]
        cols.append(jnp.sum(jnp.where(lane == e, ps, 0), axis=1, keepdims=True) + rk[:, kk:kk + 1])
    pos_ref[...] = jnp.where(lane == 0, cols[0], jnp.where(lane == 1, cols[1], 0))


def _slots(e_idx, rank, pstart_row):
    n = e_idx.shape[0]
    tm = 1024
    blk = pl.BlockSpec((tm, LANES), lambda i: (i, 0))
    return pl.pallas_call(
        _slots_kernel,
        grid=(n // tm,),
        in_specs=[blk, blk, pl.BlockSpec((1, LANES), lambda i: (0, 0))],
        out_specs=blk,
        out_shape=jax.ShapeDtypeStruct((n, LANES), I32),
        name="dispatch_slots",
        compiler_params=_params("parallel"),
    )(e_idx, rank, pstart_row)


def _row_copy(src, row, dst, dst_row, sem):
    return pltpu.make_async_copy(src.at[pl.ds(row, 1)], dst.at[pl.ds(dst_row, 1)], sem)


def _dispatch_kernel(pos_ref, cnt_ref, pst_ref, nused_ref, h_ref, x_hbm, stage_ref, zero_ref, sem, zsem):
    i = pl.program_id(0)
    tm = h_ref.shape[0]
    n_blk = x_hbm.shape[0] // EXPERT_BLOCK
    nused = nused_ref[0]
    slot = i % 2

    def wait_tile(s):
        for _ in range(TOP_K):
            pltpu.make_async_copy(stage_ref.at[s], x_hbm.at[pl.ds(0, tm)], sem.at[s]).wait()

    @pl.when(i >= 2)
    def _():
        wait_tile(slot)

    stage_ref[slot] = h_ref[...]
    for r in range(tm):
        for kk in range(TOP_K):
            _row_copy(stage_ref.at[slot], r, x_hbm, pos_ref[(i * tm + r) * TOP_K + kk],
                      sem.at[slot]).start(priority=(r * TOP_K + kk) % 2)

    @pl.when(i == 0)
    def _():
        zero_ref[...] = jnp.zeros_like(zero_ref)

        def pad_expert(e, total):
            lo = pst_ref[e] + cnt_ref[e]
            hi = pst_ref[e] + (cnt_ref[e] + EXPERT_BLOCK - 1) // EXPERT_BLOCK * EXPERT_BLOCK

            def pad_row(s, _):
                _row_copy(zero_ref, 0, x_hbm, s, zsem).start()
                return 0

            lax.fori_loop(lo, hi, pad_row, 0)
            return total + (hi - lo)

        n_pad = lax.fori_loop(0, cnt_ref.shape[0], pad_expert, 0)

        def tail_copy(blk):
            return pltpu.make_async_copy(zero_ref, x_hbm.at[pl.ds(blk * EXPERT_BLOCK, EXPERT_BLOCK)], zsem)

        def tail_start(blk, _):
            tail_copy(blk).start()
            return 0

        def pad_wait(_, c):
            _row_copy(zero_ref, 0, x_hbm, 0, zsem).wait()
            return c

        def tail_wait(blk, _):
            tail_copy(blk).wait()
            return 0

        lax.fori_loop(nused, n_blk, tail_start, 0)
        lax.fori_loop(0, n_pad, pad_wait, 0)
        lax.fori_loop(nused, n_blk, tail_wait, 0)

    @pl.when(i == pl.num_programs(0) - 1)
    def _():
        @pl.when(i >= 1)
        def _():
            wait_tile(1 - slot)

        wait_tile(slot)


def _dispatch(pos_flat, counts, pstart, nused, h2, n_slots):
    n, w = h2.shape
    tm = 256
    grid_spec = pltpu.PrefetchScalarGridSpec(
        num_scalar_prefetch=4,
        grid=(n // tm,),
        in_specs=[pl.BlockSpec((tm, w), lambda i, *_: (i, 0))],
        out_specs=pl.BlockSpec(memory_space=pl.ANY),
        scratch_shapes=[pltpu.VMEM((2, tm, w), h2.dtype), pltpu.VMEM((EXPERT_BLOCK, w), h2.dtype),
                        pltpu.SemaphoreType.DMA((2,)), pltpu.SemaphoreType.DMA(())],
    )
    return pl.pallas_call(
        _dispatch_kernel,
        grid_spec=grid_spec,
        out_shape=jax.ShapeDtypeStruct((n_slots, w), h2.dtype),
        name="dispatch_rows",
        compiler_params=_params("arbitrary"),
    )(pos_flat, counts, pstart, nused, h2)


EXPERT_X_SLOTS = 4
EXPERT_Y_SLOTS = 3


def _expert_kernel(bstart_ref, bcount_ref, nused_ref, x_hbm, wg_ref, wu_ref, wd_ref, y_hbm,
                   xs_ref, yo_ref, wgb_ref, wub_ref, wdb_ref, xsem, osem):
    e = pl.program_id(0)
    nused = nused_ref[0]
    rows = EXPERT_BLOCK
    n_blk = y_hbm.shape[0] // rows
    nx, ny = EXPERT_X_SLOTS, EXPERT_Y_SLOTS
    first = bstart_ref[e]
    count = bcount_ref[e]

    def x_copy(blk):
        src = x_hbm.at[pl.ds(jnp.minimum(blk, n_blk - 1) * rows, rows)]
        return pltpu.make_async_copy(src, xs_ref.at[blk % nx], xsem.at[blk % nx])

    def out_copy(blk):
        return pltpu.make_async_copy(yo_ref.at[blk % ny], y_hbm.at[pl.ds(blk * rows, rows), :], osem.at[blk % ny])

    @pl.when(e == 0)
    def _():
        for j in range(nx - 1):
            x_copy(j).start()

    @pl.when(count > 0)
    def _():
        wgb_ref[...] = wg_ref[0].astype(BF16)
        wub_ref[...] = wu_ref[0].astype(BF16)
        wdb_ref[...] = wd_ref[0].astype(BF16)

    def block(j, _):
        blk = first + j

        @pl.when(blk >= ny)
        def _():
            out_copy(blk - ny).wait()

        x_copy(blk).wait()
        x_copy(blk + nx - 1).start()
        xb = xs_ref[blk % nx].astype(BF16)
        hg = jnp.dot(xb, wgb_ref[...], preferred_element_type=F32)
        hu = jnp.dot(xb, wub_ref[...], preferred_element_type=F32)
        hid = hg * _sigmoid(hg) * hu
        yo_ref[blk % ny] = jnp.dot(hid.astype(BF16), wdb_ref[...], preferred_element_type=F32)
        out_copy(blk).start()

        @pl.when(blk == nused - 1)
        def _():
            for ahead in range(1, nx):
                x_copy(blk + ahead).wait()

        return 0

    lax.fori_loop(0, count, block, 0)

    @pl.when(e == pl.num_programs(0) - 1)
    def _():
        for back in range(1, ny + 1):
            @pl.when(nused - back >= 0)
            def _():
                out_copy(nused - back).wait()

        yo_ref[0] = jnp.zeros(yo_ref.shape[1:], yo_ref.dtype)

        def zero_copy(blk):
            return pltpu.make_async_copy(yo_ref.at[0], y_hbm.at[pl.ds(blk * rows, rows), :], osem.at[0])

        def fill(blk, _):
            zero_copy(blk).start()
            return 0

        def drain(blk, _):
            zero_copy(blk).wait()
            return 0

        lax.fori_loop(nused, n_blk, fill, 0)
        lax.fori_loop(nused, n_blk, drain, 0)


def _experts(x_buf, bstart, bcount, nused, w_gate_e, w_up_e, w_down_e):
    n_slots = x_buf.shape[0]
    n_exp, d, f = w_gate_e.shape
    grid_spec = pltpu.PrefetchScalarGridSpec(
        num_scalar_prefetch=3,
        grid=(n_exp,),
        in_specs=[pl.BlockSpec(memory_space=pl.ANY),
                  pl.BlockSpec((1, d, f), lambda e, *_: (e, 0, 0)),
                  pl.BlockSpec((1, d, f), lambda e, *_: (e, 0, 0)),
                  pl.BlockSpec((1, f, d), lambda e, *_: (e, 0, 0))],
        out_specs=pl.BlockSpec(memory_space=pl.ANY),
        scratch_shapes=[pltpu.VMEM((EXPERT_X_SLOTS, EXPERT_BLOCK, d), F32),
                        pltpu.VMEM((EXPERT_Y_SLOTS, EXPERT_BLOCK, d), F32),
                        pltpu.VMEM((d, f), BF16), pltpu.VMEM((d, f), BF16), pltpu.VMEM((f, d), BF16),
                        pltpu.SemaphoreType.DMA((EXPERT_X_SLOTS,)), pltpu.SemaphoreType.DMA((EXPERT_Y_SLOTS,))],
    )
    return pl.pallas_call(
        _expert_kernel,
        grid_spec=grid_spec,
        out_shape=jax.ShapeDtypeStruct((n_slots, d), F32),
        name="experts",
        compiler_params=_params("arbitrary"),
    )(bstart, bcount, nused, x_buf, w_gate_e, w_up_e, w_down_e)


def _combine_kernel(pos_ref, y_hbm, wt_ref, x1_ref, gt_ref, gpost_ref, o_ref, rows_ref, sem):
    i = pl.program_id(0)
    nsteps = pl.num_programs(0)
    tm = x1_ref.shape[0]

    def start_rows(step, slot):
        for r in range(tm):
            for kk in range(TOP_K):
                _row_copy(y_hbm, pos_ref[(step * tm + r) * TOP_K + kk], rows_ref.at[slot, kk], r,
                          sem.at[slot]).start(priority=(r * TOP_K + kk) % 2)

    def wait_rows(slot):
        for kk in range(TOP_K):
            pltpu.make_async_copy(y_hbm.at[pl.ds(0, tm), :], rows_ref.at[slot, kk], sem.at[slot]).wait()

    @pl.when(i == 0)
    def _():
        start_rows(0, 0)

    slot = i % 2
    wait_rows(slot)
    start_rows(jnp.minimum(i + 1, nsteps - 1), 1 - slot)
    wt = wt_ref[...]
    y = rows_ref[slot, 0] * wt[:, 0:1] + rows_ref[slot, 1] * wt[:, 1:2]
    o_ref[...] = x1_ref[...] + gt_ref[...] * _rms(y, gpost_ref[...])

    @pl.when(i == nsteps - 1)
    def _():
        wait_rows(1 - slot)


def _combine(pos, y_buf, wts, x1, gt2, g_post):
    n, d = x1.shape
    tm = 256
    vec = pl.BlockSpec((1, d), lambda i, p: (0, 0))
    grid_spec = pltpu.PrefetchScalarGridSpec(
        num_scalar_prefetch=1,
        grid=(n // tm,),
        in_specs=[pl.BlockSpec(memory_space=pl.ANY),
                  pl.BlockSpec((tm, LANES), lambda i, p: (i, 0)),
                  pl.BlockSpec((tm, d), lambda i, p: (i, 0)), vec, vec],
        out_specs=pl.BlockSpec((tm, d), lambda i, p: (i, 0)),
        scratch_shapes=[pltpu.VMEM((2, TOP_K, tm, d), F32), pltpu.SemaphoreType.DMA((2,))],
    )
    return pl.pallas_call(
        _combine_kernel,
        grid_spec=grid_spec,
        out_shape=jax.ShapeDtypeStruct((n, d), F32),
        name="combine",
        compiler_params=_params("arbitrary"),
    )(pos, y_buf, wts, x1, gt2, g_post)


def _segment_tables(counts_row):
    counts = counts_row[0, :N_EXPERTS]
    pcounts = (counts + EXPERT_BLOCK - 1) // EXPERT_BLOCK * EXPERT_BLOCK
    pend = jnp.cumsum(pcounts)
    pstart = pend - pcounts
    nused = (pend[-1] // EXPERT_BLOCK).astype(I32)
    pstart_row = jnp.pad(pstart, (0, LANES - N_EXPERTS)).reshape(1, LANES)
    return counts, pstart, pstart_row, pstart // EXPERT_BLOCK, pcounts // EXPERT_BLOCK, nused.reshape(1)


def _rope_tables(s):
    half = ROPE_DIM // 2
    inv = ROPE_THETA ** (-jnp.arange(half, dtype=F32) / half)
    ang = jnp.arange(s, dtype=F32)[:, None] * inv[None, :]
    cos, sin = lax.optimization_barrier((jnp.cos(ang), jnp.sin(ang)))
    pad = jnp.zeros((s, LANES - ROPE_DIM), F32)
    zero = jnp.zeros((s, half), F32)
    cos_t = jnp.concatenate([cos, cos, pad + 1.0], axis=1)
    sin1_t = jnp.concatenate([-sin, zero, pad], axis=1)
    sin2_t = jnp.concatenate([zero, sin, pad], axis=1)
    return cos_t, sin1_t, sin2_t


def _layer(x, c_col, w_ada, b_ada, g_pre_mix, g_post_mix, g_pre_ffn, g_post_ffn, w_in, mu_r, mu_k, mu_v,
           mu_w, mu_a, mu_g, w0, w_w1, w_w2, a0, w_a1, w_a2, w_g1, w_g2, k_k, k_a, r_k, gn_w, gn_b,
           w_up_att, w_up_rwkv, w_o, w_rg, b_rg, w_re, b_re, w_gate_e, w_up_e, w_down_e):
    s, d = x.shape
    att_w = ATT_HEADS * ATT_HEAD_DIM
    rwkv_w = w_up_rwkv.shape[0]
    row = lambda a: a.reshape(1, -1)

    ada = _ada(c_col, w_ada, row(b_ada))
    sh1, sc1, gt1, sh2, sc2, gt2 = (ada[:, i * d:(i + 1) * d] for i in range(6))

    h, lw, la, lg = _prenorm(x, row(g_pre_mix), sc1, sh1, row(mu_w), row(mu_a), row(mu_g), w_w1, w_a1, w_g1)
    proj = _matmul(h, w_in, BF16)

    q_aug_t, k_aug, v_t = _rope_gate(proj, *_rope_tables(s))
    o_att = _attention(q_aug_t, k_aug, v_t)

    o_rwkv = _rwkv(proj, lw, la, lg, w_w2, w_a2, w_g2, row(mu_r), row(mu_k), row(mu_v), row(w0), row(a0),
                   row(k_k), row(k_a), row(r_k), row(gn_w), row(gn_b), col0=3 * att_w, width=rwkv_w)


    w_router = jnp.pad(jnp.concatenate([w_re, w_rg], axis=1), ((0, 0), (0, LANES - N_EXPERTS - N_GROUPS)))
    b_router = jnp.pad(jnp.concatenate([b_re, b_rg]), (0, LANES - N_EXPERTS - N_GROUPS)).reshape(1, LANES)
    x1, h2, e_idx, wts, rank, counts = _route(x, o_att, o_rwkv, proj, 3 * att_w + 3 * rwkv_w, w_up_att, w_up_rwkv,
                                              w_o, gt1, row(g_post_mix), row(g_pre_ffn), sc2, sh2,
                                              w_router, b_router)

    n_pairs = s * TOP_K
    n_blk = (n_pairs + N_EXPERTS * (EXPERT_BLOCK - 1) + EXPERT_BLOCK - 1) // EXPERT_BLOCK
    counts, pstart, pstart_row, bstart, bcount, nused = _segment_tables(counts)
    pos = _slots(e_idx, rank, pstart_row)[:, :TOP_K].reshape(n_pairs)
    x_buf = _dispatch(pos, counts, pstart, nused, h2, n_blk * EXPERT_BLOCK)
    y_buf = _experts(x_buf, bstart, bcount, nused, w_gate_e, w_up_e, w_down_e)
    return _combine(pos, y_buf, wts, x1, gt2, row(g_post_ffn))


def kernel(x, c, w_ada, b_ada, g_pre_mix, g_post_mix, g_pre_ffn, g_post_ffn, w_in, mu_r, mu_k, mu_v, mu_w, mu_a, mu_g, w0, w_w1, w_w2, a0, w_a1, w_a2, w_g1, w_g2, k_k, k_a, r_k, gn_w, gn_b, w_up_att, w_up_rwkv, w_o, w_rg, b_rg, w_re, b_re, w_gate_e, w_up_e, w_down_e):
    b, s, d = x.shape
    assert b == 1, "one sequence per call"
    params = (w_ada, b_ada, g_pre_mix, g_post_mix, g_pre_ffn, g_post_ffn, w_in, mu_r, mu_k, mu_v, mu_w, mu_a,
              mu_g, w0, w_w1, w_w2, a0, w_a1, w_a2, w_g1, w_g2, k_k, k_a, r_k, gn_w, gn_b, w_up_att,
              w_up_rwkv, w_o, w_rg, b_rg, w_re, b_re, w_gate_e, w_up_e, w_down_e)
    xs = x.reshape(s, d)
    c_col = c.reshape(d, 1)
    for l in range(w_ada.shape[0]):
        xs = _layer(xs, c_col, *(p[l] for p in params))
    return xs.reshape(b, s, d)
```

```python
import math

import jax
import jax.numpy as jnp
import numpy as np
from jax import lax
from jax.experimental import pallas as pl
from jax.experimental.pallas import tpu as pltpu

F32 = jnp.float32
BF16 = jnp.bfloat16
I32 = jnp.int32
HI = lax.Precision.HIGHEST

LANES = 128
SUBLANES = 8
VMEM_LIMIT = 56 * 1024 * 1024

ATT_HEADS = 8
ATT_HEAD_DIM = 128
MOBA_BLOCK = 256
MOBA_TOPK = 3
ATT_GROUP = 4
ATT_HEADS_PER_STEP = 2
ROPE_THETA = 500000.0
ROPE_DIM = ATT_HEAD_DIM // 4
RWKV_HEAD_DIM = 64
GN_EPS = 64e-5
N_GROUPS = 8
EXPERTS_PER_GROUP = 8
N_EXPERTS = N_GROUPS * EXPERTS_PER_GROUP
TOP_K = 2
EXPERT_BLOCK = 128
RMS_EPS = 1e-6
NEG = -1e30
SCAN_CHUNK = 64
RWKV_CHUNK_GROUP = 8
RWKV_ROWS_PER_STEP = 2048
Q_SCALE = ATT_HEAD_DIM ** -0.5 * math.log2(math.e)


def _params(*sem):
    return pltpu.CompilerParams(dimension_semantics=sem, vmem_limit_bytes=VMEM_LIMIT)


def _rms(z, g):
    return z * lax.rsqrt(jnp.mean(z * z, axis=-1, keepdims=True) + RMS_EPS) * g


def _sigmoid(z):
    return 1.0 / (1.0 + jnp.exp(-z))


def _dot(a, b):
    return jnp.dot(a.astype(BF16), b.astype(BF16), preferred_element_type=F32)


def _dot_nt(a, b):
    return lax.dot_general(a.astype(BF16), b.astype(BF16), (((1,), (1,)), ((), ())),
                           preferred_element_type=F32)


def _dot_tn(a, b):
    return lax.dot_general(a.astype(BF16), b.astype(BF16), (((0,), (0,)), ((), ())),
                           preferred_element_type=F32)


def _dot_hi(a, b):
    return jnp.dot(a, b, precision=HI, preferred_element_type=F32)


def _dot_x3(a, b):
    ah, al, _ = _split3(a)
    bh, bl, _ = _split3(b)
    return (jnp.dot(ah, bh, preferred_element_type=F32) + jnp.dot(ah, bl, preferred_element_type=F32)
            + jnp.dot(al, bh, preferred_element_type=F32))


def _split3(a):
    hi = a.astype(BF16)
    r1 = a - hi.astype(F32)
    mid = r1.astype(BF16)
    lo = (r1 - mid.astype(F32)).astype(BF16)
    return hi, mid, lo


def _dot_split(a, b01):
    b = b01.astype(BF16)
    hi, mid, _ = _split3(a)
    return jnp.dot(jnp.concatenate([hi, mid], axis=1), jnp.concatenate([b, b], axis=0),
                   preferred_element_type=F32)


def _dot_split_t(b01, a):
    b = b01.astype(BF16)
    return jnp.dot(jnp.concatenate([b, b, b], axis=1), jnp.concatenate(_split3(a), axis=0),
                   preferred_element_type=F32)


def _shift_rows(z, prev_row):
    rolled = pltpu.roll(z, 1, 0)
    row = lax.broadcasted_iota(I32, z.shape, 0)
    return jnp.where(row == 0, prev_row, rolled)


def _ada_kernel(c_ref, w_ref, b_ref, o_ref):
    o_ref[...] = jnp.sum(c_ref[...] * w_ref[...], axis=0, keepdims=True) + b_ref[...]


def _ada(c_col, w_ada, b_ada):
    d, n = w_ada.shape
    tn = 1024
    return pl.pallas_call(
        _ada_kernel,
        grid=(n // tn,),
        in_specs=[pl.BlockSpec((d, 1), lambda j: (0, 0)),
                  pl.BlockSpec((d, tn), lambda j: (0, j)),
                  pl.BlockSpec((1, tn), lambda j: (0, j))],
        out_specs=pl.BlockSpec((1, tn), lambda j: (0, j)),
        out_shape=jax.ShapeDtypeStruct((1, n), F32),
        name="ada",
        compiler_params=_params("parallel"),
    )(c_col, w_ada, b_ada)


def _prenorm_kernel(x_ref, xp_ref, g_ref, sc_ref, sh_ref, muw_ref, mua_ref, mug_ref,
                    ww1_ref, wa1_ref, wg1_ref, h_ref, lw_ref, la_ref, lg_ref):
    i = pl.program_id(0)
    g, sc, sh = g_ref[...], sc_ref[...], sh_ref[...]
    h = _rms(x_ref[...], g) * (1.0 + sc) + sh
    hp = _rms(xp_ref[SUBLANES - 1:SUBLANES, :], g) * (1.0 + sc) + sh
    hp = jnp.where(i == 0, 0.0, hp)
    dh = _shift_rows(h, hp) - h
    h_ref[...] = h.astype(BF16)
    lw_ref[...] = jnp.tanh(_dot(h + dh * muw_ref[...], ww1_ref[...]))
    la_ref[...] = _dot(h + dh * mua_ref[...], wa1_ref[...])
    lg_ref[...] = _sigmoid(_dot(h + dh * mug_ref[...], wg1_ref[...]))


def _prenorm(x, g, sc, sh, mu_w, mu_a, mu_g, w_w1, w_a1, w_g1):
    s, d = x.shape
    tm = 256
    rpb = tm // SUBLANES
    vec = pl.BlockSpec((1, d), lambda i: (0, 0))
    full = lambda a: pl.BlockSpec(a.shape, lambda i: (0, 0))
    lw, la, lg = w_w1.shape[1], w_a1.shape[1], w_g1.shape[1]
    return pl.pallas_call(
        _prenorm_kernel,
        grid=(s // tm,),
        in_specs=[pl.BlockSpec((tm, d), lambda i: (i, 0)),
                  pl.BlockSpec((SUBLANES, d), lambda i: (jnp.maximum(i * rpb - 1, 0), 0)),
                  vec, vec, vec, vec, vec, vec, full(w_w1), full(w_a1), full(w_g1)],
        out_specs=[pl.BlockSpec((tm, d), lambda i: (i, 0)),
                   pl.BlockSpec((tm, lw), lambda i: (i, 0)),
                   pl.BlockSpec((tm, la), lambda i: (i, 0)),
                   pl.BlockSpec((tm, lg), lambda i: (i, 0))],
        out_shape=[jax.ShapeDtypeStruct((s, d), BF16),
                   jax.ShapeDtypeStruct((s, lw), F32),
                   jax.ShapeDtypeStruct((s, la), F32),
                   jax.ShapeDtypeStruct((s, lg), F32)],
        name="prenorm_lora",
        compiler_params=_params("parallel"),
    )(x, x, g, sc, sh, mu_w, mu_a, mu_g, w_w1, w_a1, w_g1)


def _mm_kernel(a_ref, w_ref, o_ref, wb_ref):
    @pl.when(pl.program_id(1) == 0)
    def _():
        wb_ref[...] = w_ref[...].astype(BF16)

    o_ref[...] = jnp.dot(a_ref[...], wb_ref[...], preferred_element_type=F32).astype(o_ref.dtype)


def _matmul(a, w, out_dtype, tm=512, tn=1024):
    m, k = a.shape
    n = w.shape[1]
    tn = min(tn, n)
    return pl.pallas_call(
        _mm_kernel,
        grid=(n // tn, m // tm),
        in_specs=[pl.BlockSpec((tm, k), lambda j, i: (i, 0)),
                  pl.BlockSpec((k, tn), lambda j, i: (0, j))],
        out_specs=pl.BlockSpec((tm, tn), lambda j, i: (i, j)),
        out_shape=jax.ShapeDtypeStruct((m, n), out_dtype),
        scratch_shapes=[pltpu.VMEM((k, tn), BF16)],
        name="matmul",
        compiler_params=_params("arbitrary", "arbitrary"),
    )(a, w)


def _rope_gate_kernel(p_ref, c_ref, s1_ref, s2_ref, qa_ref, ka_ref, vt_ref, km_ref):
    i = pl.program_id(0)
    bs = MOBA_BLOCK
    nbp = km_ref.shape[1]

    @pl.when(i == 0)
    def _():
        km_ref[...] = jnp.zeros_like(km_ref)

    c, s1, s2 = c_ref[...], s1_ref[...], s2_ref[...]

    def rope(z):
        return z * c + pltpu.roll(z, LANES - ROPE_DIM // 2, 1) * s1 + pltpu.roll(z, ROPE_DIM // 2, 1) * s2

    row = lax.broadcasted_iota(I32, (nbp, bs), 0)
    lane = lax.broadcasted_iota(I32, (bs, LANES), 1)
    onehot = jnp.where(lane == i, 1.0, 0.0).astype(BF16)
    for h in range(ATT_HEADS):
        q = rope(p_ref[:, h * LANES:(h + 1) * LANES].astype(F32))
        k = rope(p_ref[:, (ATT_HEADS + h) * LANES:(ATT_HEADS + h + 1) * LANES].astype(F32))
        g = lax.dot_general(km_ref[h], q, (((1,), (1,)), ((), ())), precision=HI, preferred_element_type=F32)
        g = jnp.where(row < i, g, NEG)
        sel_t = jnp.zeros(g.shape, F32)
        for _ in range(MOBA_TOPK):
            mx = jnp.max(g, axis=0, keepdims=True)
            idx = jnp.min(jnp.where(g == mx, row, nbp), axis=0, keepdims=True)
            hit = row == idx
            sel_t = jnp.where(hit & (row < i), 1.0, sel_t)
            g = jnp.where(hit, -jnp.inf, g)
        if nbp < LANES:
            sel_t = jnp.concatenate([sel_t, jnp.zeros((LANES - nbp, bs), F32)], axis=0)
        w = 2 * LANES
        qa_ref[h, :LANES, :] = (q * Q_SCALE).T.astype(BF16)
        qa_ref[h, LANES:, :] = jnp.where(sel_t > 0.5, 0.0, NEG).astype(BF16)
        ka_ref[:, h * w:h * w + LANES] = k.astype(BF16)
        ka_ref[:, h * w + LANES:(h + 1) * w] = onehot
        v = p_ref[:, (2 * ATT_HEADS + h) * LANES:(2 * ATT_HEADS + h + 1) * LANES].astype(F32)
        vt_ref[h, 0] = v.T.astype(BF16)
        km_ref[h, pl.ds(i, 1), :] = jnp.mean(k, axis=0, keepdims=True)


def _rope_gate(proj, cos_t, sin1_t, sin2_t):
    s = proj.shape[0]
    nb = s // MOBA_BLOCK
    assert nb <= LANES
    nbp = -(-nb // SUBLANES) * SUBLANES
    w_in = 3 * ATT_HEADS * ATT_HEAD_DIM
    w_out = 2 * ATT_HEADS * LANES
    tab = pl.BlockSpec((MOBA_BLOCK, LANES), lambda i: (i, 0))
    return pl.pallas_call(
        _rope_gate_kernel,
        grid=(nb,),
        in_specs=[pl.BlockSpec((MOBA_BLOCK, w_in), lambda i: (i, 0)), tab, tab, tab],
        out_specs=[pl.BlockSpec((ATT_HEADS, 2 * LANES, MOBA_BLOCK), lambda i: (0, 0, i)),
                   pl.BlockSpec((MOBA_BLOCK, w_out), lambda i: (i, 0)),
                   pl.BlockSpec((ATT_HEADS, 1, LANES, MOBA_BLOCK), lambda i: (0, i, 0, 0))],
        out_shape=[jax.ShapeDtypeStruct((ATT_HEADS, 2 * LANES, s), BF16),
                   jax.ShapeDtypeStruct((s, w_out), BF16),
                   jax.ShapeDtypeStruct((ATT_HEADS, nb, LANES, MOBA_BLOCK), BF16)],
        scratch_shapes=[pltpu.VMEM((ATT_HEADS, nbp, LANES), F32)],
        name="rope_gate",
        compiler_params=_params("arbitrary"),
    )(proj, cos_t, sin1_t, sin2_t)


def _attn_kernel(qa_ref, ka_ref, vt_ref, o_ref, s_ref):
    qi = pl.program_id(1)
    bs = MOBA_BLOCK
    grp = ATT_GROUP * bs
    w = 2 * LANES
    heads = range(ATT_HEADS_PER_STEP)

    n_groups = ka_ref.shape[0] // grp
    assert n_groups % 2 == 0

    def issue_scores(g, buf):
        base = pl.multiple_of(jnp.minimum(g, n_groups - 1) * grp, grp)
        for h in heads:
            s_ref[buf, h] = jnp.dot(ka_ref[pl.ds(base, grp), h * w:(h + 1) * w], qa_ref[h],
                                    preferred_element_type=F32)

    issue_scores(0, 0)

    own = pl.multiple_of(qi * bs, bs)
    k_i = lax.broadcasted_iota(I32, (bs, bs), 0)
    q_i = lax.broadcasted_iota(I32, (bs, bs), 1)
    carry = []
    for h in heads:
        s = jnp.dot(ka_ref[pl.ds(own, bs), h * w:h * w + LANES], qa_ref[h, :LANES, :],
                    preferred_element_type=F32)
        s = jnp.where(k_i <= q_i, s, NEG)
        m = jnp.max(s, axis=0, keepdims=True)
        p = jnp.exp2(s - m)
        l = jnp.sum(p, axis=0, keepdims=True)
        acc = jnp.dot(vt_ref[h, qi], p.astype(BF16), preferred_element_type=F32)
        carry += [m, l, acc]

    def absorb(g, buf, carry):
        out = []
        for h in heads:
            m, l, acc = carry[3 * h:3 * h + 3]
            s = s_ref[buf, h]
            mn = jnp.maximum(m, jnp.max(s, axis=0, keepdims=True))
            alpha = jnp.exp2(m - mn)
            p = jnp.exp2(s - mn)
            l = alpha * l + jnp.sum(p, axis=0, keepdims=True)
            p = p.astype(BF16)
            acc = alpha * acc
            for jb in range(ATT_GROUP):
                acc = acc + jnp.dot(vt_ref[h, g * ATT_GROUP + jb], p[jb * bs:(jb + 1) * bs],
                                    preferred_element_type=F32)
            out += [mn, l, acc]
        return out

    def body(t, carry):
        issue_scores(2 * t + 1, 1)
        carry = absorb(2 * t, 0, carry)
        issue_scores(2 * t + 2, 0)
        return tuple(absorb(2 * t + 1, 1, carry))

    n_used = (qi + ATT_GROUP - 1) // ATT_GROUP
    carry = lax.fori_loop(0, (n_used + 1) // 2, body, tuple(carry))
    for h in heads:
        m, l, acc = carry[3 * h:3 * h + 3]
        o_ref[:, h * LANES:(h + 1) * LANES] = (acc / l).T.astype(BF16)


def _attention(q_aug_t, k_aug, v_t):
    s = k_aug.shape[0]
    assert s % (ATT_GROUP * MOBA_BLOCK) == 0
    nb = s // MOBA_BLOCK
    hps = ATT_HEADS_PER_STEP
    once = pl.Buffered(1)
    return pl.pallas_call(
        _attn_kernel,
        grid=(ATT_HEADS // hps, nb),
        in_specs=[pl.BlockSpec((hps, 2 * LANES, MOBA_BLOCK), lambda h, i: (h, 0, i)),
                  pl.BlockSpec((s, hps * 2 * LANES), lambda h, i: (0, h), pipeline_mode=once),
                  pl.BlockSpec((hps, nb, LANES, MOBA_BLOCK), lambda h, i: (h, 0, 0, 0), pipeline_mode=once)],
        out_specs=pl.BlockSpec((MOBA_BLOCK, hps * LANES), lambda h, i: (i, h)),
        out_shape=jax.ShapeDtypeStruct((s, ATT_HEADS * ATT_HEAD_DIM), BF16),
        scratch_shapes=[pltpu.VMEM((2, hps, ATT_GROUP * MOBA_BLOCK, MOBA_BLOCK), F32)],
        name="moba_attention",
        compiler_params=_params("parallel", "arbitrary"),
    )(q_aug_t, k_aug, v_t)


def _rwkv_kernel(r_ref, k_ref, v_ref, lw_ref, la_ref, lg_ref, ww2_ref, wa2_ref, wg2_ref,
                 mur_ref, muk_ref, muv_ref, w0_ref, a0_ref, kk_ref, ka_ref, rk_ref, gnw_ref, gnb_ref,
                 o_ref, st_ref, prev_ref):
    j = pl.program_id(1)
    t = SCAN_CHUNK
    n = RWKV_HEAD_DIM
    ts = r_ref.shape[0]
    chunks = range(ts // t)

    @pl.when(j == 0)
    def _():
        st_ref[...] = jnp.zeros_like(st_ref)
        prev_ref[...] = jnp.zeros_like(prev_ref)

    rp, kp, vp = r_ref[...].astype(F32), k_ref[...].astype(F32), v_ref[...].astype(F32)
    r = rp + (_shift_rows(rp, prev_ref[0:1, :]) - rp) * mur_ref[...]
    k = kp + (_shift_rows(kp, prev_ref[1:2, :]) - kp) * muk_ref[...]
    v = vp + (_shift_rows(vp, prev_ref[2:3, :]) - vp) * muv_ref[...]
    prev_ref[0:1, :] = rp[ts - 1:ts, :]
    prev_ref[1:2, :] = kp[ts - 1:ts, :]
    prev_ref[2:3, :] = vp[ts - 1:ts, :]

    d = w0_ref[...] + _dot(lw_ref[...], ww2_ref[...])
    logw = -math.exp(-0.5) * _sigmoid(d)
    a = _sigmoid(a0_ref[...] + _dot(la_ref[...], wa2_ref[...]))
    g = _dot(lg_ref[...], wg2_ref[...])

    li = lax.broadcasted_iota(I32, (LANES, LANES), 0)
    lj = lax.broadcasted_iota(I32, (LANES, LANES), 1)
    same_head = (li // n) == (lj // n)
    head_sum = jnp.where(same_head, 1.0, 0.0)

    kk = k * kk_ref[...]
    kk = kk / jnp.maximum(jnp.sqrt(_dot_split(kk * kk, head_sum)), 1e-12)
    kt = k * (1.0 + (a - 1.0) * ka_ref[...])
    bonus = _dot_split(r * kt * rk_ref[...], head_sum) * v

    same_blk = (li // t) == (lj // t)
    m_strict = same_blk & (lj < li)
    m_incl = same_blk & (lj <= li)
    ti = lax.broadcasted_iota(I32, (t, t), 0)
    tj = lax.broadcasted_iota(I32, (t, t), 1)
    tril_incl = jnp.where(tj <= ti, 1.0, 0.0)
    lane_a = lax.broadcasted_iota(I32, (t, LANES), 1) < n
    eye = jnp.where(li == lj, 1.0, 0.0)

    def stack_masked(z):
        return jnp.concatenate([jnp.where(lane_a, z, 0.0), jnp.where(lane_a, 0.0, z)], axis=0)

    def stack_plain(z):
        return jnp.concatenate([z, z], axis=0)

    def rows(z, c):
        return z[c * t:(c + 1) * t]

    cum = jnp.concatenate([_dot_split_t(tril_incl, rows(logw, c)) for c in chunks], axis=0)
    g_t = jnp.exp(cum)
    g_inv = jnp.exp(-cum)
    xa_f = -kk * jnp.exp(cum - logw)
    xr_f = r * g_t
    yb_f = kk * a * g_inv
    yk_f = kt * g_inv

    zeros = jnp.zeros((2 * t, LANES), F32)
    state = [st_ref[...]]
    ys = {}

    def chain_step(c, r_hat, y_hat, pm, qm):
        def run():
            st = state[0]
            y2 = _dot_nt(r_hat, st) + y_hat
            ys[c] = y2[:t] + y2[t:]
            state[0] = (st + _dot(st, pm) + qm) * g_t[(c + 1) * t - 1:(c + 1) * t, :]
        return run

    def chunk_local(grp, between):
        idx = range(len(grp))
        xa = [stack_masked(rows(xa_f, c)) for c in grp]
        xr = [stack_masked(rows(xr_f, c)) for c in grp]
        yb = [stack_plain(rows(yb_f, c)) for c in grp]
        vs = [stack_masked(rows(v, c)) for c in grp]
        ybk = [jnp.concatenate([yb[i], stack_plain(rows(yk_f, c))], axis=0) for i, c in enumerate(grp)]
        sc = [_dot_nt(jnp.concatenate([xa[i], xr[i]], axis=0), ybk[i]) for i in idx]
        between()
        a_ab = [jnp.where(m_strict, sc[i][:2 * t, :2 * t], 0.0) for i in idx]
        a_ak = [jnp.where(m_strict, sc[i][:2 * t, 2 * t:], 0.0) for i in idx]
        a_rb = [jnp.where(m_incl, sc[i][2 * t:, :2 * t], 0.0) for i in idx]
        a_rk = [jnp.where(m_incl, sc[i][2 * t:, 2 * t:], 0.0) for i in idx]
        inv = [eye + a_ab[i] for i in idx]
        pw = [_dot(a_ab[i], a_ab[i]) for i in idx]
        for _ in range(int(math.log2(t)) - 2):
            between()
            both = [_dot(pw[i], jnp.concatenate([pw[i], inv[i]], axis=1)) for i in idx]
            pw = [both[i][:, :LANES] for i in idx]
            inv = [inv[i] + both[i][:, LANES:] for i in idx]
        between()
        inv = [inv[i] + _dot(pw[i], inv[i]) for i in idx]
        av = [_dot(a_ak[i], vs[i]) for i in idx]
        between()
        mw = [_dot(inv[i], jnp.concatenate([xa[i], av[i]], axis=1)) for i in idx]
        between()
        rw = [_dot(jnp.concatenate([a_rb[i], a_rk[i]], axis=1),
                   jnp.concatenate([mw[i], jnp.concatenate([zeros, vs[i]], axis=1)], axis=0)) for i in idx]
        between()
        pm = [jnp.where(same_head, _dot_tn(mw[i][:, :LANES], yb[i]), 0.0) for i in idx]
        qm = [jnp.where(same_head, _dot_tn(jnp.concatenate([mw[i][:, LANES:], vs[i]], axis=0), ybk[i]), 0.0)
              for i in idx]
        return [chain_step(c, xr[i] + rw[i][:, :LANES], rw[i][:, LANES:], pm[i], qm[i])
                for i, c in enumerate(grp)]

    pending = []

    def between():
        if pending:
            pending.pop(0)()

    group = RWKV_CHUNK_GROUP
    for g0 in range(0, len(chunks), group):
        steps = chunk_local(list(chunks[g0:g0 + group]), between)
        while pending:
            pending.pop(0)()
        pending.extend(steps)
    while pending:
        pending.pop(0)()
    st_ref[...] = state[0]
    y = jnp.concatenate([ys[c] for c in chunks], axis=0)

    mean = _dot_split(y, head_sum) * (1.0 / n)
    yc = y - mean
    var = _dot_split(yc * yc, head_sum) * (1.0 / n)
    yn = yc * lax.rsqrt(var + GN_EPS) * gnw_ref[...] + gnb_ref[...]
    o_ref[...] = ((yn + bonus) * g).astype(BF16)


def _rwkv(proj, lw, la, lg, w_w2, w_a2, w_g2, mu_r, mu_k, mu_v, w0, a0, k_k, k_a, r_k, gn_w, gn_b,
          col0, width):
    s = proj.shape[0]
    ts = min(RWKV_ROWS_PER_STEP, s)
    npair = width // LANES
    cb = col0 // LANES
    row = lambda a: pl.BlockSpec((ts, a.shape[1]), lambda p, j: (j, 0))
    wcol = lambda a: pl.BlockSpec((a.shape[0], LANES), lambda p, j: (0, p))
    vec = pl.BlockSpec((1, LANES), lambda p, j: (0, p))
    return pl.pallas_call(
        _rwkv_kernel,
        grid=(npair, s // ts),
        in_specs=[pl.BlockSpec((ts, LANES), lambda p, j: (j, cb + p)),
                  pl.BlockSpec((ts, LANES), lambda p, j: (j, cb + npair + p)),
                  pl.BlockSpec((ts, LANES), lambda p, j: (j, cb + 2 * npair + p)),
                  row(lw), row(la), row(lg), wcol(w_w2), wcol(w_a2), wcol(w_g2)] + [vec] * 10,
        out_specs=pl.BlockSpec((ts, LANES), lambda p, j: (j, p)),
        out_shape=jax.ShapeDtypeStruct((s, width), BF16),
        scratch_shapes=[pltpu.VMEM((LANES, LANES), F32), pltpu.VMEM((SUBLANES, LANES), F32)],
        name="rwkv7_scan",
        compiler_params=_params("parallel", "arbitrary"),
    )(proj, proj, proj, lw, la, lg, w_w2, w_a2, w_g2, mu_r, mu_k, mu_v, w0, a0, k_k, k_a, r_k, gn_w, gn_b)


def _route_kernel(x_ref, oa_ref, or_ref, ga_ref, gr_ref, wua_ref, wur_ref, wo_ref,
                  gt_ref, gpost_ref, gpre_ref, sc_ref, sh_ref, wr_ref, br_ref,
                  x1_ref, h2_ref, ei_ref, wt_ref, rk_ref, cnt_ref, run_ref, y_ref):
    i = pl.program_id(0)

    @pl.when(i == 0)
    def _():
        run_ref[...] = jnp.zeros_like(run_ref)
        y_ref[...] = jnp.zeros_like(y_ref)

    ua = jnp.dot(oa_ref[...], wua_ref[...], preferred_element_type=F32)
    ur = jnp.dot(or_ref[...], wur_ref[...], preferred_element_type=F32)
    logits = _route_logits(y_ref[...], x_ref, gt_ref, gpost_ref, gpre_ref, sc_ref, sh_ref, wr_ref, br_ref,
                           x1_ref, h2_ref)
    mix = _sigmoid(ga_ref[...].astype(F32)) * ua + _sigmoid(gr_ref[...].astype(F32)) * ur
    y_new = jnp.dot(mix.astype(BF16), wo_ref[...], preferred_element_type=F32)
    _route_choose(logits, jnp.where(i > 0, 1.0, 0.0), ei_ref, wt_ref, rk_ref, run_ref)
    cnt_ref[...] = run_ref[...].astype(I32)
    y_ref[...] = y_new


def _route_logits(y, x_ref, gt_ref, gpost_ref, gpre_ref, sc_ref, sh_ref, wr_ref, br_ref, x1_ref, h2_ref):
    x1 = x_ref[...] + gt_ref[...] * _rms(y, gpost_ref[...])
    x1_ref[...] = x1
    h2 = _rms(x1, gpre_ref[...]) * (1.0 + sc_ref[...]) + sh_ref[...]
    h2_ref[...] = h2
    return _dot_x3(h2, wr_ref[...]) + br_ref[...]


def _route_choose(logits, valid, ei_ref, wt_ref, rk_ref, run_ref):
    rs = slice(None)
    lane = lax.broadcasted_iota(I32, logits.shape, 1)
    big = jnp.int32(4 * LANES)
    gmask = (lane >= N_EXPERTS) & (lane < N_EXPERTS + N_GROUPS)
    mg = jnp.max(jnp.where(gmask, logits, -jnp.inf), axis=1, keepdims=True)
    eg = jnp.where(gmask, jnp.exp(logits - mg), 0.0)
    pg = eg / jnp.sum(eg, axis=1, keepdims=True)
    pg_top = jnp.max(pg, axis=1, keepdims=True)
    g_idx = jnp.min(jnp.where(gmask & (pg == pg_top), lane, big), axis=1, keepdims=True) - N_EXPERTS
    emask = (lane >= g_idx * EXPERTS_PER_GROUP) & (lane < (g_idx + 1) * EXPERTS_PER_GROUP)
    me = jnp.max(jnp.where(emask, logits, -jnp.inf), axis=1, keepdims=True)
    ee = jnp.where(emask, jnp.exp(logits - me), 0.0)
    pe = ee / jnp.sum(ee, axis=1, keepdims=True)
    p1 = jnp.max(pe, axis=1, keepdims=True)
    i1 = jnp.min(jnp.where(emask & (pe == p1), lane, big), axis=1, keepdims=True)
    rest = emask & (lane != i1)
    p2 = jnp.max(jnp.where(rest, pe, -jnp.inf), axis=1, keepdims=True)
    i2 = jnp.min(jnp.where(rest & (pe == p2), lane, big), axis=1, keepdims=True)
    den = p1 + p2
    ei_ref[rs, :] = jnp.where(lane == 0, i1, jnp.where(lane == 1, i2, 0))
    wt_ref[rs, :] = jnp.where(lane == 0, pg_top * p1 / den, jnp.where(lane == 1, pg_top * p2 / den, 0.0))
    tm = logits.shape[0]
    chosen = jnp.where((lane == i1) | (lane == i2), valid, 0.0)
    t_i = lax.broadcasted_iota(I32, (tm, tm), 0)
    t_j = lax.broadcasted_iota(I32, (tm, tm), 1)
    before = _dot(jnp.where(t_j < t_i, 1.0, 0.0), chosen) + run_ref[...]
    r1 = jnp.sum(jnp.where(lane == i1, before, 0.0), axis=1, keepdims=True)
    r2 = jnp.sum(jnp.where(lane == i2, before, 0.0), axis=1, keepdims=True)
    rk_ref[rs, :] = jnp.where(lane == 0, r1, jnp.where(lane == 1, r2, 0.0)).astype(I32)
    run_ref[...] += jnp.sum(chosen, axis=0, keepdims=True)


def _cast_kernel(w_ref, o_ref):
    o_ref[...] = w_ref[...].astype(o_ref.dtype)


def _to_bf16(w):
    k, n = w.shape
    tk = 512
    return pl.pallas_call(
        _cast_kernel,
        grid=(k // tk,),
        in_specs=[pl.BlockSpec((tk, n), lambda i: (i, 0))],
        out_specs=pl.BlockSpec((tk, n), lambda i: (i, 0)),
        out_shape=jax.ShapeDtypeStruct((k, n), BF16),
        name="cast_bf16",
        compiler_params=_params("parallel"),
    )(w)


def _route(x, o_att, o_rwkv, proj, gate_col0, w_up_att, w_up_rwkv, w_o, gt1, g_post, g_pre, sc2, sh2,
           w_router, b_router):
    s, d = x.shape
    ka, kr = o_att.shape[1], o_rwkv.shape[1]
    tm = 256
    gb = gate_col0 // d
    nt = s // tm
    proj_tile = lambda i: jnp.minimum(i, nt - 1)
    route_tile = lambda i: jnp.maximum(i - 1, 0)
    vec = pl.BlockSpec((1, d), lambda i: (0, 0))
    rowblk = pl.BlockSpec((tm, d), lambda i: (route_tile(i), 0))
    small = pl.BlockSpec((tm, LANES), lambda i: (route_tile(i), 0))
    lane_row = pl.BlockSpec((1, LANES), lambda i: (0, 0))
    once = pl.Buffered(1)
    return pl.pallas_call(
        _route_kernel,
        grid=(nt + 1,),
        in_specs=[rowblk,
                  pl.BlockSpec((tm, ka), lambda i: (proj_tile(i), 0)),
                  pl.BlockSpec((tm, kr), lambda i: (proj_tile(i), 0)),
                  pl.BlockSpec((tm, d), lambda i: (proj_tile(i), gb)),
                  pl.BlockSpec((tm, d), lambda i: (proj_tile(i), gb + 1)),
                  pl.BlockSpec((ka, d), lambda i: (0, 0), pipeline_mode=once),
                  pl.BlockSpec((kr, d), lambda i: (0, 0), pipeline_mode=once),
                  pl.BlockSpec((d, d), lambda i: (0, 0), pipeline_mode=once),
                  vec, vec, vec, vec, vec, pl.BlockSpec((d, LANES), lambda i: (0, 0)), lane_row],
        out_specs=[rowblk, rowblk, small, small, small, lane_row],
        out_shape=[jax.ShapeDtypeStruct((s, d), F32), jax.ShapeDtypeStruct((s, d), F32),
                   jax.ShapeDtypeStruct((s, LANES), I32), jax.ShapeDtypeStruct((s, LANES), F32),
                   jax.ShapeDtypeStruct((s, LANES), I32), jax.ShapeDtypeStruct((1, LANES), I32)],
        scratch_shapes=[pltpu.VMEM((1, LANES), F32), pltpu.VMEM((tm, d), F32)],
        name="merge_out_route",
        compiler_params=_params("arbitrary"),
    )(x, o_att, o_rwkv, proj, proj, _to_bf16(w_up_att), _to_bf16(w_up_rwkv), _to_bf16(w_o),
      gt1, g_post, g_pre, sc2, sh2, w_router, b_router)


def _slots_kernel(ei_ref, rk_ref, ps_ref, pos_ref):
    lane = lax.broadcasted_iota(I32, ei_ref.shape, 1)
    ei, rk = ei_ref[...], rk_ref[...]
    ps = ps_ref[...]
    cols = []
    for kk in range(TOP_K):
        e = ei[:, kk:kk + 1]
        cols.append(jnp.sum(jnp.where(lane == e, ps, 0), axis=1, keepdims=True) + rk[:, kk:kk + 1])
    pos_ref[...] = jnp.where(lane == 0, cols[0], jnp.where(lane == 1, cols[1], 0))


def _slots(e_idx, rank, pstart_row):
    n = e_idx.shape[0]
    tm = 1024
    blk = pl.BlockSpec((tm, LANES), lambda i: (i, 0))
    return pl.pallas_call(
        _slots_kernel,
        grid=(n // tm,),
        in_specs=[blk, blk, pl.BlockSpec((1, LANES), lambda i: (0, 0))],
        out_specs=blk,
        out_shape=jax.ShapeDtypeStruct((n, LANES), I32),
        name="dispatch_slots",
        compiler_params=_params("parallel"),
    )(e_idx, rank, pstart_row)


def _row_copy(src, row, dst, dst_row, sem):
    return pltpu.make_async_copy(src.at[pl.ds(row, 1)], dst.at[pl.ds(dst_row, 1)], sem)


def _dispatch_kernel(pos_ref, cnt_ref, pst_ref, nused_ref, h_ref, x_hbm, stage_ref, zero_ref, sem, zsem):
    i = pl.program_id(0)
    tm = h_ref.shape[0]
    n_blk = x_hbm.shape[0] // EXPERT_BLOCK
    nused = nused_ref[0]
    slot = i % 2

    def wait_tile(s):
        for _ in range(TOP_K):
            pltpu.make_async_copy(stage_ref.at[s], x_hbm.at[pl.ds(0, tm)], sem.at[s]).wait()

    @pl.when(i >= 2)
    def _():
        wait_tile(slot)

    stage_ref[slot] = h_ref[...]
    for r in range(tm):
        for kk in range(TOP_K):
            _row_copy(stage_ref.at[slot], r, x_hbm, pos_ref[(i * tm + r) * TOP_K + kk],
                      sem.at[slot]).start(priority=(r * TOP_K + kk) % 2)

    @pl.when(i == 0)
    def _():
        zero_ref[...] = jnp.zeros_like(zero_ref)

        def pad_expert(e, total):
            lo = pst_ref[e] + cnt_ref[e]
            hi = pst_ref[e] + (cnt_ref[e] + EXPERT_BLOCK - 1) // EXPERT_BLOCK * EXPERT_BLOCK

            def pad_row(s, _):
                _row_copy(zero_ref, 0, x_hbm, s, zsem).start()
                return 0

            lax.fori_loop(lo, hi, pad_row, 0)
            return total + (hi - lo)

        n_pad = lax.fori_loop(0, cnt_ref.shape[0], pad_expert, 0)

        def tail_copy(blk):
            return pltpu.make_async_copy(zero_ref, x_hbm.at[pl.ds(blk * EXPERT_BLOCK, EXPERT_BLOCK)], zsem)

        def tail_start(blk, _):
            tail_copy(blk).start()
            return 0

        def pad_wait(_, c):
            _row_copy(zero_ref, 0, x_hbm, 0, zsem).wait()
            return c

        def tail_wait(blk, _):
            tail_copy(blk).wait()
            return 0

        lax.fori_loop(nused, n_blk, tail_start, 0)
        lax.fori_loop(0, n_pad, pad_wait, 0)
        lax.fori_loop(nused, n_blk, tail_wait, 0)

    @pl.when(i == pl.num_programs(0) - 1)
    def _():
        @pl.when(i >= 1)
        def _():
            wait_tile(1 - slot)

        wait_tile(slot)


def _dispatch(pos_flat, counts, pstart, nused, h2, n_slots):
    n, w = h2.shape
    tm = 512
    grid_spec = pltpu.PrefetchScalarGridSpec(
        num_scalar_prefetch=4,
        grid=(n // tm,),
        in_specs=[pl.BlockSpec((tm, w), lambda i, *_: (i, 0))],
        out_specs=pl.BlockSpec(memory_space=pl.ANY),
        scratch_shapes=[pltpu.VMEM((2, tm, w), h2.dtype), pltpu.VMEM((EXPERT_BLOCK, w), h2.dtype),
                        pltpu.SemaphoreType.DMA((2,)), pltpu.SemaphoreType.DMA(())],
    )
    return pl.pallas_call(
        _dispatch_kernel,
        grid_spec=grid_spec,
        out_shape=jax.ShapeDtypeStruct((n_slots, w), h2.dtype),
        name="dispatch_rows",
        compiler_params=_params("arbitrary"),
    )(pos_flat, counts, pstart, nused, h2)


EXPERT_X_SLOTS = 4
EXPERT_Y_SLOTS = 3


def _expert_kernel(bstart_ref, bcount_ref, nused_ref, x_hbm, wg_ref, wu_ref, wd_ref, y_hbm,
                   xs_ref, yo_ref, wgb_ref, wub_ref, wdb_ref, xsem, osem):
    e = pl.program_id(0)
    nused = nused_ref[0]
    rows = EXPERT_BLOCK
    n_blk = y_hbm.shape[0] // rows
    nx, ny = EXPERT_X_SLOTS, EXPERT_Y_SLOTS
    first = bstart_ref[e]
    count = bcount_ref[e]

    def x_copy(blk):
        src = x_hbm.at[pl.ds(jnp.minimum(blk, n_blk - 1) * rows, rows)]
        return pltpu.make_async_copy(src, xs_ref.at[blk % nx], xsem.at[blk % nx])

    def out_copy(blk):
        return pltpu.make_async_copy(yo_ref.at[blk % ny], y_hbm.at[pl.ds(blk * rows, rows), :], osem.at[blk % ny])

    @pl.when(e == 0)
    def _():
        for j in range(nx - 1):
            x_copy(j).start()

    @pl.when(count > 0)
    def _():
        wgb_ref[...] = wg_ref[0].astype(BF16)
        wub_ref[...] = wu_ref[0].astype(BF16)
        wdb_ref[...] = wd_ref[0].astype(BF16)

    def block(j, _):
        blk = first + j

        @pl.when(blk >= ny)
        def _():
            out_copy(blk - ny).wait()

        x_copy(blk).wait()
        x_copy(blk + nx - 1).start()
        xb = xs_ref[blk % nx].astype(BF16)
        hg = jnp.dot(xb, wgb_ref[...], preferred_element_type=F32)
        hu = jnp.dot(xb, wub_ref[...], preferred_element_type=F32)
        hid = hg * _sigmoid(hg) * hu
        yo_ref[blk % ny] = jnp.dot(hid.astype(BF16), wdb_ref[...], preferred_element_type=F32)
        out_copy(blk).start()

        @pl.when(blk == nused - 1)
        def _():
            for ahead in range(1, nx):
                x_copy(blk + ahead).wait()

        return 0

    lax.fori_loop(0, count, block, 0)

    @pl.when(e == pl.num_programs(0) - 1)
    def _():
        for back in range(1, ny + 1):
            @pl.when(nused - back >= 0)
            def _():
                out_copy(nused - back).wait()

        yo_ref[0] = jnp.zeros(yo_ref.shape[1:], yo_ref.dtype)

        def zero_copy(blk):
            return pltpu.make_async_copy(yo_ref.at[0], y_hbm.at[pl.ds(blk * rows, rows), :], osem.at[0])

        def fill(blk, _):
            zero_copy(blk).start()
            return 0

        def drain(blk, _):
            zero_copy(blk).wait()
            return 0

        lax.fori_loop(nused, n_blk, fill, 0)
        lax.fori_loop(nused, n_blk, drain, 0)


def _experts(x_buf, bstart, bcount, nused, w_gate_e, w_up_e, w_down_e):
    n_slots = x_buf.shape[0]
    n_exp, d, f = w_gate_e.shape
    grid_spec = pltpu.PrefetchScalarGridSpec(
        num_scalar_prefetch=3,
        grid=(n_exp,),
        in_specs=[pl.BlockSpec(memory_space=pl.ANY),
                  pl.BlockSpec((1, d, f), lambda e, *_: (e, 0, 0)),
                  pl.BlockSpec((1, d, f), lambda e, *_: (e, 0, 0)),
                  pl.BlockSpec((1, f, d), lambda e, *_: (e, 0, 0))],
        out_specs=pl.BlockSpec(memory_space=pl.ANY),
        scratch_shapes=[pltpu.VMEM((EXPERT_X_SLOTS, EXPERT_BLOCK, d), F32),
                        pltpu.VMEM((EXPERT_Y_SLOTS, EXPERT_BLOCK, d), F32),
                        pltpu.VMEM((d, f), BF16), pltpu.VMEM((d, f), BF16), pltpu.VMEM((f, d), BF16),
                        pltpu.SemaphoreType.DMA((EXPERT_X_SLOTS,)), pltpu.SemaphoreType.DMA((EXPERT_Y_SLOTS,))],
    )
    return pl.pallas_call(
        _expert_kernel,
        grid_spec=grid_spec,
        out_shape=jax.ShapeDtypeStruct((n_slots, d), F32),
        name="experts",
        compiler_params=_params("arbitrary"),
    )(bstart, bcount, nused, x_buf, w_gate_e, w_up_e, w_down_e)


def _combine_kernel(pos_ref, y_hbm, wt_ref, x1_ref, gt_ref, gpost_ref, o_ref, rows_ref, sem):
    i = pl.program_id(0)
    nsteps = pl.num_programs(0)
    tm = x1_ref.shape[0]

    def start_rows(step, slot):
        for r in range(tm):
            for kk in range(TOP_K):
                _row_copy(y_hbm, pos_ref[(step * tm + r) * TOP_K + kk], rows_ref.at[slot, kk], r,
                          sem.at[slot]).start(priority=(r * TOP_K + kk) % 2)

    def wait_rows(slot):
        for kk in range(TOP_K):
            pltpu.make_async_copy(y_hbm.at[pl.ds(0, tm), :], rows_ref.at[slot, kk], sem.at[slot]).wait()

    @pl.when(i == 0)
    def _():
        start_rows(0, 0)

    slot = i % 2
    wait_rows(slot)
    start_rows(jnp.minimum(i + 1, nsteps - 1), 1 - slot)
    wt = wt_ref[...]
    y = rows_ref[slot, 0] * wt[:, 0:1] + rows_ref[slot, 1] * wt[:, 1:2]
    o_ref[...] = x1_ref[...] + gt_ref[...] * _rms(y, gpost_ref[...])

    @pl.when(i == nsteps - 1)
    def _():
        wait_rows(1 - slot)


def _combine(pos, y_buf, wts, x1, gt2, g_post):
    n, d = x1.shape
    tm = 512
    vec = pl.BlockSpec((1, d), lambda i, p: (0, 0))
    grid_spec = pltpu.PrefetchScalarGridSpec(
        num_scalar_prefetch=1,
        grid=(n // tm,),
        in_specs=[pl.BlockSpec(memory_space=pl.ANY),
                  pl.BlockSpec((tm, LANES), lambda i, p: (i, 0)),
                  pl.BlockSpec((tm, d), lambda i, p: (i, 0)), vec, vec],
        out_specs=pl.BlockSpec((tm, d), lambda i, p: (i, 0)),
        scratch_shapes=[pltpu.VMEM((2, TOP_K, tm, d), F32), pltpu.SemaphoreType.DMA((2,))],
    )
    return pl.pallas_call(
        _combine_kernel,
        grid_spec=grid_spec,
        out_shape=jax.ShapeDtypeStruct((n, d), F32),
        name="combine",
        compiler_params=_params("arbitrary"),
    )(pos, y_buf, wts, x1, gt2, g_post)


def _segment_tables(counts_row):
    counts = counts_row[0, :N_EXPERTS]
    pcounts = (counts + EXPERT_BLOCK - 1) // EXPERT_BLOCK * EXPERT_BLOCK
    pend = jnp.cumsum(pcounts)
    pstart = pend - pcounts
    nused = (pend[-1] // EXPERT_BLOCK).astype(I32)
    pstart_row = jnp.pad(pstart, (0, LANES - N_EXPERTS)).reshape(1, LANES)
    return counts, pstart, pstart_row, pstart // EXPERT_BLOCK, pcounts // EXPERT_BLOCK, nused.reshape(1)


def _rope_tables(s):
    half = ROPE_DIM // 2
    inv = ROPE_THETA ** (-np.arange(half, dtype=np.float64) / half)
    ang = np.arange(s, dtype=np.float64)[:, None] * inv[None, :]
    cos, sin = jnp.asarray(np.cos(ang), F32), jnp.asarray(np.sin(ang), F32)
    pad = jnp.zeros((s, LANES - ROPE_DIM), F32)
    zero = jnp.zeros((s, half), F32)
    cos_t = jnp.concatenate([cos, cos, pad + 1.0], axis=1)
    sin1_t = jnp.concatenate([-sin, zero, pad], axis=1)
    sin2_t = jnp.concatenate([zero, sin, pad], axis=1)
    return cos_t, sin1_t, sin2_t


def _layer(x, c_col, w_ada, b_ada, g_pre_mix, g_post_mix, g_pre_ffn, g_post_ffn, w_in, mu_r, mu_k, mu_v,
           mu_w, mu_a, mu_g, w0, w_w1, w_w2, a0, w_a1, w_a2, w_g1, w_g2, k_k, k_a, r_k, gn_w, gn_b,
           w_up_att, w_up_rwkv, w_o, w_rg, b_rg, w_re, b_re, w_gate_e, w_up_e, w_down_e):
    s, d = x.shape
    att_w = ATT_HEADS * ATT_HEAD_DIM
    rwkv_w = w_up_rwkv.shape[0]
    row = lambda a: a.reshape(1, -1)

    ada = _ada(c_col, w_ada, row(b_ada))
    sh1, sc1, gt1, sh2, sc2, gt2 = (ada[:, i * d:(i + 1) * d] for i in range(6))

    h, lw, la, lg = _prenorm(x, row(g_pre_mix), sc1, sh1, row(mu_w), row(mu_a), row(mu_g), w_w1, w_a1, w_g1)
    proj = _matmul(h, w_in, BF16)

    q_aug_t, k_aug, v_t = _rope_gate(proj, *_rope_tables(s))
    o_att = _attention(q_aug_t, k_aug, v_t)

    o_rwkv = _rwkv(proj, lw, la, lg, w_w2, w_a2, w_g2, row(mu_r), row(mu_k), row(mu_v), row(w0), row(a0),
                   row(k_k), row(k_a), row(r_k), row(gn_w), row(gn_b), col0=3 * att_w, width=rwkv_w)


    w_router = jnp.pad(jnp.concatenate([w_re, w_rg], axis=1), ((0, 0), (0, LANES - N_EXPERTS - N_GROUPS)))
    b_router = jnp.pad(jnp.concatenate([b_re, b_rg]), (0, LANES - N_EXPERTS - N_GROUPS)).reshape(1, LANES)
    x1, h2, e_idx, wts, rank, counts = _route(x, o_att, o_rwkv, proj, 3 * att_w + 3 * rwkv_w, w_up_att, w_up_rwkv,
                                              w_o, gt1, row(g_post_mix), row(g_pre_ffn), sc2, sh2,
                                              w_router, b_router)

    n_pairs = s * TOP_K
    n_blk = (n_pairs + N_EXPERTS * (EXPERT_BLOCK - 1) + EXPERT_BLOCK - 1) // EXPERT_BLOCK
    counts, pstart, pstart_row, bstart, bcount, nused = _segment_tables(counts)
    pos = _slots(e_idx, rank, pstart_row)[:, :TOP_K].reshape(n_pairs)
    x_buf = _dispatch(pos, counts, pstart, nused, h2, n_blk * EXPERT_BLOCK)
    y_buf = _experts(x_buf, bstart, bcount, nused, w_gate_e, w_up_e, w_down_e)
    return _combine(pos, y_buf, wts, x1, gt2, row(g_post_ffn))


def kernel(x, c, w_ada, b_ada, g_pre_mix, g_post_mix, g_pre_ffn, g_post_ffn, w_in, mu_r, mu_k, mu_v, mu_w, mu_a, mu_g, w0, w_w1, w_w2, a0, w_a1, w_a2, w_g1, w_g2, k_k, k_a, r_k, gn_w, gn_b, w_up_att, w_up_rwkv, w_o, w_rg, b_rg, w_re, b_re, w_gate_e, w_up_e, w_down_e):
    b, s, d = x.shape
    assert b == 1, "one sequence per call"
    params = (w_ada, b_ada, g_pre_mix, g_post_mix, g_pre_ffn, g_post_ffn, w_in, mu_r, mu_k, mu_v, mu_w, mu_a,
              mu_g, w0, w_w1, w_w2, a0, w_a1, w_a2, w_g1, w_g2, k_k, k_a, r_k, gn_w, gn_b, w_up_att,
              w_up_rwkv, w_o, w_rg, b_rg, w_re, b_re, w_gate_e, w_up_e, w_down_e)
    xs = x.reshape(s, d)
    c_col = c.reshape(d, 1)
    for l in range(w_ada.shape[0]):
        xs = _layer(xs, c_col, *(p[l] for p in params))
    return xs.reshape(b, s, d)
```

```python
import math

import jax
import jax.numpy as jnp
import numpy as np
from jax import lax
from jax.experimental import pallas as pl
from jax.experimental.pallas import tpu as pltpu

F32 = jnp.float32
BF16 = jnp.bfloat16
I32 = jnp.int32
HI = lax.Precision.HIGHEST

LANES = 128
SUBLANES = 8
VMEM_LIMIT = 56 * 1024 * 1024

ATT_HEADS = 8
ATT_HEAD_DIM = 128
MOBA_BLOCK = 256
MOBA_TOPK = 3
ATT_GROUP = 4
ATT_HEADS_PER_STEP = 2
ROPE_THETA = 500000.0
ROPE_DIM = ATT_HEAD_DIM // 4
RWKV_HEAD_DIM = 64
GN_EPS = 64e-5
N_GROUPS = 8
EXPERTS_PER_GROUP = 8
N_EXPERTS = N_GROUPS * EXPERTS_PER_GROUP
TOP_K = 2
EXPERT_BLOCK = 128
RMS_EPS = 1e-6
NEG = -1e30
SCAN_CHUNK = 64
RWKV_CHUNK_GROUP = 8
RWKV_ROWS_PER_STEP = 2048
Q_SCALE = ATT_HEAD_DIM ** -0.5 * math.log2(math.e)


def _params(*sem):
    return pltpu.CompilerParams(dimension_semantics=sem, vmem_limit_bytes=VMEM_LIMIT)


def _rms(z, g):
    return z * lax.rsqrt(jnp.mean(z * z, axis=-1, keepdims=True) + RMS_EPS) * g


def _sigmoid(z):
    return 1.0 / (1.0 + jnp.exp(-z))


def _dot(a, b):
    return jnp.dot(a.astype(BF16), b.astype(BF16), preferred_element_type=F32)


def _dot_nt(a, b):
    return lax.dot_general(a.astype(BF16), b.astype(BF16), (((1,), (1,)), ((), ())),
                           preferred_element_type=F32)


def _dot_tn(a, b):
    return lax.dot_general(a.astype(BF16), b.astype(BF16), (((0,), (0,)), ((), ())),
                           preferred_element_type=F32)


def _dot_hi(a, b):
    return jnp.dot(a, b, precision=HI, preferred_element_type=F32)


def _dot_x3(a, b):
    ah, al, _ = _split3(a)
    bh, bl, _ = _split3(b)
    return (jnp.dot(ah, bh, preferred_element_type=F32) + jnp.dot(ah, bl, preferred_element_type=F32)
            + jnp.dot(al, bh, preferred_element_type=F32))


def _split3(a):
    hi = a.astype(BF16)
    r1 = a - hi.astype(F32)
    mid = r1.astype(BF16)
    lo = (r1 - mid.astype(F32)).astype(BF16)
    return hi, mid, lo


def _dot_split(a, b01):
    b = b01.astype(BF16)
    hi, mid, _ = _split3(a)
    return jnp.dot(jnp.concatenate([hi, mid], axis=1), jnp.concatenate([b, b], axis=0),
                   preferred_element_type=F32)


def _dot_split_t(b01, a):
    b = b01.astype(BF16)
    return jnp.dot(jnp.concatenate([b, b, b], axis=1), jnp.concatenate(_split3(a), axis=0),
                   preferred_element_type=F32)


def _shift_rows(z, prev_row):
    rolled = pltpu.roll(z, 1, 0)
    row = lax.broadcasted_iota(I32, z.shape, 0)
    return jnp.where(row == 0, prev_row, rolled)


def _ada_kernel(c_ref, w_ref, b_ref, o_ref):
    o_ref[...] = jnp.sum(c_ref[...] * w_ref[...], axis=0, keepdims=True) + b_ref[...]


def _ada(c_col, w_ada, b_ada):
    d, n = w_ada.shape
    tn = 1024
    return pl.pallas_call(
        _ada_kernel,
        grid=(n // tn,),
        in_specs=[pl.BlockSpec((d, 1), lambda j: (0, 0)),
                  pl.BlockSpec((d, tn), lambda j: (0, j)),
                  pl.BlockSpec((1, tn), lambda j: (0, j))],
        out_specs=pl.BlockSpec((1, tn), lambda j: (0, j)),
        out_shape=jax.ShapeDtypeStruct((1, n), F32),
        name="ada",
        compiler_params=_params("parallel"),
    )(c_col, w_ada, b_ada)


def _prenorm_kernel(x_ref, xp_ref, g_ref, sc_ref, sh_ref, muw_ref, mua_ref, mug_ref,
                    ww1_ref, wa1_ref, wg1_ref, h_ref, lw_ref, la_ref, lg_ref):
    i = pl.program_id(0)
    g, sc, sh = g_ref[...], sc_ref[...], sh_ref[...]
    h = _rms(x_ref[...], g) * (1.0 + sc) + sh
    hp = _rms(xp_ref[SUBLANES - 1:SUBLANES, :], g) * (1.0 + sc) + sh
    hp = jnp.where(i == 0, 0.0, hp)
    dh = _shift_rows(h, hp) - h
    h_ref[...] = h.astype(BF16)
    lw_ref[...] = jnp.tanh(_dot(h + dh * muw_ref[...], ww1_ref[...]))
    la_ref[...] = _dot(h + dh * mua_ref[...], wa1_ref[...])
    lg_ref[...] = _sigmoid(_dot(h + dh * mug_ref[...], wg1_ref[...]))


def _prenorm(x, g, sc, sh, mu_w, mu_a, mu_g, w_w1, w_a1, w_g1):
    s, d = x.shape
    tm = 256
    rpb = tm // SUBLANES
    vec = pl.BlockSpec((1, d), lambda i: (0, 0))
    full = lambda a: pl.BlockSpec(a.shape, lambda i: (0, 0))
    lw, la, lg = w_w1.shape[1], w_a1.shape[1], w_g1.shape[1]
    return pl.pallas_call(
        _prenorm_kernel,
        grid=(s // tm,),
        in_specs=[pl.BlockSpec((tm, d), lambda i: (i, 0)),
                  pl.BlockSpec((SUBLANES, d), lambda i: (jnp.maximum(i * rpb - 1, 0), 0)),
                  vec, vec, vec, vec, vec, vec, full(w_w1), full(w_a1), full(w_g1)],
        out_specs=[pl.BlockSpec((tm, d), lambda i: (i, 0)),
                   pl.BlockSpec((tm, lw), lambda i: (i, 0)),
                   pl.BlockSpec((tm, la), lambda i: (i, 0)),
                   pl.BlockSpec((tm, lg), lambda i: (i, 0))],
        out_shape=[jax.ShapeDtypeStruct((s, d), BF16),
                   jax.ShapeDtypeStruct((s, lw), F32),
                   jax.ShapeDtypeStruct((s, la), F32),
                   jax.ShapeDtypeStruct((s, lg), F32)],
        name="prenorm_lora",
        compiler_params=_params("parallel"),
    )(x, x, g, sc, sh, mu_w, mu_a, mu_g, w_w1, w_a1, w_g1)


def _mm_kernel(a_ref, w_ref, o_ref, wb_ref):
    @pl.when(pl.program_id(1) == 0)
    def _():
        wb_ref[...] = w_ref[...].astype(BF16)

    o_ref[...] = jnp.dot(a_ref[...], wb_ref[...], preferred_element_type=F32).astype(o_ref.dtype)


def _matmul(a, w, out_dtype, tm=512, tn=1024):
    m, k = a.shape
    n = w.shape[1]
    tn = min(tn, n)
    return pl.pallas_call(
        _mm_kernel,
        grid=(n // tn, m // tm),
        in_specs=[pl.BlockSpec((tm, k), lambda j, i: (i, 0)),
                  pl.BlockSpec((k, tn), lambda j, i: (0, j))],
        out_specs=pl.BlockSpec((tm, tn), lambda j, i: (i, j)),
        out_shape=jax.ShapeDtypeStruct((m, n), out_dtype),
        scratch_shapes=[pltpu.VMEM((k, tn), BF16)],
        name="matmul",
        compiler_params=_params("arbitrary", "arbitrary"),
    )(a, w)


def _rope_gate_kernel(p_ref, c_ref, s1_ref, s2_ref, qa_ref, ka_ref, vt_ref, km_ref):
    i = pl.program_id(0)
    bs = MOBA_BLOCK
    nbp = km_ref.shape[1]

    @pl.when(i == 0)
    def _():
        km_ref[...] = jnp.zeros_like(km_ref)

    c, s1, s2 = c_ref[...], s1_ref[...], s2_ref[...]

    def rope(z):
        return z * c + pltpu.roll(z, LANES - ROPE_DIM // 2, 1) * s1 + pltpu.roll(z, ROPE_DIM // 2, 1) * s2

    row = lax.broadcasted_iota(I32, (nbp, bs), 0)
    lane = lax.broadcasted_iota(I32, (bs, LANES), 1)
    onehot = jnp.where(lane == i, 1.0, 0.0).astype(BF16)
    for h in range(ATT_HEADS):
        q = rope(p_ref[:, h * LANES:(h + 1) * LANES].astype(F32))
        k = rope(p_ref[:, (ATT_HEADS + h) * LANES:(ATT_HEADS + h + 1) * LANES].astype(F32))
        g = lax.dot_general(km_ref[h], q, (((1,), (1,)), ((), ())), precision=HI, preferred_element_type=F32)
        g = jnp.where(row < i, g, NEG)
        sel_t = jnp.zeros(g.shape, F32)
        for _ in range(MOBA_TOPK):
            mx = jnp.max(g, axis=0, keepdims=True)
            idx = jnp.min(jnp.where(g == mx, row, nbp), axis=0, keepdims=True)
            hit = row == idx
            sel_t = jnp.where(hit & (row < i), 1.0, sel_t)
            g = jnp.where(hit, -jnp.inf, g)
        if nbp < LANES:
            sel_t = jnp.concatenate([sel_t, jnp.zeros((LANES - nbp, bs), F32)], axis=0)
        w = 2 * LANES
        qa_ref[h, :LANES, :] = (q * Q_SCALE).T.astype(BF16)
        qa_ref[h, LANES:, :] = jnp.where(sel_t > 0.5, 0.0, NEG).astype(BF16)
        ka_ref[:, h * w:h * w + LANES] = k.astype(BF16)
        ka_ref[:, h * w + LANES:(h + 1) * w] = onehot
        v = p_ref[:, (2 * ATT_HEADS + h) * LANES:(2 * ATT_HEADS + h + 1) * LANES].astype(F32)
        vt_ref[h, 0] = v.T.astype(BF16)
        km_ref[h, pl.ds(i, 1), :] = jnp.mean(k, axis=0, keepdims=True)


def _rope_gate(proj, cos_t, sin1_t, sin2_t):
    s = proj.shape[0]
    nb = s // MOBA_BLOCK
    assert nb <= LANES
    nbp = -(-nb // SUBLANES) * SUBLANES
    w_in = 3 * ATT_HEADS * ATT_HEAD_DIM
    w_out = 2 * ATT_HEADS * LANES
    tab = pl.BlockSpec((MOBA_BLOCK, LANES), lambda i: (i, 0))
    return pl.pallas_call(
        _rope_gate_kernel,
        grid=(nb,),
        in_specs=[pl.BlockSpec((MOBA_BLOCK, w_in), lambda i: (i, 0)), tab, tab, tab],
        out_specs=[pl.BlockSpec((ATT_HEADS, 2 * LANES, MOBA_BLOCK), lambda i: (0, 0, i)),
                   pl.BlockSpec((MOBA_BLOCK, w_out), lambda i: (i, 0)),
                   pl.BlockSpec((ATT_HEADS, 1, LANES, MOBA_BLOCK), lambda i: (0, i, 0, 0))],
        out_shape=[jax.ShapeDtypeStruct((ATT_HEADS, 2 * LANES, s), BF16),
                   jax.ShapeDtypeStruct((s, w_out), BF16),
                   jax.ShapeDtypeStruct((ATT_HEADS, nb, LANES, MOBA_BLOCK), BF16)],
        scratch_shapes=[pltpu.VMEM((ATT_HEADS, nbp, LANES), F32)],
        name="rope_gate",
        compiler_params=_params("arbitrary"),
    )(proj, cos_t, sin1_t, sin2_t)


def _attn_kernel(qa_ref, ka_ref, vt_ref, o_ref, s_ref):
    qi = pl.program_id(1)
    bs = MOBA_BLOCK
    grp = ATT_GROUP * bs
    w = 2 * LANES
    heads = range(ATT_HEADS_PER_STEP)

    n_groups = ka_ref.shape[0] // grp
    assert n_groups % 2 == 0

    def issue_scores(g, buf):
        base = pl.multiple_of(jnp.minimum(g, n_groups - 1) * grp, grp)
        for h in heads:
            s_ref[buf, h] = jnp.dot(ka_ref[pl.ds(base, grp), h * w:(h + 1) * w], qa_ref[h],
                                    preferred_element_type=F32)

    issue_scores(0, 0)

    own = pl.multiple_of(qi * bs, bs)
    k_i = lax.broadcasted_iota(I32, (bs, bs), 0)
    q_i = lax.broadcasted_iota(I32, (bs, bs), 1)
    carry = []
    for h in heads:
        s = jnp.dot(ka_ref[pl.ds(own, bs), h * w:h * w + LANES], qa_ref[h, :LANES, :],
                    preferred_element_type=F32)
        s = jnp.where(k_i <= q_i, s, NEG)
        m = jnp.max(s, axis=0, keepdims=True)
        p = jnp.exp2(s - m)
        l = jnp.sum(p, axis=0, keepdims=True)
        acc = jnp.dot(vt_ref[h, qi], p.astype(BF16), preferred_element_type=F32)
        carry += [m, l, acc]

    def absorb(g, buf, carry):
        out = []
        for h in heads:
            m, l, acc = carry[3 * h:3 * h + 3]
            s = s_ref[buf, h]
            mn = jnp.maximum(m, jnp.max(s, axis=0, keepdims=True))
            alpha = jnp.exp2(m - mn)
            p = jnp.exp2(s - mn)
            l = alpha * l + jnp.sum(p, axis=0, keepdims=True)
            p = p.astype(BF16)
            acc = alpha * acc
            for jb in range(ATT_GROUP):
                acc = acc + jnp.dot(vt_ref[h, g * ATT_GROUP + jb], p[jb * bs:(jb + 1) * bs],
                                    preferred_element_type=F32)
            out += [mn, l, acc]
        return out

    def body(t, carry):
        issue_scores(2 * t + 1, 1)
        carry = absorb(2 * t, 0, carry)
        issue_scores(2 * t + 2, 0)
        return tuple(absorb(2 * t + 1, 1, carry))

    n_used = (qi + ATT_GROUP - 1) // ATT_GROUP
    carry = lax.fori_loop(0, (n_used + 1) // 2, body, tuple(carry))
    for h in heads:
        m, l, acc = carry[3 * h:3 * h + 3]
        o_ref[:, h * LANES:(h + 1) * LANES] = (acc / l).T.astype(BF16)


def _attention(q_aug_t, k_aug, v_t):
    s = k_aug.shape[0]
    assert s % (ATT_GROUP * MOBA_BLOCK) == 0
    nb = s // MOBA_BLOCK
    hps = ATT_HEADS_PER_STEP
    once = pl.Buffered(1)
    return pl.pallas_call(
        _attn_kernel,
        grid=(ATT_HEADS // hps, nb),
        in_specs=[pl.BlockSpec((hps, 2 * LANES, MOBA_BLOCK), lambda h, i: (h, 0, i)),
                  pl.BlockSpec((s, hps * 2 * LANES), lambda h, i: (0, h), pipeline_mode=once),
                  pl.BlockSpec((hps, nb, LANES, MOBA_BLOCK), lambda h, i: (h, 0, 0, 0), pipeline_mode=once)],
        out_specs=pl.BlockSpec((MOBA_BLOCK, hps * LANES), lambda h, i: (i, h)),
        out_shape=jax.ShapeDtypeStruct((s, ATT_HEADS * ATT_HEAD_DIM), BF16),
        scratch_shapes=[pltpu.VMEM((2, hps, ATT_GROUP * MOBA_BLOCK, MOBA_BLOCK), F32)],
        name="moba_attention",
        compiler_params=_params("parallel", "arbitrary"),
    )(q_aug_t, k_aug, v_t)


def _rwkv_kernel(r_ref, k_ref, v_ref, lw_ref, la_ref, lg_ref, ww2_ref, wa2_ref, wg2_ref,
                 mur_ref, muk_ref, muv_ref, w0_ref, a0_ref, kk_ref, ka_ref, rk_ref, gnw_ref, gnb_ref,
                 o_ref, st_ref, prev_ref):
    j = pl.program_id(1)
    t = SCAN_CHUNK
    n = RWKV_HEAD_DIM
    ts = r_ref.shape[0]
    chunks = range(ts // t)

    @pl.when(j == 0)
    def _():
        st_ref[...] = jnp.zeros_like(st_ref)
        prev_ref[...] = jnp.zeros_like(prev_ref)

    rp, kp, vp = r_ref[...].astype(F32), k_ref[...].astype(F32), v_ref[...].astype(F32)
    r = rp + (_shift_rows(rp, prev_ref[0:1, :]) - rp) * mur_ref[...]
    k = kp + (_shift_rows(kp, prev_ref[1:2, :]) - kp) * muk_ref[...]
    v = vp + (_shift_rows(vp, prev_ref[2:3, :]) - vp) * muv_ref[...]
    prev_ref[0:1, :] = rp[ts - 1:ts, :]
    prev_ref[1:2, :] = kp[ts - 1:ts, :]
    prev_ref[2:3, :] = vp[ts - 1:ts, :]

    d = w0_ref[...] + _dot(lw_ref[...], ww2_ref[...])
    logw = -math.exp(-0.5) * _sigmoid(d)
    a = _sigmoid(a0_ref[...] + _dot(la_ref[...], wa2_ref[...]))
    g = _dot(lg_ref[...], wg2_ref[...])

    li = lax.broadcasted_iota(I32, (LANES, LANES), 0)
    lj = lax.broadcasted_iota(I32, (LANES, LANES), 1)
    same_head = (li // n) == (lj // n)
    head_sum = jnp.where(same_head, 1.0, 0.0)

    kk = k * kk_ref[...]
    kk = kk / jnp.maximum(jnp.sqrt(_dot_split(kk * kk, head_sum)), 1e-12)
    kt = k * (1.0 + (a - 1.0) * ka_ref[...])
    bonus = _dot_split(r * kt * rk_ref[...], head_sum) * v

    same_blk = (li // t) == (lj // t)
    m_strict = same_blk & (lj < li)
    m_incl = same_blk & (lj <= li)
    ti = lax.broadcasted_iota(I32, (t, t), 0)
    tj = lax.broadcasted_iota(I32, (t, t), 1)
    tril_incl = jnp.where(tj <= ti, 1.0, 0.0)
    lane_a = lax.broadcasted_iota(I32, (t, LANES), 1) < n
    eye = jnp.where(li == lj, 1.0, 0.0)

    def stack_masked(z):
        return jnp.concatenate([jnp.where(lane_a, z, 0.0), jnp.where(lane_a, 0.0, z)], axis=0)

    def stack_plain(z):
        return jnp.concatenate([z, z], axis=0)

    def rows(z, c):
        return z[c * t:(c + 1) * t]

    cum = jnp.concatenate([_dot_split_t(tril_incl, rows(logw, c)) for c in chunks], axis=0)
    g_t = jnp.exp(cum)
    g_inv = jnp.exp(-cum)
    xa_f = -kk * jnp.exp(cum - logw)
    xr_f = r * g_t
    yb_f = kk * a * g_inv
    yk_f = kt * g_inv

    zeros = jnp.zeros((2 * t, LANES), F32)
    state = [st_ref[...]]
    ys = {}

    def chain_step(c, r_hat, y_hat, pm, qm):
        def run():
            st = state[0]
            y2 = _dot_nt(r_hat, st) + y_hat
            ys[c] = y2[:t] + y2[t:]
            state[0] = (st + _dot(st, pm) + qm) * g_t[(c + 1) * t - 1:(c + 1) * t, :]
        return run

    def chunk_local(grp, between):
        idx = range(len(grp))
        xa = [stack_masked(rows(xa_f, c)) for c in grp]
        xr = [stack_masked(rows(xr_f, c)) for c in grp]
        yb = [stack_plain(rows(yb_f, c)) for c in grp]
        vs = [stack_masked(rows(v, c)) for c in grp]
        ybk = [jnp.concatenate([yb[i], stack_plain(rows(yk_f, c))], axis=0) for i, c in enumerate(grp)]
        sc = [_dot_nt(jnp.concatenate([xa[i], xr[i]], axis=0), ybk[i]) for i in idx]
        between()
        a_ab = [jnp.where(m_strict, sc[i][:2 * t, :2 * t], 0.0) for i in idx]
        a_ak = [jnp.where(m_strict, sc[i][:2 * t, 2 * t:], 0.0) for i in idx]
        a_rb = [jnp.where(m_incl, sc[i][2 * t:, :2 * t], 0.0) for i in idx]
        a_rk = [jnp.where(m_incl, sc[i][2 * t:, 2 * t:], 0.0) for i in idx]
        inv = [eye + a_ab[i] for i in idx]
        pw = [_dot(a_ab[i], a_ab[i]) for i in idx]
        for _ in range(int(math.log2(t)) - 2):
            between()
            both = [_dot(pw[i], jnp.concatenate([pw[i], inv[i]], axis=1)) for i in idx]
            pw = [both[i][:, :LANES] for i in idx]
            inv = [inv[i] + both[i][:, LANES:] for i in idx]
        between()
        inv = [inv[i] + _dot(pw[i], inv[i]) for i in idx]
        av = [_dot(a_ak[i], vs[i]) for i in idx]
        between()
        mw = [_dot(inv[i], jnp.concatenate([xa[i], av[i]], axis=1)) for i in idx]
        between()
        rw = [_dot(jnp.concatenate([a_rb[i], a_rk[i]], axis=1),
                   jnp.concatenate([mw[i], jnp.concatenate([zeros, vs[i]], axis=1)], axis=0)) for i in idx]
        between()
        pm = [jnp.where(same_head, _dot_tn(mw[i][:, :LANES], yb[i]), 0.0) for i in idx]
        qm = [jnp.where(same_head, _dot_tn(jnp.concatenate([mw[i][:, LANES:], vs[i]], axis=0), ybk[i]), 0.0)
              for i in idx]
        return [chain_step(c, xr[i] + rw[i][:, :LANES], rw[i][:, LANES:], pm[i], qm[i])
                for i, c in enumerate(grp)]

    pending = []

    def between():
        if pending:
            pending.pop(0)()

    group = RWKV_CHUNK_GROUP
    for g0 in range(0, len(chunks), group):
        steps = chunk_local(list(chunks[g0:g0 + group]), between)
        while pending:
            pending.pop(0)()
        pending.extend(steps)
    while pending:
        pending.pop(0)()
    st_ref[...] = state[0]
    y = jnp.concatenate([ys[c] for c in chunks], axis=0)

    mean = _dot_split(y, head_sum) * (1.0 / n)
    yc = y - mean
    var = _dot_split(yc * yc, head_sum) * (1.0 / n)
    yn = yc * lax.rsqrt(var + GN_EPS) * gnw_ref[...] + gnb_ref[...]
    o_ref[...] = ((yn + bonus) * g).astype(BF16)


def _rwkv(proj, lw, la, lg, w_w2, w_a2, w_g2, mu_r, mu_k, mu_v, w0, a0, k_k, k_a, r_k, gn_w, gn_b,
          col0, width):
    s = proj.shape[0]
    ts = min(RWKV_ROWS_PER_STEP, s)
    npair = width // LANES
    cb = col0 // LANES
    row = lambda a: pl.BlockSpec((ts, a.shape[1]), lambda p, j: (j, 0))
    wcol = lambda a: pl.BlockSpec((a.shape[0], LANES), lambda p, j: (0, p))
    vec = pl.BlockSpec((1, LANES), lambda p, j: (0, p))
    return pl.pallas_call(
        _rwkv_kernel,
        grid=(npair, s // ts),
        in_specs=[pl.BlockSpec((ts, LANES), lambda p, j: (j, cb + p)),
                  pl.BlockSpec((ts, LANES), lambda p, j: (j, cb + npair + p)),
                  pl.BlockSpec((ts, LANES), lambda p, j: (j, cb + 2 * npair + p)),
                  row(lw), row(la), row(lg), wcol(w_w2), wcol(w_a2), wcol(w_g2)] + [vec] * 10,
        out_specs=pl.BlockSpec((ts, LANES), lambda p, j: (j, p)),
        out_shape=jax.ShapeDtypeStruct((s, width), BF16),
        scratch_shapes=[pltpu.VMEM((LANES, LANES), F32), pltpu.VMEM((SUBLANES, LANES), F32)],
        name="rwkv7_scan",
        compiler_params=_params("parallel", "arbitrary"),
    )(proj, proj, proj, lw, la, lg, w_w2, w_a2, w_g2, mu_r, mu_k, mu_v, w0, a0, k_k, k_a, r_k, gn_w, gn_b)


def _route_kernel(x_ref, oa_ref, or_ref, ga_ref, gr_ref, wua_ref, wur_ref, wo_ref,
                  gt_ref, gpost_ref, gpre_ref, sc_ref, sh_ref, wr_ref, br_ref,
                  x1_ref, h2_ref, ei_ref, wt_ref, rk_ref, cnt_ref, run_ref, y_ref):
    i = pl.program_id(0)

    @pl.when(i == 0)
    def _():
        run_ref[...] = jnp.zeros_like(run_ref)
        y_ref[...] = jnp.zeros_like(y_ref)

    ua = jnp.dot(oa_ref[...], wua_ref[...], preferred_element_type=F32)
    ur = jnp.dot(or_ref[...], wur_ref[...], preferred_element_type=F32)
    logits = _route_logits(y_ref[...], x_ref, gt_ref, gpost_ref, gpre_ref, sc_ref, sh_ref, wr_ref, br_ref,
                           x1_ref, h2_ref)
    mix = _sigmoid(ga_ref[...].astype(F32)) * ua + _sigmoid(gr_ref[...].astype(F32)) * ur
    y_new = jnp.dot(mix.astype(BF16), wo_ref[...], preferred_element_type=F32)
    _route_choose(logits, jnp.where(i > 0, 1.0, 0.0), ei_ref, wt_ref, rk_ref, run_ref)
    cnt_ref[...] = run_ref[...].astype(I32)
    y_ref[...] = y_new


def _route_logits(y, x_ref, gt_ref, gpost_ref, gpre_ref, sc_ref, sh_ref, wr_ref, br_ref, x1_ref, h2_ref):
    x1 = x_ref[...] + gt_ref[...] * _rms(y, gpost_ref[...])
    x1_ref[...] = x1
    h2 = _rms(x1, gpre_ref[...]) * (1.0 + sc_ref[...]) + sh_ref[...]
    h2_ref[...] = h2
    return _dot_x3(h2, wr_ref[...]) + br_ref[...]


def _route_choose(logits, valid, ei_ref, wt_ref, rk_ref, run_ref):
    rs = slice(None)
    lane = lax.broadcasted_iota(I32, logits.shape, 1)
    big = jnp.int32(4 * LANES)
    gmask = (lane >= N_EXPERTS) & (lane < N_EXPERTS + N_GROUPS)
    mg = jnp.max(jnp.where(gmask, logits, -jnp.inf), axis=1, keepdims=True)
    eg = jnp.where(gmask, jnp.exp(logits - mg), 0.0)
    pg = eg / jnp.sum(eg, axis=1, keepdims=True)
    pg_top = jnp.max(pg, axis=1, keepdims=True)
    g_idx = jnp.min(jnp.where(gmask & (pg == pg_top), lane, big), axis=1, keepdims=True) - N_EXPERTS
    emask = (lane >= g_idx * EXPERTS_PER_GROUP) & (lane < (g_idx + 1) * EXPERTS_PER_GROUP)
    me = jnp.max(jnp.where(emask, logits, -jnp.inf), axis=1, keepdims=True)
    ee = jnp.where(emask, jnp.exp(logits - me), 0.0)
    pe = ee / jnp.sum(ee, axis=1, keepdims=True)
    p1 = jnp.max(pe, axis=1, keepdims=True)
    i1 = jnp.min(jnp.where(emask & (pe == p1), lane, big), axis=1, keepdims=True)
    rest = emask & (lane != i1)
    p2 = jnp.max(jnp.where(rest, pe, -jnp.inf), axis=1, keepdims=True)
    i2 = jnp.min(jnp.where(rest & (pe == p2), lane, big), axis=1, keepdims=True)
    den = p1 + p2
    ei_ref[rs, :] = jnp.where(lane == 0, i1, jnp.where(lane == 1, i2, 0))
    wt_ref[rs, :] = jnp.where(lane == 0, pg_top * p1 / den, jnp.where(lane == 1, pg_top * p2 / den, 0.0))
    tm = logits.shape[0]
    chosen = jnp.where((lane == i1) | (lane == i2), valid, 0.0)
    t_i = lax.broadcasted_iota(I32, (tm, tm), 0)
    t_j = lax.broadcasted_iota(I32, (tm, tm), 1)
    before = _dot(jnp.where(t_j < t_i, 1.0, 0.0), chosen) + run_ref[...]
    r1 = jnp.sum(jnp.where(lane == i1, before, 0.0), axis=1, keepdims=True)
    r2 = jnp.sum(jnp.where(lane == i2, before, 0.0), axis=1, keepdims=True)
    rk_ref[rs, :] = jnp.where(lane == 0, r1, jnp.where(lane == 1, r2, 0.0)).astype(I32)
    run_ref[...] += jnp.sum(chosen, axis=0, keepdims=True)


def _cast_kernel(w_ref, o_ref):
    o_ref[...] = w_ref[...].astype(o_ref.dtype)


def _to_bf16(w):
    k, n = w.shape
    tk = 512
    return pl.pallas_call(
        _cast_kernel,
        grid=(k // tk,),
        in_specs=[pl.BlockSpec((tk, n), lambda i: (i, 0))],
        out_specs=pl.BlockSpec((tk, n), lambda i: (i, 0)),
        out_shape=jax.ShapeDtypeStruct((k, n), BF16),
        name="cast_bf16",
        compiler_params=_params("parallel"),
    )(w)


def _route(x, o_att, o_rwkv, proj, gate_col0, w_up_att, w_up_rwkv, w_o, gt1, g_post, g_pre, sc2, sh2,
           w_router, b_router):
    s, d = x.shape
    ka, kr = o_att.shape[1], o_rwkv.shape[1]
    tm = 256
    gb = gate_col0 // d
    nt = s // tm
    proj_tile = lambda i: jnp.minimum(i, nt - 1)
    route_tile = lambda i: jnp.maximum(i - 1, 0)
    vec = pl.BlockSpec((1, d), lambda i: (0, 0))
    rowblk = pl.BlockSpec((tm, d), lambda i: (route_tile(i), 0))
    small = pl.BlockSpec((tm, LANES), lambda i: (route_tile(i), 0))
    lane_row = pl.BlockSpec((1, LANES), lambda i: (0, 0))
    once = pl.Buffered(1)
    return pl.pallas_call(
        _route_kernel,
        grid=(nt + 1,),
        in_specs=[rowblk,
                  pl.BlockSpec((tm, ka), lambda i: (proj_tile(i), 0)),
                  pl.BlockSpec((tm, kr), lambda i: (proj_tile(i), 0)),
                  pl.BlockSpec((tm, d), lambda i: (proj_tile(i), gb)),
                  pl.BlockSpec((tm, d), lambda i: (proj_tile(i), gb + 1)),
                  pl.BlockSpec((ka, d), lambda i: (0, 0), pipeline_mode=once),
                  pl.BlockSpec((kr, d), lambda i: (0, 0), pipeline_mode=once),
                  pl.BlockSpec((d, d), lambda i: (0, 0), pipeline_mode=once),
                  vec, vec, vec, vec, vec, pl.BlockSpec((d, LANES), lambda i: (0, 0)), lane_row],
        out_specs=[rowblk, rowblk, small, small, small, lane_row],
        out_shape=[jax.ShapeDtypeStruct((s, d), F32), jax.ShapeDtypeStruct((s, d), F32),
                   jax.ShapeDtypeStruct((s, LANES), I32), jax.ShapeDtypeStruct((s, LANES), F32),
                   jax.ShapeDtypeStruct((s, LANES), I32), jax.ShapeDtypeStruct((1, LANES), I32)],
        scratch_shapes=[pltpu.VMEM((1, LANES), F32), pltpu.VMEM((tm, d), F32)],
        name="merge_out_route",
        compiler_params=_params("arbitrary"),
    )(x, o_att, o_rwkv, proj, proj, _to_bf16(w_up_att), _to_bf16(w_up_rwkv), _to_bf16(w_o),
      gt1, g_post, g_pre, sc2, sh2, w_router, b_router)


def _slots_kernel(ei_ref, rk_ref, ps_ref, pos_ref):
    lane = lax.broadcasted_iota(I32, ei_ref.shape, 1)
    ei, rk = ei_ref[...], rk_ref[...]
    ps = ps_ref[...]
    cols = []
    for kk in range(TOP_K):
        e = ei[:, kk:kk + 1]
        cols.append(jnp.sum(jnp.where(lane == e, ps, 0), axis=1, keepdims=True) + rk[:, kk:kk + 1])
    pos_ref[...] = jnp.where(lane == 0, cols[0], jnp.where(lane == 1, cols[1], 0))


def _slots(e_idx, rank, pstart_row):
    n = e_idx.shape[0]
    tm = 1024
    blk = pl.BlockSpec((tm, LANES), lambda i: (i, 0))
    return pl.pallas_call(
        _slots_kernel,
        grid=(n // tm,),
        in_specs=[blk, blk, pl.BlockSpec((1, LANES), lambda i: (0, 0))],
        out_specs=blk,
        out_shape=jax.ShapeDtypeStruct((n, LANES), I32),
        name="dispatch_slots",
        compiler_params=_params("parallel"),
    )(e_idx, rank, pstart_row)


def _row_copy(src, row, dst, dst_row, sem):
    return pltpu.make_async_copy(src.at[pl.ds(row, 1)], dst.at[pl.ds(dst_row, 1)], sem)


def _dispatch_kernel(pos_ref, cnt_ref, pst_ref, nused_ref, h_ref, x_hbm, stage_ref, zero_ref, sem, zsem):
    i = pl.program_id(0)
    tm = h_ref.shape[0]
    n_blk = x_hbm.shape[0] // EXPERT_BLOCK
    nused = nused_ref[0]
    slot = i % 2

    def wait_tile(s):
        for _ in range(TOP_K):
            pltpu.make_async_copy(stage_ref.at[s], x_hbm.at[pl.ds(0, tm)], sem.at[s]).wait()

    @pl.when(i >= 2)
    def _():
        wait_tile(slot)

    stage_ref[slot] = h_ref[...]
    for r in range(tm):
        for kk in range(TOP_K):
            _row_copy(stage_ref.at[slot], r, x_hbm, pos_ref[(i * tm + r) * TOP_K + kk],
                      sem.at[slot]).start(priority=(r * TOP_K + kk) % 2)

    @pl.when(i == 0)
    def _():
        zero_ref[...] = jnp.zeros_like(zero_ref)

        def pad_expert(start):
            def body(e, _):
                lo = pst_ref[e] + cnt_ref[e]
                length = (EXPERT_BLOCK - cnt_ref[e] % EXPERT_BLOCK) % EXPERT_BLOCK
                hi = lo + length
                size = EXPERT_BLOCK // 2
                while size >= SUBLANES:
                    at = pl.multiple_of(hi - (length & ~(size - 1)), size)

                    @pl.when((length & size) != 0)
                    def _(at=at, size=size):
                        cp = pltpu.make_async_copy(zero_ref.at[pl.ds(0, size)], x_hbm.at[pl.ds(at, size)], zsem)
                        if start:
                            cp.start()
                        else:
                            cp.wait()

                    size //= 2

                def single(s, _):
                    cp = _row_copy(zero_ref, 0, x_hbm, s, zsem)
                    if start:
                        cp.start()
                    else:
                        cp.wait()
                    return 0

                lax.fori_loop(lo, lo + (length & (SUBLANES - 1)), single, 0)
                return 0
            return body

        def tail_copy(blk):
            return pltpu.make_async_copy(zero_ref, x_hbm.at[pl.ds(blk * EXPERT_BLOCK, EXPERT_BLOCK)], zsem)

        def tail_start(blk, _):
            tail_copy(blk).start()
            return 0

        def tail_wait(blk, _):
            tail_copy(blk).wait()
            return 0

        lax.fori_loop(0, cnt_ref.shape[0], pad_expert(True), 0)
        lax.fori_loop(nused, n_blk, tail_start, 0)
        lax.fori_loop(0, cnt_ref.shape[0], pad_expert(False), 0)
        lax.fori_loop(nused, n_blk, tail_wait, 0)

    @pl.when(i == pl.num_programs(0) - 1)
    def _():
        @pl.when(i >= 1)
        def _():
            wait_tile(1 - slot)

        wait_tile(slot)


def _dispatch(pos_flat, counts, pstart, nused, h2, n_slots):
    n, w = h2.shape
    tm = 512
    grid_spec = pltpu.PrefetchScalarGridSpec(
        num_scalar_prefetch=4,
        grid=(n // tm,),
        in_specs=[pl.BlockSpec((tm, w), lambda i, *_: (i, 0))],
        out_specs=pl.BlockSpec(memory_space=pl.ANY),
        scratch_shapes=[pltpu.VMEM((2, tm, w), h2.dtype), pltpu.VMEM((EXPERT_BLOCK, w), h2.dtype),
                        pltpu.SemaphoreType.DMA((2,)), pltpu.SemaphoreType.DMA(())],
    )
    return pl.pallas_call(
        _dispatch_kernel,
        grid_spec=grid_spec,
        out_shape=jax.ShapeDtypeStruct((n_slots, w), h2.dtype),
        name="dispatch_rows",
        compiler_params=_params("arbitrary"),
    )(pos_flat, counts, pstart, nused, h2)


EXPERT_X_SLOTS = 4
EXPERT_Y_SLOTS = 3


def _expert_kernel(bstart_ref, bcount_ref, nused_ref, x_hbm, wg_ref, wu_ref, wd_ref, y_hbm,
                   xs_ref, yo_ref, wgb_ref, wub_ref, wdb_ref, xsem, osem):
    e = pl.program_id(0)
    nused = nused_ref[0]
    rows = EXPERT_BLOCK
    n_blk = y_hbm.shape[0] // rows
    nx, ny = EXPERT_X_SLOTS, EXPERT_Y_SLOTS
    first = bstart_ref[e]
    count = bcount_ref[e]

    def x_copy(blk):
        src = x_hbm.at[pl.ds(jnp.minimum(blk, n_blk - 1) * rows, rows)]
        return pltpu.make_async_copy(src, xs_ref.at[blk % nx], xsem.at[blk % nx])

    def out_copy(blk):
        return pltpu.make_async_copy(yo_ref.at[blk % ny], y_hbm.at[pl.ds(blk * rows, rows), :], osem.at[blk % ny])

    @pl.when(e == 0)
    def _():
        for j in range(nx - 1):
            x_copy(j).start()

    @pl.when(count > 0)
    def _():
        wgb_ref[...] = wg_ref[0].astype(BF16)
        wub_ref[...] = wu_ref[0].astype(BF16)
        wdb_ref[...] = wd_ref[0].astype(BF16)

    def block(j, _):
        blk = first + j

        @pl.when(blk >= ny)
        def _():
            out_copy(blk - ny).wait()

        x_copy(blk).wait()
        x_copy(blk + nx - 1).start()
        xb = xs_ref[blk % nx].astype(BF16)
        hg = jnp.dot(xb, wgb_ref[...], preferred_element_type=F32)
        hu = jnp.dot(xb, wub_ref[...], preferred_element_type=F32)
        hid = hg * _sigmoid(hg) * hu
        yo_ref[blk % ny] = jnp.dot(hid.astype(BF16), wdb_ref[...], preferred_element_type=F32)
        out_copy(blk).start()

        @pl.when(blk == nused - 1)
        def _():
            for ahead in range(1, nx):
                x_copy(blk + ahead).wait()

        return 0

    lax.fori_loop(0, count, block, 0)

    @pl.when(e == pl.num_programs(0) - 1)
    def _():
        for back in range(1, ny + 1):
            @pl.when(nused - back >= 0)
            def _():
                out_copy(nused - back).wait()

        yo_ref[0] = jnp.zeros(yo_ref.shape[1:], yo_ref.dtype)

        def zero_copy(blk):
            return pltpu.make_async_copy(yo_ref.at[0], y_hbm.at[pl.ds(blk * rows, rows), :], osem.at[0])

        def fill(blk, _):
            zero_copy(blk).start()
            return 0

        def drain(blk, _):
            zero_copy(blk).wait()
            return 0

        lax.fori_loop(nused, n_blk, fill, 0)
        lax.fori_loop(nused, n_blk, drain, 0)


def _experts(x_buf, bstart, bcount, nused, w_gate_e, w_up_e, w_down_e):
    n_slots = x_buf.shape[0]
    n_exp, d, f = w_gate_e.shape
    grid_spec = pltpu.PrefetchScalarGridSpec(
        num_scalar_prefetch=3,
        grid=(n_exp,),
        in_specs=[pl.BlockSpec(memory_space=pl.ANY),
                  pl.BlockSpec((1, d, f), lambda e, *_: (e, 0, 0)),
                  pl.BlockSpec((1, d, f), lambda e, *_: (e, 0, 0)),
                  pl.BlockSpec((1, f, d), lambda e, *_: (e, 0, 0))],
        out_specs=pl.BlockSpec(memory_space=pl.ANY),
        scratch_shapes=[pltpu.VMEM((EXPERT_X_SLOTS, EXPERT_BLOCK, d), F32),
                        pltpu.VMEM((EXPERT_Y_SLOTS, EXPERT_BLOCK, d), F32),
                        pltpu.VMEM((d, f), BF16), pltpu.VMEM((d, f), BF16), pltpu.VMEM((f, d), BF16),
                        pltpu.SemaphoreType.DMA((EXPERT_X_SLOTS,)), pltpu.SemaphoreType.DMA((EXPERT_Y_SLOTS,))],
    )
    return pl.pallas_call(
        _expert_kernel,
        grid_spec=grid_spec,
        out_shape=jax.ShapeDtypeStruct((n_slots, d), F32),
        name="experts",
        compiler_params=_params("arbitrary"),
    )(bstart, bcount, nused, x_buf, w_gate_e, w_up_e, w_down_e)


def _combine_kernel(pos_ref, y_hbm, wt_ref, x1_ref, gt_ref, gpost_ref, o_ref, rows_ref, sem):
    i = pl.program_id(0)
    nsteps = pl.num_programs(0)
    tm = x1_ref.shape[0]

    def start_rows(step, slot):
        for r in range(tm):
            for kk in range(TOP_K):
                _row_copy(y_hbm, pos_ref[(step * tm + r) * TOP_K + kk], rows_ref.at[slot, kk], r,
                          sem.at[slot]).start(priority=(r * TOP_K + kk) % 2)

    def wait_rows(slot):
        for kk in range(TOP_K):
            pltpu.make_async_copy(y_hbm.at[pl.ds(0, tm), :], rows_ref.at[slot, kk], sem.at[slot]).wait()

    @pl.when(i == 0)
    def _():
        start_rows(0, 0)

    slot = i % 2
    wait_rows(slot)
    start_rows(jnp.minimum(i + 1, nsteps - 1), 1 - slot)
    wt = wt_ref[...]
    y = rows_ref[slot, 0] * wt[:, 0:1] + rows_ref[slot, 1] * wt[:, 1:2]
    o_ref[...] = x1_ref[...] + gt_ref[...] * _rms(y, gpost_ref[...])

    @pl.when(i == nsteps - 1)
    def _():
        wait_rows(1 - slot)


def _combine(pos, y_buf, wts, x1, gt2, g_post):
    n, d = x1.shape
    tm = 512
    vec = pl.BlockSpec((1, d), lambda i, p: (0, 0))
    grid_spec = pltpu.PrefetchScalarGridSpec(
        num_scalar_prefetch=1,
        grid=(n // tm,),
        in_specs=[pl.BlockSpec(memory_space=pl.ANY),
                  pl.BlockSpec((tm, LANES), lambda i, p: (i, 0)),
                  pl.BlockSpec((tm, d), lambda i, p: (i, 0)), vec, vec],
        out_specs=pl.BlockSpec((tm, d), lambda i, p: (i, 0)),
        scratch_shapes=[pltpu.VMEM((2, TOP_K, tm, d), F32), pltpu.SemaphoreType.DMA((2,))],
    )
    return pl.pallas_call(
        _combine_kernel,
        grid_spec=grid_spec,
        out_shape=jax.ShapeDtypeStruct((n, d), F32),
        name="combine",
        compiler_params=_params("arbitrary"),
    )(pos, y_buf, wts, x1, gt2, g_post)


def _segment_tables(counts_row):
    counts = counts_row[0, :N_EXPERTS]
    pcounts = (counts + EXPERT_BLOCK - 1) // EXPERT_BLOCK * EXPERT_BLOCK
    pend = jnp.cumsum(pcounts)
    pstart = pend - pcounts
    nused = (pend[-1] // EXPERT_BLOCK).astype(I32)
    pstart_row = jnp.pad(pstart, (0, LANES - N_EXPERTS)).reshape(1, LANES)
    return counts, pstart, pstart_row, pstart // EXPERT_BLOCK, pcounts // EXPERT_BLOCK, nused.reshape(1)


def _rope_tables(s):
    half = ROPE_DIM // 2
    inv = ROPE_THETA ** (-np.arange(half, dtype=np.float64) / half)
    ang = np.arange(s, dtype=np.float64)[:, None] * inv[None, :]
    cos, sin = jnp.asarray(np.cos(ang), F32), jnp.asarray(np.sin(ang), F32)
    pad = jnp.zeros((s, LANES - ROPE_DIM), F32)
    zero = jnp.zeros((s, half), F32)
    cos_t = jnp.concatenate([cos, cos, pad + 1.0], axis=1)
    sin1_t = jnp.concatenate([-sin, zero, pad], axis=1)
    sin2_t = jnp.concatenate([zero, sin, pad], axis=1)
    return cos_t, sin1_t, sin2_t


def _layer(x, c_col, w_ada, b_ada, g_pre_mix, g_post_mix, g_pre_ffn, g_post_ffn, w_in, mu_r, mu_k, mu_v,
           mu_w, mu_a, mu_g, w0, w_w1, w_w2, a0, w_a1, w_a2, w_g1, w_g2, k_k, k_a, r_k, gn_w, gn_b,
           w_up_att, w_up_rwkv, w_o, w_rg, b_rg, w_re, b_re, w_gate_e, w_up_e, w_down_e):
    s, d = x.shape
    att_w = ATT_HEADS * ATT_HEAD_DIM
    rwkv_w = w_up_rwkv.shape[0]
    row = lambda a: a.reshape(1, -1)

    ada = _ada(c_col, w_ada, row(b_ada))
    sh1, sc1, gt1, sh2, sc2, gt2 = (ada[:, i * d:(i + 1) * d] for i in range(6))

    h, lw, la, lg = _prenorm(x, row(g_pre_mix), sc1, sh1, row(mu_w), row(mu_a), row(mu_g), w_w1, w_a1, w_g1)
    proj = _matmul(h, w_in, BF16)

    q_aug_t, k_aug, v_t = _rope_gate(proj, *_rope_tables(s))
    o_att = _attention(q_aug_t, k_aug, v_t)

    o_rwkv = _rwkv(proj, lw, la, lg, w_w2, w_a2, w_g2, row(mu_r), row(mu_k), row(mu_v), row(w0), row(a0),
                   row(k_k), row(k_a), row(r_k), row(gn_w), row(gn_b), col0=3 * att_w, width=rwkv_w)


    w_router = jnp.pad(jnp.concatenate([w_re, w_rg], axis=1), ((0, 0), (0, LANES - N_EXPERTS - N_GROUPS)))
    b_router = jnp.pad(jnp.concatenate([b_re, b_rg]), (0, LANES - N_EXPERTS - N_GROUPS)).reshape(1, LANES)
    x1, h2, e_idx, wts, rank, counts = _route(x, o_att, o_rwkv, proj, 3 * att_w + 3 * rwkv_w, w_up_att, w_up_rwkv,
                                              w_o, gt1, row(g_post_mix), row(g_pre_ffn), sc2, sh2,
                                              w_router, b_router)

    n_pairs = s * TOP_K
    n_blk = (n_pairs + N_EXPERTS * (EXPERT_BLOCK - 1) + EXPERT_BLOCK - 1) // EXPERT_BLOCK
    counts, pstart, pstart_row, bstart, bcount, nused = _segment_tables(counts)
    pos = _slots(e_idx, rank, pstart_row)[:, :TOP_K].reshape(n_pairs)
    x_buf = _dispatch(pos, counts, pstart, nused, h2, n_blk * EXPERT_BLOCK)
    y_buf = _experts(x_buf, bstart, bcount, nused, w_gate_e, w_up_e, w_down_e)
    return _combine(pos, y_buf, wts, x1, gt2, row(g_post_ffn))


def kernel(x, c, w_ada, b_ada, g_pre_mix, g_post_mix, g_pre_ffn, g_post_ffn, w_in, mu_r, mu_k, mu_v, mu_w, mu_a, mu_g, w0, w_w1, w_w2, a0, w_a1, w_a2, w_g1, w_g2, k_k, k_a, r_k, gn_w, gn_b, w_up_att, w_up_rwkv, w_o, w_rg, b_rg, w_re, b_re, w_gate_e, w_up_e, w_down_e):
    b, s, d = x.shape
    assert b == 1, "one sequence per call"
    params = (w_ada, b_ada, g_pre_mix, g_post_mix, g_pre_ffn, g_post_ffn, w_in, mu_r, mu_k, mu_v, mu_w, mu_a,
              mu_g, w0, w_w1, w_w2, a0, w_a1, w_a2, w_g1, w_g2, k_k, k_a, r_k, gn_w, gn_b, w_up_att,
              w_up_rwkv, w_o, w_rg, b_rg, w_re, b_re, w_gate_e, w_up_e, w_down_e)
    xs = x.reshape(s, d)
    c_col = c.reshape(d, 1)
    for l in range(w_ada.shape[0]):
        xs = _layer(xs, c_col, *(p[l] for p in params))
    return xs.reshape(b, s, d)
```

```python
import math

import jax
import jax.numpy as jnp
import numpy as np
from jax import lax
from jax.experimental import pallas as pl
from jax.experimental.pallas import tpu as pltpu

F32 = jnp.float32
BF16 = jnp.bfloat16
I32 = jnp.int32
HI = lax.Precision.HIGHEST

LANES = 128
SUBLANES = 8
VMEM_LIMIT = 56 * 1024 * 1024

ATT_HEADS = 8
ATT_HEAD_DIM = 128
MOBA_BLOCK = 256
MOBA_TOPK = 3
ATT_GROUP = 4
ATT_HEADS_PER_STEP = 2
ROPE_THETA = 500000.0
ROPE_DIM = ATT_HEAD_DIM // 4
RWKV_HEAD_DIM = 64
GN_EPS = 64e-5
N_GROUPS = 8
EXPERTS_PER_GROUP = 8
N_EXPERTS = N_GROUPS * EXPERTS_PER_GROUP
TOP_K = 2
EXPERT_BLOCK = 128
RMS_EPS = 1e-6
NEG = -1e30
SCAN_CHUNK = 64
RWKV_CHUNK_GROUP = 8
RWKV_ROWS_PER_STEP = 2048
Q_SCALE = ATT_HEAD_DIM ** -0.5 * math.log2(math.e)


def _params(*sem):
    return pltpu.CompilerParams(dimension_semantics=sem, vmem_limit_bytes=VMEM_LIMIT)


def _rms(z, g):
    return z * lax.rsqrt(jnp.mean(z * z, axis=-1, keepdims=True) + RMS_EPS) * g


def _sigmoid(z):
    return 1.0 / (1.0 + jnp.exp(-z))


def _dot(a, b):
    return jnp.dot(a.astype(BF16), b.astype(BF16), preferred_element_type=F32)


def _dot_nt(a, b):
    return lax.dot_general(a.astype(BF16), b.astype(BF16), (((1,), (1,)), ((), ())),
                           preferred_element_type=F32)


def _dot_tn(a, b):
    return lax.dot_general(a.astype(BF16), b.astype(BF16), (((0,), (0,)), ((), ())),
                           preferred_element_type=F32)


def _dot_hi(a, b):
    return jnp.dot(a, b, precision=HI, preferred_element_type=F32)


def _dot_x3(a, b):
    ah, al, _ = _split3(a)
    bh, bl, _ = _split3(b)
    return (jnp.dot(ah, bh, preferred_element_type=F32) + jnp.dot(ah, bl, preferred_element_type=F32)
            + jnp.dot(al, bh, preferred_element_type=F32))


def _split3(a):
    hi = a.astype(BF16)
    r1 = a - hi.astype(F32)
    mid = r1.astype(BF16)
    lo = (r1 - mid.astype(F32)).astype(BF16)
    return hi, mid, lo


def _dot_split(a, b01):
    b = b01.astype(BF16)
    hi, mid, _ = _split3(a)
    return jnp.dot(jnp.concatenate([hi, mid], axis=1), jnp.concatenate([b, b], axis=0),
                   preferred_element_type=F32)


def _dot_split_t(b01, a):
    b = b01.astype(BF16)
    return jnp.dot(jnp.concatenate([b, b, b], axis=1), jnp.concatenate(_split3(a), axis=0),
                   preferred_element_type=F32)


def _shift_rows(z, prev_row):
    rolled = pltpu.roll(z, 1, 0)
    row = lax.broadcasted_iota(I32, z.shape, 0)
    return jnp.where(row == 0, prev_row, rolled)


def _ada_kernel(c_ref, w_ref, b_ref, o_ref):
    o_ref[...] = jnp.sum(c_ref[...] * w_ref[...], axis=0, keepdims=True) + b_ref[...]


def _ada(c_col, w_ada, b_ada):
    d, n = w_ada.shape
    tn = 1024
    return pl.pallas_call(
        _ada_kernel,
        grid=(n // tn,),
        in_specs=[pl.BlockSpec((d, 1), lambda j: (0, 0)),
                  pl.BlockSpec((d, tn), lambda j: (0, j)),
                  pl.BlockSpec((1, tn), lambda j: (0, j))],
        out_specs=pl.BlockSpec((1, tn), lambda j: (0, j)),
        out_shape=jax.ShapeDtypeStruct((1, n), F32),
        name="ada",
        compiler_params=_params("parallel"),
    )(c_col, w_ada, b_ada)


def _prenorm_kernel(x_ref, xp_ref, g_ref, sc_ref, sh_ref, muw_ref, mua_ref, mug_ref,
                    ww1_ref, wa1_ref, wg1_ref, h_ref, lw_ref, la_ref, lg_ref):
    i = pl.program_id(0)
    g, sc, sh = g_ref[...], sc_ref[...], sh_ref[...]
    h = _rms(x_ref[...], g) * (1.0 + sc) + sh
    hp = _rms(xp_ref[SUBLANES - 1:SUBLANES, :], g) * (1.0 + sc) + sh
    hp = jnp.where(i == 0, 0.0, hp)
    dh = _shift_rows(h, hp) - h
    h_ref[...] = h.astype(BF16)
    lw_ref[...] = jnp.tanh(_dot(h + dh * muw_ref[...], ww1_ref[...]))
    la_ref[...] = _dot(h + dh * mua_ref[...], wa1_ref[...])
    lg_ref[...] = _sigmoid(_dot(h + dh * mug_ref[...], wg1_ref[...]))


def _prenorm(x, g, sc, sh, mu_w, mu_a, mu_g, w_w1, w_a1, w_g1):
    s, d = x.shape
    tm = 256
    rpb = tm // SUBLANES
    vec = pl.BlockSpec((1, d), lambda i: (0, 0))
    full = lambda a: pl.BlockSpec(a.shape, lambda i: (0, 0))
    lw, la, lg = w_w1.shape[1], w_a1.shape[1], w_g1.shape[1]
    return pl.pallas_call(
        _prenorm_kernel,
        grid=(s // tm,),
        in_specs=[pl.BlockSpec((tm, d), lambda i: (i, 0)),
                  pl.BlockSpec((SUBLANES, d), lambda i: (jnp.maximum(i * rpb - 1, 0), 0)),
                  vec, vec, vec, vec, vec, vec, full(w_w1), full(w_a1), full(w_g1)],
        out_specs=[pl.BlockSpec((tm, d), lambda i: (i, 0)),
                   pl.BlockSpec((tm, lw), lambda i: (i, 0)),
                   pl.BlockSpec((tm, la), lambda i: (i, 0)),
                   pl.BlockSpec((tm, lg), lambda i: (i, 0))],
        out_shape=[jax.ShapeDtypeStruct((s, d), BF16),
                   jax.ShapeDtypeStruct((s, lw), F32),
                   jax.ShapeDtypeStruct((s, la), F32),
                   jax.ShapeDtypeStruct((s, lg), F32)],
        name="prenorm_lora",
        compiler_params=_params("parallel"),
    )(x, x, g, sc, sh, mu_w, mu_a, mu_g, w_w1, w_a1, w_g1)


def _mm_kernel(a_ref, w_ref, o_ref, wb_ref):
    @pl.when(pl.program_id(1) == 0)
    def _():
        wb_ref[...] = w_ref[...].astype(BF16)

    o_ref[...] = jnp.dot(a_ref[...], wb_ref[...], preferred_element_type=F32).astype(o_ref.dtype)


def _matmul(a, w, out_dtype, tm=512, tn=1024):
    m, k = a.shape
    n = w.shape[1]
    tn = min(tn, n)
    return pl.pallas_call(
        _mm_kernel,
        grid=(n // tn, m // tm),
        in_specs=[pl.BlockSpec((tm, k), lambda j, i: (i, 0)),
                  pl.BlockSpec((k, tn), lambda j, i: (0, j))],
        out_specs=pl.BlockSpec((tm, tn), lambda j, i: (i, j)),
        out_shape=jax.ShapeDtypeStruct((m, n), out_dtype),
        scratch_shapes=[pltpu.VMEM((k, tn), BF16)],
        name="matmul",
        compiler_params=_params("arbitrary", "arbitrary"),
    )(a, w)


def _rope_gate_kernel(p_ref, c_ref, s1_ref, s2_ref, qa_ref, ka_ref, vt_ref, km_ref):
    i = pl.program_id(0)
    bs = MOBA_BLOCK
    nbp = km_ref.shape[1]

    @pl.when(i == 0)
    def _():
        km_ref[...] = jnp.zeros_like(km_ref)

    c, s1, s2 = c_ref[...], s1_ref[...], s2_ref[...]

    def rope(z):
        return z * c + pltpu.roll(z, LANES - ROPE_DIM // 2, 1) * s1 + pltpu.roll(z, ROPE_DIM // 2, 1) * s2

    row = lax.broadcasted_iota(I32, (nbp, bs), 0)
    lane = lax.broadcasted_iota(I32, (bs, LANES), 1)
    onehot = jnp.where(lane == i, 1.0, 0.0).astype(BF16)
    for h in range(ATT_HEADS):
        q = rope(p_ref[:, h * LANES:(h + 1) * LANES].astype(F32))
        k = rope(p_ref[:, (ATT_HEADS + h) * LANES:(ATT_HEADS + h + 1) * LANES].astype(F32))
        g = lax.dot_general(km_ref[h], q, (((1,), (1,)), ((), ())), precision=HI, preferred_element_type=F32)
        g = jnp.where(row < i, g, NEG)
        sel_t = jnp.zeros(g.shape, F32)
        for _ in range(MOBA_TOPK):
            mx = jnp.max(g, axis=0, keepdims=True)
            idx = jnp.min(jnp.where(g == mx, row, nbp), axis=0, keepdims=True)
            hit = row == idx
            sel_t = jnp.where(hit & (row < i), 1.0, sel_t)
            g = jnp.where(hit, -jnp.inf, g)
        if nbp < LANES:
            sel_t = jnp.concatenate([sel_t, jnp.zeros((LANES - nbp, bs), F32)], axis=0)
        w = 2 * LANES
        qa_ref[h, :LANES, :] = (q * Q_SCALE).T.astype(BF16)
        qa_ref[h, LANES:, :] = jnp.where(sel_t > 0.5, 0.0, NEG).astype(BF16)
        ka_ref[:, h * w:h * w + LANES] = k.astype(BF16)
        ka_ref[:, h * w + LANES:(h + 1) * w] = onehot
        v = p_ref[:, (2 * ATT_HEADS + h) * LANES:(2 * ATT_HEADS + h + 1) * LANES].astype(F32)
        vt_ref[h, 0] = v.T.astype(BF16)
        km_ref[h, pl.ds(i, 1), :] = jnp.mean(k, axis=0, keepdims=True)


def _rope_gate(proj, cos_t, sin1_t, sin2_t):
    s = proj.shape[0]
    nb = s // MOBA_BLOCK
    assert nb <= LANES
    nbp = -(-nb // SUBLANES) * SUBLANES
    w_in = 3 * ATT_HEADS * ATT_HEAD_DIM
    w_out = 2 * ATT_HEADS * LANES
    tab = pl.BlockSpec((MOBA_BLOCK, LANES), lambda i: (i, 0))
    return pl.pallas_call(
        _rope_gate_kernel,
        grid=(nb,),
        in_specs=[pl.BlockSpec((MOBA_BLOCK, w_in), lambda i: (i, 0)), tab, tab, tab],
        out_specs=[pl.BlockSpec((ATT_HEADS, 2 * LANES, MOBA_BLOCK), lambda i: (0, 0, i)),
                   pl.BlockSpec((MOBA_BLOCK, w_out), lambda i: (i, 0)),
                   pl.BlockSpec((ATT_HEADS, 1, LANES, MOBA_BLOCK), lambda i: (0, i, 0, 0))],
        out_shape=[jax.ShapeDtypeStruct((ATT_HEADS, 2 * LANES, s), BF16),
                   jax.ShapeDtypeStruct((s, w_out), BF16),
                   jax.ShapeDtypeStruct((ATT_HEADS, nb, LANES, MOBA_BLOCK), BF16)],
        scratch_shapes=[pltpu.VMEM((ATT_HEADS, nbp, LANES), F32)],
        name="rope_gate",
        compiler_params=_params("arbitrary"),
    )(proj, cos_t, sin1_t, sin2_t)


def _attn_kernel(qa_ref, ka_ref, vt_ref, o_ref, s_ref):
    qi = pl.program_id(1)
    bs = MOBA_BLOCK
    grp = ATT_GROUP * bs
    w = 2 * LANES
    heads = range(ATT_HEADS_PER_STEP)

    n_groups = ka_ref.shape[0] // grp

    def issue_scores(g, buf):
        base = pl.multiple_of(jnp.minimum(g, n_groups - 1) * grp, grp)
        for h in heads:
            s_ref[buf, h] = jnp.dot(ka_ref[pl.ds(base, grp), h * w:(h + 1) * w], qa_ref[h],
                                    preferred_element_type=F32)

    issue_scores(0, 0)

    own = pl.multiple_of(qi * bs, bs)
    k_i = lax.broadcasted_iota(I32, (bs, bs), 0)
    q_i = lax.broadcasted_iota(I32, (bs, bs), 1)
    carry = []
    for h in heads:
        s = jnp.dot(ka_ref[pl.ds(own, bs), h * w:h * w + LANES], qa_ref[h, :LANES, :],
                    preferred_element_type=F32)
        s = jnp.where(k_i <= q_i, s, NEG)
        m = jnp.max(s, axis=0, keepdims=True)
        p = jnp.exp2(s - m)
        l = jnp.sum(p, axis=0, keepdims=True)
        acc = jnp.dot(vt_ref[h, qi], p.astype(BF16), preferred_element_type=F32)
        carry += [m, l, acc]

    def absorb(g, buf, carry):
        out = []
        for h in heads:
            m, l, acc = carry[3 * h:3 * h + 3]
            s = s_ref[buf, h]
            mn = jnp.maximum(m, jnp.max(s, axis=0, keepdims=True))
            alpha = jnp.exp2(m - mn)
            p = jnp.exp2(s - mn)
            l = alpha * l + jnp.sum(p, axis=0, keepdims=True)
            p = p.astype(BF16)
            acc = alpha * acc
            for jb in range(ATT_GROUP):
                acc = acc + jnp.dot(vt_ref[h, g * ATT_GROUP + jb], p[jb * bs:(jb + 1) * bs],
                                    preferred_element_type=F32)
            out += [mn, l, acc]
        return out

    def body(t, carry):
        issue_scores(2 * t + 1, 1)
        carry = absorb(2 * t, 0, carry)
        issue_scores(2 * t + 2, 0)
        return tuple(absorb(2 * t + 1, 1, carry))

    n_used = (qi + ATT_GROUP - 1) // ATT_GROUP
    carry = lax.fori_loop(0, n_used // 2, body, tuple(carry))
    carry = lax.cond(n_used % 2 == 1, lambda c: tuple(absorb(n_used - 1, 0, c)), lambda c: c, carry)
    for h in heads:
        m, l, acc = carry[3 * h:3 * h + 3]
        o_ref[:, h * LANES:(h + 1) * LANES] = (acc / l).T.astype(BF16)


def _attention(q_aug_t, k_aug, v_t):
    s = k_aug.shape[0]
    assert s % (ATT_GROUP * MOBA_BLOCK) == 0
    nb = s // MOBA_BLOCK
    hps = ATT_HEADS_PER_STEP
    once = pl.Buffered(1)
    return pl.pallas_call(
        _attn_kernel,
        grid=(ATT_HEADS // hps, nb),
        in_specs=[pl.BlockSpec((hps, 2 * LANES, MOBA_BLOCK), lambda h, i: (h, 0, i)),
                  pl.BlockSpec((s, hps * 2 * LANES), lambda h, i: (0, h), pipeline_mode=once),
                  pl.BlockSpec((hps, nb, LANES, MOBA_BLOCK), lambda h, i: (h, 0, 0, 0), pipeline_mode=once)],
        out_specs=pl.BlockSpec((MOBA_BLOCK, hps * LANES), lambda h, i: (i, h)),
        out_shape=jax.ShapeDtypeStruct((s, ATT_HEADS * ATT_HEAD_DIM), BF16),
        scratch_shapes=[pltpu.VMEM((2, hps, ATT_GROUP * MOBA_BLOCK, MOBA_BLOCK), F32)],
        name="moba_attention",
        compiler_params=_params("parallel", "arbitrary"),
    )(q_aug_t, k_aug, v_t)


def _rwkv_kernel(r_ref, k_ref, v_ref, lw_ref, la_ref, lg_ref, ww2_ref, wa2_ref, wg2_ref,
                 mur_ref, muk_ref, muv_ref, w0_ref, a0_ref, kk_ref, ka_ref, rk_ref, gnw_ref, gnb_ref,
                 o_ref, st_ref, prev_ref):
    j = pl.program_id(1)
    t = SCAN_CHUNK
    n = RWKV_HEAD_DIM
    ts = r_ref.shape[0]
    chunks = range(ts // t)

    @pl.when(j == 0)
    def _():
        st_ref[...] = jnp.zeros_like(st_ref)
        prev_ref[...] = jnp.zeros_like(prev_ref)

    rp, kp, vp = r_ref[...].astype(F32), k_ref[...].astype(F32), v_ref[...].astype(F32)
    r = rp + (_shift_rows(rp, prev_ref[0:1, :]) - rp) * mur_ref[...]
    k = kp + (_shift_rows(kp, prev_ref[1:2, :]) - kp) * muk_ref[...]
    v = vp + (_shift_rows(vp, prev_ref[2:3, :]) - vp) * muv_ref[...]
    prev_ref[0:1, :] = rp[ts - 1:ts, :]
    prev_ref[1:2, :] = kp[ts - 1:ts, :]
    prev_ref[2:3, :] = vp[ts - 1:ts, :]

    d = w0_ref[...] + _dot(lw_ref[...], ww2_ref[...])
    logw = -math.exp(-0.5) * _sigmoid(d)
    a = _sigmoid(a0_ref[...] + _dot(la_ref[...], wa2_ref[...]))
    g = _dot(lg_ref[...], wg2_ref[...])

    li = lax.broadcasted_iota(I32, (LANES, LANES), 0)
    lj = lax.broadcasted_iota(I32, (LANES, LANES), 1)
    same_head = (li // n) == (lj // n)
    head_sum = jnp.where(same_head, 1.0, 0.0)

    kk = k * kk_ref[...]
    kk = kk / jnp.maximum(jnp.sqrt(_dot_split(kk * kk, head_sum)), 1e-12)
    kt = k * (1.0 + (a - 1.0) * ka_ref[...])
    bonus = _dot_split(r * kt * rk_ref[...], head_sum) * v

    same_blk = (li // t) == (lj // t)
    m_strict = same_blk & (lj < li)
    m_incl = same_blk & (lj <= li)
    ti = lax.broadcasted_iota(I32, (t, t), 0)
    tj = lax.broadcasted_iota(I32, (t, t), 1)
    tril_incl = jnp.where(tj <= ti, 1.0, 0.0)
    lane_a = lax.broadcasted_iota(I32, (t, LANES), 1) < n
    eye = jnp.where(li == lj, 1.0, 0.0)

    def stack_masked(z):
        return jnp.concatenate([jnp.where(lane_a, z, 0.0), jnp.where(lane_a, 0.0, z)], axis=0)

    def stack_plain(z):
        return jnp.concatenate([z, z], axis=0)

    def rows(z, c):
        return z[c * t:(c + 1) * t]

    cum = jnp.concatenate([_dot_split_t(tril_incl, rows(logw, c)) for c in chunks], axis=0)
    g_t = jnp.exp(cum)
    g_inv = jnp.exp(-cum)
    xa_f = -kk * jnp.exp(cum - logw)
    xr_f = r * g_t
    yb_f = kk * a * g_inv
    yk_f = kt * g_inv

    zeros = jnp.zeros((2 * t, LANES), F32)
    state = [st_ref[...]]
    ys = {}

    def chain_step(c, r_hat, y_hat, pm, qm):
        def run():
            st = state[0]
            y2 = _dot_nt(r_hat, st) + y_hat
            ys[c] = y2[:t] + y2[t:]
            state[0] = (st + _dot(st, pm) + qm) * g_t[(c + 1) * t - 1:(c + 1) * t, :]
        return run

    def chunk_local(grp, between):
        idx = range(len(grp))
        xa = [stack_masked(rows(xa_f, c)) for c in grp]
        xr = [stack_masked(rows(xr_f, c)) for c in grp]
        yb = [stack_plain(rows(yb_f, c)) for c in grp]
        vs = [stack_masked(rows(v, c)) for c in grp]
        ybk = [jnp.concatenate([yb[i], stack_plain(rows(yk_f, c))], axis=0) for i, c in enumerate(grp)]
        sc = [_dot_nt(jnp.concatenate([xa[i], xr[i]], axis=0), ybk[i]) for i in idx]
        between()
        a_ab = [jnp.where(m_strict, sc[i][:2 * t, :2 * t], 0.0) for i in idx]
        a_ak = [jnp.where(m_strict, sc[i][:2 * t, 2 * t:], 0.0) for i in idx]
        a_rb = [jnp.where(m_incl, sc[i][2 * t:, :2 * t], 0.0) for i in idx]
        a_rk = [jnp.where(m_incl, sc[i][2 * t:, 2 * t:], 0.0) for i in idx]
        inv = [eye + a_ab[i] for i in idx]
        pw = [_dot(a_ab[i], a_ab[i]) for i in idx]
        for _ in range(int(math.log2(t)) - 2):
            between()
            both = [_dot(pw[i], jnp.concatenate([pw[i], inv[i]], axis=1)) for i in idx]
            pw = [both[i][:, :LANES] for i in idx]
            inv = [inv[i] + both[i][:, LANES:] for i in idx]
        between()
        inv = [inv[i] + _dot(pw[i], inv[i]) for i in idx]
        av = [_dot(a_ak[i], vs[i]) for i in idx]
        between()
        mw = [_dot(inv[i], jnp.concatenate([xa[i], av[i]], axis=1)) for i in idx]
        between()
        rw = [_dot(jnp.concatenate([a_rb[i], a_rk[i]], axis=1),
                   jnp.concatenate([mw[i], jnp.concatenate([zeros, vs[i]], axis=1)], axis=0)) for i in idx]
        between()
        pm = [jnp.where(same_head, _dot_tn(mw[i][:, :LANES], yb[i]), 0.0) for i in idx]
        qm = [jnp.where(same_head, _dot_tn(jnp.concatenate([mw[i][:, LANES:], vs[i]], axis=0), ybk[i]), 0.0)
              for i in idx]
        return [chain_step(c, xr[i] + rw[i][:, :LANES], rw[i][:, LANES:], pm[i], qm[i])
                for i, c in enumerate(grp)]

    pending = []

    def between():
        if pending:
            pending.pop(0)()

    group = RWKV_CHUNK_GROUP
    for g0 in range(0, len(chunks), group):
        steps = chunk_local(list(chunks[g0:g0 + group]), between)
        while pending:
            pending.pop(0)()
        pending.extend(steps)
    while pending:
        pending.pop(0)()
    st_ref[...] = state[0]
    y = jnp.concatenate([ys[c] for c in chunks], axis=0)

    mean = _dot_split(y, head_sum) * (1.0 / n)
    yc = y - mean
    var = _dot_split(yc * yc, head_sum) * (1.0 / n)
    yn = yc * lax.rsqrt(var + GN_EPS) * gnw_ref[...] + gnb_ref[...]
    o_ref[...] = ((yn + bonus) * g).astype(BF16)


def _rwkv(proj, lw, la, lg, w_w2, w_a2, w_g2, mu_r, mu_k, mu_v, w0, a0, k_k, k_a, r_k, gn_w, gn_b,
          col0, width):
    s = proj.shape[0]
    ts = min(RWKV_ROWS_PER_STEP, s)
    npair = width // LANES
    cb = col0 // LANES
    row = lambda a: pl.BlockSpec((ts, a.shape[1]), lambda p, j: (j, 0))
    wcol = lambda a: pl.BlockSpec((a.shape[0], LANES), lambda p, j: (0, p))
    vec = pl.BlockSpec((1, LANES), lambda p, j: (0, p))
    return pl.pallas_call(
        _rwkv_kernel,
        grid=(npair, s // ts),
        in_specs=[pl.BlockSpec((ts, LANES), lambda p, j: (j, cb + p)),
                  pl.BlockSpec((ts, LANES), lambda p, j: (j, cb + npair + p)),
                  pl.BlockSpec((ts, LANES), lambda p, j: (j, cb + 2 * npair + p)),
                  row(lw), row(la), row(lg), wcol(w_w2), wcol(w_a2), wcol(w_g2)] + [vec] * 10,
        out_specs=pl.BlockSpec((ts, LANES), lambda p, j: (j, p)),
        out_shape=jax.ShapeDtypeStruct((s, width), BF16),
        scratch_shapes=[pltpu.VMEM((LANES, LANES), F32), pltpu.VMEM((SUBLANES, LANES), F32)],
        name="rwkv7_scan",
        compiler_params=_params("parallel", "arbitrary"),
    )(proj, proj, proj, lw, la, lg, w_w2, w_a2, w_g2, mu_r, mu_k, mu_v, w0, a0, k_k, k_a, r_k, gn_w, gn_b)


def _route_kernel(x_ref, oa_ref, or_ref, ga_ref, gr_ref, wua_ref, wur_ref, wo_ref,
                  gt_ref, gpost_ref, gpre_ref, sc_ref, sh_ref, wr_ref, br_ref,
                  x1_ref, h2_ref, ei_ref, wt_ref, rk_ref, cnt_ref, run_ref, y_ref):
    i = pl.program_id(0)

    @pl.when(i == 0)
    def _():
        run_ref[...] = jnp.zeros_like(run_ref)
        y_ref[...] = jnp.zeros_like(y_ref)

    ua = jnp.dot(oa_ref[...], wua_ref[...], preferred_element_type=F32)
    ur = jnp.dot(or_ref[...], wur_ref[...], preferred_element_type=F32)
    logits = _route_logits(y_ref[...], x_ref, gt_ref, gpost_ref, gpre_ref, sc_ref, sh_ref, wr_ref, br_ref,
                           x1_ref, h2_ref)
    mix = _sigmoid(ga_ref[...].astype(F32)) * ua + _sigmoid(gr_ref[...].astype(F32)) * ur
    y_new = jnp.dot(mix.astype(BF16), wo_ref[...], preferred_element_type=F32)
    _route_choose(logits, jnp.where(i > 0, 1.0, 0.0), ei_ref, wt_ref, rk_ref, run_ref)
    cnt_ref[...] = run_ref[...].astype(I32)
    y_ref[...] = y_new


def _route_logits(y, x_ref, gt_ref, gpost_ref, gpre_ref, sc_ref, sh_ref, wr_ref, br_ref, x1_ref, h2_ref):
    x1 = x_ref[...] + gt_ref[...] * _rms(y, gpost_ref[...])
    x1_ref[...] = x1
    h2 = _rms(x1, gpre_ref[...]) * (1.0 + sc_ref[...]) + sh_ref[...]
    h2_ref[...] = h2
    return _dot_x3(h2, wr_ref[...]) + br_ref[...]


def _route_choose(logits, valid, ei_ref, wt_ref, rk_ref, run_ref):
    rs = slice(None)
    lane = lax.broadcasted_iota(I32, logits.shape, 1)
    big = jnp.int32(4 * LANES)
    gmask = (lane >= N_EXPERTS) & (lane < N_EXPERTS + N_GROUPS)
    mg = jnp.max(jnp.where(gmask, logits, -jnp.inf), axis=1, keepdims=True)
    eg = jnp.where(gmask, jnp.exp(logits - mg), 0.0)
    pg = eg / jnp.sum(eg, axis=1, keepdims=True)
    pg_top = jnp.max(pg, axis=1, keepdims=True)
    g_idx = jnp.min(jnp.where(gmask & (pg == pg_top), lane, big), axis=1, keepdims=True) - N_EXPERTS
    emask = (lane >= g_idx * EXPERTS_PER_GROUP) & (lane < (g_idx + 1) * EXPERTS_PER_GROUP)
    me = jnp.max(jnp.where(emask, logits, -jnp.inf), axis=1, keepdims=True)
    ee = jnp.where(emask, jnp.exp(logits - me), 0.0)
    pe = ee / jnp.sum(ee, axis=1, keepdims=True)
    p1 = jnp.max(pe, axis=1, keepdims=True)
    i1 = jnp.min(jnp.where(emask & (pe == p1), lane, big), axis=1, keepdims=True)
    rest = emask & (lane != i1)
    p2 = jnp.max(jnp.where(rest, pe, -jnp.inf), axis=1, keepdims=True)
    i2 = jnp.min(jnp.where(rest & (pe == p2), lane, big), axis=1, keepdims=True)
    den = p1 + p2
    ei_ref[rs, :] = jnp.where(lane == 0, i1, jnp.where(lane == 1, i2, 0))
    wt_ref[rs, :] = jnp.where(lane == 0, pg_top * p1 / den, jnp.where(lane == 1, pg_top * p2 / den, 0.0))
    tm = logits.shape[0]
    chosen = jnp.where((lane == i1) | (lane == i2), valid, 0.0)
    t_i = lax.broadcasted_iota(I32, (tm, tm), 0)
    t_j = lax.broadcasted_iota(I32, (tm, tm), 1)
    before = _dot(jnp.where(t_j < t_i, 1.0, 0.0), chosen) + run_ref[...]
    r1 = jnp.sum(jnp.where(lane == i1, before, 0.0), axis=1, keepdims=True)
    r2 = jnp.sum(jnp.where(lane == i2, before, 0.0), axis=1, keepdims=True)
    rk_ref[rs, :] = jnp.where(lane == 0, r1, jnp.where(lane == 1, r2, 0.0)).astype(I32)
    run_ref[...] += jnp.sum(chosen, axis=0, keepdims=True)


def _cast_kernel(w_ref, o_ref):
    o_ref[...] = w_ref[...].astype(o_ref.dtype)


def _to_bf16(w):
    k, n = w.shape
    tk = 512
    return pl.pallas_call(
        _cast_kernel,
        grid=(k // tk,),
        in_specs=[pl.BlockSpec((tk, n), lambda i: (i, 0))],
        out_specs=pl.BlockSpec((tk, n), lambda i: (i, 0)),
        out_shape=jax.ShapeDtypeStruct((k, n), BF16),
        name="cast_bf16",
        compiler_params=_params("parallel"),
    )(w)


def _route(x, o_att, o_rwkv, proj, gate_col0, w_up_att, w_up_rwkv, w_o, gt1, g_post, g_pre, sc2, sh2,
           w_router, b_router):
    s, d = x.shape
    ka, kr = o_att.shape[1], o_rwkv.shape[1]
    tm = 256
    gb = gate_col0 // d
    nt = s // tm
    proj_tile = lambda i: jnp.minimum(i, nt - 1)
    route_tile = lambda i: jnp.maximum(i - 1, 0)
    vec = pl.BlockSpec((1, d), lambda i: (0, 0))
    rowblk = pl.BlockSpec((tm, d), lambda i: (route_tile(i), 0))
    small = pl.BlockSpec((tm, LANES), lambda i: (route_tile(i), 0))
    lane_row = pl.BlockSpec((1, LANES), lambda i: (0, 0))
    once = pl.Buffered(1)
    return pl.pallas_call(
        _route_kernel,
        grid=(nt + 1,),
        in_specs=[rowblk,
                  pl.BlockSpec((tm, ka), lambda i: (proj_tile(i), 0)),
                  pl.BlockSpec((tm, kr), lambda i: (proj_tile(i), 0)),
                  pl.BlockSpec((tm, d), lambda i: (proj_tile(i), gb)),
                  pl.BlockSpec((tm, d), lambda i: (proj_tile(i), gb + 1)),
                  pl.BlockSpec((ka, d), lambda i: (0, 0), pipeline_mode=once),
                  pl.BlockSpec((kr, d), lambda i: (0, 0), pipeline_mode=once),
                  pl.BlockSpec((d, d), lambda i: (0, 0), pipeline_mode=once),
                  vec, vec, vec, vec, vec, pl.BlockSpec((d, LANES), lambda i: (0, 0)), lane_row],
        out_specs=[rowblk, rowblk, small, small, small, lane_row],
        out_shape=[jax.ShapeDtypeStruct((s, d), F32), jax.ShapeDtypeStruct((s, d), F32),
                   jax.ShapeDtypeStruct((s, LANES), I32), jax.ShapeDtypeStruct((s, LANES), F32),
                   jax.ShapeDtypeStruct((s, LANES), I32), jax.ShapeDtypeStruct((1, LANES), I32)],
        scratch_shapes=[pltpu.VMEM((1, LANES), F32), pltpu.VMEM((tm, d), F32)],
        name="merge_out_route",
        compiler_params=_params("arbitrary"),
    )(x, o_att, o_rwkv, proj, proj, _to_bf16(w_up_att), _to_bf16(w_up_rwkv), _to_bf16(w_o),
      gt1, g_post, g_pre, sc2, sh2, w_router, b_router)


def _slots_kernel(ei_ref, rk_ref, ps_ref, pos_ref):
    lane = lax.broadcasted_iota(I32, ei_ref.shape, 1)
    ei, rk = ei_ref[...], rk_ref[...]
    ps = ps_ref[...]
    cols = []
    for kk in range(TOP_K):
        e = ei[:, kk:kk + 1]
        cols.append(jnp.sum(jnp.where(lane == e, ps, 0), axis=1, keepdims=True) + rk[:, kk:kk + 1])
    pos_ref[...] = jnp.where(lane == 0, cols[0], jnp.where(lane == 1, cols[1], 0))


def _slots(e_idx, rank, pstart_row):
    n = e_idx.shape[0]
    tm = 1024
    blk = pl.BlockSpec((tm, LANES), lambda i: (i, 0))
    return pl.pallas_call(
        _slots_kernel,
        grid=(n // tm,),
        in_specs=[blk, blk, pl.BlockSpec((1, LANES), lambda i: (0, 0))],
        out_specs=blk,
        out_shape=jax.ShapeDtypeStruct((n, LANES), I32),
        name="dispatch_slots",
        compiler_params=_params("parallel"),
    )(e_idx, rank, pstart_row)


def _row_copy(src, row, dst, dst_row, sem):
    return pltpu.make_async_copy(src.at[pl.ds(row, 1)], dst.at[pl.ds(dst_row, 1)], sem)


def _dispatch_kernel(pos_ref, cnt_ref, pst_ref, nused_ref, h_ref, x_hbm, stage_ref, zero_ref, sem, zsem):
    i = pl.program_id(0)
    tm = h_ref.shape[0]
    n_blk = x_hbm.shape[0] // EXPERT_BLOCK
    nused = nused_ref[0]
    slot = i % 2

    def wait_tile(s):
        for _ in range(TOP_K):
            pltpu.make_async_copy(stage_ref.at[s], x_hbm.at[pl.ds(0, tm)], sem.at[s]).wait()

    @pl.when(i >= 2)
    def _():
        wait_tile(slot)

    stage_ref[slot] = h_ref[...]
    for r in range(tm):
        for kk in range(TOP_K):
            _row_copy(stage_ref.at[slot], r, x_hbm, pos_ref[(i * tm + r) * TOP_K + kk],
                      sem.at[slot]).start(priority=(r * TOP_K + kk) % 2)

    @pl.when(i == 0)
    def _():
        zero_ref[...] = jnp.zeros_like(zero_ref)

        def pad_expert(start):
            def body(e, _):
                lo = pst_ref[e] + cnt_ref[e]
                length = (EXPERT_BLOCK - cnt_ref[e] % EXPERT_BLOCK) % EXPERT_BLOCK
                hi = lo + length
                size = EXPERT_BLOCK // 2
                while size >= SUBLANES:
                    at = pl.multiple_of(hi - (length & ~(size - 1)), size)

                    @pl.when((length & size) != 0)
                    def _(at=at, size=size):
                        cp = pltpu.make_async_copy(zero_ref.at[pl.ds(0, size)], x_hbm.at[pl.ds(at, size)], zsem)
                        if start:
                            cp.start()
                        else:
                            cp.wait()

                    size //= 2

                def single(s, _):
                    cp = _row_copy(zero_ref, 0, x_hbm, s, zsem)
                    if start:
                        cp.start()
                    else:
                        cp.wait()
                    return 0

                lax.fori_loop(lo, lo + (length & (SUBLANES - 1)), single, 0)
                return 0
            return body

        def tail_copy(blk):
            return pltpu.make_async_copy(zero_ref, x_hbm.at[pl.ds(blk * EXPERT_BLOCK, EXPERT_BLOCK)], zsem)

        def tail_start(blk, _):
            tail_copy(blk).start()
            return 0

        def tail_wait(blk, _):
            tail_copy(blk).wait()
            return 0

        lax.fori_loop(0, cnt_ref.shape[0], pad_expert(True), 0)
        lax.fori_loop(nused, n_blk, tail_start, 0)
        lax.fori_loop(0, cnt_ref.shape[0], pad_expert(False), 0)
        lax.fori_loop(nused, n_blk, tail_wait, 0)

    @pl.when(i == pl.num_programs(0) - 1)
    def _():
        @pl.when(i >= 1)
        def _():
            wait_tile(1 - slot)

        wait_tile(slot)


def _dispatch(pos_flat, counts, pstart, nused, h2, n_slots):
    n, w = h2.shape
    tm = 512
    grid_spec = pltpu.PrefetchScalarGridSpec(
        num_scalar_prefetch=4,
        grid=(n // tm,),
        in_specs=[pl.BlockSpec((tm, w), lambda i, *_: (i, 0))],
        out_specs=pl.BlockSpec(memory_space=pl.ANY),
        scratch_shapes=[pltpu.VMEM((2, tm, w), h2.dtype), pltpu.VMEM((EXPERT_BLOCK, w), h2.dtype),
                        pltpu.SemaphoreType.DMA((2,)), pltpu.SemaphoreType.DMA(())],
    )
    return pl.pallas_call(
        _dispatch_kernel,
        grid_spec=grid_spec,
        out_shape=jax.ShapeDtypeStruct((n_slots, w), h2.dtype),
        name="dispatch_rows",
        compiler_params=_params("arbitrary"),
    )(pos_flat, counts, pstart, nused, h2)


EXPERT_X_SLOTS = 4
EXPERT_Y_SLOTS = 3


def _expert_kernel(bstart_ref, bcount_ref, nused_ref, x_hbm, wg_ref, wu_ref, wd_ref, y_hbm,
                   xs_ref, yo_ref, wgb_ref, wub_ref, wdb_ref, xsem, osem):
    e = pl.program_id(0)
    nused = nused_ref[0]
    rows = EXPERT_BLOCK
    n_blk = y_hbm.shape[0] // rows
    nx, ny = EXPERT_X_SLOTS, EXPERT_Y_SLOTS
    first = bstart_ref[e]
    count = bcount_ref[e]

    def x_copy(blk):
        src = x_hbm.at[pl.ds(jnp.minimum(blk, n_blk - 1) * rows, rows)]
        return pltpu.make_async_copy(src, xs_ref.at[blk % nx], xsem.at[blk % nx])

    def out_copy(blk):
        return pltpu.make_async_copy(yo_ref.at[blk % ny], y_hbm.at[pl.ds(blk * rows, rows), :], osem.at[blk % ny])

    @pl.when(e == 0)
    def _():
        for j in range(nx - 1):
            x_copy(j).start()

    @pl.when(count > 0)
    def _():
        wgb_ref[...] = wg_ref[0].astype(BF16)
        wub_ref[...] = wu_ref[0].astype(BF16)
        wdb_ref[...] = wd_ref[0].astype(BF16)

    def block(j, _):
        blk = first + j

        @pl.when(blk >= ny)
        def _():
            out_copy(blk - ny).wait()

        x_copy(blk).wait()
        x_copy(blk + nx - 1).start()
        xb = xs_ref[blk % nx].astype(BF16)
        hg = jnp.dot(xb, wgb_ref[...], preferred_element_type=F32)
        hu = jnp.dot(xb, wub_ref[...], preferred_element_type=F32)
        hid = hg * _sigmoid(hg) * hu
        yo_ref[blk % ny] = jnp.dot(hid.astype(BF16), wdb_ref[...], preferred_element_type=F32)
        out_copy(blk).start()

        @pl.when(blk == nused - 1)
        def _():
            for ahead in range(1, nx):
                x_copy(blk + ahead).wait()

        return 0

    lax.fori_loop(0, count, block, 0)

    @pl.when(e == pl.num_programs(0) - 1)
    def _():
        for back in range(1, ny + 1):
            @pl.when(nused - back >= 0)
            def _():
                out_copy(nused - back).wait()

        yo_ref[0] = jnp.zeros(yo_ref.shape[1:], yo_ref.dtype)

        def zero_copy(blk):
            return pltpu.make_async_copy(yo_ref.at[0], y_hbm.at[pl.ds(blk * rows, rows), :], osem.at[0])

        def fill(blk, _):
            zero_copy(blk).start()
            return 0

        def drain(blk, _):
            zero_copy(blk).wait()
            return 0

        lax.fori_loop(nused, n_blk, fill, 0)
        lax.fori_loop(nused, n_blk, drain, 0)


def _experts(x_buf, bstart, bcount, nused, w_gate_e, w_up_e, w_down_e):
    n_slots = x_buf.shape[0]
    n_exp, d, f = w_gate_e.shape
    grid_spec = pltpu.PrefetchScalarGridSpec(
        num_scalar_prefetch=3,
        grid=(n_exp,),
        in_specs=[pl.BlockSpec(memory_space=pl.ANY),
                  pl.BlockSpec((1, d, f), lambda e, *_: (e, 0, 0)),
                  pl.BlockSpec((1, d, f), lambda e, *_: (e, 0, 0)),
                  pl.BlockSpec((1, f, d), lambda e, *_: (e, 0, 0))],
        out_specs=pl.BlockSpec(memory_space=pl.ANY),
        scratch_shapes=[pltpu.VMEM((EXPERT_X_SLOTS, EXPERT_BLOCK, d), F32),
                        pltpu.VMEM((EXPERT_Y_SLOTS, EXPERT_BLOCK, d), F32),
                        pltpu.VMEM((d, f), BF16), pltpu.VMEM((d, f), BF16), pltpu.VMEM((f, d), BF16),
                        pltpu.SemaphoreType.DMA((EXPERT_X_SLOTS,)), pltpu.SemaphoreType.DMA((EXPERT_Y_SLOTS,))],
    )
    return pl.pallas_call(
        _expert_kernel,
        grid_spec=grid_spec,
        out_shape=jax.ShapeDtypeStruct((n_slots, d), F32),
        name="experts",
        compiler_params=_params("arbitrary"),
    )(bstart, bcount, nused, x_buf, w_gate_e, w_up_e, w_down_e)


def _combine_kernel(pos_ref, y_hbm, wt_ref, x1_ref, gt_ref, gpost_ref, o_ref, rows_ref, sem):
    i = pl.program_id(0)
    nsteps = pl.num_programs(0)
    tm = x1_ref.shape[0]

    def start_rows(step, slot):
        for r in range(tm):
            for kk in range(TOP_K):
                _row_copy(y_hbm, pos_ref[(step * tm + r) * TOP_K + kk], rows_ref.at[slot, kk], r,
                          sem.at[slot]).start(priority=(r * TOP_K + kk) % 2)

    def wait_rows(slot):
        for kk in range(TOP_K):
            pltpu.make_async_copy(y_hbm.at[pl.ds(0, tm), :], rows_ref.at[slot, kk], sem.at[slot]).wait()

    @pl.when(i == 0)
    def _():
        start_rows(0, 0)

    slot = i % 2
    wait_rows(slot)
    start_rows(jnp.minimum(i + 1, nsteps - 1), 1 - slot)
    wt = wt_ref[...]
    y = rows_ref[slot, 0] * wt[:, 0:1] + rows_ref[slot, 1] * wt[:, 1:2]
    o_ref[...] = x1_ref[...] + gt_ref[...] * _rms(y, gpost_ref[...])

    @pl.when(i == nsteps - 1)
    def _():
        wait_rows(1 - slot)


def _combine(pos, y_buf, wts, x1, gt2, g_post):
    n, d = x1.shape
    tm = 512
    vec = pl.BlockSpec((1, d), lambda i, p: (0, 0))
    grid_spec = pltpu.PrefetchScalarGridSpec(
        num_scalar_prefetch=1,
        grid=(n // tm,),
        in_specs=[pl.BlockSpec(memory_space=pl.ANY),
                  pl.BlockSpec((tm, LANES), lambda i, p: (i, 0)),
                  pl.BlockSpec((tm, d), lambda i, p: (i, 0)), vec, vec],
        out_specs=pl.BlockSpec((tm, d), lambda i, p: (i, 0)),
        scratch_shapes=[pltpu.VMEM((2, TOP_K, tm, d), F32), pltpu.SemaphoreType.DMA((2,))],
    )
    return pl.pallas_call(
        _combine_kernel,
        grid_spec=grid_spec,
        out_shape=jax.ShapeDtypeStruct((n, d), F32),
        name="combine",
        compiler_params=_params("arbitrary"),
    )(pos, y_buf, wts, x1, gt2, g_post)


def _segment_tables(counts_row):
    counts = counts_row[0, :N_EXPERTS]
    pcounts = (counts + EXPERT_BLOCK - 1) // EXPERT_BLOCK * EXPERT_BLOCK
    pend = jnp.cumsum(pcounts)
    pstart = pend - pcounts
    nused = (pend[-1] // EXPERT_BLOCK).astype(I32)
    pstart_row = jnp.pad(pstart, (0, LANES - N_EXPERTS)).reshape(1, LANES)
    return counts, pstart, pstart_row, pstart // EXPERT_BLOCK, pcounts // EXPERT_BLOCK, nused.reshape(1)


def _rope_tables(s):
    half = ROPE_DIM // 2
    inv = ROPE_THETA ** (-np.arange(half, dtype=np.float64) / half)
    ang = np.arange(s, dtype=np.float64)[:, None] * inv[None, :]
    cos, sin = jnp.asarray(np.cos(ang), F32), jnp.asarray(np.sin(ang), F32)
    pad = jnp.zeros((s, LANES - ROPE_DIM), F32)
    zero = jnp.zeros((s, half), F32)
    cos_t = jnp.concatenate([cos, cos, pad + 1.0], axis=1)
    sin1_t = jnp.concatenate([-sin, zero, pad], axis=1)
    sin2_t = jnp.concatenate([zero, sin, pad], axis=1)
    return cos_t, sin1_t, sin2_t


def _layer(x, c_col, w_ada, b_ada, g_pre_mix, g_post_mix, g_pre_ffn, g_post_ffn, w_in, mu_r, mu_k, mu_v,
           mu_w, mu_a, mu_g, w0, w_w1, w_w2, a0, w_a1, w_a2, w_g1, w_g2, k_k, k_a, r_k, gn_w, gn_b,
           w_up_att, w_up_rwkv, w_o, w_rg, b_rg, w_re, b_re, w_gate_e, w_up_e, w_down_e):
    s, d = x.shape
    att_w = ATT_HEADS * ATT_HEAD_DIM
    rwkv_w = w_up_rwkv.shape[0]
    row = lambda a: a.reshape(1, -1)

    ada = _ada(c_col, w_ada, row(b_ada))
    sh1, sc1, gt1, sh2, sc2, gt2 = (ada[:, i * d:(i + 1) * d] for i in range(6))

    h, lw, la, lg = _prenorm(x, row(g_pre_mix), sc1, sh1, row(mu_w), row(mu_a), row(mu_g), w_w1, w_a1, w_g1)
    proj = _matmul(h, w_in, BF16)

    q_aug_t, k_aug, v_t = _rope_gate(proj, *_rope_tables(s))
    o_att = _attention(q_aug_t, k_aug, v_t)

    o_rwkv = _rwkv(proj, lw, la, lg, w_w2, w_a2, w_g2, row(mu_r), row(mu_k), row(mu_v), row(w0), row(a0),
                   row(k_k), row(k_a), row(r_k), row(gn_w), row(gn_b), col0=3 * att_w, width=rwkv_w)


    w_router = jnp.pad(jnp.concatenate([w_re, w_rg], axis=1), ((0, 0), (0, LANES - N_EXPERTS - N_GROUPS)))
    b_router = jnp.pad(jnp.concatenate([b_re, b_rg]), (0, LANES - N_EXPERTS - N_GROUPS)).reshape(1, LANES)
    x1, h2, e_idx, wts, rank, counts = _route(x, o_att, o_rwkv, proj, 3 * att_w + 3 * rwkv_w, w_up_att, w_up_rwkv,
                                              w_o, gt1, row(g_post_mix), row(g_pre_ffn), sc2, sh2,
                                              w_router, b_router)

    n_pairs = s * TOP_K
    n_blk = (n_pairs + N_EXPERTS * (EXPERT_BLOCK - 1) + EXPERT_BLOCK - 1) // EXPERT_BLOCK
    counts, pstart, pstart_row, bstart, bcount, nused = _segment_tables(counts)
    pos = _slots(e_idx, rank, pstart_row)[:, :TOP_K].reshape(n_pairs)
    x_buf = _dispatch(pos, counts, pstart, nused, h2, n_blk * EXPERT_BLOCK)
    y_buf = _experts(x_buf, bstart, bcount, nused, w_gate_e, w_up_e, w_down_e)
    return _combine(pos, y_buf, wts, x1, gt2, row(g_post_ffn))


def kernel(x, c, w_ada, b_ada, g_pre_mix, g_post_mix, g_pre_ffn, g_post_ffn, w_in, mu_r, mu_k, mu_v, mu_w, mu_a, mu_g, w0, w_w1, w_w2, a0, w_a1, w_a2, w_g1, w_g2, k_k, k_a, r_k, gn_w, gn_b, w_up_att, w_up_rwkv, w_o, w_rg, b_rg, w_re, b_re, w_gate_e, w_up_e, w_down_e):
    b, s, d = x.shape
    assert b == 1, "one sequence per call"
    params = (w_ada, b_ada, g_pre_mix, g_post_mix, g_pre_ffn, g_post_ffn, w_in, mu_r, mu_k, mu_v, mu_w, mu_a,
              mu_g, w0, w_w1, w_w2, a0, w_a1, w_a2, w_g1, w_g2, k_k, k_a, r_k, gn_w, gn_b, w_up_att,
              w_up_rwkv, w_o, w_rg, b_rg, w_re, b_re, w_gate_e, w_up_e, w_down_e)
    xs = x.reshape(s, d)
    c_col = c.reshape(d, 1)
    for l in range(w_ada.shape[0]):
        xs = _layer(xs, c_col, *(p[l] for p in params))
    return xs.reshape(b, s, d)
```

```python
import math

import jax
import jax.numpy as jnp
import numpy as np
from jax import lax
from jax.experimental import pallas as pl
from jax.experimental.pallas import tpu as pltpu

F32 = jnp.float32
BF16 = jnp.bfloat16
I32 = jnp.int32
HI = lax.Precision.HIGHEST

LANES = 128
SUBLANES = 8
VMEM_LIMIT = 56 * 1024 * 1024

ATT_HEADS = 8
ATT_HEAD_DIM = 128
MOBA_BLOCK = 256
MOBA_TOPK = 3
ATT_GROUP = 4
ATT_HEADS_PER_STEP = 2
ROPE_THETA = 500000.0
ROPE_DIM = ATT_HEAD_DIM // 4
RWKV_HEAD_DIM = 64
GN_EPS = 64e-5
N_GROUPS = 8
EXPERTS_PER_GROUP = 8
N_EXPERTS = N_GROUPS * EXPERTS_PER_GROUP
TOP_K = 2
EXPERT_BLOCK = 128
RMS_EPS = 1e-6
NEG = -1e30
SCAN_CHUNK = 64
RWKV_CHUNK_GROUP = 8
RWKV_ROWS_PER_STEP = 2048
Q_SCALE = ATT_HEAD_DIM ** -0.5 * math.log2(math.e)


def _params(*sem):
    return pltpu.CompilerParams(dimension_semantics=sem, vmem_limit_bytes=VMEM_LIMIT)


def _rms(z, g):
    return z * lax.rsqrt(jnp.mean(z * z, axis=-1, keepdims=True) + RMS_EPS) * g


def _sigmoid(z):
    return 1.0 / (1.0 + jnp.exp(-z))


def _dot(a, b):
    return jnp.dot(a.astype(BF16), b.astype(BF16), preferred_element_type=F32)


def _dot_nt(a, b):
    return lax.dot_general(a.astype(BF16), b.astype(BF16), (((1,), (1,)), ((), ())),
                           preferred_element_type=F32)


def _dot_tn(a, b):
    return lax.dot_general(a.astype(BF16), b.astype(BF16), (((0,), (0,)), ((), ())),
                           preferred_element_type=F32)


def _dot_hi(a, b):
    return jnp.dot(a, b, precision=HI, preferred_element_type=F32)


def _dot_x3(a, b):
    ah, al, _ = _split3(a)
    bh, bl, _ = _split3(b)
    return (jnp.dot(ah, bh, preferred_element_type=F32) + jnp.dot(ah, bl, preferred_element_type=F32)
            + jnp.dot(al, bh, preferred_element_type=F32))


def _split3(a):
    hi = a.astype(BF16)
    r1 = a - hi.astype(F32)
    mid = r1.astype(BF16)
    lo = (r1 - mid.astype(F32)).astype(BF16)
    return hi, mid, lo


def _dot_split(a, b01):
    b = b01.astype(BF16)
    hi, mid, _ = _split3(a)
    return jnp.dot(jnp.concatenate([hi, mid], axis=1), jnp.concatenate([b, b], axis=0),
                   preferred_element_type=F32)


def _dot_split_t(b01, a):
    b = b01.astype(BF16)
    return jnp.dot(jnp.concatenate([b, b, b], axis=1), jnp.concatenate(_split3(a), axis=0),
                   preferred_element_type=F32)


def _shift_rows(z, prev_row):
    rolled = pltpu.roll(z, 1, 0)
    row = lax.broadcasted_iota(I32, z.shape, 0)
    return jnp.where(row == 0, prev_row, rolled)


def _ada_kernel(c_ref, w_ref, b_ref, o_ref):
    o_ref[...] = jnp.sum(c_ref[...] * w_ref[...], axis=0, keepdims=True) + b_ref[...]


def _ada(c_col, w_ada, b_ada):
    d, n = w_ada.shape
    tn = 1024
    return pl.pallas_call(
        _ada_kernel,
        grid=(n // tn,),
        in_specs=[pl.BlockSpec((d, 1), lambda j: (0, 0)),
                  pl.BlockSpec((d, tn), lambda j: (0, j)),
                  pl.BlockSpec((1, tn), lambda j: (0, j))],
        out_specs=pl.BlockSpec((1, tn), lambda j: (0, j)),
        out_shape=jax.ShapeDtypeStruct((1, n), F32),
        name="ada",
        compiler_params=_params("parallel"),
    )(c_col, w_ada, b_ada)


def _prenorm_kernel(x_ref, xp_ref, g_ref, sc_ref, sh_ref, muw_ref, mua_ref, mug_ref,
                    ww1_ref, wa1_ref, wg1_ref, h_ref, lw_ref, la_ref, lg_ref):
    i = pl.program_id(0)
    g, sc, sh = g_ref[...], sc_ref[...], sh_ref[...]
    h = _rms(x_ref[...], g) * (1.0 + sc) + sh
    hp = _rms(xp_ref[SUBLANES - 1:SUBLANES, :], g) * (1.0 + sc) + sh
    hp = jnp.where(i == 0, 0.0, hp)
    dh = _shift_rows(h, hp) - h
    h_ref[...] = h.astype(BF16)
    lw_ref[...] = jnp.tanh(_dot(h + dh * muw_ref[...], ww1_ref[...]))
    la_ref[...] = _dot(h + dh * mua_ref[...], wa1_ref[...])
    lg_ref[...] = _sigmoid(_dot(h + dh * mug_ref[...], wg1_ref[...]))


def _prenorm(x, g, sc, sh, mu_w, mu_a, mu_g, w_w1, w_a1, w_g1):
    s, d = x.shape
    tm = 256
    rpb = tm // SUBLANES
    vec = pl.BlockSpec((1, d), lambda i: (0, 0))
    full = lambda a: pl.BlockSpec(a.shape, lambda i: (0, 0))
    lw, la, lg = w_w1.shape[1], w_a1.shape[1], w_g1.shape[1]
    return pl.pallas_call(
        _prenorm_kernel,
        grid=(s // tm,),
        in_specs=[pl.BlockSpec((tm, d), lambda i: (i, 0)),
                  pl.BlockSpec((SUBLANES, d), lambda i: (jnp.maximum(i * rpb - 1, 0), 0)),
                  vec, vec, vec, vec, vec, vec, full(w_w1), full(w_a1), full(w_g1)],
        out_specs=[pl.BlockSpec((tm, d), lambda i: (i, 0)),
                   pl.BlockSpec((tm, lw), lambda i: (i, 0)),
                   pl.BlockSpec((tm, la), lambda i: (i, 0)),
                   pl.BlockSpec((tm, lg), lambda i: (i, 0))],
        out_shape=[jax.ShapeDtypeStruct((s, d), BF16),
                   jax.ShapeDtypeStruct((s, lw), F32),
                   jax.ShapeDtypeStruct((s, la), F32),
                   jax.ShapeDtypeStruct((s, lg), F32)],
        name="prenorm_lora",
        compiler_params=_params("parallel"),
    )(x, x, g, sc, sh, mu_w, mu_a, mu_g, w_w1, w_a1, w_g1)


def _mm_kernel(a_ref, w_ref, o_ref, wb_ref):
    @pl.when(pl.program_id(1) == 0)
    def _():
        wb_ref[...] = w_ref[...].astype(BF16)

    o_ref[...] = jnp.dot(a_ref[...], wb_ref[...], preferred_element_type=F32).astype(o_ref.dtype)


def _matmul(a, w, out_dtype, tm=512, tn=1024):
    m, k = a.shape
    n = w.shape[1]
    tn = min(tn, n)
    return pl.pallas_call(
        _mm_kernel,
        grid=(n // tn, m // tm),
        in_specs=[pl.BlockSpec((tm, k), lambda j, i: (i, 0)),
                  pl.BlockSpec((k, tn), lambda j, i: (0, j))],
        out_specs=pl.BlockSpec((tm, tn), lambda j, i: (i, j)),
        out_shape=jax.ShapeDtypeStruct((m, n), out_dtype),
        scratch_shapes=[pltpu.VMEM((k, tn), BF16)],
        name="matmul",
        compiler_params=_params("arbitrary", "arbitrary"),
    )(a, w)


def _rope_gate_kernel(p_ref, c_ref, s1_ref, s2_ref, qa_ref, ka_ref, vt_ref, km_ref):
    i = pl.program_id(0)
    bs = MOBA_BLOCK
    nbp = km_ref.shape[1]

    @pl.when(i == 0)
    def _():
        km_ref[...] = jnp.zeros_like(km_ref)

    c, s1, s2 = c_ref[...], s1_ref[...], s2_ref[...]

    def rope(z):
        return z * c + pltpu.roll(z, LANES - ROPE_DIM // 2, 1) * s1 + pltpu.roll(z, ROPE_DIM // 2, 1) * s2

    row = lax.broadcasted_iota(I32, (nbp, bs), 0)
    lane = lax.broadcasted_iota(I32, (bs, LANES), 1)
    onehot = jnp.where(lane == i, 1.0, 0.0).astype(BF16)
    for h in range(ATT_HEADS):
        q = rope(p_ref[:, h * LANES:(h + 1) * LANES].astype(F32))
        k = rope(p_ref[:, (ATT_HEADS + h) * LANES:(ATT_HEADS + h + 1) * LANES].astype(F32))
        g = lax.dot_general(km_ref[h], q, (((1,), (1,)), ((), ())), precision=HI, preferred_element_type=F32)
        g = jnp.where(row < i, g, NEG)
        sel_t = jnp.zeros(g.shape, F32)
        for _ in range(MOBA_TOPK):
            mx = jnp.max(g, axis=0, keepdims=True)
            idx = jnp.min(jnp.where(g == mx, row, nbp), axis=0, keepdims=True)
            hit = row == idx
            sel_t = jnp.where(hit & (row < i), 1.0, sel_t)
            g = jnp.where(hit, -jnp.inf, g)
        if nbp < LANES:
            sel_t = jnp.concatenate([sel_t, jnp.zeros((LANES - nbp, bs), F32)], axis=0)
        w = 2 * LANES
        qa_ref[h, :LANES, :] = (q * Q_SCALE).T.astype(BF16)
        qa_ref[h, LANES:, :] = jnp.where(sel_t > 0.5, 0.0, NEG).astype(BF16)
        ka_ref[:, h * w:h * w + LANES] = k.astype(BF16)
        ka_ref[:, h * w + LANES:(h + 1) * w] = onehot
        v = p_ref[:, (2 * ATT_HEADS + h) * LANES:(2 * ATT_HEADS + h + 1) * LANES].astype(F32)
        vt_ref[h, 0] = v.T.astype(BF16)
        km_ref[h, pl.ds(i, 1), :] = jnp.mean(k, axis=0, keepdims=True)


def _rope_gate(proj, cos_t, sin1_t, sin2_t):
    s = proj.shape[0]
    nb = s // MOBA_BLOCK
    assert nb <= LANES
    nbp = -(-nb // SUBLANES) * SUBLANES
    w_in = 3 * ATT_HEADS * ATT_HEAD_DIM
    w_out = 2 * ATT_HEADS * LANES
    tab = pl.BlockSpec((MOBA_BLOCK, LANES), lambda i: (i, 0))
    return pl.pallas_call(
        _rope_gate_kernel,
        grid=(nb,),
        in_specs=[pl.BlockSpec((MOBA_BLOCK, w_in), lambda i: (i, 0)), tab, tab, tab],
        out_specs=[pl.BlockSpec((ATT_HEADS, 2 * LANES, MOBA_BLOCK), lambda i: (0, 0, i)),
                   pl.BlockSpec((MOBA_BLOCK, w_out), lambda i: (i, 0)),
                   pl.BlockSpec((ATT_HEADS, 1, LANES, MOBA_BLOCK), lambda i: (0, i, 0, 0))],
        out_shape=[jax.ShapeDtypeStruct((ATT_HEADS, 2 * LANES, s), BF16),
                   jax.ShapeDtypeStruct((s, w_out), BF16),
                   jax.ShapeDtypeStruct((ATT_HEADS, nb, LANES, MOBA_BLOCK), BF16)],
        scratch_shapes=[pltpu.VMEM((ATT_HEADS, nbp, LANES), F32)],
        name="rope_gate",
        compiler_params=_params("arbitrary"),
    )(proj, cos_t, sin1_t, sin2_t)


def _attn_kernel(qa_ref, ka_ref, vt_ref, o_ref, s_ref):
    qi = pl.program_id(1)
    bs = MOBA_BLOCK
    grp = ATT_GROUP * bs
    w = 2 * LANES
    heads = range(ATT_HEADS_PER_STEP)

    n_groups = ka_ref.shape[0] // grp

    def issue_scores(g, buf):
        base = pl.multiple_of(jnp.minimum(g, n_groups - 1) * grp, grp)
        for h in heads:
            s_ref[buf, h] = jnp.dot(ka_ref[pl.ds(base, grp), h * w:(h + 1) * w], qa_ref[h],
                                    preferred_element_type=F32)

    issue_scores(0, 0)

    own = pl.multiple_of(qi * bs, bs)
    k_i = lax.broadcasted_iota(I32, (bs, bs), 0)
    q_i = lax.broadcasted_iota(I32, (bs, bs), 1)
    carry = []
    for h in heads:
        s = jnp.dot(ka_ref[pl.ds(own, bs), h * w:h * w + LANES], qa_ref[h, :LANES, :],
                    preferred_element_type=F32)
        s = jnp.where(k_i <= q_i, s, NEG)
        m = jnp.max(s, axis=0, keepdims=True)
        p = jnp.exp2(s - m)
        l = jnp.sum(p, axis=0, keepdims=True)
        acc = jnp.dot(vt_ref[h, qi], p.astype(BF16), preferred_element_type=F32)
        carry += [m, l, acc]

    def absorb(g, buf, carry):
        out = []
        for h in heads:
            m, l, acc = carry[3 * h:3 * h + 3]
            s = s_ref[buf, h]
            mn = jnp.maximum(m, jnp.max(s, axis=0, keepdims=True))
            alpha = jnp.exp2(m - mn)
            p = jnp.exp2(s - mn)
            l = alpha * l + jnp.sum(p, axis=0, keepdims=True)
            p = p.astype(BF16)
            acc = alpha * acc
            for jb in range(ATT_GROUP):
                acc = acc + jnp.dot(vt_ref[h, g * ATT_GROUP + jb], p[jb * bs:(jb + 1) * bs],
                                    preferred_element_type=F32)
            out += [mn, l, acc]
        return out

    def body(t, carry):
        issue_scores(2 * t + 1, 1)
        carry = absorb(2 * t, 0, carry)
        issue_scores(2 * t + 2, 0)
        return tuple(absorb(2 * t + 1, 1, carry))

    n_used = (qi + ATT_GROUP - 1) // ATT_GROUP
    carry = lax.fori_loop(0, n_used // 2, body, tuple(carry))
    carry = lax.cond(n_used % 2 == 1, lambda c: tuple(absorb(n_used - 1, 0, c)), lambda c: c, carry)
    for h in heads:
        m, l, acc = carry[3 * h:3 * h + 3]
        o_ref[:, h * LANES:(h + 1) * LANES] = (acc / l).T.astype(BF16)


def _attention(q_aug_t, k_aug, v_t):
    s = k_aug.shape[0]
    assert s % (ATT_GROUP * MOBA_BLOCK) == 0
    nb = s // MOBA_BLOCK
    hps = ATT_HEADS_PER_STEP
    once = pl.Buffered(1)
    return pl.pallas_call(
        _attn_kernel,
        grid=(ATT_HEADS // hps, nb),
        in_specs=[pl.BlockSpec((hps, 2 * LANES, MOBA_BLOCK), lambda h, i: (h, 0, i)),
                  pl.BlockSpec((s, hps * 2 * LANES), lambda h, i: (0, h), pipeline_mode=once),
                  pl.BlockSpec((hps, nb, LANES, MOBA_BLOCK), lambda h, i: (h, 0, 0, 0), pipeline_mode=once)],
        out_specs=pl.BlockSpec((MOBA_BLOCK, hps * LANES), lambda h, i: (i, h)),
        out_shape=jax.ShapeDtypeStruct((s, ATT_HEADS * ATT_HEAD_DIM), BF16),
        scratch_shapes=[pltpu.VMEM((2, hps, ATT_GROUP * MOBA_BLOCK, MOBA_BLOCK), F32)],
        name="moba_attention",
        compiler_params=_params("parallel", "arbitrary"),
    )(q_aug_t, k_aug, v_t)


def _rwkv_kernel(r_ref, k_ref, v_ref, lw_ref, la_ref, lg_ref, ww2_ref, wa2_ref, wg2_ref,
                 mur_ref, muk_ref, muv_ref, w0_ref, a0_ref, kk_ref, ka_ref, rk_ref, gnw_ref, gnb_ref,
                 o_ref, st_ref, prev_ref):
    j = pl.program_id(1)
    t = SCAN_CHUNK
    n = RWKV_HEAD_DIM
    ts = r_ref.shape[0]
    chunks = range(ts // t)

    @pl.when(j == 0)
    def _():
        st_ref[...] = jnp.zeros_like(st_ref)
        prev_ref[...] = jnp.zeros_like(prev_ref)

    rp, kp, vp = r_ref[...].astype(F32), k_ref[...].astype(F32), v_ref[...].astype(F32)
    r = rp + (_shift_rows(rp, prev_ref[0:1, :]) - rp) * mur_ref[...]
    k = kp + (_shift_rows(kp, prev_ref[1:2, :]) - kp) * muk_ref[...]
    v = vp + (_shift_rows(vp, prev_ref[2:3, :]) - vp) * muv_ref[...]
    prev_ref[0:1, :] = rp[ts - 1:ts, :]
    prev_ref[1:2, :] = kp[ts - 1:ts, :]
    prev_ref[2:3, :] = vp[ts - 1:ts, :]

    d = w0_ref[...] + _dot(lw_ref[...], ww2_ref[...])
    logw = -math.exp(-0.5) * _sigmoid(d)
    a = _sigmoid(a0_ref[...] + _dot(la_ref[...], wa2_ref[...]))
    g = _dot(lg_ref[...], wg2_ref[...])

    li = lax.broadcasted_iota(I32, (LANES, LANES), 0)
    lj = lax.broadcasted_iota(I32, (LANES, LANES), 1)
    same_head = (li // n) == (lj // n)
    head_sum = jnp.where(same_head, 1.0, 0.0)

    kk = k * kk_ref[...]
    kk = kk / jnp.maximum(jnp.sqrt(_dot_split(kk * kk, head_sum)), 1e-12)
    kt = k * (1.0 + (a - 1.0) * ka_ref[...])
    bonus = _dot_split(r * kt * rk_ref[...], head_sum) * v

    same_blk = (li // t) == (lj // t)
    m_strict = same_blk & (lj < li)
    m_incl = same_blk & (lj <= li)
    ti = lax.broadcasted_iota(I32, (t, t), 0)
    tj = lax.broadcasted_iota(I32, (t, t), 1)
    tril_incl = jnp.where(tj <= ti, 1.0, 0.0)
    lane_a = lax.broadcasted_iota(I32, (t, LANES), 1) < n
    eye = jnp.where(li == lj, 1.0, 0.0)

    def stack_masked(z):
        return jnp.concatenate([jnp.where(lane_a, z, 0.0), jnp.where(lane_a, 0.0, z)], axis=0)

    def stack_plain(z):
        return jnp.concatenate([z, z], axis=0)

    def rows(z, c):
        return z[c * t:(c + 1) * t]

    cum = jnp.concatenate([_dot_split_t(tril_incl, rows(logw, c)) for c in chunks], axis=0)
    g_t = jnp.exp(cum)
    g_inv = jnp.exp(-cum)
    xa_f = -kk * jnp.exp(cum - logw)
    xr_f = r * g_t
    yb_f = kk * a * g_inv
    yk_f = kt * g_inv

    zeros = jnp.zeros((2 * t, LANES), F32)
    state = [st_ref[...]]
    ys = {}

    def chain_step(c, r_hat, y_hat, pm, qm):
        def run():
            st = state[0]
            y2 = _dot_nt(r_hat, st) + y_hat
            ys[c] = y2[:t] + y2[t:]
            state[0] = (st + _dot(st, pm) + qm) * g_t[(c + 1) * t - 1:(c + 1) * t, :]
        return run

    def chunk_local(grp, between):
        idx = range(len(grp))
        xa = [stack_masked(rows(xa_f, c)) for c in grp]
        xr = [stack_masked(rows(xr_f, c)) for c in grp]
        yb = [stack_plain(rows(yb_f, c)) for c in grp]
        vs = [stack_masked(rows(v, c)) for c in grp]
        ybk = [jnp.concatenate([yb[i], stack_plain(rows(yk_f, c))], axis=0) for i, c in enumerate(grp)]
        sc = [_dot_nt(jnp.concatenate([xa[i], xr[i]], axis=0), ybk[i]) for i in idx]
        between()
        a_ab = [jnp.where(m_strict, sc[i][:2 * t, :2 * t], 0.0) for i in idx]
        a_ak = [jnp.where(m_strict, sc[i][:2 * t, 2 * t:], 0.0) for i in idx]
        a_rb = [jnp.where(m_incl, sc[i][2 * t:, :2 * t], 0.0) for i in idx]
        a_rk = [jnp.where(m_incl, sc[i][2 * t:, 2 * t:], 0.0) for i in idx]
        inv = [eye + a_ab[i] for i in idx]
        pw = [_dot(a_ab[i], a_ab[i]) for i in idx]
        for _ in range(int(math.log2(t)) - 2):
            between()
            both = [_dot(pw[i], jnp.concatenate([pw[i], inv[i]], axis=1)) for i in idx]
            pw = [both[i][:, :LANES] for i in idx]
            inv = [inv[i] + both[i][:, LANES:] for i in idx]
        between()
        inv = [inv[i] + _dot(pw[i], inv[i]) for i in idx]
        av = [_dot(a_ak[i], vs[i]) for i in idx]
        between()
        mw = [_dot(inv[i], jnp.concatenate([xa[i], av[i]], axis=1)) for i in idx]
        between()
        rw = [_dot(jnp.concatenate([a_rb[i], a_rk[i]], axis=1),
                   jnp.concatenate([mw[i], jnp.concatenate([zeros, vs[i]], axis=1)], axis=0)) for i in idx]
        between()
        pm = [jnp.where(same_head, _dot_tn(mw[i][:, :LANES], yb[i]), 0.0) for i in idx]
        qm = [jnp.where(same_head, _dot_tn(jnp.concatenate([mw[i][:, LANES:], vs[i]], axis=0), ybk[i]), 0.0)
              for i in idx]
        return [chain_step(c, xr[i] + rw[i][:, :LANES], rw[i][:, LANES:], pm[i], qm[i])
                for i, c in enumerate(grp)]

    pending = []

    def between():
        if pending:
            pending.pop(0)()

    group = RWKV_CHUNK_GROUP
    for g0 in range(0, len(chunks), group):
        steps = chunk_local(list(chunks[g0:g0 + group]), between)
        while pending:
            pending.pop(0)()
        pending.extend(steps)
    while pending:
        pending.pop(0)()
    st_ref[...] = state[0]
    y = jnp.concatenate([ys[c] for c in chunks], axis=0)

    mean = _dot_split(y, head_sum) * (1.0 / n)
    yc = y - mean
    var = _dot_split(yc * yc, head_sum) * (1.0 / n)
    yn = yc * lax.rsqrt(var + GN_EPS) * gnw_ref[...] + gnb_ref[...]
    o_ref[...] = ((yn + bonus) * g).astype(BF16)


def _rwkv(proj, lw, la, lg, w_w2, w_a2, w_g2, mu_r, mu_k, mu_v, w0, a0, k_k, k_a, r_k, gn_w, gn_b,
          col0, width):
    s = proj.shape[0]
    ts = min(RWKV_ROWS_PER_STEP, s)
    npair = width // LANES
    cb = col0 // LANES
    row = lambda a: pl.BlockSpec((ts, a.shape[1]), lambda p, j: (j, 0))
    wcol = lambda a: pl.BlockSpec((a.shape[0], LANES), lambda p, j: (0, p))
    vec = pl.BlockSpec((1, LANES), lambda p, j: (0, p))
    return pl.pallas_call(
        _rwkv_kernel,
        grid=(npair, s // ts),
        in_specs=[pl.BlockSpec((ts, LANES), lambda p, j: (j, cb + p)),
                  pl.BlockSpec((ts, LANES), lambda p, j: (j, cb + npair + p)),
                  pl.BlockSpec((ts, LANES), lambda p, j: (j, cb + 2 * npair + p)),
                  row(lw), row(la), row(lg), wcol(w_w2), wcol(w_a2), wcol(w_g2)] + [vec] * 10,
        out_specs=pl.BlockSpec((ts, LANES), lambda p, j: (j, p)),
        out_shape=jax.ShapeDtypeStruct((s, width), BF16),
        scratch_shapes=[pltpu.VMEM((LANES, LANES), F32), pltpu.VMEM((SUBLANES, LANES), F32)],
        name="rwkv7_scan",
        compiler_params=_params("parallel", "arbitrary"),
    )(proj, proj, proj, lw, la, lg, w_w2, w_a2, w_g2, mu_r, mu_k, mu_v, w0, a0, k_k, k_a, r_k, gn_w, gn_b)


def _route_kernel(x_ref, oa_ref, or_ref, ga_ref, gr_ref, wua_ref, wur_ref, wo_ref,
                  gt_ref, gpost_ref, gpre_ref, sc_ref, sh_ref, wr_ref, br_ref,
                  x1_ref, h2_ref, ei_ref, wt_ref, rk_ref, cnt_ref, run_ref, y_ref):
    i = pl.program_id(0)

    @pl.when(i == 0)
    def _():
        run_ref[...] = jnp.zeros_like(run_ref)
        y_ref[...] = jnp.zeros_like(y_ref)

    d = wo_ref.shape[0]
    halves = [slice(0, d // 2), slice(d // 2, d)]
    def up(cs):
        return (jnp.dot(oa_ref[...], wua_ref[:, cs], preferred_element_type=F32),
                jnp.dot(or_ref[...], wur_ref[:, cs], preferred_element_type=F32))

    def out(cs, ua, ur):
        mix = _sigmoid(ga_ref[:, cs].astype(F32)) * ua + _sigmoid(gr_ref[:, cs].astype(F32)) * ur
        return jnp.dot(mix.astype(BF16), wo_ref[cs, :], preferred_element_type=F32)

    up0, up1 = up(halves[0]), up(halves[1])
    logits = _route_logits(y_ref[...], x_ref, gt_ref, gpost_ref, gpre_ref, sc_ref, sh_ref, wr_ref, br_ref,
                           x1_ref, h2_ref)
    y_new = out(halves[0], *up0) + out(halves[1], *up1)
    _route_choose(logits, jnp.where(i > 0, 1.0, 0.0), ei_ref, wt_ref, rk_ref, run_ref)
    cnt_ref[...] = run_ref[...].astype(I32)
    y_ref[...] = y_new


def _route_logits(y, x_ref, gt_ref, gpost_ref, gpre_ref, sc_ref, sh_ref, wr_ref, br_ref, x1_ref, h2_ref):
    x1 = x_ref[...] + gt_ref[...] * _rms(y, gpost_ref[...])
    x1_ref[...] = x1
    h2 = _rms(x1, gpre_ref[...]) * (1.0 + sc_ref[...]) + sh_ref[...]
    h2_ref[...] = h2
    return _dot_x3(h2, wr_ref[...]) + br_ref[...]


def _route_choose(logits, valid, ei_ref, wt_ref, rk_ref, run_ref):
    rs = slice(None)
    lane = lax.broadcasted_iota(I32, logits.shape, 1)
    big = jnp.int32(4 * LANES)
    gmask = (lane >= N_EXPERTS) & (lane < N_EXPERTS + N_GROUPS)
    mg = jnp.max(jnp.where(gmask, logits, -jnp.inf), axis=1, keepdims=True)
    eg = jnp.where(gmask, jnp.exp(logits - mg), 0.0)
    pg = eg / jnp.sum(eg, axis=1, keepdims=True)
    pg_top = jnp.max(pg, axis=1, keepdims=True)
    g_idx = jnp.min(jnp.where(gmask & (pg == pg_top), lane, big), axis=1, keepdims=True) - N_EXPERTS
    emask = (lane >= g_idx * EXPERTS_PER_GROUP) & (lane < (g_idx + 1) * EXPERTS_PER_GROUP)
    me = jnp.max(jnp.where(emask, logits, -jnp.inf), axis=1, keepdims=True)
    ee = jnp.where(emask, jnp.exp(logits - me), 0.0)
    pe = ee / jnp.sum(ee, axis=1, keepdims=True)
    p1 = jnp.max(pe, axis=1, keepdims=True)
    i1 = jnp.min(jnp.where(emask & (pe == p1), lane, big), axis=1, keepdims=True)
    rest = emask & (lane != i1)
    p2 = jnp.max(jnp.where(rest, pe, -jnp.inf), axis=1, keepdims=True)
    i2 = jnp.min(jnp.where(rest & (pe == p2), lane, big), axis=1, keepdims=True)
    den = p1 + p2
    ei_ref[rs, :] = jnp.where(lane == 0, i1, jnp.where(lane == 1, i2, 0))
    wt_ref[rs, :] = jnp.where(lane == 0, pg_top * p1 / den, jnp.where(lane == 1, pg_top * p2 / den, 0.0))
    tm = logits.shape[0]
    chosen = jnp.where((lane == i1) | (lane == i2), valid, 0.0)
    t_i = lax.broadcasted_iota(I32, (tm, tm), 0)
    t_j = lax.broadcasted_iota(I32, (tm, tm), 1)
    before = _dot(jnp.where(t_j < t_i, 1.0, 0.0), chosen) + run_ref[...]
    r1 = jnp.sum(jnp.where(lane == i1, before, 0.0), axis=1, keepdims=True)
    r2 = jnp.sum(jnp.where(lane == i2, before, 0.0), axis=1, keepdims=True)
    rk_ref[rs, :] = jnp.where(lane == 0, r1, jnp.where(lane == 1, r2, 0.0)).astype(I32)
    run_ref[...] += jnp.sum(chosen, axis=0, keepdims=True)


def _cast_kernel(w_ref, o_ref):
    o_ref[...] = w_ref[...].astype(o_ref.dtype)


def _to_bf16(w):
    k, n = w.shape
    tk = 512
    return pl.pallas_call(
        _cast_kernel,
        grid=(k // tk,),
        in_specs=[pl.BlockSpec((tk, n), lambda i: (i, 0))],
        out_specs=pl.BlockSpec((tk, n), lambda i: (i, 0)),
        out_shape=jax.ShapeDtypeStruct((k, n), BF16),
        name="cast_bf16",
        compiler_params=_params("parallel"),
    )(w)


def _route(x, o_att, o_rwkv, proj, gate_col0, w_up_att, w_up_rwkv, w_o, gt1, g_post, g_pre, sc2, sh2,
           w_router, b_router):
    s, d = x.shape
    ka, kr = o_att.shape[1], o_rwkv.shape[1]
    tm = 256
    gb = gate_col0 // d
    nt = s // tm
    proj_tile = lambda i: jnp.minimum(i, nt - 1)
    route_tile = lambda i: jnp.maximum(i - 1, 0)
    vec = pl.BlockSpec((1, d), lambda i: (0, 0))
    rowblk = pl.BlockSpec((tm, d), lambda i: (route_tile(i), 0))
    small = pl.BlockSpec((tm, LANES), lambda i: (route_tile(i), 0))
    lane_row = pl.BlockSpec((1, LANES), lambda i: (0, 0))
    once = pl.Buffered(1)
    return pl.pallas_call(
        _route_kernel,
        grid=(nt + 1,),
        in_specs=[rowblk,
                  pl.BlockSpec((tm, ka), lambda i: (proj_tile(i), 0)),
                  pl.BlockSpec((tm, kr), lambda i: (proj_tile(i), 0)),
                  pl.BlockSpec((tm, d), lambda i: (proj_tile(i), gb)),
                  pl.BlockSpec((tm, d), lambda i: (proj_tile(i), gb + 1)),
                  pl.BlockSpec((ka, d), lambda i: (0, 0), pipeline_mode=once),
                  pl.BlockSpec((kr, d), lambda i: (0, 0), pipeline_mode=once),
                  pl.BlockSpec((d, d), lambda i: (0, 0), pipeline_mode=once),
                  vec, vec, vec, vec, vec, pl.BlockSpec((d, LANES), lambda i: (0, 0)), lane_row],
        out_specs=[rowblk, rowblk, small, small, small, lane_row],
        out_shape=[jax.ShapeDtypeStruct((s, d), F32), jax.ShapeDtypeStruct((s, d), F32),
                   jax.ShapeDtypeStruct((s, LANES), I32), jax.ShapeDtypeStruct((s, LANES), F32),
                   jax.ShapeDtypeStruct((s, LANES), I32), jax.ShapeDtypeStruct((1, LANES), I32)],
        scratch_shapes=[pltpu.VMEM((1, LANES), F32), pltpu.VMEM((tm, d), F32)],
        name="merge_out_route",
        compiler_params=_params("arbitrary"),
    )(x, o_att, o_rwkv, proj, proj, _to_bf16(w_up_att), _to_bf16(w_up_rwkv), _to_bf16(w_o),
      gt1, g_post, g_pre, sc2, sh2, w_router, b_router)


def _slots_kernel(ei_ref, rk_ref, ps_ref, pos_ref):
    lane = lax.broadcasted_iota(I32, ei_ref.shape, 1)
    ei, rk = ei_ref[...], rk_ref[...]
    ps = ps_ref[...]
    cols = []
    for kk in range(TOP_K):
        e = ei[:, kk:kk + 1]
        cols.append(jnp.sum(jnp.where(lane == e, ps, 0), axis=1, keepdims=True) + rk[:, kk:kk + 1])
    pos_ref[...] = jnp.where(lane == 0, cols[0], jnp.where(lane == 1, cols[1], 0))


def _slots(e_idx, rank, pstart_row):
    n = e_idx.shape[0]
    tm = 1024
    blk = pl.BlockSpec((tm, LANES), lambda i: (i, 0))
    return pl.pallas_call(
        _slots_kernel,
        grid=(n // tm,),
        in_specs=[blk, blk, pl.BlockSpec((1, LANES), lambda i: (0, 0))],
        out_specs=blk,
        out_shape=jax.ShapeDtypeStruct((n, LANES), I32),
        name="dispatch_slots",
        compiler_params=_params("parallel"),
    )(e_idx, rank, pstart_row)


def _row_copy(src, row, dst, dst_row, sem):
    return pltpu.make_async_copy(src.at[pl.ds(row, 1)], dst.at[pl.ds(dst_row, 1)], sem)


def _dispatch_kernel(pos_ref, cnt_ref, pst_ref, nused_ref, h_ref, x_hbm, stage_ref, zero_ref, sem, zsem):
    i = pl.program_id(0)
    tm = h_ref.shape[0]
    n_blk = x_hbm.shape[0] // EXPERT_BLOCK
    nused = nused_ref[0]
    slot = i % 2

    def wait_tile(s):
        for _ in range(TOP_K):
            pltpu.make_async_copy(stage_ref.at[s], x_hbm.at[pl.ds(0, tm)], sem.at[s]).wait()

    @pl.when(i >= 2)
    def _():
        wait_tile(slot)

    stage_ref[slot] = h_ref[...]
    for r in range(tm):
        for kk in range(TOP_K):
            _row_copy(stage_ref.at[slot], r, x_hbm, pos_ref[(i * tm + r) * TOP_K + kk],
                      sem.at[slot]).start(priority=(r * TOP_K + kk) % 2)

    @pl.when(i == 0)
    def _():
        zero_ref[...] = jnp.zeros_like(zero_ref)

        def pad_expert(start):
            def body(e, _):
                lo = pst_ref[e] + cnt_ref[e]
                length = (EXPERT_BLOCK - cnt_ref[e] % EXPERT_BLOCK) % EXPERT_BLOCK
                hi = lo + length
                size = EXPERT_BLOCK // 2
                while size >= SUBLANES:
                    at = pl.multiple_of(hi - (length & ~(size - 1)), size)

                    @pl.when((length & size) != 0)
                    def _(at=at, size=size):
                        cp = pltpu.make_async_copy(zero_ref.at[pl.ds(0, size)], x_hbm.at[pl.ds(at, size)], zsem)
                        if start:
                            cp.start()
                        else:
                            cp.wait()

                    size //= 2

                def single(s, _):
                    cp = _row_copy(zero_ref, 0, x_hbm, s, zsem)
                    if start:
                        cp.start()
                    else:
                        cp.wait()
                    return 0

                lax.fori_loop(lo, lo + (length & (SUBLANES - 1)), single, 0)
                return 0
            return body

        def tail_copy(blk):
            return pltpu.make_async_copy(zero_ref, x_hbm.at[pl.ds(blk * EXPERT_BLOCK, EXPERT_BLOCK)], zsem)

        def tail_start(blk, _):
            tail_copy(blk).start()
            return 0

        def tail_wait(blk, _):
            tail_copy(blk).wait()
            return 0

        lax.fori_loop(0, cnt_ref.shape[0], pad_expert(True), 0)
        lax.fori_loop(nused, n_blk, tail_start, 0)
        lax.fori_loop(0, cnt_ref.shape[0], pad_expert(False), 0)
        lax.fori_loop(nused, n_blk, tail_wait, 0)

    @pl.when(i == pl.num_programs(0) - 1)
    def _():
        @pl.when(i >= 1)
        def _():
            wait_tile(1 - slot)

        wait_tile(slot)


def _dispatch(pos_flat, counts, pstart, nused, h2, n_slots):
    n, w = h2.shape
    tm = 512
    grid_spec = pltpu.PrefetchScalarGridSpec(
        num_scalar_prefetch=4,
        grid=(n // tm,),
        in_specs=[pl.BlockSpec((tm, w), lambda i, *_: (i, 0))],
        out_specs=pl.BlockSpec(memory_space=pl.ANY),
        scratch_shapes=[pltpu.VMEM((2, tm, w), h2.dtype), pltpu.VMEM((EXPERT_BLOCK, w), h2.dtype),
                        pltpu.SemaphoreType.DMA((2,)), pltpu.SemaphoreType.DMA(())],
    )
    return pl.pallas_call(
        _dispatch_kernel,
        grid_spec=grid_spec,
        out_shape=jax.ShapeDtypeStruct((n_slots, w), h2.dtype),
        name="dispatch_rows",
        compiler_params=_params("arbitrary"),
    )(pos_flat, counts, pstart, nused, h2)


EXPERT_X_SLOTS = 4
EXPERT_Y_SLOTS = 3


def _expert_kernel(bstart_ref, bcount_ref, nused_ref, x_hbm, wg_ref, wu_ref, wd_ref, y_hbm,
                   xs_ref, yo_ref, wgb_ref, wub_ref, wdb_ref, xsem, osem):
    e = pl.program_id(0)
    nused = nused_ref[0]
    rows = EXPERT_BLOCK
    n_blk = y_hbm.shape[0] // rows
    nx, ny = EXPERT_X_SLOTS, EXPERT_Y_SLOTS
    first = bstart_ref[e]
    count = bcount_ref[e]

    def x_copy(blk):
        src = x_hbm.at[pl.ds(jnp.minimum(blk, n_blk - 1) * rows, rows)]
        return pltpu.make_async_copy(src, xs_ref.at[blk % nx], xsem.at[blk % nx])

    def out_copy(blk):
        return pltpu.make_async_copy(yo_ref.at[blk % ny], y_hbm.at[pl.ds(blk * rows, rows), :], osem.at[blk % ny])

    @pl.when(e == 0)
    def _():
        for j in range(nx - 1):
            x_copy(j).start()

    @pl.when(count > 0)
    def _():
        wgb_ref[...] = wg_ref[0].astype(BF16)
        wub_ref[...] = wu_ref[0].astype(BF16)
        wdb_ref[...] = wd_ref[0].astype(BF16)

    def block(j, _):
        blk = first + j

        @pl.when(blk >= ny)
        def _():
            out_copy(blk - ny).wait()

        x_copy(blk).wait()
        x_copy(blk + nx - 1).start()
        xb = xs_ref[blk % nx].astype(BF16)
        hg = jnp.dot(xb, wgb_ref[...], preferred_element_type=F32)
        hu = jnp.dot(xb, wub_ref[...], preferred_element_type=F32)
        hid = hg * _sigmoid(hg) * hu
        yo_ref[blk % ny] = jnp.dot(hid.astype(BF16), wdb_ref[...], preferred_element_type=F32)
        out_copy(blk).start()

        @pl.when(blk == nused - 1)
        def _():
            for ahead in range(1, nx):
                x_copy(blk + ahead).wait()

        return 0

    lax.fori_loop(0, count, block, 0)

    @pl.when(e == pl.num_programs(0) - 1)
    def _():
        for back in range(1, ny + 1):
            @pl.when(nused - back >= 0)
            def _():
                out_copy(nused - back).wait()

        yo_ref[0] = jnp.zeros(yo_ref.shape[1:], yo_ref.dtype)

        def zero_copy(blk):
            return pltpu.make_async_copy(yo_ref.at[0], y_hbm.at[pl.ds(blk * rows, rows), :], osem.at[0])

        def fill(blk, _):
            zero_copy(blk).start()
            return 0

        def drain(blk, _):
            zero_copy(blk).wait()
            return 0

        lax.fori_loop(nused, n_blk, fill, 0)
        lax.fori_loop(nused, n_blk, drain, 0)


def _experts(x_buf, bstart, bcount, nused, w_gate_e, w_up_e, w_down_e):
    n_slots = x_buf.shape[0]
    n_exp, d, f = w_gate_e.shape
    grid_spec = pltpu.PrefetchScalarGridSpec(
        num_scalar_prefetch=3,
        grid=(n_exp,),
        in_specs=[pl.BlockSpec(memory_space=pl.ANY),
                  pl.BlockSpec((1, d, f), lambda e, *_: (e, 0, 0)),
                  pl.BlockSpec((1, d, f), lambda e, *_: (e, 0, 0)),
                  pl.BlockSpec((1, f, d), lambda e, *_: (e, 0, 0))],
        out_specs=pl.BlockSpec(memory_space=pl.ANY),
        scratch_shapes=[pltpu.VMEM((EXPERT_X_SLOTS, EXPERT_BLOCK, d), F32),
                        pltpu.VMEM((EXPERT_Y_SLOTS, EXPERT_BLOCK, d), F32),
                        pltpu.VMEM((d, f), BF16), pltpu.VMEM((d, f), BF16), pltpu.VMEM((f, d), BF16),
                        pltpu.SemaphoreType.DMA((EXPERT_X_SLOTS,)), pltpu.SemaphoreType.DMA((EXPERT_Y_SLOTS,))],
    )
    return pl.pallas_call(
        _expert_kernel,
        grid_spec=grid_spec,
        out_shape=jax.ShapeDtypeStruct((n_slots, d), F32),
        name="experts",
        compiler_params=_params("arbitrary"),
    )(bstart, bcount, nused, x_buf, w_gate_e, w_up_e, w_down_e)


def _combine_kernel(pos_ref, y_hbm, wt_ref, x1_ref, gt_ref, gpost_ref, o_ref, rows_ref, sem):
    i = pl.program_id(0)
    nsteps = pl.num_programs(0)
    tm = x1_ref.shape[0]

    def start_rows(step, slot):
        for r in range(tm):
            for kk in range(TOP_K):
                _row_copy(y_hbm, pos_ref[(step * tm + r) * TOP_K + kk], rows_ref.at[slot, kk], r,
                          sem.at[slot]).start(priority=(r * TOP_K + kk) % 2)

    def wait_rows(slot):
        for kk in range(TOP_K):
            pltpu.make_async_copy(y_hbm.at[pl.ds(0, tm), :], rows_ref.at[slot, kk], sem.at[slot]).wait()

    @pl.when(i == 0)
    def _():
        start_rows(0, 0)

    slot = i % 2
    wait_rows(slot)
    start_rows(jnp.minimum(i + 1, nsteps - 1), 1 - slot)
    wt = wt_ref[...]
    y = rows_ref[slot, 0] * wt[:, 0:1] + rows_ref[slot, 1] * wt[:, 1:2]
    o_ref[...] = x1_ref[...] + gt_ref[...] * _rms(y, gpost_ref[...])

    @pl.when(i == nsteps - 1)
    def _():
        wait_rows(1 - slot)


def _combine(pos, y_buf, wts, x1, gt2, g_post):
    n, d = x1.shape
    tm = 512
    vec = pl.BlockSpec((1, d), lambda i, p: (0, 0))
    grid_spec = pltpu.PrefetchScalarGridSpec(
        num_scalar_prefetch=1,
        grid=(n // tm,),
        in_specs=[pl.BlockSpec(memory_space=pl.ANY),
                  pl.BlockSpec((tm, LANES), lambda i, p: (i, 0)),
                  pl.BlockSpec((tm, d), lambda i, p: (i, 0)), vec, vec],
        out_specs=pl.BlockSpec((tm, d), lambda i, p: (i, 0)),
        scratch_shapes=[pltpu.VMEM((2, TOP_K, tm, d), F32), pltpu.SemaphoreType.DMA((2,))],
    )
    return pl.pallas_call(
        _combine_kernel,
        grid_spec=grid_spec,
        out_shape=jax.ShapeDtypeStruct((n, d), F32),
        name="combine",
        compiler_params=_params("arbitrary"),
    )(pos, y_buf, wts, x1, gt2, g_post)


def _segment_tables(counts_row):
    counts = counts_row[0, :N_EXPERTS]
    pcounts = (counts + EXPERT_BLOCK - 1) // EXPERT_BLOCK * EXPERT_BLOCK
    pend = jnp.cumsum(pcounts)
    pstart = pend - pcounts
    nused = (pend[-1] // EXPERT_BLOCK).astype(I32)
    pstart_row = jnp.pad(pstart, (0, LANES - N_EXPERTS)).reshape(1, LANES)
    return counts, pstart, pstart_row, pstart // EXPERT_BLOCK, pcounts // EXPERT_BLOCK, nused.reshape(1)


def _rope_tables(s):
    half = ROPE_DIM // 2
    inv = ROPE_THETA ** (-np.arange(half, dtype=np.float64) / half)
    ang = np.arange(s, dtype=np.float64)[:, None] * inv[None, :]
    cos, sin = jnp.asarray(np.cos(ang), F32), jnp.asarray(np.sin(ang), F32)
    pad = jnp.zeros((s, LANES - ROPE_DIM), F32)
    zero = jnp.zeros((s, half), F32)
    cos_t = jnp.concatenate([cos, cos, pad + 1.0], axis=1)
    sin1_t = jnp.concatenate([-sin, zero, pad], axis=1)
    sin2_t = jnp.concatenate([zero, sin, pad], axis=1)
    return cos_t, sin1_t, sin2_t


def _layer(x, c_col, w_ada, b_ada, g_pre_mix, g_post_mix, g_pre_ffn, g_post_ffn, w_in, mu_r, mu_k, mu_v,
           mu_w, mu_a, mu_g, w0, w_w1, w_w2, a0, w_a1, w_a2, w_g1, w_g2, k_k, k_a, r_k, gn_w, gn_b,
           w_up_att, w_up_rwkv, w_o, w_rg, b_rg, w_re, b_re, w_gate_e, w_up_e, w_down_e):
    s, d = x.shape
    att_w = ATT_HEADS * ATT_HEAD_DIM
    rwkv_w = w_up_rwkv.shape[0]
    row = lambda a: a.reshape(1, -1)

    ada = _ada(c_col, w_ada, row(b_ada))
    sh1, sc1, gt1, sh2, sc2, gt2 = (ada[:, i * d:(i + 1) * d] for i in range(6))

    h, lw, la, lg = _prenorm(x, row(g_pre_mix), sc1, sh1, row(mu_w), row(mu_a), row(mu_g), w_w1, w_a1, w_g1)
    proj = _matmul(h, w_in, BF16)

    q_aug_t, k_aug, v_t = _rope_gate(proj, *_rope_tables(s))
    o_att = _attention(q_aug_t, k_aug, v_t)

    o_rwkv = _rwkv(proj, lw, la, lg, w_w2, w_a2, w_g2, row(mu_r), row(mu_k), row(mu_v), row(w0), row(a0),
                   row(k_k), row(k_a), row(r_k), row(gn_w), row(gn_b), col0=3 * att_w, width=rwkv_w)


    w_router = jnp.pad(jnp.concatenate([w_re, w_rg], axis=1), ((0, 0), (0, LANES - N_EXPERTS - N_GROUPS)))
    b_router = jnp.pad(jnp.concatenate([b_re, b_rg]), (0, LANES - N_EXPERTS - N_GROUPS)).reshape(1, LANES)
    x1, h2, e_idx, wts, rank, counts = _route(x, o_att, o_rwkv, proj, 3 * att_w + 3 * rwkv_w, w_up_att, w_up_rwkv,
                                              w_o, gt1, row(g_post_mix), row(g_pre_ffn), sc2, sh2,
                                              w_router, b_router)

    n_pairs = s * TOP_K
    n_blk = (n_pairs + N_EXPERTS * (EXPERT_BLOCK - 1) + EXPERT_BLOCK - 1) // EXPERT_BLOCK
    counts, pstart, pstart_row, bstart, bcount, nused = _segment_tables(counts)
    pos = _slots(e_idx, rank, pstart_row)[:, :TOP_K].reshape(n_pairs)
    x_buf = _dispatch(pos, counts, pstart, nused, h2, n_blk * EXPERT_BLOCK)
    y_buf = _experts(x_buf, bstart, bcount, nused, w_gate_e, w_up_e, w_down_e)
    return _combine(pos, y_buf, wts, x1, gt2, row(g_post_ffn))


def kernel(x, c, w_ada, b_ada, g_pre_mix, g_post_mix, g_pre_ffn, g_post_ffn, w_in, mu_r, mu_k, mu_v, mu_w, mu_a, mu_g, w0, w_w1, w_w2, a0, w_a1, w_a2, w_g1, w_g2, k_k, k_a, r_k, gn_w, gn_b, w_up_att, w_up_rwkv, w_o, w_rg, b_rg, w_re, b_re, w_gate_e, w_up_e, w_down_e):
    b, s, d = x.shape
    assert b == 1, "one sequence per call"
    params = (w_ada, b_ada, g_pre_mix, g_post_mix, g_pre_ffn, g_post_ffn, w_in, mu_r, mu_k, mu_v, mu_w, mu_a,
              mu_g, w0, w_w1, w_w2, a0, w_a1, w_a2, w_g1, w_g2, k_k, k_a, r_k, gn_w, gn_b, w_up_att,
              w_up_rwkv, w_o, w_rg, b_rg, w_re, b_re, w_gate_e, w_up_e, w_down_e)
    xs = x.reshape(s, d)
    c_col = c.reshape(d, 1)
    for l in range(w_ada.shape[0]):
        xs = _layer(xs, c_col, *(p[l] for p in params))
    return xs.reshape(b, s, d)
```
